```python
import jax, jax.numpy as jnp
from jax import lax
import numpy as np

D_MODEL = 2048
BATCH = 4
SEQ = 2048
DEPTH = 1
DEC_BATCH = 128
DEC_SEQ = 8
PAST_LEN = 16384
PAGE_SIZE = 128

D_MIX = D_MODEL
D_A = (3 * D_MIX) // 8
D_B = (3 * D_MIX) // 8
D_X = D_MIX - D_A - D_B
H_A = 4
HD_A = D_A // H_A
CHUNK_A = 128
H_B = 4
DV_B = D_B // H_B
DK_B = DV_B // 2
D_BK = H_B * DK_B
GATE_RANK = 16
GATE_TAU = 16.0
CHUNK_B = 64
H_X = 4
HD_X = D_X // H_X
N_MEM = 256
EPS = 1e-6
IN_WIDTHS = (D_A, D_A, D_A, D_BK, D_BK, D_B, GATE_RANK, D_B, D_X, D_X)
D_IN = sum(IN_WIDTHS)

kernel_name = "hybrid_chunkmlp_gla_memxattn_step"


def rmsnorm(x, w):
    xf = x.astype(jnp.float32)
    y = xf * lax.rsqrt(jnp.mean(xf * xf, axis=-1, keepdims=True) + EPS)
    return (y * w.astype(jnp.float32)).astype(x.dtype)


def split_points():
    return [int(c) for c in np.cumsum(IN_WIDTHS)[:-1]]


def spatial_gating(u, v, w_s, b_s):
    bn, t = u.shape[0], u.shape[1]
    ln = min(t, CHUNK_A)
    n = t // ln
    mask = jnp.tril(jnp.ones((ln, ln), dtype=bool))
    w = jnp.where(mask, w_s[:, :ln, :ln], 0)
    vc = v.reshape(bn, n, ln, H_A, HD_A)
    mixed = jnp.einsum('hij,bnjhd->bnihd', w, vc) + b_s[:, :ln].T[None, None, :, :, None]
    return u * mixed.reshape(bn, t, H_A, HD_A)


def gla(q, k, v, log_a, s0):
    f32 = jnp.float32
    bn, t, h, dk = q.shape
    dv = v.shape[-1]
    c = min(t, CHUNK_B)
    n = t // c
    qc = q.astype(f32).reshape(bn, n, c, h, dk) * (dk ** -0.5)
    kc = k.astype(f32).reshape(bn, n, c, h, dk)
    vc = v.astype(f32).reshape(bn, n, c, h, dv)
    cum = jnp.cumsum(log_a.astype(f32).reshape(bn, n, c, h, dk), axis=2)
    q_in = qc * jnp.exp(cum)
    k_in = kc * jnp.exp(-cum)
    k_out = kc * jnp.exp(cum[:, :, -1:] - cum)
    decay = jnp.exp(cum[:, :, -1])
    mask = jnp.tril(jnp.ones((c, c), dtype=bool))
    scores = jnp.where(mask, jnp.einsum('bnihd,bnjhd->bnhij', q_in, k_in), 0.0)
    o_intra = jnp.einsum('bnhij,bnjhe->bnihe', scores, vc)

    def step(s, inp):
        q_i, k_o, v_i, dec = inp
        o = jnp.einsum('bihd,bhde->bihe', q_i, s)
        s = dec[..., None] * s + jnp.einsum('bjhd,bjhe->bhde', k_o, v_i)
        return s, o

    xs = (jnp.moveaxis(q_in, 1, 0), jnp.moveaxis(k_out, 1, 0),
          jnp.moveaxis(vc, 1, 0), jnp.moveaxis(decay, 1, 0))
    s_fin, o_inter = lax.scan(step, s0.astype(f32), xs)
    o = o_intra + jnp.moveaxis(o_inter, 0, 1)
    return o.reshape(bn, t, h, dv), s_fin


def memory_kv(mem, mem_norm_w, w_mem_kv):
    m = rmsnorm(mem, mem_norm_w) @ w_mem_kv
    mk, mv = jnp.split(m, 2, axis=-1)
    bn = mem.shape[0]
    return mk.reshape(bn, N_MEM, H_X, HD_X), mv.reshape(bn, N_MEM, H_X, HD_X)


def memory_attention(q, mk, mv):
    s = jnp.einsum('bthd,bnhd->bhtn', q, mk).astype(jnp.float32) * (HD_X ** -0.5)
    p = jax.nn.softmax(s, axis=-1)
    return jnp.einsum('bhtn,bnhd->bthd', p.astype(mv.dtype), mv)


def mixer_layer(x, mk, mv, s0, norm_w, w_in, a_vnorm_w, a_ws, a_bs,
                b_wa, b_ba, b_onorm_w, w_out):
    bn, t, _ = x.shape
    h = rmsnorm(x, norm_w)
    proj = h @ w_in
    a_u, a_v, a_g, b_q, b_k, b_v, b_r, b_g, x_q, x_g = jnp.split(proj, split_points(), axis=-1)
    a_vn = rmsnorm(a_v, a_vnorm_w)
    a_o = spatial_gating(a_u.reshape(bn, t, H_A, HD_A), a_vn.reshape(bn, t, H_A, HD_A),
                         a_ws, a_bs).reshape(bn, t, D_A)
    log_a = jax.nn.log_sigmoid((b_r @ b_wa + b_ba).astype(jnp.float32)) / GATE_TAU
    o_b, s_new = gla(b_q.reshape(bn, t, H_B, DK_B), b_k.reshape(bn, t, H_B, DK_B),
                     b_v.reshape(bn, t, H_B, DV_B), log_a.reshape(bn, t, H_B, DK_B), s0)
    o_b = rmsnorm(o_b, b_onorm_w).astype(x.dtype).reshape(bn, t, D_B)
    o_x = memory_attention(x_q.reshape(bn, t, H_X, HD_X), mk, mv).reshape(bn, t, D_X)
    branches = jnp.concatenate([a_o * jax.nn.silu(a_g),
                                o_b * jax.nn.silu(b_g),
                                o_x * jax.nn.silu(x_g)], axis=-1)
    y = x + branches @ w_out
    return y, a_vn, s_new.astype(x.dtype)


def setup_inputs(seed: int = 0) -> dict:
    key = jax.random.key(seed)
    ks = jax.random.split(key, 20)
    f32 = jnp.float32
    nrm = lambda k, s: jax.random.normal(k, s, f32)
    return {
        "x_prompt": nrm(ks[0], (BATCH, SEQ, D_MODEL)),
        "x_sample": nrm(ks[1], (DEC_BATCH, DEC_SEQ, D_MODEL)),
        "mem_prompt": nrm(ks[2], (BATCH, N_MEM, D_MODEL)),
        "state_gla": nrm(ks[3], (DEPTH, DEC_BATCH, H_B, DK_B, DV_B)),
        "cache_mem_k": nrm(ks[4], (DEPTH, DEC_BATCH, N_MEM, H_X, HD_X)),
        "cache_mem_v": nrm(ks[5], (DEPTH, DEC_BATCH, N_MEM, H_X, HD_X)),
        "norm_w": 1.0 + 0.02 * nrm(ks[6], (DEPTH, D_MODEL)),
        "w_in": nrm(ks[7], (DEPTH, D_MODEL, D_IN)) * D_MODEL ** -0.5,
        "a_vnorm_w": 1.0 + 0.02 * nrm(ks[8], (DEPTH, D_A)),
        "a_ws": nrm(ks[9], (DEPTH, H_A, CHUNK_A, CHUNK_A)) * CHUNK_A ** -0.5,
        "a_bs": 0.02 * nrm(ks[10], (DEPTH, H_A, CHUNK_A)),
        "b_wa": nrm(ks[11], (DEPTH, GATE_RANK, D_BK)) * GATE_RANK ** -0.5,
        "b_ba": 0.02 * nrm(ks[12], (DEPTH, D_BK)),
        "b_onorm_w": 1.0 + 0.02 * nrm(ks[13], (DEPTH, DV_B)),
        "mem_norm_w": 1.0 + 0.02 * nrm(ks[14], (DEPTH, D_MODEL)),
        "w_mem_kv": nrm(ks[15], (DEPTH, D_MODEL, 2 * D_X)) * D_MODEL ** -0.5,
        "w_out": nrm(ks[16], (DEPTH, D_MIX, D_MODEL)) * D_MIX ** -0.5,
        "final_norm_w": 1.0 + 0.02 * nrm(ks[17], (D_MODEL,)),
    }


def reference(x_prompt, x_sample, mem_prompt, state_gla, cache_mem_k, cache_mem_v,
              norm_w, w_in, a_vnorm_w, a_ws, a_bs, b_wa, b_ba, b_onorm_w,
              mem_norm_w, w_mem_kv, w_out, final_norm_w):
    bp = x_prompt.shape[0]
    hp, hs = x_prompt, x_sample
    mem_k_p, mem_v_p, st_p, st_s, av_s = [], [], [], [], []
    for l in range(DEPTH):
        layer_w = (norm_w[l], w_in[l], a_vnorm_w[l], a_ws[l], a_bs[l],
                   b_wa[l], b_ba[l], b_onorm_w[l], w_out[l])
        mk, mv = memory_kv(mem_prompt, mem_norm_w[l], w_mem_kv[l])
        s0 = jnp.zeros((bp, H_B, DK_B, DV_B), dtype=jnp.float32)
        hp, _, sp = mixer_layer(hp, mk, mv, s0, *layer_w)
        hs, avs, ss = mixer_layer(hs, cache_mem_k[l], cache_mem_v[l], state_gla[l], *layer_w)
        mem_k_p.append(mk)
        mem_v_p.append(mv)
        st_p.append(sp)
        st_s.append(ss)
        av_s.append(avs)
    y_prompt = rmsnorm(hp, final_norm_w)
    y_sample = rmsnorm(hs, final_norm_w)
    return (y_prompt, y_sample, jnp.stack(mem_k_p), jnp.stack(mem_v_p),
            jnp.stack(st_p), jnp.stack(st_s), jnp.stack(av_s))
```

```python
import functools

import jax
import jax.numpy as jnp
from jax import lax
from jax.experimental import pallas as pl
from jax.experimental.pallas import tpu as pltpu

F32 = jnp.float32
BF16 = jnp.bfloat16

D_MODEL = 2048
D_A = 768
H_A = 4
HD_A = 192
CHUNK_A = 128
D_B = 768
H_B = 4
DV_B = 192
DK_B = 96
D_BK = 384
GATE_RANK = 16
GATE_TAU = 16.0
CHUNK_B = 64
D_X = 512
H_X = 4
HD_X = 128
N_MEM = 256
EPS = 1e-6

LANES = 128
C_AU, C_AV, C_AG = 0, 768, 1536
C_BQ, C_BK, C_BV, C_BG = 2304, 2688, 3072, 3840
C_XQ, C_XG, C_BR = 4608, 5120, 5632
D_PROJ = 5760
WIN_START = (0, 128, 384, 512)
WIN = 256

VMEM_LIMIT = 56 * 1024 * 1024


def _dot(a, b):
    return jnp.dot(a, b, preferred_element_type=F32)


def _dot_nt(a, b):
    return lax.dot_general(a, b, (((1,), (1,)), ((), ())), preferred_element_type=F32)


def _dot_tn(a, b):
    return lax.dot_general(a, b, (((0,), (0,)), ((), ())), preferred_element_type=F32)


def _split(x):
    hi = x.astype(BF16)
    lo = (x - hi.astype(F32)).astype(BF16)
    return hi, lo


def _silu(x):
    return x / (1.0 + jnp.exp(-x))


def _log_sigmoid(x):
    return jnp.minimum(x, 0.0) - jnp.log1p(jnp.exp(-jnp.abs(x)))


def _head_id(idx, width, n):
    h = jnp.zeros_like(idx)
    for i in range(1, n):
        h = h + (idx >= i * width).astype(jnp.int32)
    return h


def _iota(shape, dim):
    return lax.broadcasted_iota(jnp.int32, shape, dim)


def _merge_windows(r):
    rows = r[0].shape[0]
    lo = _iota((rows, LANES), 1) < 64
    tiles = [r[0][:, :LANES], jnp.where(lo, r[0][:, LANES:], r[1][:, :LANES]), r[1][:, LANES:],
             r[2][:, :LANES], jnp.where(lo, r[2][:, LANES:], r[3][:, :LANES]), r[3][:, LANES:]]
    return jnp.concatenate(tiles, axis=1)


def _group_a(proj_ref, wa_ref, abt_ref, avw_ref, mask):
    u = proj_ref[:, C_AU:C_AU + D_A].astype(F32)
    v = proj_ref[:, C_AV:C_AV + D_A].astype(F32)
    g = proj_ref[:, C_AG:C_AG + D_A].astype(F32)
    rows = u.shape[0]
    ms = jnp.mean(v * v, axis=-1, keepdims=True)
    vn = v * lax.rsqrt(ms + EPS) * avw_ref[...]
    vb = vn.astype(BF16)
    r = []
    for gi in range(H_A):
        w = jnp.where(mask, wa_ref[gi], 0.0).astype(BF16)
        r.append(_dot(w, vb[:, WIN_START[gi]:WIN_START[gi] + WIN]))
    mixed = _merge_windows(r)
    col_g = _head_id(_iota((rows, D_A), 1), HD_A, H_A)
    bias = jnp.zeros((rows, D_A), F32)
    for gi in range(H_A):
        bias = jnp.where(col_g == gi, abt_ref[:, gi:gi + 1], bias)
    return u * (mixed + bias) * _silu(g), vn


def _gla_intra_windows(q_in, k_in, vb, mask):
    rows = q_in.shape[0]
    col_h = _head_id(_iota((rows, D_BK), 1), DK_B, H_B)
    kb = k_in.astype(BF16)
    r = []
    for h in range(H_B):
        qh = jnp.where(col_h == h, q_in, 0.0).astype(BF16)
        s = jnp.where(mask, _dot_nt(qh, kb), 0.0).astype(BF16)
        r.append(_dot(s, vb[:, WIN_START[h]:WIN_START[h] + WIN]))
    return r


def _norm_matmul_kernel(x_ref, nw_ref, w_ref, o_ref, h_ref):
    @pl.when(pl.program_id(1) == 0)
    def _():
        x = x_ref[...]
        ms = jnp.mean(x * x, axis=-1, keepdims=True)
        h_ref[...] = (x * lax.rsqrt(ms + EPS) * nw_ref[...]).astype(BF16)

    o_ref[...] = _dot(h_ref[...], w_ref[...]).astype(o_ref.dtype)


def _norm_matmul(x, nw, w, *, bm, bn, out_dtype, split_out=False, name):
    m, k = x.shape
    n = w.shape[1]
    assert m % bm == 0 and n % bn == 0
    if split_out:
        out_shape = jax.ShapeDtypeStruct((n // bn, m, bn), out_dtype)
        out_spec = pl.BlockSpec((None, bm, bn), lambda i, j: (j, i, 0))
    else:
        out_shape = jax.ShapeDtypeStruct((m, n), out_dtype)
        out_spec = pl.BlockSpec((bm, bn), lambda i, j: (i, j))
    return pl.pallas_call(
        _norm_matmul_kernel,
        grid=(m // bm, n // bn),
        in_specs=[pl.BlockSpec((bm, k), lambda i, j: (i, 0)),
                  pl.BlockSpec((1, k), lambda i, j: (0, 0)),
                  pl.BlockSpec((k, bn), lambda i, j: (0, j))],
        out_specs=out_spec,
        out_shape=out_shape,
        scratch_shapes=[pltpu.VMEM((bm, k), BF16)],
        compiler_params=pltpu.CompilerParams(
            dimension_semantics=("arbitrary", "arbitrary"), vmem_limit_bytes=VMEM_LIMIT),
        name=name,
    )(x, nw.reshape(1, k), w)


def _out_proj_kernel(br_ref, w_ref, x_ref, fw_ref, y_ref):
    acc = _dot(br_ref[...], w_ref[...]) + x_ref[...]
    ms = jnp.mean(acc * acc, axis=-1, keepdims=True)
    y_ref[...] = acc * lax.rsqrt(ms + EPS) * fw_ref[...]


def _out_proj(br, w, x, fw, *, bm, name):
    m, k = br.shape
    n = w.shape[1]
    return pl.pallas_call(
        _out_proj_kernel,
        grid=(m // bm,),
        in_specs=[pl.BlockSpec((bm, k), lambda i: (i, 0)),
                  pl.BlockSpec((k, n), lambda i: (0, 0)),
                  pl.BlockSpec((bm, n), lambda i: (i, 0)),
                  pl.BlockSpec((1, n), lambda i: (0, 0))],
        out_specs=pl.BlockSpec((bm, n), lambda i: (i, 0)),
        out_shape=jax.ShapeDtypeStruct((m, n), F32),
        compiler_params=pltpu.CompilerParams(
            dimension_semantics=("arbitrary",), vmem_limit_bytes=VMEM_LIMIT),
        name=name,
    )(br, w, x, fw.reshape(1, n))


PB = 128


def _prompt_mixer_kernel(proj_ref, mk_ref, mv_ref, wa_ref, abt_ref, avw_ref, bwa_ref, bba_ref,
                         onw_ref, out_ref, st_ref, sbd_ref, bdm_ref, ind_ref):
    b = pl.program_id(0)
    t = pl.program_id(1)
    nt = pl.num_programs(1)

    @pl.when((b == 0) & (t == 0))
    def _():
        rh = _head_id(_iota((D_BK, D_B), 0), DK_B, H_B)
        ch = _head_id(_iota((D_BK, D_B), 1), DV_B, H_B)
        bdm_ref[...] = jnp.where(rh == ch, 1.0, 0.0).astype(F32)
        r2 = _head_id(_iota((D_B, D_B), 0), DV_B, H_B)
        c2 = _head_id(_iota((D_B, D_B), 1), DV_B, H_B)
        ind_ref[...] = jnp.where(r2 == c2, 1.0, 0.0).astype(BF16)

    @pl.when(t == 0)
    def _():
        sbd_ref[...] = jnp.zeros_like(sbd_ref)

    ri = _iota((PB, PB), 0)
    ci = _iota((PB, PB), 1)
    a_br, _ = _group_a(proj_ref, wa_ref, abt_ref, avw_ref, ci <= ri)
    out_ref[:, 0:D_A] = a_br.astype(BF16)

    q = proj_ref[:, C_BQ:C_BQ + D_BK].astype(F32)
    k = proj_ref[:, C_BK:C_BK + D_BK].astype(F32)
    vb = proj_ref[:, C_BV:C_BV + D_B]
    pre = _dot(proj_ref[:, C_BR:C_BR + LANES], bwa_ref[...]) + bba_ref[...]
    log_a = _log_sigmoid(pre) * (1.0 / GATE_TAU)
    r64 = _iota((CHUNK_B, CHUNK_B), 0)
    c64 = _iota((CHUNK_B, CHUNK_B), 1)
    tril = c64 <= r64
    tril_bf = jnp.where(tril, 1.0, 0.0).astype(BF16)
    ones_bf = jnp.ones((CHUNK_B, LANES), BF16)
    o_chunks = []
    for c in range(PB // CHUNK_B):
        sl = slice(c * CHUNK_B, (c + 1) * CHUNK_B)
        hi, lo = _split(log_a[sl])
        cum = _dot(tril_bf, hi) + _dot(tril_bf, lo)
        tot_col = _dot_tn(hi, ones_bf) + _dot_tn(lo, ones_bf)
        tot_row = cum[CHUNK_B - 1:CHUNK_B, :]
        q_in = q[sl] * (DK_B ** -0.5) * jnp.exp(cum)
        k_in = k[sl] * jnp.exp(-cum)
        k_out = k[sl] * jnp.exp(tot_row - cum)
        o_intra = _merge_windows(_gla_intra_windows(q_in, k_in, vb[sl], tril))
        s_old = sbd_ref[...]
        o_inter = _dot(q_in.astype(BF16), s_old.astype(BF16))
        kv = _dot_tn(k_out.astype(BF16), vb[sl])
        dec = jnp.exp(tot_col)
        dec = jnp.concatenate([dec] * (D_B // LANES), axis=1)
        sbd_ref[...] = s_old * dec + jnp.where(bdm_ref[...] > 0.5, kv, 0.0)
        o_chunks.append(o_intra + o_inter)
    o = jnp.concatenate(o_chunks, axis=0)
    hi, lo = _split(o * o)
    ssq = _dot(hi, ind_ref[...]) + _dot(lo, ind_ref[...])
    o_n = o * lax.rsqrt(ssq * (1.0 / DV_B) + EPS) * onw_ref[...]
    bg = proj_ref[:, C_BG:C_BG + D_B].astype(F32)
    out_ref[:, D_A:D_A + D_B] = (o_n * _silu(bg)).astype(BF16)

    @pl.when(t == nt - 1)
    def _():
        for h in range(H_B):
            off = h * DV_B - WIN_START[h]
            blk = sbd_ref[h * DK_B:(h + 1) * DK_B, WIN_START[h]:WIN_START[h] + WIN]
            if off:
                blk = pltpu.roll(blk, WIN - off, 1)
            st_ref[0, h] = blk[:, :DV_B]

    for h in range(H_X):
        hs = slice(h * HD_X, (h + 1) * HD_X)
        qh = proj_ref[:, C_XQ + h * HD_X:C_XQ + (h + 1) * HD_X]
        s = _dot_nt(qh, mk_ref[:, hs].astype(BF16)) * (HD_X ** -0.5)
        e = jnp.exp(s - jnp.max(s, axis=-1, keepdims=True))
        den = jnp.sum(e, axis=-1, keepdims=True)
        ox = _dot(e.astype(BF16), mv_ref[:, hs].astype(BF16)) / den
        xg = proj_ref[:, C_XG + h * HD_X:C_XG + (h + 1) * HD_X].astype(F32)
        out_ref[:, D_A + D_B + h * HD_X:D_A + D_B + (h + 1) * HD_X] = (ox * _silu(xg)).astype(BF16)


def _prompt_mixer(proj, memkv, wa, abt, avw, bwa, bba, onw, *, batch, seq):
    nt = seq // PB
    const = lambda *shape: pl.BlockSpec(shape, lambda b, t: (0,) * len(shape))
    return pl.pallas_call(
        _prompt_mixer_kernel,
        grid=(batch, nt),
        in_specs=[pl.BlockSpec((PB, D_PROJ), lambda b, t: (b * nt + t, 0)),
                  pl.BlockSpec((None, N_MEM, D_X), lambda b, t: (0, b, 0)),
                  pl.BlockSpec((None, N_MEM, D_X), lambda b, t: (1, b, 0)),
                  const(H_A, CHUNK_A, CHUNK_A), const(CHUNK_A, H_A), const(1, D_A),
                  const(LANES, D_BK), const(1, D_BK), const(1, D_B)],
        out_specs=[pl.BlockSpec((PB, D_MODEL), lambda b, t: (b * nt + t, 0)),
                   pl.BlockSpec((1, H_B, DK_B, DV_B), lambda b, t: (b, 0, 0, 0))],
        out_shape=[jax.ShapeDtypeStruct((batch * seq, D_MODEL), BF16),
                   jax.ShapeDtypeStruct((batch, H_B, DK_B, DV_B), F32)],
        scratch_shapes=[pltpu.VMEM((D_BK, D_B), F32),
                        pltpu.VMEM((D_BK, D_B), F32),
                        pltpu.VMEM((D_B, D_B), BF16)],
        compiler_params=pltpu.CompilerParams(
            dimension_semantics=("arbitrary", "arbitrary"), vmem_limit_bytes=VMEM_LIMIT),
        name="prompt_mixer",
    )(proj, memkv, memkv, wa, abt, avw, bwa, bba, onw)


NS = 8
TS = 8
SB = NS * TS
SBP = 128


def _sample_mixer_kernel(proj_ref, st_ref, ck_ref, cv_ref, wa_ref, abt_ref, avw_ref, bwa_ref,
                         bba_ref, onw_ref, out_ref, stn_ref, cvs_ref,
                         qin_ref, xq_ref, kot_ref, lath_ref, latl_ref, vhm_ref, ghm_ref, ohm_ref,
                         ox_ref):
    ri = _iota((SB, SB), 0)
    ci = _iota((SB, SB), 1)
    same_seq = jnp.right_shift(ri, 3) == jnp.right_shift(ci, 3)
    causal = same_seq & (ci <= ri)

    a_br, vn = _group_a(proj_ref, wa_ref, abt_ref, avw_ref, causal)
    out_ref[:, 0:D_A] = a_br.astype(BF16)
    cvs_ref[...] = vn

    q = proj_ref[:, C_BQ:C_BQ + D_BK].astype(F32)
    k = proj_ref[:, C_BK:C_BK + D_BK].astype(F32)
    vb = proj_ref[:, C_BV:C_BV + D_B]
    bgb = proj_ref[:, C_BG:C_BG + D_B]
    pre = _dot(proj_ref[:, C_BR:C_BR + LANES], bwa_ref[...]) + bba_ref[...]
    log_a = _log_sigmoid(pre) * (1.0 / GATE_TAU)
    causal_bf = jnp.where(causal, 1.0, 0.0).astype(BF16)
    seq_bf = jnp.where(same_seq, 1.0, 0.0).astype(BF16)
    hi, lo = _split(log_a)
    cum = _dot(causal_bf, hi) + _dot(causal_bf, lo)
    tot = _dot(seq_bf, hi) + _dot(seq_bf, lo)
    q_in = q * (DK_B ** -0.5) * jnp.exp(cum)
    k_in = k * jnp.exp(-cum)
    k_out = k * jnp.exp(tot - cum)
    qin_ref[...] = q_in
    zpad = jnp.zeros((SBP - SB, D_BK), F32)
    kot_ref[...] = jnp.concatenate([k_out, zpad], axis=0).T.astype(BF16)
    lat_hi, lat_lo = _split(jnp.concatenate([log_a, zpad], axis=0).T)
    lath_ref[...] = lat_hi
    latl_ref[...] = lat_lo
    xq_ref[...] = proj_ref[:, C_XQ:C_XQ + D_X].astype(F32)

    wins = _gla_intra_windows(q_in, k_in, vb, causal)
    sel_r = _iota((D_B, DV_B), 0)
    sel_c = _iota((D_B, DV_B), 1)
    vhm_ref[...] = jnp.zeros_like(vhm_ref)
    for h in range(H_B):
        off = h * DV_B - WIN_START[h]
        w = wins[h]
        if off:
            w = pltpu.roll(w, WIN - off, 1)
        ohm_ref[h] = w[:, :DV_B]
        sel = jnp.where(sel_r == sel_c + h * DV_B, 1.0, 0.0).astype(BF16)
        vhm_ref[h, 0:SB, :] = _dot(vb, sel)
        ghm_ref[h] = _dot(bgb, sel)

    mask_x = jnp.right_shift(_iota((H_X * TS, D_X), 0), 3) == jnp.right_shift(_iota((H_X * TS, D_X), 1), 7)
    mask_b = jnp.right_shift(_iota((H_B * TS, D_BK), 0), 3) == _head_id(_iota((H_B * TS, D_BK), 1), DK_B, H_B)
    row_seq = jnp.right_shift(_iota((SBP, DV_B), 0), 3)

    def per_seq(s, carry):
        r0 = pl.multiple_of(s * TS, TS)
        q8 = xq_ref[pl.ds(r0, TS), :]
        qbd = jnp.where(mask_x, jnp.concatenate([q8] * H_X, axis=0), 0.0).astype(BF16)
        sc = _dot_nt(qbd, ck_ref[s].astype(BF16)) * (HD_X ** -0.5)
        e = jnp.exp(sc - jnp.max(sc, axis=-1, keepdims=True))
        den = jnp.sum(e, axis=-1, keepdims=True)
        o = _dot(e.astype(BF16), cv_ref[s].astype(BF16)) / den
        ox_ref[pl.ds(r0, TS), :] = jnp.concatenate(
            [o[h * TS:(h + 1) * TS, h * HD_X:(h + 1) * HD_X] for h in range(H_X)], axis=1)
        qi8 = qin_ref[pl.ds(r0, TS), :]
        qbd2 = jnp.where(mask_b, jnp.concatenate([qi8] * H_B, axis=0), 0.0).astype(BF16)
        s0 = st_ref[s]
        o_inter = _dot(qbd2, s0.astype(BF16))
        in_seq = row_seq == s
        for h in range(H_B):
            ohm_ref[h, pl.ds(r0, TS), :] += o_inter[h * TS:(h + 1) * TS]
        row_sel = jnp.where(in_seq, 1.0, 0.0).astype(BF16)
        dec = jnp.exp(_dot(lath_ref[...], row_sel) + _dot(latl_ref[...], row_sel))
        kot = kot_ref[...]
        kv = jnp.concatenate(
            [_dot(kot[h * DK_B:(h + 1) * DK_B],
                  jnp.where(in_seq, vhm_ref[h], 0.0).astype(BF16)) for h in range(H_B)], axis=0)
        stn_ref[s] = s0 * dec + kv
        return carry

    lax.fori_loop(0, NS, per_seq, 0)

    selt_r = _iota((DV_B, D_B), 0)
    selt_c = _iota((DV_B, D_B), 1)
    b_br = jnp.zeros((SB, D_B), F32)
    for h in range(H_B):
        o_h = ohm_ref[h]
        ms = jnp.mean(o_h * o_h, axis=-1, keepdims=True)
        ob = (o_h * lax.rsqrt(ms + EPS) * onw_ref[...] * _silu(ghm_ref[h])).astype(BF16)
        selt = jnp.where(selt_c == selt_r + h * DV_B, 1.0, 0.0).astype(BF16)
        b_br = b_br + _dot(ob, selt)
    out_ref[:, D_A:D_A + D_B] = b_br.astype(BF16)

    xg = proj_ref[:, C_XG:C_XG + D_X].astype(F32)
    out_ref[:, D_A + D_B:D_MODEL] = (ox_ref[...] * _silu(xg)).astype(BF16)


def _sample_mixer(proj, state, ck, cv, wa, abt, avw, bwa, bba, onw):
    nseq = state.shape[0]
    const = lambda *shape: pl.BlockSpec(shape, lambda i: (0,) * len(shape))
    return pl.pallas_call(
        _sample_mixer_kernel,
        grid=(nseq // NS,),
        in_specs=[pl.BlockSpec((SB, D_PROJ), lambda i: (i, 0)),
                  pl.BlockSpec((NS, D_BK, DV_B), lambda i: (i, 0, 0)),
                  pl.BlockSpec((NS, N_MEM, D_X), lambda i: (i, 0, 0)),
                  pl.BlockSpec((NS, N_MEM, D_X), lambda i: (i, 0, 0)),
                  const(H_A, SB, SB), const(SB, H_A), const(1, D_A),
                  const(LANES, D_BK), const(1, D_BK), const(1, DV_B)],
        out_specs=[pl.BlockSpec((SB, D_MODEL), lambda i: (i, 0)),
                   pl.BlockSpec((NS, D_BK, DV_B), lambda i: (i, 0, 0)),
                   pl.BlockSpec((SB, D_A), lambda i: (i, 0))],
        out_shape=[jax.ShapeDtypeStruct((nseq * TS, D_MODEL), BF16),
                   jax.ShapeDtypeStruct((nseq, D_BK, DV_B), F32),
                   jax.ShapeDtypeStruct((nseq * TS, D_A), F32)],
        scratch_shapes=[pltpu.VMEM((SB, D_BK), F32),
                        pltpu.VMEM((SB, D_X), F32),
                        pltpu.VMEM((D_BK, SBP), BF16),
                        pltpu.VMEM((D_BK, SBP), BF16),
                        pltpu.VMEM((D_BK, SBP), BF16),
                        pltpu.VMEM((H_B, SBP, DV_B), F32),
                        pltpu.VMEM((H_B, SB, DV_B), F32),
                        pltpu.VMEM((H_B, SB, DV_B), F32),
                        pltpu.VMEM((SB, D_X), F32)],
        compiler_params=pltpu.CompilerParams(
            dimension_semantics=("arbitrary",), vmem_limit_bytes=VMEM_LIMIT),
        name="sample_mixer",
    )(proj, state, ck, cv, wa, abt, avw, bwa, bba, onw)


def kernel(x_prompt, x_sample, mem_prompt, state_gla, cache_mem_k, cache_mem_v, norm_w, w_in,
           a_vnorm_w, a_ws, a_bs, b_wa, b_ba, b_onorm_w, mem_norm_w, w_mem_kv, w_out, final_norm_w):
    batch, seq, _ = x_prompt.shape
    nseq, tdec, _ = x_sample.shape
    depth = w_in.shape[0]
    assert depth == 1 and tdec == TS and seq % PB == 0 and nseq % NS == 0

    w = w_in[0]
    c0 = 3 * D_A + 2 * D_BK + D_B
    w_cat = jnp.concatenate(
        [w[:, :c0], w[:, c0 + GATE_RANK:], w[:, c0:c0 + GATE_RANK],
         jnp.zeros((D_MODEL, LANES - GATE_RANK), w.dtype)], axis=1).astype(BF16)
    w_out_bf = w_out[0].astype(BF16)
    w_mem_bf = w_mem_kv[0].astype(BF16)
    bwa = jnp.concatenate([b_wa[0], jnp.zeros((LANES - GATE_RANK, D_BK), F32)], axis=0).astype(BF16)
    bba = b_ba[0].reshape(1, D_BK)
    avw = a_vnorm_w[0].reshape(1, D_A)
    onw_p = jnp.tile(b_onorm_w[0], H_B).reshape(1, D_B)
    onw_s = b_onorm_w[0].reshape(1, DV_B)
    wa_p = a_ws[0]
    abt_p = a_bs[0].T
    wa_s = jnp.tile(a_ws[0][:, :TS, :TS], (1, NS, NS))
    abt_s = jnp.tile(a_bs[0][:, :TS], (1, NS)).T

    xp = x_prompt.reshape(batch * seq, D_MODEL)
    xs = x_sample.reshape(nseq * TS, D_MODEL)
    mem = mem_prompt.reshape(batch * N_MEM, D_MODEL)

    memkv = _norm_matmul(mem, mem_norm_w[0], w_mem_bf, bm=batch * N_MEM, bn=D_X,
                         out_dtype=F32, split_out=True, name="mem_kv")
    proj_p = _norm_matmul(xp, norm_w[0], w_cat, bm=1024, bn=640, out_dtype=BF16, name="in_proj_p")
    proj_s = _norm_matmul(xs, norm_w[0], w_cat, bm=1024, bn=640, out_dtype=BF16, name="in_proj_s")

    br_p, st_p = _prompt_mixer(proj_p, memkv, wa_p, abt_p, avw, bwa, bba, onw_p,
                               batch=batch, seq=seq)
    br_s, st_s, cvs = _sample_mixer(
        proj_s, state_gla[0].reshape(nseq, D_BK, DV_B),
        cache_mem_k[0].reshape(nseq, N_MEM, D_X), cache_mem_v[0].reshape(nseq, N_MEM, D_X),
        wa_s, abt_s, avw, bwa, bba, onw_s)

    y_p = _out_proj(br_p, w_out_bf, xp, final_norm_w, bm=512, name="out_proj_p")
    y_s = _out_proj(br_s, w_out_bf, xs, final_norm_w, bm=512, name="out_proj_s")

    return (y_p.reshape(batch, seq, D_MODEL),
            y_s.reshape(nseq, TS, D_MODEL),
            memkv[0].reshape(1, batch, N_MEM, H_X, HD_X),
            memkv[1].reshape(1, batch, N_MEM, H_X, HD_X),
            st_p.reshape(1, batch, H_B, DK_B, DV_B),
            st_s.reshape(1, nseq, H_B, DK_B, DV_B),
            cvs.reshape(1, nseq, TS, D_A))
```

```python
import functools

import jax
import jax.numpy as jnp
from jax import lax
from jax.experimental import pallas as pl
from jax.experimental.pallas import tpu as pltpu

F32 = jnp.float32
BF16 = jnp.bfloat16

D_MODEL = 2048
D_A = 768
H_A = 4
HD_A = 192
CHUNK_A = 128
D_B = 768
H_B = 4
DV_B = 192
DK_B = 96
D_BK = 384
GATE_RANK = 16
GATE_TAU = 16.0
CHUNK_B = 64
D_X = 512
H_X = 4
HD_X = 128
N_MEM = 256
EPS = 1e-6

LANES = 128
C_AU, C_AV, C_AG = 0, 768, 1536
C_BQ, C_BK, C_BV, C_BG = 2304, 2688, 3072, 3840
C_XQ, C_XG, C_BR = 4608, 5120, 5632
D_PROJ = 5760
WIN_START = (0, 128, 384, 512)
WIN = 256
IN_BM = 1024
IN_BN = 768

VMEM_LIMIT = 56 * 1024 * 1024


def _dot(a, b):
    return jnp.dot(a, b, preferred_element_type=F32)


def _dot_nt(a, b):
    return lax.dot_general(a, b, (((1,), (1,)), ((), ())), preferred_element_type=F32)


def _dot_tn(a, b):
    return lax.dot_general(a, b, (((0,), (0,)), ((), ())), preferred_element_type=F32)


def _split(x):
    hi = x.astype(BF16)
    lo = (x - hi.astype(F32)).astype(BF16)
    return hi, lo


def _silu(x):
    return x / (1.0 + jnp.exp(-x))


def _log_sigmoid(x):
    return jnp.minimum(x, 0.0) - jnp.log1p(jnp.exp(-jnp.abs(x)))


def _head_id(idx, width, n):
    h = jnp.zeros_like(idx)
    for i in range(1, n):
        h = h + (idx >= i * width).astype(jnp.int32)
    return h


def _iota(shape, dim):
    return lax.broadcasted_iota(jnp.int32, shape, dim)


def _merge_windows(r):
    rows = r[0].shape[0]
    lo = _iota((rows, LANES), 1) < 64
    tiles = [r[0][:, :LANES], jnp.where(lo, r[0][:, LANES:], r[1][:, :LANES]), r[1][:, LANES:],
             r[2][:, :LANES], jnp.where(lo, r[2][:, LANES:], r[3][:, :LANES]), r[3][:, LANES:]]
    return jnp.concatenate(tiles, axis=1)


def _group_a(proj_ref, wa_ref, abt_ref, avw_ref, mask):
    u = proj_ref[:, C_AU:C_AU + D_A].astype(F32)
    v = proj_ref[:, C_AV:C_AV + D_A].astype(F32)
    g = proj_ref[:, C_AG:C_AG + D_A].astype(F32)
    rows = u.shape[0]
    ms = jnp.mean(v * v, axis=-1, keepdims=True)
    vn = v * lax.rsqrt(ms + EPS) * avw_ref[...]
    vb = vn.astype(BF16)
    r = []
    for gi in range(H_A):
        w = jnp.where(mask, wa_ref[gi], 0.0).astype(BF16)
        r.append(_dot(w, vb[:, WIN_START[gi]:WIN_START[gi] + WIN]))
    mixed = _merge_windows(r)
    col_g = _head_id(_iota((rows, D_A), 1), HD_A, H_A)
    bias = jnp.zeros((rows, D_A), F32)
    for gi in range(H_A):
        bias = jnp.where(col_g == gi, abt_ref[:, gi:gi + 1], bias)
    return u * (mixed + bias) * _silu(g), vn


def _gla_intra_windows(q_in, k_in, vb, mask):
    rows = q_in.shape[0]
    col_h = _head_id(_iota((rows, D_BK), 1), DK_B, H_B)
    kb = k_in.astype(BF16)
    r = []
    for h in range(H_B):
        qh = jnp.where(col_h == h, q_in, 0.0).astype(BF16)
        s = jnp.where(mask, _dot_nt(qh, kb), 0.0).astype(BF16)
        r.append(_dot(s, vb[:, WIN_START[h]:WIN_START[h] + WIN]))
    return r


def _norm_matmul_kernel(x_ref, nw_ref, *refs, n_blocks, transposed, last_cols):
    w_refs, o_ref, h_ref = refs[:-2], refs[-2], refs[-1]
    j = pl.program_id(1)

    @pl.when(j == 0)
    def _():
        x = x_ref[...]
        ms = jnp.mean(x * x, axis=-1, keepdims=True)
        h_ref[...] = (x * lax.rsqrt(ms + EPS) * nw_ref[...]).astype(BF16)

    mm = _dot_nt if transposed else _dot
    start = 0
    for p, w_ref in enumerate(w_refs):
        stop = start + n_blocks[p]
        is_last_piece = p == len(w_refs) - 1
        full_stop = stop - 1 if (is_last_piece and last_cols is not None) else stop

        @pl.when((j >= start) & (j < full_stop))
        def _(w_ref=w_ref):
            o_ref[...] = mm(h_ref[...], w_ref[...].astype(BF16)).astype(o_ref.dtype)

        if full_stop != stop:
            @pl.when(j == stop - 1)
            def _(w_ref=w_ref):
                w = w_ref[:last_cols, :] if transposed else w_ref[:, :last_cols]
                o_ref[:, :last_cols] = mm(h_ref[...], w.astype(BF16)).astype(o_ref.dtype)
        start = stop


def _norm_matmul(x, nw, weights, n_blocks, n_out, *, bm, bn, transposed, out_dtype,
                 split_out=False, last_cols=None, name):
    m, k = x.shape
    assert m % bm == 0
    nj = sum(n_blocks)
    if split_out:
        out_shape = jax.ShapeDtypeStruct((nj, m, bn), out_dtype)
        out_spec = pl.BlockSpec((None, bm, bn), lambda i, j: (j, i, 0))
    else:
        out_shape = jax.ShapeDtypeStruct((m, n_out), out_dtype)
        out_spec = pl.BlockSpec((bm, bn), lambda i, j: (i, j))
    in_specs = [pl.BlockSpec((bm, k), lambda i, j: (i, 0)),
                pl.BlockSpec((1, k), lambda i, j: (0, 0))]
    start = 0
    for nb in n_blocks:
        def blk(j, start=start, nb=nb):
            return jnp.clip(j - start, 0, nb - 1)
        if transposed:
            in_specs.append(pl.BlockSpec((bn, k), lambda i, j, blk=blk: (blk(j), 0)))
        else:
            in_specs.append(pl.BlockSpec((k, bn), lambda i, j, blk=blk: (0, blk(j))))
        start += nb
    return pl.pallas_call(
        functools.partial(_norm_matmul_kernel, n_blocks=tuple(n_blocks), transposed=transposed,
                          last_cols=last_cols),
        grid=(m // bm, nj),
        in_specs=in_specs,
        out_specs=out_spec,
        out_shape=out_shape,
        scratch_shapes=[pltpu.VMEM((bm, k), BF16)],
        compiler_params=pltpu.CompilerParams(
            dimension_semantics=("arbitrary", "arbitrary"), vmem_limit_bytes=VMEM_LIMIT),
        name=name,
    )(x, nw.reshape(1, k), *weights)


def _out_proj_kernel(br_ref, w_ref, x_ref, fw_ref, y_ref):
    acc = _dot(br_ref[...], w_ref[...]) + x_ref[...]
    ms = jnp.mean(acc * acc, axis=-1, keepdims=True)
    y_ref[...] = acc * lax.rsqrt(ms + EPS) * fw_ref[...]


def _out_proj(br, w, x, fw, *, bm, name):
    m, k = br.shape
    n = w.shape[1]
    return pl.pallas_call(
        _out_proj_kernel,
        grid=(m // bm,),
        in_specs=[pl.BlockSpec((bm, k), lambda i: (i, 0)),
                  pl.BlockSpec((k, n), lambda i: (0, 0)),
                  pl.BlockSpec((bm, n), lambda i: (i, 0)),
                  pl.BlockSpec((1, n), lambda i: (0, 0))],
        out_specs=pl.BlockSpec((bm, n), lambda i: (i, 0)),
        out_shape=jax.ShapeDtypeStruct((m, n), F32),
        compiler_params=pltpu.CompilerParams(
            dimension_semantics=("arbitrary",), vmem_limit_bytes=VMEM_LIMIT),
        name=name,
    )(br, w, x, fw.reshape(1, n))


PB = 128


def _prompt_mixer_kernel(proj_ref, mk_ref, mv_ref, wa_ref, abt_ref, avw_ref, bwa_ref, bba_ref,
                         onw_ref, out_ref, st_ref, sbd_ref, bdm_ref, ind_ref):
    b = pl.program_id(0)
    t = pl.program_id(1)
    nt = pl.num_programs(1)

    @pl.when((b == 0) & (t == 0))
    def _():
        rh = _head_id(_iota((D_BK, D_B), 0), DK_B, H_B)
        ch = _head_id(_iota((D_BK, D_B), 1), DV_B, H_B)
        bdm_ref[...] = jnp.where(rh == ch, 1.0, 0.0).astype(F32)
        r2 = _head_id(_iota((D_B, D_B), 0), DV_B, H_B)
        c2 = _head_id(_iota((D_B, D_B), 1), DV_B, H_B)
        ind_ref[...] = jnp.where(r2 == c2, 1.0, 0.0).astype(BF16)

    @pl.when(t == 0)
    def _():
        sbd_ref[...] = jnp.zeros_like(sbd_ref)

    ri = _iota((PB, PB), 0)
    ci = _iota((PB, PB), 1)
    a_br, _ = _group_a(proj_ref, wa_ref, abt_ref, avw_ref, ci <= ri)
    out_ref[:, 0:D_A] = a_br.astype(BF16)

    q = proj_ref[:, C_BQ:C_BQ + D_BK].astype(F32)
    k = proj_ref[:, C_BK:C_BK + D_BK].astype(F32)
    vb = proj_ref[:, C_BV:C_BV + D_B]
    pre = _dot(proj_ref[:, C_BR:C_BR + LANES], bwa_ref[...]) + bba_ref[...]
    log_a = _log_sigmoid(pre) * (1.0 / GATE_TAU)
    r64 = _iota((CHUNK_B, CHUNK_B), 0)
    c64 = _iota((CHUNK_B, CHUNK_B), 1)
    tril = c64 <= r64
    tril_bf = jnp.where(tril, 1.0, 0.0).astype(BF16)
    ones_bf = jnp.ones((CHUNK_B, LANES), BF16)
    o_chunks = []
    for c in range(PB // CHUNK_B):
        sl = slice(c * CHUNK_B, (c + 1) * CHUNK_B)
        hi, lo = _split(log_a[sl])
        cum = _dot(tril_bf, hi) + _dot(tril_bf, lo)
        tot_col = _dot_tn(hi, ones_bf) + _dot_tn(lo, ones_bf)
        tot_row = cum[CHUNK_B - 1:CHUNK_B, :]
        q_in = q[sl] * (DK_B ** -0.5) * jnp.exp(cum)
        k_in = k[sl] * jnp.exp(-cum)
        k_out = k[sl] * jnp.exp(tot_row - cum)
        o_intra = _merge_windows(_gla_intra_windows(q_in, k_in, vb[sl], tril))
        s_old = sbd_ref[...]
        o_inter = _dot(q_in.astype(BF16), s_old.astype(BF16))
        kv = _dot_tn(k_out.astype(BF16), vb[sl])
        dec = jnp.exp(tot_col)
        dec = jnp.concatenate([dec] * (D_B // LANES), axis=1)
        sbd_ref[...] = s_old * dec + jnp.where(bdm_ref[...] > 0.5, kv, 0.0)
        o_chunks.append(o_intra + o_inter)
    o = jnp.concatenate(o_chunks, axis=0)
    hi, lo = _split(o * o)
    ssq = _dot(hi, ind_ref[...]) + _dot(lo, ind_ref[...])
    o_n = o * lax.rsqrt(ssq * (1.0 / DV_B) + EPS) * onw_ref[...]
    bg = proj_ref[:, C_BG:C_BG + D_B].astype(F32)
    out_ref[:, D_A:D_A + D_B] = (o_n * _silu(bg)).astype(BF16)

    @pl.when(t == nt - 1)
    def _():
        for h in range(H_B):
            off = h * DV_B - WIN_START[h]
            blk = sbd_ref[h * DK_B:(h + 1) * DK_B, WIN_START[h]:WIN_START[h] + WIN]
            if off:
                blk = pltpu.roll(blk, WIN - off, 1)
            st_ref[0, h] = blk[:, :DV_B]

    for h in range(H_X):
        hs = slice(h * HD_X, (h + 1) * HD_X)
        qh = proj_ref[:, C_XQ + h * HD_X:C_XQ + (h + 1) * HD_X]
        s = _dot_nt(qh, mk_ref[:, hs].astype(BF16)) * (HD_X ** -0.5)
        e = jnp.exp(s - jnp.max(s, axis=-1, keepdims=True))
        den = jnp.sum(e, axis=-1, keepdims=True)
        ox = _dot(e.astype(BF16), mv_ref[:, hs].astype(BF16)) / den
        xg = proj_ref[:, C_XG + h * HD_X:C_XG + (h + 1) * HD_X].astype(F32)
        out_ref[:, D_A + D_B + h * HD_X:D_A + D_B + (h + 1) * HD_X] = (ox * _silu(xg)).astype(BF16)


def _prompt_mixer(proj, memkv, wa, abt, avw, bwa, bba, onw, *, batch, seq):
    nt = seq // PB
    const = lambda *shape: pl.BlockSpec(shape, lambda b, t: (0,) * len(shape))
    return pl.pallas_call(
        _prompt_mixer_kernel,
        grid=(batch, nt),
        in_specs=[pl.BlockSpec((PB, D_PROJ), lambda b, t: (b * nt + t, 0)),
                  pl.BlockSpec((None, N_MEM, D_X), lambda b, t: (0, b, 0)),
                  pl.BlockSpec((None, N_MEM, D_X), lambda b, t: (1, b, 0)),
                  const(H_A, CHUNK_A, CHUNK_A), const(CHUNK_A, H_A), const(1, D_A),
                  const(LANES, D_BK), const(1, D_BK), const(1, D_B)],
        out_specs=[pl.BlockSpec((PB, D_MODEL), lambda b, t: (b * nt + t, 0)),
                   pl.BlockSpec((1, H_B, DK_B, DV_B), lambda b, t: (b, 0, 0, 0))],
        out_shape=[jax.ShapeDtypeStruct((batch * seq, D_MODEL), BF16),
                   jax.ShapeDtypeStruct((batch, H_B, DK_B, DV_B), F32)],
        scratch_shapes=[pltpu.VMEM((D_BK, D_B), F32),
                        pltpu.VMEM((D_BK, D_B), F32),
                        pltpu.VMEM((D_B, D_B), BF16)],
        compiler_params=pltpu.CompilerParams(
            dimension_semantics=("arbitrary", "arbitrary"), vmem_limit_bytes=VMEM_LIMIT),
        name="prompt_mixer",
    )(proj, memkv, memkv, wa, abt, avw, bwa, bba, onw)


NS = 8
TS = 8
SB = NS * TS
SBP = 128


def _sample_mixer_kernel(proj_ref, st_ref, ck_ref, cv_ref, wa_ref, abt_ref, avw_ref, bwa_ref,
                         bba_ref, onw_ref, out_ref, stn_ref, cvs_ref,
                         qin_ref, xq_ref, kot_ref, lath_ref, latl_ref, vhm_ref, ghm_ref, ohm_ref,
                         ox_ref):
    ri = _iota((SB, SB), 0)
    ci = _iota((SB, SB), 1)
    same_seq = jnp.right_shift(ri, 3) == jnp.right_shift(ci, 3)
    causal = same_seq & (ci <= ri)

    a_br, vn = _group_a(proj_ref, wa_ref, abt_ref, avw_ref, causal)
    out_ref[:, 0:D_A] = a_br.astype(BF16)
    cvs_ref[...] = vn

    q = proj_ref[:, C_BQ:C_BQ + D_BK].astype(F32)
    k = proj_ref[:, C_BK:C_BK + D_BK].astype(F32)
    vb = proj_ref[:, C_BV:C_BV + D_B]
    bgb = proj_ref[:, C_BG:C_BG + D_B]
    pre = _dot(proj_ref[:, C_BR:C_BR + LANES], bwa_ref[...]) + bba_ref[...]
    log_a = _log_sigmoid(pre) * (1.0 / GATE_TAU)
    causal_bf = jnp.where(causal, 1.0, 0.0).astype(BF16)
    seq_bf = jnp.where(same_seq, 1.0, 0.0).astype(BF16)
    hi, lo = _split(log_a)
    cum = _dot(causal_bf, hi) + _dot(causal_bf, lo)
    tot = _dot(seq_bf, hi) + _dot(seq_bf, lo)
    q_in = q * (DK_B ** -0.5) * jnp.exp(cum)
    k_in = k * jnp.exp(-cum)
    k_out = k * jnp.exp(tot - cum)
    qin_ref[...] = q_in
    zpad = jnp.zeros((SBP - SB, D_BK), F32)
    kot_ref[...] = jnp.concatenate([k_out, zpad], axis=0).T.astype(BF16)
    lat_hi, lat_lo = _split(jnp.concatenate([log_a, zpad], axis=0).T)
    lath_ref[...] = lat_hi
    latl_ref[...] = lat_lo
    xq_ref[...] = proj_ref[:, C_XQ:C_XQ + D_X].astype(F32)

    wins = _gla_intra_windows(q_in, k_in, vb, causal)
    sel_r = _iota((D_B, DV_B), 0)
    sel_c = _iota((D_B, DV_B), 1)
    vhm_ref[...] = jnp.zeros_like(vhm_ref)
    for h in range(H_B):
        off = h * DV_B - WIN_START[h]
        w = wins[h]
        if off:
            w = pltpu.roll(w, WIN - off, 1)
        ohm_ref[h] = w[:, :DV_B]
        sel = jnp.where(sel_r == sel_c + h * DV_B, 1.0, 0.0).astype(BF16)
        vhm_ref[h, 0:SB, :] = _dot(vb, sel)
        ghm_ref[h] = _dot(bgb, sel)

    mask_x = (jnp.right_shift(_iota((H_X * TS, N_MEM * H_X), 0), 3)
              == jnp.bitwise_and(_iota((H_X * TS, N_MEM * H_X), 1), H_X - 1))
    mask_b = jnp.right_shift(_iota((H_B * TS, D_BK), 0), 3) == _head_id(_iota((H_B * TS, D_BK), 1), DK_B, H_B)
    row_seq = jnp.right_shift(_iota((SBP, DV_B), 0), 3)

    def per_seq(s, carry):
        r0 = pl.multiple_of(s * TS, TS)
        q8 = xq_ref[pl.ds(r0, TS), :]
        q32 = jnp.concatenate([q8[:, h * HD_X:(h + 1) * HD_X] for h in range(H_X)], axis=0)
        sc = _dot_nt(q32.astype(BF16), ck_ref[s].astype(BF16)) * (HD_X ** -0.5)
        sc = jnp.where(mask_x, sc, -1e30)
        e = jnp.exp(sc - jnp.max(sc, axis=-1, keepdims=True))
        den = jnp.sum(e, axis=-1, keepdims=True)
        o = _dot(e.astype(BF16), cv_ref[s].astype(BF16)) / den
        ox_ref[pl.ds(r0, TS), :] = jnp.concatenate(
            [o[h * TS:(h + 1) * TS] for h in range(H_X)], axis=1)
        qi8 = qin_ref[pl.ds(r0, TS), :]
        qbd2 = jnp.where(mask_b, jnp.concatenate([qi8] * H_B, axis=0), 0.0).astype(BF16)
        s0 = jnp.concatenate([st_ref[s, h] for h in range(H_B)], axis=0)
        o_inter = _dot(qbd2, s0.astype(BF16))
        in_seq = row_seq == s
        for h in range(H_B):
            ohm_ref[h, pl.ds(r0, TS), :] += o_inter[h * TS:(h + 1) * TS]
        row_sel = jnp.where(in_seq, 1.0, 0.0).astype(BF16)
        dec = jnp.exp(_dot(lath_ref[...], row_sel) + _dot(latl_ref[...], row_sel))
        kot = kot_ref[...]
        kv = jnp.concatenate(
            [_dot(kot[h * DK_B:(h + 1) * DK_B],
                  jnp.where(in_seq, vhm_ref[h], 0.0).astype(BF16)) for h in range(H_B)], axis=0)
        s_new = s0 * dec + kv
        for h in range(H_B):
            stn_ref[s, h] = s_new[h * DK_B:(h + 1) * DK_B]
        return carry

    lax.fori_loop(0, NS, per_seq, 0)

    selt_r = _iota((DV_B, D_B), 0)
    selt_c = _iota((DV_B, D_B), 1)
    b_br = jnp.zeros((SB, D_B), F32)
    for h in range(H_B):
        o_h = ohm_ref[h]
        ms = jnp.mean(o_h * o_h, axis=-1, keepdims=True)
        ob = (o_h * lax.rsqrt(ms + EPS) * onw_ref[...] * _silu(ghm_ref[h])).astype(BF16)
        selt = jnp.where(selt_c == selt_r + h * DV_B, 1.0, 0.0).astype(BF16)
        b_br = b_br + _dot(ob, selt)
    out_ref[:, D_A:D_A + D_B] = b_br.astype(BF16)

    xg = proj_ref[:, C_XG:C_XG + D_X].astype(F32)
    out_ref[:, D_A + D_B:D_MODEL] = (ox_ref[...] * _silu(xg)).astype(BF16)


def _sample_mixer(proj, state, ck, cv, wa, abt, avw, bwa, bba, onw):
    nseq = state.shape[1]
    const = lambda *shape: pl.BlockSpec(shape, lambda i: (0,) * len(shape))
    return pl.pallas_call(
        _sample_mixer_kernel,
        grid=(nseq // NS,),
        in_specs=[pl.BlockSpec((SB, D_PROJ), lambda i: (i, 0)),
                  pl.BlockSpec((None, NS, H_B, DK_B, DV_B), lambda i: (0, i, 0, 0, 0)),
                  pl.BlockSpec((NS, N_MEM * H_X, HD_X), lambda i: (i, 0, 0)),
                  pl.BlockSpec((NS, N_MEM * H_X, HD_X), lambda i: (i, 0, 0)),
                  const(H_A, SB, SB), const(SB, H_A), const(1, D_A),
                  const(LANES, D_BK), const(1, D_BK), const(1, DV_B)],
        out_specs=[pl.BlockSpec((SB, D_MODEL), lambda i: (i, 0)),
                   pl.BlockSpec((None, NS, H_B, DK_B, DV_B), lambda i: (0, i, 0, 0, 0)),
                   pl.BlockSpec((SB, D_A), lambda i: (i, 0))],
        out_shape=[jax.ShapeDtypeStruct((nseq * TS, D_MODEL), BF16),
                   jax.ShapeDtypeStruct((1, nseq, H_B, DK_B, DV_B), F32),
                   jax.ShapeDtypeStruct((nseq * TS, D_A), F32)],
        scratch_shapes=[pltpu.VMEM((SB, D_BK), F32),
                        pltpu.VMEM((SB, D_X), F32),
                        pltpu.VMEM((D_BK, SBP), BF16),
                        pltpu.VMEM((D_BK, SBP), BF16),
                        pltpu.VMEM((D_BK, SBP), BF16),
                        pltpu.VMEM((H_B, SBP, DV_B), F32),
                        pltpu.VMEM((H_B, SB, DV_B), F32),
                        pltpu.VMEM((H_B, SB, DV_B), F32),
                        pltpu.VMEM((SB, D_X), F32)],
        compiler_params=pltpu.CompilerParams(
            dimension_semantics=("arbitrary",), vmem_limit_bytes=VMEM_LIMIT),
        name="sample_mixer",
    )(proj, state, ck, cv, wa, abt, avw, bwa, bba, onw)


def kernel(x_prompt, x_sample, mem_prompt, state_gla, cache_mem_k, cache_mem_v, norm_w, w_in,
           a_vnorm_w, a_ws, a_bs, b_wa, b_ba, b_onorm_w, mem_norm_w, w_mem_kv, w_out, final_norm_w):
    batch, seq, _ = x_prompt.shape
    nseq, tdec, _ = x_sample.shape
    depth = w_in.shape[0]
    assert depth == 1 and tdec == TS and seq % PB == 0 and nseq % NS == 0

    w_t = jnp.transpose(w_in[0])
    n_tail = -(-(D_PROJ - C_BG) // IN_BN) * IN_BN
    w_tail_t = jnp.concatenate(
        [w_t[C_BG + GATE_RANK:], w_t[C_BG:C_BG + GATE_RANK],
         jnp.zeros((n_tail - (w_t.shape[0] - C_BG), D_MODEL), w_t.dtype)], axis=0)
    last_cols = -(-(C_BR + GATE_RANK - (C_BG + n_tail - IN_BN)) // WIN) * WIN
    in_blocks = (C_BG // IN_BN, n_tail // IN_BN)
    w_out_bf = w_out[0].astype(BF16)
    bwa = jnp.concatenate([b_wa[0], jnp.zeros((LANES - GATE_RANK, D_BK), F32)], axis=0).astype(BF16)
    bba = b_ba[0].reshape(1, D_BK)
    avw = a_vnorm_w[0].reshape(1, D_A)
    onw_p = jnp.tile(b_onorm_w[0], H_B).reshape(1, D_B)
    onw_s = b_onorm_w[0].reshape(1, DV_B)
    wa_p = a_ws[0]
    abt_p = a_bs[0].T
    wa_s = jnp.tile(a_ws[0][:, :TS, :TS], (1, NS, NS))
    abt_s = jnp.tile(a_bs[0][:, :TS], (1, NS)).T

    xp = x_prompt.reshape(batch * seq, D_MODEL)
    xs = x_sample.reshape(nseq * TS, D_MODEL)
    mem = mem_prompt.reshape(batch * N_MEM, D_MODEL)

    memkv = _norm_matmul(mem, mem_norm_w[0], [w_mem_kv[0]], (2,), None, bm=batch * N_MEM, bn=D_X,
                         transposed=False, out_dtype=F32, split_out=True, name="mem_kv")
    proj_p = _norm_matmul(xp, norm_w[0], [w_t, w_tail_t], in_blocks, D_PROJ, bm=IN_BM, bn=IN_BN,
                          transposed=True, out_dtype=BF16, last_cols=last_cols, name="in_proj_p")
    proj_s = _norm_matmul(xs, norm_w[0], [w_t, w_tail_t], in_blocks, D_PROJ, bm=IN_BM, bn=IN_BN,
                          transposed=True, out_dtype=BF16, last_cols=last_cols, name="in_proj_s")

    br_p, st_p = _prompt_mixer(proj_p, memkv, wa_p, abt_p, avw, bwa, bba, onw_p,
                               batch=batch, seq=seq)
    br_s, st_s, cvs = _sample_mixer(
        proj_s, state_gla,
        cache_mem_k.reshape(nseq, N_MEM * H_X, HD_X), cache_mem_v.reshape(nseq, N_MEM * H_X, HD_X),
        wa_s, abt_s, avw, bwa, bba, onw_s)

    y_p = _out_proj(br_p, w_out_bf, xp, final_norm_w, bm=512, name="out_proj_p")
    y_s = _out_proj(br_s, w_out_bf, xs, final_norm_w, bm=512, name="out_proj_s")

    return (y_p.reshape(batch, seq, D_MODEL),
            y_s.reshape(nseq, TS, D_MODEL),
            memkv[0].reshape(1, batch, N_MEM, H_X, HD_X),
            memkv[1].reshape(1, batch, N_MEM, H_X, HD_X),
            st_p.reshape(1, batch, H_B, DK_B, DV_B),
            st_s,
            cvs.reshape(1, nseq, TS, D_A))
```

```python
import functools

import jax
import jax.numpy as jnp
from jax import lax
from jax.experimental import pallas as pl
from jax.experimental.pallas import tpu as pltpu

F32 = jnp.float32
BF16 = jnp.bfloat16

D_MODEL = 2048
D_A = 768
H_A = 4
HD_A = 192
CHUNK_A = 128
D_B = 768
H_B = 4
DV_B = 192
DK_B = 96
D_BK = 384
GATE_RANK = 16
GATE_TAU = 16.0
CHUNK_B = 64
D_X = 512
H_X = 4
HD_X = 128
N_MEM = 256
EPS = 1e-6

LANES = 128
MXU_COLS = 256
C_AU, C_AV, C_AG = 0, 768, 1536
C_BQ, C_BK, C_BV, C_BG = 2304, 2688, 3072, 3840
C_XQ, C_XG, C_BR = 4608, 5120, 5632
D_PROJ = 5760
D_PROJ_PAD = 5888
WIN_START = (0, 128, 384, 512)
WIN = 256
IN_BN = 768
OUT_BN = 512

VMEM_LIMIT = 60 * 1024 * 1024


def _dot(a, b):
    return jnp.dot(a, b, preferred_element_type=F32)


def _dot_nt(a, b):
    return lax.dot_general(a, b, (((1,), (1,)), ((), ())), preferred_element_type=F32)


def _dot_tn(a, b):
    return lax.dot_general(a, b, (((0,), (0,)), ((), ())), preferred_element_type=F32)


def _split(x):
    hi = x.astype(BF16)
    lo = (x - hi.astype(F32)).astype(BF16)
    return hi, lo


def _silu(x):
    return x / (1.0 + jnp.exp(-x))


def _log_sigmoid(x):
    return jnp.minimum(x, 0.0) - jnp.log1p(jnp.exp(-jnp.abs(x)))


def _head_id(idx, width, n):
    h = jnp.zeros_like(idx)
    for i in range(1, n):
        h = h + (idx >= i * width).astype(jnp.int32)
    return h


def _iota(shape, dim):
    return lax.broadcasted_iota(jnp.int32, shape, dim)


def _merge_windows(r):
    rows = r[0].shape[0]
    lo = _iota((rows, LANES), 1) < 64
    tiles = [r[0][:, :LANES], jnp.where(lo, r[0][:, LANES:], r[1][:, :LANES]), r[1][:, LANES:],
             r[2][:, :LANES], jnp.where(lo, r[2][:, LANES:], r[3][:, :LANES]), r[3][:, LANES:]]
    return jnp.concatenate(tiles, axis=1)


def _group_a(proj_ref, wa_ref, abt_ref, avw_ref, mask):
    u = proj_ref[:, C_AU:C_AU + D_A].astype(F32)
    v = proj_ref[:, C_AV:C_AV + D_A].astype(F32)
    g = proj_ref[:, C_AG:C_AG + D_A].astype(F32)
    rows = u.shape[0]
    ms = jnp.mean(v * v, axis=-1, keepdims=True)
    vn = v * lax.rsqrt(ms + EPS) * avw_ref[...]
    vb = vn.astype(BF16)
    r = []
    for gi in range(H_A):
        w = jnp.where(mask, wa_ref[gi], 0.0).astype(BF16)
        r.append(_dot(w, vb[:, WIN_START[gi]:WIN_START[gi] + WIN]))
    mixed = _merge_windows(r)
    col_g = _head_id(_iota((rows, D_A), 1), HD_A, H_A)
    bias = jnp.zeros((rows, D_A), F32)
    for gi in range(H_A):
        bias = jnp.where(col_g == gi, abt_ref[:, gi:gi + 1], bias)
    return u * (mixed + bias) * _silu(g), vn


def _gla_intra_windows(q_in, k_in, vb, mask):
    rows = q_in.shape[0]
    col_h = _head_id(_iota((rows, D_BK), 1), DK_B, H_B)
    kb = k_in.astype(BF16)
    r = []
    for h in range(H_B):
        qh = jnp.where(col_h == h, q_in, 0.0).astype(BF16)
        s = jnp.where(mask, _dot_nt(qh, kb), 0.0).astype(BF16)
        r.append(_dot(s, vb[:, WIN_START[h]:WIN_START[h] + WIN]))
    return r


def _norm_matmul_kernel(x_ref, nw_ref, w_ref, o_ref, h_ref, *, transposed, last_cols):
    j = pl.program_id(1)
    nj = pl.num_programs(1)

    @pl.when(j == 0)
    def _():
        x = x_ref[...]
        ms = jnp.mean(x * x, axis=-1, keepdims=True)
        h_ref[...] = (x * lax.rsqrt(ms + EPS) * nw_ref[...]).astype(BF16)

    mm = _dot_nt if transposed else _dot
    if last_cols is None:
        o_ref[...] = mm(h_ref[...], w_ref[...].astype(BF16)).astype(o_ref.dtype)
    else:
        @pl.when(j < nj - 1)
        def _():
            o_ref[...] = mm(h_ref[...], w_ref[...].astype(BF16)).astype(o_ref.dtype)

        @pl.when(j == nj - 1)
        def _():
            w = w_ref[:last_cols, :] if transposed else w_ref[:, :last_cols]
            o_ref[:, :last_cols] = mm(h_ref[...], w.astype(BF16)).astype(o_ref.dtype)


def _norm_matmul(x, nw, w, n_out, *, bm, bn, transposed, out_dtype, split_out=False,
                 last_cols=None, name):
    m, k = x.shape
    n = w.shape[0] if transposed else w.shape[1]
    nj = -(-n // bn)
    assert m % bm == 0
    if split_out:
        out_shape = jax.ShapeDtypeStruct((nj, m, bn), out_dtype)
        out_spec = pl.BlockSpec((None, bm, bn), lambda i, j: (j, i, 0))
    else:
        out_shape = jax.ShapeDtypeStruct((m, n_out), out_dtype)
        out_spec = pl.BlockSpec((bm, bn), lambda i, j: (i, j))
    if transposed:
        w_spec = pl.BlockSpec((bn, k), lambda i, j: (j, 0))
    else:
        w_spec = pl.BlockSpec((k, bn), lambda i, j: (0, j))
    return pl.pallas_call(
        functools.partial(_norm_matmul_kernel, transposed=transposed, last_cols=last_cols),
        grid=(m // bm, nj),
        in_specs=[pl.BlockSpec((bm, k), lambda i, j: (i, 0)),
                  pl.BlockSpec((1, k), lambda i, j: (0, 0)),
                  w_spec],
        out_specs=out_spec,
        out_shape=out_shape,
        scratch_shapes=[pltpu.VMEM((bm, k), BF16)],
        compiler_params=pltpu.CompilerParams(
            dimension_semantics=("arbitrary", "arbitrary"), vmem_limit_bytes=VMEM_LIMIT),
        name=name,
    )(x, nw.reshape(1, k), w)


def _out_proj_kernel(br_ref, w_ref, x_ref, fw_ref, y_ref):
    acc = _dot(br_ref[...], w_ref[...]) + x_ref[...]
    ms = jnp.mean(acc * acc, axis=-1, keepdims=True)
    y_ref[...] = acc * lax.rsqrt(ms + EPS) * fw_ref[...]


def _out_proj(br, w, x, fw, *, bm, name):
    m, k = br.shape
    n = w.shape[1]
    return pl.pallas_call(
        _out_proj_kernel,
        grid=(m // bm,),
        in_specs=[pl.BlockSpec((bm, k), lambda i: (i, 0)),
                  pl.BlockSpec((k, n), lambda i: (0, 0)),
                  pl.BlockSpec((bm, n), lambda i: (i, 0)),
                  pl.BlockSpec((1, n), lambda i: (0, 0))],
        out_specs=pl.BlockSpec((bm, n), lambda i: (i, 0)),
        out_shape=jax.ShapeDtypeStruct((m, n), F32),
        compiler_params=pltpu.CompilerParams(
            dimension_semantics=("arbitrary",), vmem_limit_bytes=VMEM_LIMIT),
        name=name,
    )(br, w, x, fw.reshape(1, n))


PB = 128
PT = 256


def _prompt_sub_block(proj_ref, out_ref, mk_ref, mv_ref, wa_ref, abt_ref, avw_ref, bwa_ref, bba_ref,
                      onw_ref, sbd_ref, bdm_ref, ind_ref):
    ri = _iota((PB, PB), 0)
    ci = _iota((PB, PB), 1)
    a_br, _ = _group_a(proj_ref, wa_ref, abt_ref, avw_ref, ci <= ri)
    out_ref[:, 0:D_A] = a_br.astype(BF16)
    yield

    q = proj_ref[:, C_BQ:C_BQ + D_BK].astype(F32)
    k = proj_ref[:, C_BK:C_BK + D_BK].astype(F32)
    vb = proj_ref[:, C_BV:C_BV + D_B]
    pre = _dot(proj_ref[:, C_BR:C_BR + LANES], bwa_ref[...]) + bba_ref[...]
    log_a = _log_sigmoid(pre) * (1.0 / GATE_TAU)
    r64 = _iota((CHUNK_B, CHUNK_B), 0)
    c64 = _iota((CHUNK_B, CHUNK_B), 1)
    tril = c64 <= r64
    tril_bf = jnp.where(tril, 1.0, 0.0).astype(BF16)
    ones_bf = jnp.ones((CHUNK_B, LANES), BF16)
    o_chunks = []
    for c in range(PB // CHUNK_B):
        sl = slice(c * CHUNK_B, (c + 1) * CHUNK_B)
        hi, lo = _split(log_a[sl])
        cum = _dot(tril_bf, hi) + _dot(tril_bf, lo)
        tot_col = _dot_tn(hi, ones_bf) + _dot_tn(lo, ones_bf)
        tot_row = cum[CHUNK_B - 1:CHUNK_B, :]
        q_in = q[sl] * (DK_B ** -0.5) * jnp.exp(cum)
        k_in = k[sl] * jnp.exp(-cum)
        k_out = k[sl] * jnp.exp(tot_row - cum)
        o_intra = _merge_windows(_gla_intra_windows(q_in, k_in, vb[sl], tril))
        s_old = sbd_ref[...]
        o_inter = _dot(q_in.astype(BF16), s_old.astype(BF16))
        kv = _dot_tn(k_out.astype(BF16), vb[sl])
        dec = jnp.exp(tot_col)
        dec = jnp.concatenate([dec] * (D_B // LANES), axis=1)
        sbd_ref[...] = s_old * dec + jnp.where(bdm_ref[...] > 0.5, kv, 0.0)
        o_chunks.append(o_intra + o_inter)
        yield
    o = jnp.concatenate(o_chunks, axis=0)
    hi, lo = _split(o * o)
    ssq = _dot(hi, ind_ref[...]) + _dot(lo, ind_ref[...])
    o_n = o * lax.rsqrt(ssq * (1.0 / DV_B) + EPS) * onw_ref[...]
    bg = proj_ref[:, C_BG:C_BG + D_B].astype(F32)
    out_ref[:, D_A:D_A + D_B] = (o_n * _silu(bg)).astype(BF16)
    yield

    for h in range(H_X):
        hs = slice(h * HD_X, (h + 1) * HD_X)
        qh = proj_ref[:, C_XQ + h * HD_X:C_XQ + (h + 1) * HD_X]
        s = _dot_nt(qh, mk_ref[:, hs].astype(BF16)) * (HD_X ** -0.5)
        e = jnp.exp(s - jnp.max(s, axis=-1, keepdims=True))
        den = jnp.sum(e, axis=-1, keepdims=True)
        ox = _dot(e.astype(BF16), mv_ref[:, hs].astype(BF16)) / den
        xg = proj_ref[:, C_XG + h * HD_X:C_XG + (h + 1) * HD_X].astype(F32)
        out_ref[:, D_A + D_B + h * HD_X:D_A + D_B + (h + 1) * HD_X] = (ox * _silu(xg)).astype(BF16)
        if h % 2 == 1:
            yield


def _prompt_layer_kernel(xn_ref, wt_ref, wo_ref, nw_ref, fw_ref, mk_ref, mv_ref, wa_ref, abt_ref,
                         avw_ref, bwa_ref, bba_ref, onw_ref, y_ref, st_ref,
                         pa_ref, pb_ref, xk_ref, br_ref, h_ref, sbd_ref, bdm_ref, ind_ref, *, nt):
    s = pl.program_id(0)
    cur = jnp.maximum(s - 1, 0)
    t = lax.rem(cur, nt)

    @pl.when(s == 0)
    def _():
        rh = _head_id(_iota((D_BK, D_B), 0), DK_B, H_B)
        ch = _head_id(_iota((D_BK, D_B), 1), DV_B, H_B)
        bdm_ref[...] = jnp.where(rh == ch, 1.0, 0.0).astype(F32)
        r2 = _head_id(_iota((D_B, D_B), 0), DV_B, H_B)
        c2 = _head_id(_iota((D_B, D_B), 1), DV_B, H_B)
        ind_ref[...] = jnp.where(r2 == c2, 1.0, 0.0).astype(BF16)
        pb_ref[...] = jnp.zeros_like(pb_ref)
        xk_ref[...] = jnp.zeros_like(xk_ref)

    @pl.when(t == 0)
    def _():
        sbd_ref[...] = jnp.zeros_like(sbd_ref)

    def in_proj_stages(pn_ref):
        x = xn_ref[...]
        ms = jnp.mean(x * x, axis=-1, keepdims=True)
        h_ref[...] = (x * lax.rsqrt(ms + EPS) * nw_ref[...]).astype(BF16)
        yield
        for c0 in range(0, D_PROJ_PAD, IN_BN):
            n = min(IN_BN, D_PROJ_PAD - c0)
            pn_ref[:, c0:c0 + n] = _dot_nt(h_ref[...], wt_ref[c0:c0 + n, :]).astype(BF16)
            yield

    def out_proj_stages(sb):
        rows = pl.ds(sb * PB, PB)
        ssq = jnp.zeros((PB, 1), F32)
        for c0 in range(0, D_MODEL, OUT_BN):
            cols = pl.ds(c0, OUT_BN)
            acc = _dot(br_ref[rows, :], wo_ref[:, cols]) + xk_ref[rows, cols]
            y_ref[rows, cols] = acc
            ssq = ssq + jnp.sum(acc * acc, axis=-1, keepdims=True)
            yield
        y_ref[rows, :] = y_ref[rows, :] * lax.rsqrt(ssq * (1.0 / D_MODEL) + EPS) * fw_ref[...]
        yield

    def body(pn_ref, pc_ref):
        streams = {"P": in_proj_stages(pn_ref)}
        for sb in range(PT // PB):
            rows = pl.ds(sb * PB, PB)
            streams["M%d" % sb] = _prompt_sub_block(
                pc_ref.at[rows, :], br_ref.at[rows, :], mk_ref, mv_ref, wa_ref, abt_ref, avw_ref,
                bwa_ref, bba_ref, onw_ref, sbd_ref, bdm_ref, ind_ref)
            streams["O%d" % sb] = out_proj_stages(sb)
        order = ("P P M0 P M0 P M0 P M0 M0 M0 "
                 "P M1 O0 P M1 O0 P M1 O0 P M1 O0 M1 M1 O0 "
                 "O1 O1 O1 O1 O1").split()
        for name in order:
            next(streams[name])
        for name, gen in streams.items():
            assert next(gen, "done") == "done", name
        xk_ref[...] = xn_ref[...]

    @pl.when(lax.rem(s, 2) == 0)
    def _():
        body(pa_ref, pb_ref)

    @pl.when(lax.rem(s, 2) == 1)
    def _():
        body(pb_ref, pa_ref)

    @pl.when((t == nt - 1) & (s > 0))
    def _():
        for h in range(H_B):
            off = h * DV_B - WIN_START[h]
            blk = sbd_ref[h * DK_B:(h + 1) * DK_B, WIN_START[h]:WIN_START[h] + WIN]
            if off:
                blk = pltpu.roll(blk, WIN - off, 1)
            st_ref[0, h] = blk[:, :DV_B]


def _prompt_layer(xp, w_all_t, w_out_bf, nw, fw, memkv, wa, abt, avw, bwa, bba, onw, *, batch, seq):
    nt = seq // PT
    nblk = batch * nt
    cur = lambda s: jnp.maximum(s - 1, 0)
    const = lambda *shape: pl.BlockSpec(shape, lambda s: (0,) * len(shape))
    resident = lambda *shape: pl.BlockSpec(shape, lambda s: (0,) * len(shape),
                                           pipeline_mode=pl.Buffered(1))
    return pl.pallas_call(
        functools.partial(_prompt_layer_kernel, nt=nt),
        grid=(nblk + 1,),
        in_specs=[pl.BlockSpec((PT, D_MODEL), lambda s: (jnp.minimum(s, nblk - 1), 0)),
                  resident(D_PROJ_PAD, D_MODEL), resident(D_MODEL, D_MODEL),
                  const(1, D_MODEL), const(1, D_MODEL),
                  pl.BlockSpec((None, N_MEM, D_X), lambda s: (0, cur(s) // nt, 0)),
                  pl.BlockSpec((None, N_MEM, D_X), lambda s: (1, cur(s) // nt, 0)),
                  const(H_A, CHUNK_A, CHUNK_A), const(CHUNK_A, H_A), const(1, D_A),
                  const(LANES, D_BK), const(1, D_BK), const(1, D_B)],
        out_specs=[pl.BlockSpec((PT, D_MODEL), lambda s: (cur(s), 0)),
                   pl.BlockSpec((1, H_B, DK_B, DV_B), lambda s: (cur(s) // nt, 0, 0, 0))],
        out_shape=[jax.ShapeDtypeStruct((batch * seq, D_MODEL), F32),
                   jax.ShapeDtypeStruct((batch, H_B, DK_B, DV_B), F32)],
        scratch_shapes=[pltpu.VMEM((PT, D_PROJ_PAD), BF16),
                        pltpu.VMEM((PT, D_PROJ_PAD), BF16),
                        pltpu.VMEM((PT, D_MODEL), F32),
                        pltpu.VMEM((PT, D_MODEL), BF16),
                        pltpu.VMEM((PT, D_MODEL), BF16),
                        pltpu.VMEM((D_BK, D_B), F32),
                        pltpu.VMEM((D_BK, D_B), F32),
                        pltpu.VMEM((D_B, D_B), BF16)],
        compiler_params=pltpu.CompilerParams(
            dimension_semantics=("arbitrary",), vmem_limit_bytes=VMEM_LIMIT),
        name="prompt_layer",
    )(xp, w_all_t, w_out_bf, nw.reshape(1, D_MODEL), fw.reshape(1, D_MODEL), memkv, memkv,
      wa, abt, avw, bwa, bba, onw)


NS = 8
TS = 8
SB = NS * TS
SBP = 128


def _sample_mixer_kernel(proj_ref, st_ref, ck_ref, cv_ref, wa_ref, abt_ref, avw_ref, bwa_ref,
                         bba_ref, onw_ref, out_ref, stn_ref, cvs_ref,
                         qin_ref, xq_ref, kot_ref, lath_ref, latl_ref, vhm_ref, ghm_ref, ohm_ref,
                         ox_ref):
    ri = _iota((SB, SB), 0)
    ci = _iota((SB, SB), 1)
    same_seq = jnp.right_shift(ri, 3) == jnp.right_shift(ci, 3)
    causal = same_seq & (ci <= ri)

    a_br, vn = _group_a(proj_ref, wa_ref, abt_ref, avw_ref, causal)
    out_ref[:, 0:D_A] = a_br.astype(BF16)
    cvs_ref[...] = vn

    q = proj_ref[:, C_BQ:C_BQ + D_BK].astype(F32)
    k = proj_ref[:, C_BK:C_BK + D_BK].astype(F32)
    vb = proj_ref[:, C_BV:C_BV + D_B]
    bgb = proj_ref[:, C_BG:C_BG + D_B]
    pre = _dot(proj_ref[:, C_BR:C_BR + LANES], bwa_ref[...]) + bba_ref[...]
    log_a = _log_sigmoid(pre) * (1.0 / GATE_TAU)
    causal_bf = jnp.where(causal, 1.0, 0.0).astype(BF16)
    seq_bf = jnp.where(same_seq, 1.0, 0.0).astype(BF16)
    hi, lo = _split(log_a)
    cum = _dot(causal_bf, hi) + _dot(causal_bf, lo)
    tot = _dot(seq_bf, hi) + _dot(seq_bf, lo)
    q_in = q * (DK_B ** -0.5) * jnp.exp(cum)
    k_in = k * jnp.exp(-cum)
    k_out = k * jnp.exp(tot - cum)
    qin_ref[...] = q_in
    zpad = jnp.zeros((SBP - SB, D_BK), F32)
    kot_ref[...] = jnp.concatenate([k_out, zpad], axis=0).T.astype(BF16)
    lat_hi, lat_lo = _split(jnp.concatenate([log_a, zpad], axis=0).T)
    lath_ref[...] = lat_hi
    latl_ref[...] = lat_lo
    xq_ref[...] = proj_ref[:, C_XQ:C_XQ + D_X].astype(F32)

    wins = _gla_intra_windows(q_in, k_in, vb, causal)
    sel_r = _iota((D_B, DV_B), 0)
    sel_c = _iota((D_B, DV_B), 1)
    vhm_ref[...] = jnp.zeros_like(vhm_ref)
    for h in range(H_B):
        off = h * DV_B - WIN_START[h]
        w = wins[h]
        if off:
            w = pltpu.roll(w, WIN - off, 1)
        ohm_ref[h] = w[:, :DV_B]
        sel = jnp.where(sel_r == sel_c + h * DV_B, 1.0, 0.0).astype(BF16)
        vhm_ref[h, 0:SB, :] = _dot(vb, sel)
        ghm_ref[h] = _dot(bgb, sel)

    mask_x = (jnp.right_shift(_iota((H_X * TS, N_MEM * H_X), 0), 3)
              == jnp.bitwise_and(_iota((H_X * TS, N_MEM * H_X), 1), H_X - 1))
    mask_b = jnp.right_shift(_iota((H_B * TS, D_BK), 0), 3) == _head_id(_iota((H_B * TS, D_BK), 1), DK_B, H_B)
    row_seq = jnp.right_shift(_iota((SBP, DV_B), 0), 3)

    def per_seq(s, carry):
        r0 = pl.multiple_of(s * TS, TS)
        q8 = xq_ref[pl.ds(r0, TS), :]
        q32 = jnp.concatenate([q8[:, h * HD_X:(h + 1) * HD_X] for h in range(H_X)], axis=0)
        sc = _dot_nt(q32.astype(BF16), ck_ref[s].astype(BF16)) * (HD_X ** -0.5)
        sc = jnp.where(mask_x, sc, -1e30)
        e = jnp.exp(sc - jnp.max(sc, axis=-1, keepdims=True))
        den = jnp.sum(e, axis=-1, keepdims=True)
        o = _dot(e.astype(BF16), cv_ref[s].astype(BF16)) / den
        ox_ref[pl.ds(r0, TS), :] = jnp.concatenate(
            [o[h * TS:(h + 1) * TS] for h in range(H_X)], axis=1)
        qi8 = qin_ref[pl.ds(r0, TS), :]
        qbd2 = jnp.where(mask_b, jnp.concatenate([qi8] * H_B, axis=0), 0.0).astype(BF16)
        s0 = jnp.concatenate([st_ref[s, h] for h in range(H_B)], axis=0)
        o_inter = _dot(qbd2, s0.astype(BF16))
        in_seq = row_seq == s
        for h in range(H_B):
            ohm_ref[h, pl.ds(r0, TS), :] += o_inter[h * TS:(h + 1) * TS]
        row_sel = jnp.where(in_seq, 1.0, 0.0).astype(BF16)
        dec = jnp.exp(_dot(lath_ref[...], row_sel) + _dot(latl_ref[...], row_sel))
        kot = kot_ref[...]
        kv = jnp.concatenate(
            [_dot(kot[h * DK_B:(h + 1) * DK_B],
                  jnp.where(in_seq, vhm_ref[h], 0.0).astype(BF16)) for h in range(H_B)], axis=0)
        s_new = s0 * dec + kv
        for h in range(H_B):
            stn_ref[s, h] = s_new[h * DK_B:(h + 1) * DK_B]
        return carry

    lax.fori_loop(0, NS, per_seq, 0)

    selt_r = _iota((DV_B, D_B), 0)
    selt_c = _iota((DV_B, D_B), 1)
    b_br = jnp.zeros((SB, D_B), F32)
    for h in range(H_B):
        o_h = ohm_ref[h]
        ms = jnp.mean(o_h * o_h, axis=-1, keepdims=True)
        ob = (o_h * lax.rsqrt(ms + EPS) * onw_ref[...] * _silu(ghm_ref[h])).astype(BF16)
        selt = jnp.where(selt_c == selt_r + h * DV_B, 1.0, 0.0).astype(BF16)
        b_br = b_br + _dot(ob, selt)
    out_ref[:, D_A:D_A + D_B] = b_br.astype(BF16)

    xg = proj_ref[:, C_XG:C_XG + D_X].astype(F32)
    out_ref[:, D_A + D_B:D_MODEL] = (ox_ref[...] * _silu(xg)).astype(BF16)


def _sample_mixer(proj, state, ck, cv, wa, abt, avw, bwa, bba, onw):
    nseq = state.shape[1]
    const = lambda *shape: pl.BlockSpec(shape, lambda i: (0,) * len(shape))
    return pl.pallas_call(
        _sample_mixer_kernel,
        grid=(nseq // NS,),
        in_specs=[pl.BlockSpec((SB, D_PROJ), lambda i: (i, 0)),
                  pl.BlockSpec((None, NS, H_B, DK_B, DV_B), lambda i: (0, i, 0, 0, 0)),
                  pl.BlockSpec((NS, N_MEM * H_X, HD_X), lambda i: (i, 0, 0)),
                  pl.BlockSpec((NS, N_MEM * H_X, HD_X), lambda i: (i, 0, 0)),
                  const(H_A, SB, SB), const(SB, H_A), const(1, D_A),
                  const(LANES, D_BK), const(1, D_BK), const(1, DV_B)],
        out_specs=[pl.BlockSpec((SB, D_MODEL), lambda i: (i, 0)),
                   pl.BlockSpec((None, NS, H_B, DK_B, DV_B), lambda i: (0, i, 0, 0, 0)),
                   pl.BlockSpec((SB, D_A), lambda i: (i, 0))],
        out_shape=[jax.ShapeDtypeStruct((nseq * TS, D_MODEL), BF16),
                   jax.ShapeDtypeStruct((1, nseq, H_B, DK_B, DV_B), F32),
                   jax.ShapeDtypeStruct((nseq * TS, D_A), F32)],
        scratch_shapes=[pltpu.VMEM((SB, D_BK), F32),
                        pltpu.VMEM((SB, D_X), F32),
                        pltpu.VMEM((D_BK, SBP), BF16),
                        pltpu.VMEM((D_BK, SBP), BF16),
                        pltpu.VMEM((D_BK, SBP), BF16),
                        pltpu.VMEM((H_B, SBP, DV_B), F32),
                        pltpu.VMEM((H_B, SB, DV_B), F32),
                        pltpu.VMEM((H_B, SB, DV_B), F32),
                        pltpu.VMEM((SB, D_X), F32)],
        compiler_params=pltpu.CompilerParams(
            dimension_semantics=("arbitrary",), vmem_limit_bytes=VMEM_LIMIT),
        name="sample_mixer",
    )(proj, state, ck, cv, wa, abt, avw, bwa, bba, onw)


def kernel(x_prompt, x_sample, mem_prompt, state_gla, cache_mem_k, cache_mem_v, norm_w, w_in,
           a_vnorm_w, a_ws, a_bs, b_wa, b_ba, b_onorm_w, mem_norm_w, w_mem_kv, w_out, final_norm_w):
    batch, seq, _ = x_prompt.shape
    nseq, tdec, _ = x_sample.shape
    depth = w_in.shape[0]
    assert depth == 1 and tdec == TS and seq % PT == 0 and nseq % NS == 0

    w_t = jnp.transpose(w_in[0])
    w_all_t = jnp.concatenate(
        [w_t[:C_BG], w_t[C_BG + GATE_RANK:], w_t[C_BG:C_BG + GATE_RANK],
         jnp.zeros((D_PROJ_PAD - w_t.shape[0], D_MODEL), w_t.dtype)], axis=0).astype(BF16)
    last_cols = D_PROJ_PAD - (D_PROJ_PAD // IN_BN) * IN_BN
    w_out_bf = w_out[0].astype(BF16)
    bwa = jnp.concatenate([b_wa[0], jnp.zeros((LANES - GATE_RANK, D_BK), F32)], axis=0).astype(BF16)
    bba = b_ba[0].reshape(1, D_BK)
    avw = a_vnorm_w[0].reshape(1, D_A)
    onw_p = jnp.tile(b_onorm_w[0], H_B).reshape(1, D_B)
    onw_s = b_onorm_w[0].reshape(1, DV_B)
    wa_p = a_ws[0]
    abt_p = a_bs[0].T
    wa_s = jnp.tile(a_ws[0][:, :TS, :TS], (1, NS, NS))
    abt_s = jnp.tile(a_bs[0][:, :TS], (1, NS)).T

    xp = x_prompt.reshape(batch * seq, D_MODEL)
    xs = x_sample.reshape(nseq * TS, D_MODEL)
    mem = mem_prompt.reshape(batch * N_MEM, D_MODEL)

    memkv = _norm_matmul(mem, mem_norm_w[0], w_mem_kv[0], None, bm=batch * N_MEM, bn=D_X,
                         transposed=False, out_dtype=F32, split_out=True, name="mem_kv")
    y_p, st_p = _prompt_layer(xp, w_all_t, w_out_bf, norm_w[0], final_norm_w, memkv, wa_p, abt_p,
                              avw, bwa, bba, onw_p, batch=batch, seq=seq)

    proj_s = _norm_matmul(xs, norm_w[0], w_all_t, D_PROJ, bm=nseq * TS, bn=IN_BN, transposed=True,
                          out_dtype=BF16, last_cols=last_cols, name="in_proj_s")
    br_s, st_s, cvs = _sample_mixer(
        proj_s, state_gla,
        cache_mem_k.reshape(nseq, N_MEM * H_X, HD_X), cache_mem_v.reshape(nseq, N_MEM * H_X, HD_X),
        wa_s, abt_s, avw, bwa, bba, onw_s)
    y_s = _out_proj(br_s, w_out_bf, xs, final_norm_w, bm=512, name="out_proj_s")

    return (y_p.reshape(batch, seq, D_MODEL),
            y_s.reshape(nseq, TS, D_MODEL),
            memkv[0].reshape(1, batch, N_MEM, H_X, HD_X),
            memkv[1].reshape(1, batch, N_MEM, H_X, HD_X),
            st_p.reshape(1, batch, H_B, DK_B, DV_B),
            st_s,
            cvs.reshape(1, nseq, TS, D_A))
```

```python
import functools

import jax
import jax.numpy as jnp
from jax import lax
from jax.experimental import pallas as pl
from jax.experimental.pallas import tpu as pltpu

F32 = jnp.float32
BF16 = jnp.bfloat16

D_MODEL = 2048
D_A = 768
H_A = 4
HD_A = 192
CHUNK_A = 128
D_B = 768
H_B = 4
DV_B = 192
DK_B = 96
D_BK = 384
GATE_RANK = 16
GATE_TAU = 16.0
CHUNK_B = 64
D_X = 512
H_X = 4
HD_X = 128
N_MEM = 256
EPS = 1e-6

LANES = 128
MXU_COLS = 256
C_AU, C_AV, C_AG = 0, 768, 1536
C_BQ, C_BK, C_BV, C_BG = 2304, 2688, 3072, 3840
C_XQ, C_XG, C_BR = 4608, 5120, 5632
D_PROJ = 5760
D_PROJ_PAD = 5888
WIN_START = (0, 128, 384, 512)
WIN = 256
IN_BN = 768
OUT_BN = 512

VMEM_LIMIT = 60 * 1024 * 1024


def _dot(a, b):
    return jnp.dot(a, b, preferred_element_type=F32)


def _dot_nt(a, b):
    return lax.dot_general(a, b, (((1,), (1,)), ((), ())), preferred_element_type=F32)


def _dot_tn(a, b):
    return lax.dot_general(a, b, (((0,), (0,)), ((), ())), preferred_element_type=F32)


def _split(x):
    hi = x.astype(BF16)
    lo = (x - hi.astype(F32)).astype(BF16)
    return hi, lo


def _silu(x):
    return x / (1.0 + jnp.exp(-x))


def _log_sigmoid(x):
    return jnp.minimum(x, 0.0) - jnp.log1p(jnp.exp(-jnp.abs(x)))


def _head_id(idx, width, n):
    h = jnp.zeros_like(idx)
    for i in range(1, n):
        h = h + (idx >= i * width).astype(jnp.int32)
    return h


def _iota(shape, dim):
    return lax.broadcasted_iota(jnp.int32, shape, dim)


def _merge_windows(r):
    rows = r[0].shape[0]
    lo = _iota((rows, LANES), 1) < 64
    tiles = [r[0][:, :LANES], jnp.where(lo, r[0][:, LANES:], r[1][:, :LANES]), r[1][:, LANES:],
             r[2][:, :LANES], jnp.where(lo, r[2][:, LANES:], r[3][:, :LANES]), r[3][:, LANES:]]
    return jnp.concatenate(tiles, axis=1)


def _group_a(proj_ref, wa_ref, abt_ref, avw_ref, mask):
    u = proj_ref[:, C_AU:C_AU + D_A].astype(F32)
    v = proj_ref[:, C_AV:C_AV + D_A].astype(F32)
    g = proj_ref[:, C_AG:C_AG + D_A].astype(F32)
    rows = u.shape[0]
    ms = jnp.mean(v * v, axis=-1, keepdims=True)
    vn = v * lax.rsqrt(ms + EPS) * avw_ref[...]
    vb = vn.astype(BF16)
    r = []
    for gi in range(H_A):
        w = jnp.where(mask, wa_ref[gi], 0.0).astype(BF16)
        r.append(_dot(w, vb[:, WIN_START[gi]:WIN_START[gi] + WIN]))
    mixed = _merge_windows(r)
    col_g = _head_id(_iota((rows, D_A), 1), HD_A, H_A)
    bias = jnp.zeros((rows, D_A), F32)
    for gi in range(H_A):
        bias = jnp.where(col_g == gi, abt_ref[:, gi:gi + 1], bias)
    return u * (mixed + bias) * _silu(g), vn


def _gla_intra_windows(q_in, k_in, vb, mask):
    rows = q_in.shape[0]
    col_h = _head_id(_iota((rows, D_BK), 1), DK_B, H_B)
    kb = k_in.astype(BF16)
    r = []
    for h in range(H_B):
        qh = jnp.where(col_h == h, q_in, 0.0).astype(BF16)
        s = jnp.where(mask, _dot_nt(qh, kb), 0.0).astype(BF16)
        r.append(_dot(s, vb[:, WIN_START[h]:WIN_START[h] + WIN]))
    return r


def _norm_matmul_kernel(x_ref, nw_ref, w_ref, o_ref, h_ref, *, transposed, last_cols):
    j = pl.program_id(1)
    nj = pl.num_programs(1)

    @pl.when(j == 0)
    def _():
        x = x_ref[...]
        ms = jnp.mean(x * x, axis=-1, keepdims=True)
        h_ref[...] = (x * lax.rsqrt(ms + EPS) * nw_ref[...]).astype(BF16)

    mm = _dot_nt if transposed else _dot
    if last_cols is None:
        o_ref[...] = mm(h_ref[...], w_ref[...].astype(BF16)).astype(o_ref.dtype)
    else:
        @pl.when(j < nj - 1)
        def _():
            o_ref[...] = mm(h_ref[...], w_ref[...].astype(BF16)).astype(o_ref.dtype)

        @pl.when(j == nj - 1)
        def _():
            w = w_ref[:last_cols, :] if transposed else w_ref[:, :last_cols]
            o_ref[:, :last_cols] = mm(h_ref[...], w.astype(BF16)).astype(o_ref.dtype)


def _norm_matmul(x, nw, w, n_out, *, bm, bn, transposed, out_dtype, split_out=False,
                 last_cols=None, name):
    m, k = x.shape
    n = w.shape[0] if transposed else w.shape[1]
    nj = -(-n // bn)
    assert m % bm == 0
    if split_out:
        out_shape = jax.ShapeDtypeStruct((nj, m, bn), out_dtype)
        out_spec = pl.BlockSpec((None, bm, bn), lambda i, j: (j, i, 0))
    else:
        out_shape = jax.ShapeDtypeStruct((m, n_out), out_dtype)
        out_spec = pl.BlockSpec((bm, bn), lambda i, j: (i, j))
    if transposed:
        w_spec = pl.BlockSpec((bn, k), lambda i, j: (j, 0))
    else:
        w_spec = pl.BlockSpec((k, bn), lambda i, j: (0, j))
    return pl.pallas_call(
        functools.partial(_norm_matmul_kernel, transposed=transposed, last_cols=last_cols),
        grid=(m // bm, nj),
        in_specs=[pl.BlockSpec((bm, k), lambda i, j: (i, 0)),
                  pl.BlockSpec((1, k), lambda i, j: (0, 0)),
                  w_spec],
        out_specs=out_spec,
        out_shape=out_shape,
        scratch_shapes=[pltpu.VMEM((bm, k), BF16)],
        compiler_params=pltpu.CompilerParams(
            dimension_semantics=("arbitrary", "arbitrary"), vmem_limit_bytes=VMEM_LIMIT),
        name=name,
    )(x, nw.reshape(1, k), w)


PREP_BN = MXU_COLS
N_MAIN_BLOCKS = C_BG // PREP_BN


def _w_in_prep_kernel(wt_hbm, o_ref, buf, sem):
    r = pl.program_id(0)
    last = pl.num_programs(0) - 1
    slot = lax.rem(r, 2)

    def block_copy(rr, sl):
        start = pl.multiple_of(jnp.where(rr < N_MAIN_BLOCKS, rr * PREP_BN, rr * PREP_BN + GATE_RANK), 8)
        return pltpu.make_async_copy(wt_hbm.at[pl.ds(start, PREP_BN), :], buf.at[sl], sem.at[sl])

    def gate_copy(sl):
        return pltpu.make_async_copy(wt_hbm.at[pl.ds(C_BG, GATE_RANK), :],
                                     buf.at[sl, pl.ds(0, GATE_RANK), :], sem.at[sl])

    @pl.when(r == 0)
    def _():
        block_copy(r, slot).start()

    @pl.when(r + 1 < last)
    def _():
        block_copy(r + 1, 1 - slot).start()

    @pl.when(r + 1 == last)
    def _():
        gate_copy(1 - slot).start()

    @pl.when(r < last)
    def _():
        block_copy(r, slot).wait()
        o_ref[...] = buf[slot].T.astype(BF16)

    @pl.when(r == last)
    def _():
        gate_copy(slot).wait()
        rows = jnp.concatenate([buf[slot, 0:GATE_RANK, :],
                                jnp.zeros((PREP_BN - GATE_RANK, D_MODEL), F32)], axis=0)
        o_ref[...] = rows.T.astype(BF16)


def _w_in_prep(w_t):
    k = w_t.shape[1]
    return pl.pallas_call(
        _w_in_prep_kernel,
        grid=(D_PROJ_PAD // PREP_BN,),
        in_specs=[pl.BlockSpec(memory_space=pl.ANY)],
        out_specs=pl.BlockSpec((k, PREP_BN), lambda r: (0, r)),
        out_shape=jax.ShapeDtypeStruct((k, D_PROJ_PAD), BF16),
        scratch_shapes=[pltpu.VMEM((2, PREP_BN, k), F32), pltpu.SemaphoreType.DMA((2,))],
        compiler_params=pltpu.CompilerParams(
            dimension_semantics=("arbitrary",), vmem_limit_bytes=VMEM_LIMIT),
        name="w_in_prep",
    )(w_t)


def _cast_kernel(x_ref, o_ref):
    o_ref[...] = x_ref[...].astype(o_ref.dtype)


def _cast_bf16(w, *, bm, name):
    m, n = w.shape
    return pl.pallas_call(
        _cast_kernel,
        grid=(m // bm,),
        in_specs=[pl.BlockSpec((bm, n), lambda i: (i, 0))],
        out_specs=pl.BlockSpec((bm, n), lambda i: (i, 0)),
        out_shape=jax.ShapeDtypeStruct((m, n), BF16),
        compiler_params=pltpu.CompilerParams(dimension_semantics=("arbitrary",)),
        name=name,
    )(w)


def _out_proj_kernel(br_ref, w_ref, x_ref, fw_ref, y_ref):
    acc = _dot(br_ref[...], w_ref[...]) + x_ref[...]
    ms = jnp.mean(acc * acc, axis=-1, keepdims=True)
    y_ref[...] = acc * lax.rsqrt(ms + EPS) * fw_ref[...]


def _out_proj(br, w, x, fw, *, bm, name):
    m, k = br.shape
    n = w.shape[1]
    return pl.pallas_call(
        _out_proj_kernel,
        grid=(m // bm,),
        in_specs=[pl.BlockSpec((bm, k), lambda i: (i, 0)),
                  pl.BlockSpec((k, n), lambda i: (0, 0)),
                  pl.BlockSpec((bm, n), lambda i: (i, 0)),
                  pl.BlockSpec((1, n), lambda i: (0, 0))],
        out_specs=pl.BlockSpec((bm, n), lambda i: (i, 0)),
        out_shape=jax.ShapeDtypeStruct((m, n), F32),
        compiler_params=pltpu.CompilerParams(
            dimension_semantics=("arbitrary",), vmem_limit_bytes=VMEM_LIMIT),
        name=name,
    )(br, w, x, fw.reshape(1, n))


PB = 128
PT = 256


def _prompt_sub_block(proj_ref, out_ref, mk_ref, mv_ref, wa_ref, abt_ref, avw_ref, bwa_ref, bba_ref,
                      onw_ref, sbd_ref, ind_ref):
    ri = _iota((PB, PB), 0)
    ci = _iota((PB, PB), 1)
    a_br, _ = _group_a(proj_ref, wa_ref, abt_ref, avw_ref, ci <= ri)
    out_ref[:, 0:D_A] = a_br.astype(BF16)
    yield

    q = proj_ref[:, C_BQ:C_BQ + D_BK].astype(F32)
    k = proj_ref[:, C_BK:C_BK + D_BK].astype(F32)
    vb = proj_ref[:, C_BV:C_BV + D_B]
    pre = _dot(proj_ref[:, C_BR:C_BR + LANES], bwa_ref[...]) + bba_ref[...]
    log_a = _log_sigmoid(pre) * (1.0 / GATE_TAU)
    r64 = _iota((CHUNK_B, CHUNK_B), 0)
    c64 = _iota((CHUNK_B, CHUNK_B), 1)
    tril = c64 <= r64
    tril_bf = jnp.where(tril, 1.0, 0.0).astype(BF16)
    row_pad = jnp.zeros((LANES - CHUNK_B, D_BK), F32)
    v_pad = jnp.zeros((LANES - CHUNK_B, WIN), BF16)
    win_col = _iota((CHUNK_B, WIN), 1)
    o_chunks = []
    for c in range(PB // CHUNK_B):
        sl = slice(c * CHUNK_B, (c + 1) * CHUNK_B)
        hi, lo = _split(log_a[sl])
        cum = _dot(tril_bf, hi) + _dot(tril_bf, lo)
        tot_row = cum[CHUNK_B - 1:CHUNK_B, :]
        la_t = jnp.concatenate([log_a[sl], row_pad], axis=0).T
        q_in = q[sl] * (DK_B ** -0.5) * jnp.exp(cum)
        k_in = k[sl] * jnp.exp(-cum)
        k_out = k[sl] * jnp.exp(tot_row - cum)
        o_intra = _merge_windows(_gla_intra_windows(q_in, k_in, vb[sl], tril))
        s_old = sbd_ref[...]
        o_inter = _dot(q_in.astype(BF16), s_old.astype(BF16))
        kot = jnp.concatenate([k_out, row_pad], axis=0).T.astype(BF16)
        for h in range(H_B):
            rs = slice(h * DK_B, (h + 1) * DK_B)
            ws = slice(WIN_START[h], WIN_START[h] + WIN)
            dec = jnp.exp(jnp.sum(la_t[rs], axis=1, keepdims=True))
            lo_col = h * DV_B - WIN_START[h]
            in_head = (win_col >= lo_col) & (win_col < lo_col + DV_B)
            v_h = jnp.where(in_head, vb[sl, ws], jnp.zeros((), BF16))
            kv = _dot(kot[rs], jnp.concatenate([v_h, v_pad], axis=0))
            sbd_ref[rs, ws] = s_old[rs, ws] * dec + kv
        o_chunks.append(o_intra + o_inter)
        yield
    o = jnp.concatenate(o_chunks, axis=0)
    ssq = _dot((o * o).astype(BF16), ind_ref[...])
    o_n = o * lax.rsqrt(ssq * (1.0 / DV_B) + EPS) * onw_ref[...]
    bg = proj_ref[:, C_BG:C_BG + D_B].astype(F32)
    out_ref[:, D_A:D_A + D_B] = (o_n * _silu(bg)).astype(BF16)
    yield

    for h in range(H_X):
        hs = slice(h * HD_X, (h + 1) * HD_X)
        qh = proj_ref[:, C_XQ + h * HD_X:C_XQ + (h + 1) * HD_X]
        s = _dot_nt(qh, mk_ref[:, hs].astype(BF16)) * (HD_X ** -0.5)
        e = jnp.exp(s - jnp.max(s, axis=-1, keepdims=True))
        den = jnp.sum(e, axis=-1, keepdims=True)
        ox = _dot(e.astype(BF16), mv_ref[:, hs].astype(BF16)) / den
        xg = proj_ref[:, C_XG + h * HD_X:C_XG + (h + 1) * HD_X].astype(F32)
        out_ref[:, D_A + D_B + h * HD_X:D_A + D_B + (h + 1) * HD_X] = (ox * _silu(xg)).astype(BF16)
        if h % 2 == 1:
            yield


def _prompt_layer_kernel(xn_ref, wi_ref, wo_ref, nw_ref, fw_ref, mk_ref, mv_ref, wa_ref, abt_ref,
                         avw_ref, bwa_ref, bba_ref, onw_ref, y_ref, st_ref,
                         pa_ref, pb_ref, xk_ref, br_ref, h_ref, sbd_ref, ind_ref, *, nt):
    s = pl.program_id(0)
    cur = jnp.maximum(s - 1, 0)
    t = lax.rem(cur, nt)

    @pl.when(s == 0)
    def _():
        r2 = _head_id(_iota((D_B, D_B), 0), DV_B, H_B)
        c2 = _head_id(_iota((D_B, D_B), 1), DV_B, H_B)
        ind_ref[...] = jnp.where(r2 == c2, 1.0, 0.0).astype(BF16)
        pb_ref[...] = jnp.zeros_like(pb_ref)
        xk_ref[...] = jnp.zeros_like(xk_ref)

    @pl.when(t == 0)
    def _():
        sbd_ref[...] = jnp.zeros_like(sbd_ref)

    def in_proj_stages(pn_ref):
        x = xn_ref[...]
        ms = jnp.mean(x * x, axis=-1, keepdims=True)
        h_ref[...] = (x * lax.rsqrt(ms + EPS) * nw_ref[...]).astype(BF16)
        yield
        for c0 in range(0, D_PROJ_PAD, IN_BN):
            n = min(IN_BN, D_PROJ_PAD - c0)
            pn_ref[:, c0:c0 + n] = _dot(h_ref[...], wi_ref[:, c0:c0 + n]).astype(BF16)
            yield

    def out_proj_stages(sb):
        rows = pl.ds(sb * PB, PB)
        ssq = jnp.zeros((PB, 1), F32)
        for c0 in range(0, D_MODEL, OUT_BN):
            cols = pl.ds(c0, OUT_BN)
            acc = _dot(br_ref[rows, :], wo_ref[:, cols]) + xk_ref[rows, cols]
            y_ref[rows, cols] = acc
            ssq = ssq + jnp.sum(acc * acc, axis=-1, keepdims=True)
            yield
        y_ref[rows, :] = y_ref[rows, :] * lax.rsqrt(ssq * (1.0 / D_MODEL) + EPS) * fw_ref[...]
        yield

    def body(pn_ref, pc_ref):
        streams = {"P": in_proj_stages(pn_ref)}
        for sb in range(PT // PB):
            rows = pl.ds(sb * PB, PB)
            streams["M%d" % sb] = _prompt_sub_block(
                pc_ref.at[rows, :], br_ref.at[rows, :], mk_ref, mv_ref, wa_ref, abt_ref, avw_ref,
                bwa_ref, bba_ref, onw_ref, sbd_ref, ind_ref)
            streams["O%d" % sb] = out_proj_stages(sb)
        order = ("P P M0 P M0 P M0 P M0 M0 M0 "
                 "P M1 O0 P M1 O0 P M1 O0 P M1 O0 M1 M1 O0 "
                 "O1 O1 O1 O1 O1").split()
        for name in order:
            next(streams[name])
        for name, gen in streams.items():
            assert next(gen, "done") == "done", name
        xk_ref[...] = xn_ref[...]

    @pl.when(lax.rem(s, 2) == 0)
    def _():
        body(pa_ref, pb_ref)

    @pl.when(lax.rem(s, 2) == 1)
    def _():
        body(pb_ref, pa_ref)

    @pl.when((t == nt - 1) & (s > 0))
    def _():
        for h in range(H_B):
            off = h * DV_B - WIN_START[h]
            blk = sbd_ref[h * DK_B:(h + 1) * DK_B, WIN_START[h]:WIN_START[h] + WIN]
            if off:
                blk = pltpu.roll(blk, WIN - off, 1)
            st_ref[0, h] = blk[:, :DV_B]


def _prompt_layer(xp, w_in_bf, w_out_bf, nw, fw, memkv, wa, abt, avw, bwa, bba, onw, *, batch, seq):
    nt = seq // PT
    nblk = batch * nt
    cur = lambda s: jnp.maximum(s - 1, 0)
    const = lambda *shape: pl.BlockSpec(shape, lambda s: (0,) * len(shape))
    resident = lambda *shape: pl.BlockSpec(shape, lambda s: (0,) * len(shape),
                                           pipeline_mode=pl.Buffered(1))
    return pl.pallas_call(
        functools.partial(_prompt_layer_kernel, nt=nt),
        grid=(nblk + 1,),
        in_specs=[pl.BlockSpec((PT, D_MODEL), lambda s: (jnp.minimum(s, nblk - 1), 0)),
                  resident(D_MODEL, D_PROJ_PAD), resident(D_MODEL, D_MODEL),
                  const(1, D_MODEL), const(1, D_MODEL),
                  pl.BlockSpec((None, N_MEM, D_X), lambda s: (0, cur(s) // nt, 0)),
                  pl.BlockSpec((None, N_MEM, D_X), lambda s: (1, cur(s) // nt, 0)),
                  const(H_A, CHUNK_A, CHUNK_A), const(CHUNK_A, H_A), const(1, D_A),
                  const(LANES, D_BK), const(1, D_BK), const(1, D_B)],
        out_specs=[pl.BlockSpec((PT, D_MODEL), lambda s: (cur(s), 0)),
                   pl.BlockSpec((1, H_B, DK_B, DV_B), lambda s: (cur(s) // nt, 0, 0, 0))],
        out_shape=[jax.ShapeDtypeStruct((batch * seq, D_MODEL), F32),
                   jax.ShapeDtypeStruct((batch, H_B, DK_B, DV_B), F32)],
        scratch_shapes=[pltpu.VMEM((PT, D_PROJ_PAD), BF16),
                        pltpu.VMEM((PT, D_PROJ_PAD), BF16),
                        pltpu.VMEM((PT, D_MODEL), F32),
                        pltpu.VMEM((PT, D_MODEL), BF16),
                        pltpu.VMEM((PT, D_MODEL), BF16),
                        pltpu.VMEM((D_BK, D_B), F32),
                        pltpu.VMEM((D_B, D_B), BF16)],
        compiler_params=pltpu.CompilerParams(
            dimension_semantics=("arbitrary",), vmem_limit_bytes=VMEM_LIMIT),
        name="prompt_layer",
    )(xp, w_in_bf, w_out_bf, nw.reshape(1, D_MODEL), fw.reshape(1, D_MODEL), memkv, memkv,
      wa, abt, avw, bwa, bba, onw)


NS = 8
TS = 8
SB = NS * TS
SBP = 128


def _sample_mixer_kernel(proj_ref, st_ref, ck_ref, cv_ref, wa_ref, abt_ref, avw_ref, bwa_ref,
                         bba_ref, onw_ref, out_ref, stn_ref, cvs_ref,
                         qin_ref, xq_ref, kot_ref, lath_ref, latl_ref, vhm_ref, ghm_ref, ohm_ref,
                         ox_ref):
    ri = _iota((SB, SB), 0)
    ci = _iota((SB, SB), 1)
    same_seq = jnp.right_shift(ri, 3) == jnp.right_shift(ci, 3)
    causal = same_seq & (ci <= ri)

    a_br, vn = _group_a(proj_ref, wa_ref, abt_ref, avw_ref, causal)
    out_ref[:, 0:D_A] = a_br.astype(BF16)
    cvs_ref[...] = vn

    q = proj_ref[:, C_BQ:C_BQ + D_BK].astype(F32)
    k = proj_ref[:, C_BK:C_BK + D_BK].astype(F32)
    vb = proj_ref[:, C_BV:C_BV + D_B]
    bgb = proj_ref[:, C_BG:C_BG + D_B]
    pre = _dot(proj_ref[:, C_BR:C_BR + LANES], bwa_ref[...]) + bba_ref[...]
    log_a = _log_sigmoid(pre) * (1.0 / GATE_TAU)
    causal_bf = jnp.where(causal, 1.0, 0.0).astype(BF16)
    seq_bf = jnp.where(same_seq, 1.0, 0.0).astype(BF16)
    hi, lo = _split(log_a)
    cum = _dot(causal_bf, hi) + _dot(causal_bf, lo)
    tot = _dot(seq_bf, hi) + _dot(seq_bf, lo)
    q_in = q * (DK_B ** -0.5) * jnp.exp(cum)
    k_in = k * jnp.exp(-cum)
    k_out = k * jnp.exp(tot - cum)
    qin_ref[...] = q_in
    zpad = jnp.zeros((SBP - SB, D_BK), F32)
    kot_ref[...] = jnp.concatenate([k_out, zpad], axis=0).T.astype(BF16)
    lat_hi, lat_lo = _split(jnp.concatenate([log_a, zpad], axis=0).T)
    lath_ref[...] = lat_hi
    latl_ref[...] = lat_lo
    xq_ref[...] = proj_ref[:, C_XQ:C_XQ + D_X].astype(F32)

    wins = _gla_intra_windows(q_in, k_in, vb, causal)
    sel_r = _iota((D_B, DV_B), 0)
    sel_c = _iota((D_B, DV_B), 1)
    vhm_ref[...] = jnp.zeros_like(vhm_ref)
    for h in range(H_B):
        off = h * DV_B - WIN_START[h]
        w = wins[h]
        if off:
            w = pltpu.roll(w, WIN - off, 1)
        ohm_ref[h] = w[:, :DV_B]
        sel = jnp.where(sel_r == sel_c + h * DV_B, 1.0, 0.0).astype(BF16)
        vhm_ref[h, 0:SB, :] = _dot(vb, sel)
        ghm_ref[h] = _dot(bgb, sel)

    mask_x = (jnp.right_shift(_iota((H_X * TS, N_MEM * H_X), 0), 3)
              == jnp.bitwise_and(_iota((H_X * TS, N_MEM * H_X), 1), H_X - 1))
    mask_b = jnp.right_shift(_iota((H_B * TS, D_BK), 0), 3) == _head_id(_iota((H_B * TS, D_BK), 1), DK_B, H_B)
    row_seq = jnp.right_shift(_iota((SBP, DV_B), 0), 3)

    def per_seq(s, carry):
        r0 = pl.multiple_of(s * TS, TS)
        q8 = xq_ref[pl.ds(r0, TS), :]
        q32 = jnp.concatenate([q8[:, h * HD_X:(h + 1) * HD_X] for h in range(H_X)], axis=0)
        sc = _dot_nt(q32.astype(BF16), ck_ref[s].astype(BF16)) * (HD_X ** -0.5)
        sc = jnp.where(mask_x, sc, -1e30)
        e = jnp.exp(sc - jnp.max(sc, axis=-1, keepdims=True))
        den = jnp.sum(e, axis=-1, keepdims=True)
        o = _dot(e.astype(BF16), cv_ref[s].astype(BF16)) / den
        ox_ref[pl.ds(r0, TS), :] = jnp.concatenate(
            [o[h * TS:(h + 1) * TS] for h in range(H_X)], axis=1)
        qi8 = qin_ref[pl.ds(r0, TS), :]
        qbd2 = jnp.where(mask_b, jnp.concatenate([qi8] * H_B, axis=0), 0.0).astype(BF16)
        s0 = jnp.concatenate([st_ref[s, h] for h in range(H_B)], axis=0)
        o_inter = _dot(qbd2, s0.astype(BF16))
        in_seq = row_seq == s
        for h in range(H_B):
            ohm_ref[h, pl.ds(r0, TS), :] += o_inter[h * TS:(h + 1) * TS]
        row_sel = jnp.where(in_seq, 1.0, 0.0).astype(BF16)
        dec = jnp.exp(_dot(lath_ref[...], row_sel) + _dot(latl_ref[...], row_sel))
        kot = kot_ref[...]
        kv = jnp.concatenate(
            [_dot(kot[h * DK_B:(h + 1) * DK_B],
                  jnp.where(in_seq, vhm_ref[h], 0.0).astype(BF16)) for h in range(H_B)], axis=0)
        s_new = s0 * dec + kv
        for h in range(H_B):
            stn_ref[s, h] = s_new[h * DK_B:(h + 1) * DK_B]
        return carry

    lax.fori_loop(0, NS, per_seq, 0)

    selt_r = _iota((DV_B, D_B), 0)
    selt_c = _iota((DV_B, D_B), 1)
    b_br = jnp.zeros((SB, D_B), F32)
    for h in range(H_B):
        o_h = ohm_ref[h]
        ms = jnp.mean(o_h * o_h, axis=-1, keepdims=True)
        ob = (o_h * lax.rsqrt(ms + EPS) * onw_ref[...] * _silu(ghm_ref[h])).astype(BF16)
        selt = jnp.where(selt_c == selt_r + h * DV_B, 1.0, 0.0).astype(BF16)
        b_br = b_br + _dot(ob, selt)
    out_ref[:, D_A:D_A + D_B] = b_br.astype(BF16)

    xg = proj_ref[:, C_XG:C_XG + D_X].astype(F32)
    out_ref[:, D_A + D_B:D_MODEL] = (ox_ref[...] * _silu(xg)).astype(BF16)


def _sample_mixer(proj, state, ck, cv, wa, abt, avw, bwa, bba, onw):
    nseq = state.shape[1]
    const = lambda *shape: pl.BlockSpec(shape, lambda i: (0,) * len(shape))
    return pl.pallas_call(
        _sample_mixer_kernel,
        grid=(nseq // NS,),
        in_specs=[pl.BlockSpec((SB, D_PROJ), lambda i: (i, 0)),
                  pl.BlockSpec((None, NS, H_B, DK_B, DV_B), lambda i: (0, i, 0, 0, 0)),
                  pl.BlockSpec((NS, N_MEM * H_X, HD_X), lambda i: (i, 0, 0)),
                  pl.BlockSpec((NS, N_MEM * H_X, HD_X), lambda i: (i, 0, 0)),
                  const(H_A, SB, SB), const(SB, H_A), const(1, D_A),
                  const(LANES, D_BK), const(1, D_BK), const(1, DV_B)],
        out_specs=[pl.BlockSpec((SB, D_MODEL), lambda i: (i, 0)),
                   pl.BlockSpec((None, NS, H_B, DK_B, DV_B), lambda i: (0, i, 0, 0, 0)),
                   pl.BlockSpec((SB, D_A), lambda i: (i, 0))],
        out_shape=[jax.ShapeDtypeStruct((nseq * TS, D_MODEL), BF16),
                   jax.ShapeDtypeStruct((1, nseq, H_B, DK_B, DV_B), F32),
                   jax.ShapeDtypeStruct((nseq * TS, D_A), F32)],
        scratch_shapes=[pltpu.VMEM((SB, D_BK), F32),
                        pltpu.VMEM((SB, D_X), F32),
                        pltpu.VMEM((D_BK, SBP), BF16),
                        pltpu.VMEM((D_BK, SBP), BF16),
                        pltpu.VMEM((D_BK, SBP), BF16),
                        pltpu.VMEM((H_B, SBP, DV_B), F32),
                        pltpu.VMEM((H_B, SB, DV_B), F32),
                        pltpu.VMEM((H_B, SB, DV_B), F32),
                        pltpu.VMEM((SB, D_X), F32)],
        compiler_params=pltpu.CompilerParams(
            dimension_semantics=("arbitrary",), vmem_limit_bytes=VMEM_LIMIT),
        name="sample_mixer",
    )(proj, state, ck, cv, wa, abt, avw, bwa, bba, onw)


def kernel(x_prompt, x_sample, mem_prompt, state_gla, cache_mem_k, cache_mem_v, norm_w, w_in,
           a_vnorm_w, a_ws, a_bs, b_wa, b_ba, b_onorm_w, mem_norm_w, w_mem_kv, w_out, final_norm_w):
    batch, seq, _ = x_prompt.shape
    nseq, tdec, _ = x_sample.shape
    depth = w_in.shape[0]
    assert depth == 1 and tdec == TS and seq % PT == 0 and nseq % NS == 0

    w_in_bf = _w_in_prep(jnp.transpose(w_in[0]))
    last_cols = D_PROJ_PAD - (D_PROJ_PAD // IN_BN) * IN_BN
    w_out_bf = _cast_bf16(w_out[0], bm=256, name="w_out_prep")
    bwa = jnp.concatenate([b_wa[0], jnp.zeros((LANES - GATE_RANK, D_BK), F32)], axis=0).astype(BF16)
    bba = b_ba[0].reshape(1, D_BK)
    avw = a_vnorm_w[0].reshape(1, D_A)
    onw_p = jnp.tile(b_onorm_w[0], H_B).reshape(1, D_B)
    onw_s = b_onorm_w[0].reshape(1, DV_B)
    wa_p = a_ws[0]
    abt_p = a_bs[0].T
    wa_s = jnp.tile(a_ws[0][:, :TS, :TS], (1, NS, NS))
    abt_s = jnp.tile(a_bs[0][:, :TS], (1, NS)).T

    xp = x_prompt.reshape(batch * seq, D_MODEL)
    xs = x_sample.reshape(nseq * TS, D_MODEL)
    mem = mem_prompt.reshape(batch * N_MEM, D_MODEL)

    memkv = _norm_matmul(mem, mem_norm_w[0], w_mem_kv[0], None, bm=batch * N_MEM, bn=D_X,
                         transposed=False, out_dtype=F32, split_out=True, name="mem_kv")
    y_p, st_p = _prompt_layer(xp, w_in_bf, w_out_bf, norm_w[0], final_norm_w, memkv, wa_p, abt_p,
                              avw, bwa, bba, onw_p, batch=batch, seq=seq)

    proj_s = _norm_matmul(xs, norm_w[0], w_in_bf, D_PROJ, bm=nseq * TS, bn=IN_BN, transposed=False,
                          out_dtype=BF16, last_cols=last_cols, name="in_proj_s")
    br_s, st_s, cvs = _sample_mixer(
        proj_s, state_gla,
        cache_mem_k.reshape(nseq, N_MEM * H_X, HD_X), cache_mem_v.reshape(nseq, N_MEM * H_X, HD_X),
        wa_s, abt_s, avw, bwa, bba, onw_s)
    y_s = _out_proj(br_s, w_out_bf, xs, final_norm_w, bm=512, name="out_proj_s")

    return (y_p.reshape(batch, seq, D_MODEL),
            y_s.reshape(nseq, TS, D_MODEL),
            memkv[0].reshape(1, batch, N_MEM, H_X, HD_X),
            memkv[1].reshape(1, batch, N_MEM, H_X, HD_X),
            st_p.reshape(1, batch, H_B, DK_B, DV_B),
            st_s,
            cvs.reshape(1, nseq, TS, D_A))
```

```python
import functools

import jax
import jax.numpy as jnp
from jax import lax
from jax.experimental import pallas as pl
from jax.experimental.pallas import tpu as pltpu

F32 = jnp.float32
BF16 = jnp.bfloat16

D_MODEL = 2048
D_A = 768
H_A = 4
HD_A = 192
CHUNK_A = 128
D_B = 768
H_B = 4
DV_B = 192
DK_B = 96
D_BK = 384
GATE_RANK = 16
GATE_TAU = 16.0
CHUNK_B = 64
D_X = 512
H_X = 4
HD_X = 128
N_MEM = 256
EPS = 1e-6

LANES = 128
MXU_COLS = 256
C_AU, C_AV, C_AG = 0, 768, 1536
C_BQ, C_BK, C_BV, C_BG = 2304, 2688, 3072, 3840
C_XQ, C_XG, C_BR = 4608, 5120, 5632
D_PROJ = 5760
D_PROJ_PAD = 5888
WIN_START = (0, 128, 384, 512)
WIN = 256
IN_BN = 768
OUT_BN = 512

VMEM_LIMIT = 60 * 1024 * 1024


def _dot(a, b):
    return jnp.dot(a, b, preferred_element_type=F32)


def _dot_nt(a, b):
    return lax.dot_general(a, b, (((1,), (1,)), ((), ())), preferred_element_type=F32)


def _dot_tn(a, b):
    return lax.dot_general(a, b, (((0,), (0,)), ((), ())), preferred_element_type=F32)


def _split(x):
    hi = x.astype(BF16)
    lo = (x - hi.astype(F32)).astype(BF16)
    return hi, lo


def _silu(x):
    return x / (1.0 + jnp.exp(-x))


def _log_sigmoid(x):
    return jnp.minimum(x, 0.0) - jnp.log1p(jnp.exp(-jnp.abs(x)))


def _head_id(idx, width, n):
    h = jnp.zeros_like(idx)
    for i in range(1, n):
        h = h + (idx >= i * width).astype(jnp.int32)
    return h


def _iota(shape, dim):
    return lax.broadcasted_iota(jnp.int32, shape, dim)


def _merge_windows(r):
    rows = r[0].shape[0]
    lo = _iota((rows, LANES), 1) < 64
    tiles = [r[0][:, :LANES], jnp.where(lo, r[0][:, LANES:], r[1][:, :LANES]), r[1][:, LANES:],
             r[2][:, :LANES], jnp.where(lo, r[2][:, LANES:], r[3][:, :LANES]), r[3][:, LANES:]]
    return jnp.concatenate(tiles, axis=1)


def _group_a(proj_ref, wa_ref, abt_ref, avw_ref, mask):
    u = proj_ref[:, C_AU:C_AU + D_A].astype(F32)
    v = proj_ref[:, C_AV:C_AV + D_A].astype(F32)
    g = proj_ref[:, C_AG:C_AG + D_A].astype(F32)
    rows = u.shape[0]
    ms = jnp.mean(v * v, axis=-1, keepdims=True)
    vn = v * lax.rsqrt(ms + EPS) * avw_ref[...]
    vb = vn.astype(BF16)
    r = []
    for gi in range(H_A):
        w = jnp.where(mask, wa_ref[gi], 0.0).astype(BF16)
        r.append(_dot(w, vb[:, WIN_START[gi]:WIN_START[gi] + WIN]))
    mixed = _merge_windows(r)
    col_g = _head_id(_iota((rows, D_A), 1), HD_A, H_A)
    bias = jnp.zeros((rows, D_A), F32)
    for gi in range(H_A):
        bias = jnp.where(col_g == gi, abt_ref[:, gi:gi + 1], bias)
    return u * (mixed + bias) * _silu(g), vn


def _gla_intra_windows(q_in, k_in, vb, mask):
    rows = q_in.shape[0]
    col_h = _head_id(_iota((rows, D_BK), 1), DK_B, H_B)
    kb = k_in.astype(BF16)
    r = []
    for h in range(H_B):
        qh = jnp.where(col_h == h, q_in, 0.0).astype(BF16)
        s = jnp.where(mask, _dot_nt(qh, kb), 0.0).astype(BF16)
        r.append(_dot(s, vb[:, WIN_START[h]:WIN_START[h] + WIN]))
    return r


def _norm_matmul_kernel(x_ref, nw_ref, w_ref, o_ref, h_ref, *, transposed, last_cols):
    j = pl.program_id(1)
    nj = pl.num_programs(1)

    @pl.when(j == 0)
    def _():
        x = x_ref[...]
        ms = jnp.mean(x * x, axis=-1, keepdims=True)
        h_ref[...] = (x * lax.rsqrt(ms + EPS) * nw_ref[...]).astype(BF16)

    mm = _dot_nt if transposed else _dot
    if last_cols is None:
        o_ref[...] = mm(h_ref[...], w_ref[...].astype(BF16)).astype(o_ref.dtype)
    else:
        @pl.when(j < nj - 1)
        def _():
            o_ref[...] = mm(h_ref[...], w_ref[...].astype(BF16)).astype(o_ref.dtype)

        @pl.when(j == nj - 1)
        def _():
            w = w_ref[:last_cols, :] if transposed else w_ref[:, :last_cols]
            o_ref[:, :last_cols] = mm(h_ref[...], w.astype(BF16)).astype(o_ref.dtype)


def _norm_matmul(x, nw, w, n_out, *, bm, bn, transposed, out_dtype, split_out=False,
                 last_cols=None, name):
    m, k = x.shape
    n = w.shape[0] if transposed else w.shape[1]
    nj = -(-n // bn)
    assert m % bm == 0
    if split_out:
        out_shape = jax.ShapeDtypeStruct((nj, m, bn), out_dtype)
        out_spec = pl.BlockSpec((None, bm, bn), lambda i, j: (j, i, 0))
    else:
        out_shape = jax.ShapeDtypeStruct((m, n_out), out_dtype)
        out_spec = pl.BlockSpec((bm, bn), lambda i, j: (i, j))
    if transposed:
        w_spec = pl.BlockSpec((bn, k), lambda i, j: (j, 0))
    else:
        w_spec = pl.BlockSpec((k, bn), lambda i, j: (0, j))
    return pl.pallas_call(
        functools.partial(_norm_matmul_kernel, transposed=transposed, last_cols=last_cols),
        grid=(m // bm, nj),
        in_specs=[pl.BlockSpec((bm, k), lambda i, j: (i, 0)),
                  pl.BlockSpec((1, k), lambda i, j: (0, 0)),
                  w_spec],
        out_specs=out_spec,
        out_shape=out_shape,
        scratch_shapes=[pltpu.VMEM((bm, k), BF16)],
        compiler_params=pltpu.CompilerParams(
            dimension_semantics=("arbitrary", "arbitrary"), vmem_limit_bytes=VMEM_LIMIT),
        name=name,
    )(x, nw.reshape(1, k), w)


def _mem_kv_kernel(x_ref, nw_ref, w_ref, o_ref):
    x = x_ref[...]
    ms = jnp.mean(x * x, axis=-1, keepdims=True)
    h = (x * lax.rsqrt(ms + EPS) * nw_ref[...]).astype(BF16)
    kv = _dot(h, w_ref[...].astype(BF16))
    o_ref[0] = kv[:, :D_X]
    o_ref[1] = kv[:, D_X:]


def _mem_kv(mem, nw, w, *, batch):
    return pl.pallas_call(
        _mem_kv_kernel,
        grid=(batch,),
        in_specs=[pl.BlockSpec((N_MEM, D_MODEL), lambda i: (i, 0)),
                  pl.BlockSpec((1, D_MODEL), lambda i: (0, 0)),
                  pl.BlockSpec((D_MODEL, 2 * D_X), lambda i: (0, 0))],
        out_specs=pl.BlockSpec((2, N_MEM, D_X), lambda i: (0, i, 0)),
        out_shape=jax.ShapeDtypeStruct((2, batch * N_MEM, D_X), F32),
        compiler_params=pltpu.CompilerParams(
            dimension_semantics=("arbitrary",), vmem_limit_bytes=VMEM_LIMIT),
        name="mem_kv",
    )(mem, nw.reshape(1, D_MODEL), w)


PREP_BN = MXU_COLS
N_MAIN_BLOCKS = C_BG // PREP_BN


GATE_BLOCK = D_PROJ_PAD // PREP_BN - 1
N_IN_BLOCKS = GATE_BLOCK + 1
N_OUT_BLOCKS = D_MODEL // PREP_BN


def _weight_prep_kernel(wt_hbm, wo_hbm, oi_ref, oo_ref, buf, sem):
    r = pl.program_id(0)
    slot = lax.rem(r, 2)

    def in_copy(rr, sl):
        start = pl.multiple_of(jnp.where(rr < N_MAIN_BLOCKS, rr * PREP_BN, rr * PREP_BN + GATE_RANK), 8)
        return pltpu.make_async_copy(wt_hbm.at[pl.ds(start, PREP_BN), :], buf.at[sl], sem.at[sl])

    def gate_copy(sl):
        return pltpu.make_async_copy(wt_hbm.at[pl.ds(C_BG, GATE_RANK), :],
                                     buf.at[sl, pl.ds(0, GATE_RANK), :], sem.at[sl])

    def out_copy(rr, sl):
        start = pl.multiple_of((rr - N_IN_BLOCKS) * PREP_BN, PREP_BN)
        return pltpu.make_async_copy(wo_hbm.at[pl.ds(start, PREP_BN), :], buf.at[sl], sem.at[sl])

    def start_fetch(rr, sl):
        @pl.when(rr < GATE_BLOCK)
        def _():
            in_copy(rr, sl).start()

        @pl.when(rr == GATE_BLOCK)
        def _():
            gate_copy(sl).start()

        @pl.when(rr > GATE_BLOCK)
        def _():
            out_copy(rr, sl).start()

    @pl.when(r == 0)
    def _():
        start_fetch(r, slot)

    @pl.when(r + 1 < pl.num_programs(0))
    def _():
        start_fetch(r + 1, 1 - slot)

    @pl.when(r < GATE_BLOCK)
    def _():
        in_copy(r, slot).wait()
        oi_ref[...] = buf[slot].T.astype(BF16)

    @pl.when(r == GATE_BLOCK)
    def _():
        gate_copy(slot).wait()
        rows = jnp.concatenate([buf[slot, 0:GATE_RANK, :],
                                jnp.zeros((PREP_BN - GATE_RANK, D_MODEL), F32)], axis=0)
        oi_ref[...] = rows.T.astype(BF16)

    @pl.when(r > GATE_BLOCK)
    def _():
        out_copy(r, slot).wait()
        oo_ref[...] = buf[slot].astype(BF16)


def _weight_prep(w_t, w_out):
    k = w_t.shape[1]
    return pl.pallas_call(
        _weight_prep_kernel,
        grid=(N_IN_BLOCKS + N_OUT_BLOCKS,),
        in_specs=[pl.BlockSpec(memory_space=pl.ANY), pl.BlockSpec(memory_space=pl.ANY)],
        out_specs=[pl.BlockSpec((k, PREP_BN), lambda r: (0, jnp.minimum(r, GATE_BLOCK))),
                   pl.BlockSpec((PREP_BN, D_MODEL), lambda r: (jnp.maximum(r - N_IN_BLOCKS, 0), 0))],
        out_shape=[jax.ShapeDtypeStruct((k, D_PROJ_PAD), BF16),
                   jax.ShapeDtypeStruct((D_MODEL, D_MODEL), BF16)],
        scratch_shapes=[pltpu.VMEM((2, PREP_BN, k), F32), pltpu.SemaphoreType.DMA((2,))],
        compiler_params=pltpu.CompilerParams(
            dimension_semantics=("arbitrary",), vmem_limit_bytes=VMEM_LIMIT),
        name="weight_prep",
    )(w_t, w_out)


def _out_proj_kernel(br_ref, w_ref, x_ref, fw_ref, y_ref):
    acc = _dot(br_ref[...], w_ref[...]) + x_ref[...]
    ms = jnp.mean(acc * acc, axis=-1, keepdims=True)
    y_ref[...] = acc * lax.rsqrt(ms + EPS) * fw_ref[...]


def _out_proj(br, w, x, fw, *, bm, name):
    m, k = br.shape
    n = w.shape[1]
    return pl.pallas_call(
        _out_proj_kernel,
        grid=(m // bm,),
        in_specs=[pl.BlockSpec((bm, k), lambda i: (i, 0)),
                  pl.BlockSpec((k, n), lambda i: (0, 0)),
                  pl.BlockSpec((bm, n), lambda i: (i, 0)),
                  pl.BlockSpec((1, n), lambda i: (0, 0))],
        out_specs=pl.BlockSpec((bm, n), lambda i: (i, 0)),
        out_shape=jax.ShapeDtypeStruct((m, n), F32),
        compiler_params=pltpu.CompilerParams(
            dimension_semantics=("arbitrary",), vmem_limit_bytes=VMEM_LIMIT),
        name=name,
    )(br, w, x, fw.reshape(1, n))


PB = 128
PT = 256


def _group_a_chunks(proj_ref, wa_ref, abt_ref, avw_ref):
    u = proj_ref[:, C_AU:C_AU + D_A].astype(F32)
    v = proj_ref[:, C_AV:C_AV + D_A].astype(F32)
    g = proj_ref[:, C_AG:C_AG + D_A].astype(F32)
    n_chunks = u.shape[0] // CHUNK_A
    ms = jnp.mean(v * v, axis=-1, keepdims=True)
    vb = (v * lax.rsqrt(ms + EPS) * avw_ref[...]).astype(BF16)
    tril = _iota((CHUNK_A, CHUNK_A), 1) <= _iota((CHUNK_A, CHUNK_A), 0)
    r = []
    for gi in range(H_A):
        w = jnp.where(tril, wa_ref[gi], 0.0).astype(BF16)
        ws = slice(WIN_START[gi], WIN_START[gi] + WIN)
        rhs = jnp.concatenate([vb[c * CHUNK_A:(c + 1) * CHUNK_A, ws] for c in range(n_chunks)], axis=1)
        res = _dot(w, rhs)
        r.append(jnp.concatenate([res[:, c * WIN:(c + 1) * WIN] for c in range(n_chunks)], axis=0))
    mixed = _merge_windows(r)
    col_g = _head_id(_iota((CHUNK_A, D_A), 1), HD_A, H_A)
    bias = jnp.zeros((CHUNK_A, D_A), F32)
    for gi in range(H_A):
        bias = jnp.where(col_g == gi, abt_ref[:, gi:gi + 1], bias)
    bias = jnp.concatenate([bias] * n_chunks, axis=0)
    return u * (mixed + bias) * _silu(g)


def _prompt_block(proj_ref, out_ref, mk_ref, mv_ref, wa_ref, abt_ref, avw_ref, bwa_ref, bba_ref,
                  onw_ref, sbd_ref, ind_ref, o_ref):
    out_ref[:, 0:D_A] = _group_a_chunks(proj_ref, wa_ref, abt_ref, avw_ref).astype(BF16)
    yield

    q = proj_ref[:, C_BQ:C_BQ + D_BK].astype(F32)
    k = proj_ref[:, C_BK:C_BK + D_BK].astype(F32)
    vb = proj_ref[:, C_BV:C_BV + D_B]
    pre = _dot(proj_ref[:, C_BR:C_BR + LANES], bwa_ref[...]) + bba_ref[...]
    log_a = _log_sigmoid(pre) * (1.0 / GATE_TAU)
    r64 = _iota((CHUNK_B, CHUNK_B), 0)
    c64 = _iota((CHUNK_B, CHUNK_B), 1)
    tril = c64 <= r64
    tril_bf = jnp.where(tril, 1.0, 0.0).astype(BF16)
    row_pad = jnp.zeros((LANES - CHUNK_B, D_BK), F32)
    v_pad = jnp.zeros((LANES - CHUNK_B, WIN), BF16)
    win_col = _iota((CHUNK_B, WIN), 1)
    col_hq = _head_id(_iota((CHUNK_B, D_BK), 1), DK_B, H_B)
    tril_heads = (_iota((H_B * CHUNK_B, CHUNK_B), 1)
                  <= jnp.bitwise_and(_iota((H_B * CHUNK_B, CHUNK_B), 0), CHUNK_B - 1))
    for c in range(PT // CHUNK_B):
        sl = slice(c * CHUNK_B, (c + 1) * CHUNK_B)
        hi, lo = _split(log_a[sl])
        cum2 = _dot(tril_bf, jnp.concatenate([hi, lo], axis=1))
        cum = cum2[:, :D_BK] + cum2[:, D_BK:]
        tot_row = cum[CHUNK_B - 1:CHUNK_B, :]
        la_t = jnp.concatenate([log_a[sl], row_pad], axis=0).T
        q_in = q[sl] * (DK_B ** -0.5) * jnp.exp(cum)
        k_in = k[sl] * jnp.exp(-cum)
        k_out = k[sl] * jnp.exp(tot_row - cum)
        q_heads = jnp.concatenate([jnp.where(col_hq == h, q_in, 0.0) for h in range(H_B)], axis=0)
        sc = _dot_nt(q_heads.astype(BF16), k_in.astype(BF16))
        sc = jnp.where(tril_heads, sc, 0.0).astype(BF16)
        o_intra = _merge_windows(
            [_dot(sc[h * CHUNK_B:(h + 1) * CHUNK_B], vb[sl, WIN_START[h]:WIN_START[h] + WIN])
             for h in range(H_B)])
        s_old = sbd_ref[...]
        o_inter = _dot(q_in.astype(BF16), s_old.astype(BF16))
        kot = jnp.concatenate([k_out, row_pad], axis=0).T.astype(BF16)
        for h in range(H_B):
            rs = slice(h * DK_B, (h + 1) * DK_B)
            ws = slice(WIN_START[h], WIN_START[h] + WIN)
            dec = jnp.exp(jnp.sum(la_t[rs], axis=1, keepdims=True))
            lo_col = h * DV_B - WIN_START[h]
            in_head = (win_col >= lo_col) & (win_col < lo_col + DV_B)
            v_h = jnp.where(in_head, vb[sl, ws], jnp.zeros((), BF16))
            kv = _dot(kot[rs], jnp.concatenate([v_h, v_pad], axis=0))
            sbd_ref[rs, ws] = s_old[rs, ws] * dec + kv
        o_ref[sl, :] = o_intra + o_inter
        yield
    o = o_ref[...]
    ssq = _dot((o * o).astype(BF16), ind_ref[...])
    o_n = o * lax.rsqrt(ssq * (1.0 / DV_B) + EPS) * onw_ref[...]
    bg = proj_ref[:, C_BG:C_BG + D_B].astype(F32)
    out_ref[:, D_A:D_A + D_B] = (o_n * _silu(bg)).astype(BF16)
    yield

    for h in range(H_X):
        hs = slice(h * HD_X, (h + 1) * HD_X)
        qh = proj_ref[:, C_XQ + h * HD_X:C_XQ + (h + 1) * HD_X]
        s = _dot_nt(qh, mk_ref[:, hs].astype(BF16)) * (HD_X ** -0.5)
        e = jnp.exp(s - jnp.max(s, axis=-1, keepdims=True))
        den = jnp.sum(e, axis=-1, keepdims=True)
        ox = _dot(e.astype(BF16), mv_ref[:, hs].astype(BF16)) / den
        xg = proj_ref[:, C_XG + h * HD_X:C_XG + (h + 1) * HD_X].astype(F32)
        out_ref[:, D_A + D_B + h * HD_X:D_A + D_B + (h + 1) * HD_X] = (ox * _silu(xg)).astype(BF16)
        if h % 2 == 1:
            yield


def _prompt_layer_kernel(xn_ref, wi_ref, wo_ref, nw_ref, fw_ref, mk_ref, mv_ref, wa_ref, abt_ref,
                         avw_ref, bwa_ref, bba_ref, onw_ref, y_ref, st_ref,
                         pa_ref, pb_ref, xk_ref, br_ref, h_ref, sbd_ref, ind_ref, o_ref, *, nt):
    s = pl.program_id(0)
    cur = jnp.maximum(s - 1, 0)
    t = lax.rem(cur, nt)

    @pl.when(s == 0)
    def _():
        r2 = _head_id(_iota((D_B, D_B), 0), DV_B, H_B)
        c2 = _head_id(_iota((D_B, D_B), 1), DV_B, H_B)
        ind_ref[...] = jnp.where(r2 == c2, 1.0, 0.0).astype(BF16)
        pb_ref[...] = jnp.zeros_like(pb_ref)
        xk_ref[...] = jnp.zeros_like(xk_ref)

    @pl.when(t == 0)
    def _():
        sbd_ref[...] = jnp.zeros_like(sbd_ref)

    def in_proj_stages(pn_ref):
        x = xn_ref[...]
        ms = jnp.mean(x * x, axis=-1, keepdims=True)
        h_ref[...] = (x * lax.rsqrt(ms + EPS) * nw_ref[...]).astype(BF16)
        yield
        for c0 in range(0, D_PROJ_PAD, IN_BN):
            n = min(IN_BN, D_PROJ_PAD - c0)
            pn_ref[:, c0:c0 + n] = _dot(h_ref[...], wi_ref[:, c0:c0 + n]).astype(BF16)
            yield

    def out_proj_stages():
        ssq = jnp.zeros((PT, 1), F32)
        for c0 in range(0, D_MODEL, OUT_BN):
            cols = pl.ds(c0, OUT_BN)
            acc = _dot(br_ref[...], wo_ref[:, cols]) + xk_ref[:, cols]
            y_ref[:, cols] = acc
            ssq = ssq + jnp.sum(acc * acc, axis=-1, keepdims=True)
            yield
        y_ref[...] = y_ref[...] * lax.rsqrt(ssq * (1.0 / D_MODEL) + EPS) * fw_ref[...]
        yield

    def body(pn_ref, pc_ref):
        streams = {"P": in_proj_stages(pn_ref),
                   "M": _prompt_block(pc_ref, br_ref, mk_ref, mv_ref, wa_ref, abt_ref, avw_ref,
                                      bwa_ref, bba_ref, onw_ref, sbd_ref, ind_ref, o_ref),
                   "O": out_proj_stages()}
        order = "P P M P M P M P M P M P M P M P M O O O O O".split()
        for name in order:
            next(streams[name])
        for name, gen in streams.items():
            assert next(gen, "done") == "done", name
        xk_ref[...] = xn_ref[...]

    @pl.when(lax.rem(s, 2) == 0)
    def _():
        body(pa_ref, pb_ref)

    @pl.when(lax.rem(s, 2) == 1)
    def _():
        body(pb_ref, pa_ref)

    @pl.when((t == nt - 1) & (s > 0))
    def _():
        for h in range(H_B):
            off = h * DV_B - WIN_START[h]
            blk = sbd_ref[h * DK_B:(h + 1) * DK_B, WIN_START[h]:WIN_START[h] + WIN]
            if off:
                blk = pltpu.roll(blk, WIN - off, 1)
            st_ref[0, h] = blk[:, :DV_B]


def _prompt_layer(xp, w_in_bf, w_out_bf, nw, fw, memkv, wa, abt, avw, bwa, bba, onw, *, batch, seq):
    nt = seq // PT
    nblk = batch * nt
    cur = lambda s: jnp.maximum(s - 1, 0)
    const = lambda *shape: pl.BlockSpec(shape, lambda s: (0,) * len(shape))
    resident = lambda *shape: pl.BlockSpec(shape, lambda s: (0,) * len(shape),
                                           pipeline_mode=pl.Buffered(1))
    return pl.pallas_call(
        functools.partial(_prompt_layer_kernel, nt=nt),
        grid=(nblk + 1,),
        in_specs=[pl.BlockSpec((PT, D_MODEL), lambda s: (jnp.minimum(s, nblk - 1), 0)),
                  resident(D_MODEL, D_PROJ_PAD), resident(D_MODEL, D_MODEL),
                  const(1, D_MODEL), const(1, D_MODEL),
                  pl.BlockSpec((None, N_MEM, D_X), lambda s: (0, cur(s) // nt, 0)),
                  pl.BlockSpec((None, N_MEM, D_X), lambda s: (1, cur(s) // nt, 0)),
                  const(H_A, CHUNK_A, CHUNK_A), const(CHUNK_A, H_A), const(1, D_A),
                  const(LANES, D_BK), const(1, D_BK), const(1, D_B)],
        out_specs=[pl.BlockSpec((PT, D_MODEL), lambda s: (cur(s), 0)),
                   pl.BlockSpec((1, H_B, DK_B, DV_B), lambda s: (cur(s) // nt, 0, 0, 0))],
        out_shape=[jax.ShapeDtypeStruct((batch * seq, D_MODEL), F32),
                   jax.ShapeDtypeStruct((batch, H_B, DK_B, DV_B), F32)],
        scratch_shapes=[pltpu.VMEM((PT, D_PROJ_PAD), BF16),
                        pltpu.VMEM((PT, D_PROJ_PAD), BF16),
                        pltpu.VMEM((PT, D_MODEL), F32),
                        pltpu.VMEM((PT, D_MODEL), BF16),
                        pltpu.VMEM((PT, D_MODEL), BF16),
                        pltpu.VMEM((D_BK, D_B), F32),
                        pltpu.VMEM((D_B, D_B), BF16),
                        pltpu.VMEM((PT, D_B), F32)],
        compiler_params=pltpu.CompilerParams(
            dimension_semantics=("arbitrary",), vmem_limit_bytes=VMEM_LIMIT),
        name="prompt_layer",
    )(xp, w_in_bf, w_out_bf, nw.reshape(1, D_MODEL), fw.reshape(1, D_MODEL), memkv, memkv,
      wa, abt, avw, bwa, bba, onw)


NS = 8
TS = 8
SB = NS * TS
SBP = 128


def _sample_mixer_kernel(proj_ref, st_ref, ck_ref, cv_ref, wa_ref, abt_ref, avw_ref, bwa_ref,
                         bba_ref, onw_ref, out_ref, stn_ref, cvs_ref,
                         qin_ref, xq_ref, kot_ref, lath_ref, latl_ref, vhm_ref, ghm_ref, ohm_ref,
                         ox_ref):
    ri = _iota((SB, SB), 0)
    ci = _iota((SB, SB), 1)
    same_seq = jnp.right_shift(ri, 3) == jnp.right_shift(ci, 3)
    causal = same_seq & (ci <= ri)

    a_br, vn = _group_a(proj_ref, wa_ref, abt_ref, avw_ref, causal)
    out_ref[:, 0:D_A] = a_br.astype(BF16)
    cvs_ref[...] = vn

    q = proj_ref[:, C_BQ:C_BQ + D_BK].astype(F32)
    k = proj_ref[:, C_BK:C_BK + D_BK].astype(F32)
    vb = proj_ref[:, C_BV:C_BV + D_B]
    bgb = proj_ref[:, C_BG:C_BG + D_B]
    pre = _dot(proj_ref[:, C_BR:C_BR + LANES], bwa_ref[...]) + bba_ref[...]
    log_a = _log_sigmoid(pre) * (1.0 / GATE_TAU)
    causal_bf = jnp.where(causal, 1.0, 0.0).astype(BF16)
    seq_bf = jnp.where(same_seq, 1.0, 0.0).astype(BF16)
    hi, lo = _split(log_a)
    cum = _dot(causal_bf, hi) + _dot(causal_bf, lo)
    tot = _dot(seq_bf, hi) + _dot(seq_bf, lo)
    q_in = q * (DK_B ** -0.5) * jnp.exp(cum)
    k_in = k * jnp.exp(-cum)
    k_out = k * jnp.exp(tot - cum)
    qin_ref[...] = q_in
    zpad = jnp.zeros((SBP - SB, D_BK), F32)
    kot_ref[...] = jnp.concatenate([k_out, zpad], axis=0).T.astype(BF16)
    lat_hi, lat_lo = _split(jnp.concatenate([log_a, zpad], axis=0).T)
    lath_ref[...] = lat_hi
    latl_ref[...] = lat_lo
    xq_ref[...] = proj_ref[:, C_XQ:C_XQ + D_X].astype(F32)

    wins = _gla_intra_windows(q_in, k_in, vb, causal)
    sel_r = _iota((D_B, DV_B), 0)
    sel_c = _iota((D_B, DV_B), 1)
    vhm_ref[...] = jnp.zeros_like(vhm_ref)
    for h in range(H_B):
        off = h * DV_B - WIN_START[h]
        w = wins[h]
        if off:
            w = pltpu.roll(w, WIN - off, 1)
        ohm_ref[h] = w[:, :DV_B]
        sel = jnp.where(sel_r == sel_c + h * DV_B, 1.0, 0.0).astype(BF16)
        vhm_ref[h, 0:SB, :] = _dot(vb, sel)
        ghm_ref[h] = _dot(bgb, sel)

    mask_x = (jnp.right_shift(_iota((H_X * TS, N_MEM * H_X), 0), 3)
              == jnp.bitwise_and(_iota((H_X * TS, N_MEM * H_X), 1), H_X - 1))
    mask_b = jnp.right_shift(_iota((H_B * TS, D_BK), 0), 3) == _head_id(_iota((H_B * TS, D_BK), 1), DK_B, H_B)
    row_seq = jnp.right_shift(_iota((SBP, DV_B), 0), 3)

    def per_seq(s, carry):
        r0 = pl.multiple_of(s * TS, TS)
        q8 = xq_ref[pl.ds(r0, TS), :]
        q32 = jnp.concatenate([q8[:, h * HD_X:(h + 1) * HD_X] for h in range(H_X)], axis=0)
        sc = _dot_nt(q32.astype(BF16), ck_ref[s].astype(BF16)) * (HD_X ** -0.5)
        sc = jnp.where(mask_x, sc, -1e30)
        e = jnp.exp(sc - jnp.max(sc, axis=-1, keepdims=True))
        den = jnp.sum(e, axis=-1, keepdims=True)
        o = _dot(e.astype(BF16), cv_ref[s].astype(BF16)) / den
        ox_ref[pl.ds(r0, TS), :] = jnp.concatenate(
            [o[h * TS:(h + 1) * TS] for h in range(H_X)], axis=1)
        qi8 = qin_ref[pl.ds(r0, TS), :]
        qbd2 = jnp.where(mask_b, jnp.concatenate([qi8] * H_B, axis=0), 0.0).astype(BF16)
        s0 = jnp.concatenate([st_ref[s, h] for h in range(H_B)], axis=0)
        o_inter = _dot(qbd2, s0.astype(BF16))
        in_seq = row_seq == s
        for h in range(H_B):
            ohm_ref[h, pl.ds(r0, TS), :] += o_inter[h * TS:(h + 1) * TS]
        row_sel = jnp.where(in_seq, 1.0, 0.0).astype(BF16)
        dec = jnp.exp(_dot(lath_ref[...], row_sel) + _dot(latl_ref[...], row_sel))
        kot = kot_ref[...]
        kv = jnp.concatenate(
            [_dot(kot[h * DK_B:(h + 1) * DK_B],
                  jnp.where(in_seq, vhm_ref[h], 0.0).astype(BF16)) for h in range(H_B)], axis=0)
        s_new = s0 * dec + kv
        for h in range(H_B):
            stn_ref[s, h] = s_new[h * DK_B:(h + 1) * DK_B]
        return carry

    lax.fori_loop(0, NS, per_seq, 0)

    selt_r = _iota((DV_B, D_B), 0)
    selt_c = _iota((DV_B, D_B), 1)
    b_br = jnp.zeros((SB, D_B), F32)
    for h in range(H_B):
        o_h = ohm_ref[h]
        ms = jnp.mean(o_h * o_h, axis=-1, keepdims=True)
        ob = (o_h * lax.rsqrt(ms + EPS) * onw_ref[...] * _silu(ghm_ref[h])).astype(BF16)
        selt = jnp.where(selt_c == selt_r + h * DV_B, 1.0, 0.0).astype(BF16)
        b_br = b_br + _dot(ob, selt)
    out_ref[:, D_A:D_A + D_B] = b_br.astype(BF16)

    xg = proj_ref[:, C_XG:C_XG + D_X].astype(F32)
    out_ref[:, D_A + D_B:D_MODEL] = (ox_ref[...] * _silu(xg)).astype(BF16)


def _sample_mixer(proj, state, ck, cv, wa, abt, avw, bwa, bba, onw):
    nseq = state.shape[1]
    const = lambda *shape: pl.BlockSpec(shape, lambda i: (0,) * len(shape))
    return pl.pallas_call(
        _sample_mixer_kernel,
        grid=(nseq // NS,),
        in_specs=[pl.BlockSpec((SB, D_PROJ), lambda i: (i, 0)),
                  pl.BlockSpec((None, NS, H_B, DK_B, DV_B), lambda i: (0, i, 0, 0, 0)),
                  pl.BlockSpec((NS, N_MEM * H_X, HD_X), lambda i: (i, 0, 0)),
                  pl.BlockSpec((NS, N_MEM * H_X, HD_X), lambda i: (i, 0, 0)),
                  const(H_A, SB, SB), const(SB, H_A), const(1, D_A),
                  const(LANES, D_BK), const(1, D_BK), const(1, DV_B)],
        out_specs=[pl.BlockSpec((SB, D_MODEL), lambda i: (i, 0)),
                   pl.BlockSpec((None, NS, H_B, DK_B, DV_B), lambda i: (0, i, 0, 0, 0)),
                   pl.BlockSpec((SB, D_A), lambda i: (i, 0))],
        out_shape=[jax.ShapeDtypeStruct((nseq * TS, D_MODEL), BF16),
                   jax.ShapeDtypeStruct((1, nseq, H_B, DK_B, DV_B), F32),
                   jax.ShapeDtypeStruct((nseq * TS, D_A), F32)],
        scratch_shapes=[pltpu.VMEM((SB, D_BK), F32),
                        pltpu.VMEM((SB, D_X), F32),
                        pltpu.VMEM((D_BK, SBP), BF16),
                        pltpu.VMEM((D_BK, SBP), BF16),
                        pltpu.VMEM((D_BK, SBP), BF16),
                        pltpu.VMEM((H_B, SBP, DV_B), F32),
                        pltpu.VMEM((H_B, SB, DV_B), F32),
                        pltpu.VMEM((H_B, SB, DV_B), F32),
                        pltpu.VMEM((SB, D_X), F32)],
        compiler_params=pltpu.CompilerParams(
            dimension_semantics=("arbitrary",), vmem_limit_bytes=VMEM_LIMIT),
        name="sample_mixer",
    )(proj, state, ck, cv, wa, abt, avw, bwa, bba, onw)


def kernel(x_prompt, x_sample, mem_prompt, state_gla, cache_mem_k, cache_mem_v, norm_w, w_in,
           a_vnorm_w, a_ws, a_bs, b_wa, b_ba, b_onorm_w, mem_norm_w, w_mem_kv, w_out, final_norm_w):
    batch, seq, _ = x_prompt.shape
    nseq, tdec, _ = x_sample.shape
    depth = w_in.shape[0]
    assert depth == 1 and tdec == TS and seq % PT == 0 and nseq % NS == 0

    w_in_bf, w_out_bf = _weight_prep(jnp.transpose(w_in[0]), w_out[0])
    last_cols = D_PROJ_PAD - (D_PROJ_PAD // IN_BN) * IN_BN
    bwa = jnp.concatenate([b_wa[0], jnp.zeros((LANES - GATE_RANK, D_BK), F32)], axis=0).astype(BF16)
    bba = b_ba[0].reshape(1, D_BK)
    avw = a_vnorm_w[0].reshape(1, D_A)
    onw_p = jnp.tile(b_onorm_w[0], H_B).reshape(1, D_B)
    onw_s = b_onorm_w[0].reshape(1, DV_B)
    wa_p = a_ws[0]
    abt_p = a_bs[0].T
    wa_s = jnp.tile(a_ws[0][:, :TS, :TS], (1, NS, NS))
    abt_s = jnp.tile(a_bs[0][:, :TS], (1, NS)).T

    xp = x_prompt.reshape(batch * seq, D_MODEL)
    xs = x_sample.reshape(nseq * TS, D_MODEL)
    mem = mem_prompt.reshape(batch * N_MEM, D_MODEL)

    memkv = _mem_kv(mem, mem_norm_w[0], w_mem_kv[0], batch=batch)
    y_p, st_p = _prompt_layer(xp, w_in_bf, w_out_bf, norm_w[0], final_norm_w, memkv, wa_p, abt_p,
                              avw, bwa, bba, onw_p, batch=batch, seq=seq)

    proj_s = _norm_matmul(xs, norm_w[0], w_in_bf, D_PROJ, bm=nseq * TS, bn=IN_BN, transposed=False,
                          out_dtype=BF16, last_cols=last_cols, name="in_proj_s")
    br_s, st_s, cvs = _sample_mixer(
        proj_s, state_gla,
        cache_mem_k.reshape(nseq, N_MEM * H_X, HD_X), cache_mem_v.reshape(nseq, N_MEM * H_X, HD_X),
        wa_s, abt_s, avw, bwa, bba, onw_s)
    y_s = _out_proj(br_s, w_out_bf, xs, final_norm_w, bm=512, name="out_proj_s")

    return (y_p.reshape(batch, seq, D_MODEL),
            y_s.reshape(nseq, TS, D_MODEL),
            memkv[0].reshape(1, batch, N_MEM, H_X, HD_X),
            memkv[1].reshape(1, batch, N_MEM, H_X, HD_X),
            st_p.reshape(1, batch, H_B, DK_B, DV_B),
            st_s,
            cvs.reshape(1, nseq, TS, D_A))
```

```python
import functools

import jax
import jax.numpy as jnp
from jax import lax
from jax.experimental import pallas as pl
from jax.experimental.pallas import tpu as pltpu

F32 = jnp.float32
BF16 = jnp.bfloat16

D_MODEL = 2048
D_A = 768
H_A = 4
HD_A = 192
CHUNK_A = 128
D_B = 768
H_B = 4
DV_B = 192
DK_B = 96
D_BK = 384
GATE_RANK = 16
GATE_TAU = 16.0
CHUNK_B = 64
D_X = 512
H_X = 4
HD_X = 128
N_MEM = 256
EPS = 1e-6

LANES = 128
MXU_COLS = 256
C_AU, C_AV, C_AG = 0, 768, 1536
C_BQ, C_BK, C_BV, C_BG = 2304, 2688, 3072, 3840
C_XQ, C_XG, C_BR = 4608, 5120, 5632
D_PROJ = 5760
D_PROJ_PAD = 5888
WIN_START = (0, 128, 384, 512)
WIN = 256
IN_BN = 768
OUT_BN = 512

VMEM_LIMIT = 60 * 1024 * 1024


def _dot(a, b):
    return jnp.dot(a, b, preferred_element_type=F32)


def _dot_nt(a, b):
    return lax.dot_general(a, b, (((1,), (1,)), ((), ())), preferred_element_type=F32)


def _dot_tn(a, b):
    return lax.dot_general(a, b, (((0,), (0,)), ((), ())), preferred_element_type=F32)


def _split(x):
    hi = x.astype(BF16)
    lo = (x - hi.astype(F32)).astype(BF16)
    return hi, lo


def _silu(x):
    return x / (1.0 + jnp.exp(-x))


def _log_sigmoid(x):
    return jnp.minimum(x, 0.0) - jnp.log1p(jnp.exp(-jnp.abs(x)))


def _head_id(idx, width, n):
    h = jnp.zeros_like(idx)
    for i in range(1, n):
        h = h + (idx >= i * width).astype(jnp.int32)
    return h


def _iota(shape, dim):
    return lax.broadcasted_iota(jnp.int32, shape, dim)


def _merge_windows(r):
    rows = r[0].shape[0]
    lo = _iota((rows, LANES), 1) < 64
    tiles = [r[0][:, :LANES], jnp.where(lo, r[0][:, LANES:], r[1][:, :LANES]), r[1][:, LANES:],
             r[2][:, :LANES], jnp.where(lo, r[2][:, LANES:], r[3][:, :LANES]), r[3][:, LANES:]]
    return jnp.concatenate(tiles, axis=1)


def _group_a(proj_ref, wa_ref, abt_ref, avw_ref, mask):
    u = proj_ref[:, C_AU:C_AU + D_A].astype(F32)
    v = proj_ref[:, C_AV:C_AV + D_A].astype(F32)
    g = proj_ref[:, C_AG:C_AG + D_A].astype(F32)
    rows = u.shape[0]
    ms = jnp.mean(v * v, axis=-1, keepdims=True)
    vn = v * lax.rsqrt(ms + EPS) * avw_ref[...]
    vb = vn.astype(BF16)
    r = []
    for gi in range(H_A):
        w = jnp.where(mask, wa_ref[gi], 0.0).astype(BF16)
        r.append(_dot(w, vb[:, WIN_START[gi]:WIN_START[gi] + WIN]))
    mixed = _merge_windows(r)
    col_g = _head_id(_iota((rows, D_A), 1), HD_A, H_A)
    bias = jnp.zeros((rows, D_A), F32)
    for gi in range(H_A):
        bias = jnp.where(col_g == gi, abt_ref[:, gi:gi + 1], bias)
    return u * (mixed + bias) * _silu(g), vn


def _gla_intra_windows(q_in, k_in, vb, mask):
    rows = q_in.shape[0]
    col_h = _head_id(_iota((rows, D_BK), 1), DK_B, H_B)
    kb = k_in.astype(BF16)
    r = []
    for h in range(H_B):
        qh = jnp.where(col_h == h, q_in, 0.0).astype(BF16)
        s = jnp.where(mask, _dot_nt(qh, kb), 0.0).astype(BF16)
        r.append(_dot(s, vb[:, WIN_START[h]:WIN_START[h] + WIN]))
    return r


def _norm_matmul_kernel(x_ref, nw_ref, w_ref, o_ref, h_ref, *, transposed, last_cols):
    j = pl.program_id(1)
    nj = pl.num_programs(1)

    @pl.when(j == 0)
    def _():
        x = x_ref[...]
        ms = jnp.mean(x * x, axis=-1, keepdims=True)
        h_ref[...] = (x * lax.rsqrt(ms + EPS) * nw_ref[...]).astype(BF16)

    mm = _dot_nt if transposed else _dot
    if last_cols is None:
        o_ref[...] = mm(h_ref[...], w_ref[...].astype(BF16)).astype(o_ref.dtype)
    else:
        @pl.when(j < nj - 1)
        def _():
            o_ref[...] = mm(h_ref[...], w_ref[...].astype(BF16)).astype(o_ref.dtype)

        @pl.when(j == nj - 1)
        def _():
            w = w_ref[:last_cols, :] if transposed else w_ref[:, :last_cols]
            o_ref[:, :last_cols] = mm(h_ref[...], w.astype(BF16)).astype(o_ref.dtype)


def _norm_matmul(x, nw, w, n_out, *, bm, bn, transposed, out_dtype, split_out=False,
                 last_cols=None, name):
    m, k = x.shape
    n = w.shape[0] if transposed else w.shape[1]
    nj = -(-n // bn)
    assert m % bm == 0
    if split_out:
        out_shape = jax.ShapeDtypeStruct((nj, m, bn), out_dtype)
        out_spec = pl.BlockSpec((None, bm, bn), lambda i, j: (j, i, 0))
    else:
        out_shape = jax.ShapeDtypeStruct((m, n_out), out_dtype)
        out_spec = pl.BlockSpec((bm, bn), lambda i, j: (i, j))
    if transposed:
        w_spec = pl.BlockSpec((bn, k), lambda i, j: (j, 0))
    else:
        w_spec = pl.BlockSpec((k, bn), lambda i, j: (0, j))
    return pl.pallas_call(
        functools.partial(_norm_matmul_kernel, transposed=transposed, last_cols=last_cols),
        grid=(m // bm, nj),
        in_specs=[pl.BlockSpec((bm, k), lambda i, j: (i, 0)),
                  pl.BlockSpec((1, k), lambda i, j: (0, 0)),
                  w_spec],
        out_specs=out_spec,
        out_shape=out_shape,
        scratch_shapes=[pltpu.VMEM((bm, k), BF16)],
        compiler_params=pltpu.CompilerParams(
            dimension_semantics=("arbitrary", "arbitrary"), vmem_limit_bytes=VMEM_LIMIT),
        name=name,
    )(x, nw.reshape(1, k), w)


def _mem_kv_kernel(x_ref, nw_ref, w_ref, o_ref):
    x = x_ref[...]
    ms = jnp.mean(x * x, axis=-1, keepdims=True)
    h = (x * lax.rsqrt(ms + EPS) * nw_ref[...]).astype(BF16)
    kv = _dot(h, w_ref[...].astype(BF16))
    o_ref[0] = kv[:, :D_X]
    o_ref[1] = kv[:, D_X:]


def _mem_kv(mem, nw, w, *, batch):
    return pl.pallas_call(
        _mem_kv_kernel,
        grid=(batch,),
        in_specs=[pl.BlockSpec((N_MEM, D_MODEL), lambda i: (i, 0)),
                  pl.BlockSpec((1, D_MODEL), lambda i: (0, 0)),
                  pl.BlockSpec((D_MODEL, 2 * D_X), lambda i: (0, 0))],
        out_specs=pl.BlockSpec((2, N_MEM, D_X), lambda i: (0, i, 0)),
        out_shape=jax.ShapeDtypeStruct((2, batch * N_MEM, D_X), F32),
        compiler_params=pltpu.CompilerParams(
            dimension_semantics=("arbitrary",), vmem_limit_bytes=VMEM_LIMIT),
        name="mem_kv",
    )(mem, nw.reshape(1, D_MODEL), w)


PREP_BN = MXU_COLS
N_MAIN_BLOCKS = C_BG // PREP_BN


GATE_BLOCK = D_PROJ_PAD // PREP_BN - 1
N_IN_BLOCKS = GATE_BLOCK + 1
N_OUT_BLOCKS = D_MODEL // PREP_BN


def _weight_prep_kernel(wt_hbm, wo_hbm, oi_ref, oo_ref, buf, sem):
    r = pl.program_id(0)
    slot = lax.rem(r, 2)

    def in_copy(rr, sl):
        start = pl.multiple_of(jnp.where(rr < N_MAIN_BLOCKS, rr * PREP_BN, rr * PREP_BN + GATE_RANK), 8)
        return pltpu.make_async_copy(wt_hbm.at[pl.ds(start, PREP_BN), :], buf.at[sl], sem.at[sl])

    def gate_copy(sl):
        return pltpu.make_async_copy(wt_hbm.at[pl.ds(C_BG, GATE_RANK), :],
                                     buf.at[sl, pl.ds(0, GATE_RANK), :], sem.at[sl])

    def out_copy(rr, sl):
        start = pl.multiple_of((rr - N_IN_BLOCKS) * PREP_BN, PREP_BN)
        return pltpu.make_async_copy(wo_hbm.at[pl.ds(start, PREP_BN), :], buf.at[sl], sem.at[sl])

    def start_fetch(rr, sl):
        @pl.when(rr < GATE_BLOCK)
        def _():
            in_copy(rr, sl).start()

        @pl.when(rr == GATE_BLOCK)
        def _():
            gate_copy(sl).start()

        @pl.when(rr > GATE_BLOCK)
        def _():
            out_copy(rr, sl).start()

    @pl.when(r == 0)
    def _():
        start_fetch(r, slot)

    @pl.when(r + 1 < pl.num_programs(0))
    def _():
        start_fetch(r + 1, 1 - slot)

    @pl.when(r < GATE_BLOCK)
    def _():
        in_copy(r, slot).wait()
        oi_ref[...] = buf[slot].T.astype(BF16)

    @pl.when(r == GATE_BLOCK)
    def _():
        gate_copy(slot).wait()
        rows = jnp.concatenate([buf[slot, 0:GATE_RANK, :],
                                jnp.zeros((PREP_BN - GATE_RANK, D_MODEL), F32)], axis=0)
        oi_ref[...] = rows.T.astype(BF16)

    @pl.when(r > GATE_BLOCK)
    def _():
        out_copy(r, slot).wait()
        oo_ref[...] = buf[slot].astype(BF16)


def _weight_prep(w_t, w_out):
    k = w_t.shape[1]
    return pl.pallas_call(
        _weight_prep_kernel,
        grid=(N_IN_BLOCKS + N_OUT_BLOCKS,),
        in_specs=[pl.BlockSpec(memory_space=pl.ANY), pl.BlockSpec(memory_space=pl.ANY)],
        out_specs=[pl.BlockSpec((k, PREP_BN), lambda r: (0, jnp.minimum(r, GATE_BLOCK))),
                   pl.BlockSpec((PREP_BN, D_MODEL), lambda r: (jnp.maximum(r - N_IN_BLOCKS, 0), 0))],
        out_shape=[jax.ShapeDtypeStruct((k, D_PROJ_PAD), BF16),
                   jax.ShapeDtypeStruct((D_MODEL, D_MODEL), BF16)],
        scratch_shapes=[pltpu.VMEM((2, PREP_BN, k), F32), pltpu.SemaphoreType.DMA((2,))],
        compiler_params=pltpu.CompilerParams(
            dimension_semantics=("arbitrary",), vmem_limit_bytes=VMEM_LIMIT),
        name="weight_prep",
    )(w_t, w_out)


def _out_proj_kernel(br_ref, w_ref, x_ref, fw_ref, y_ref):
    acc = _dot(br_ref[...], w_ref[...]) + x_ref[...]
    ms = jnp.mean(acc * acc, axis=-1, keepdims=True)
    y_ref[...] = acc * lax.rsqrt(ms + EPS) * fw_ref[...]


def _out_proj(br, w, x, fw, *, bm, name):
    m, k = br.shape
    n = w.shape[1]
    return pl.pallas_call(
        _out_proj_kernel,
        grid=(m // bm,),
        in_specs=[pl.BlockSpec((bm, k), lambda i: (i, 0)),
                  pl.BlockSpec((k, n), lambda i: (0, 0)),
                  pl.BlockSpec((bm, n), lambda i: (i, 0)),
                  pl.BlockSpec((1, n), lambda i: (0, 0))],
        out_specs=pl.BlockSpec((bm, n), lambda i: (i, 0)),
        out_shape=jax.ShapeDtypeStruct((m, n), F32),
        compiler_params=pltpu.CompilerParams(
            dimension_semantics=("arbitrary",), vmem_limit_bytes=VMEM_LIMIT),
        name=name,
    )(br, w, x, fw.reshape(1, n))


PB = 128
PT = 256


def _group_a_chunks(proj_ref, wa_ref, abt_ref, avw_ref):
    u = proj_ref[:, C_AU:C_AU + D_A].astype(F32)
    v = proj_ref[:, C_AV:C_AV + D_A].astype(F32)
    g = proj_ref[:, C_AG:C_AG + D_A].astype(F32)
    n_chunks = u.shape[0] // CHUNK_A
    ms = jnp.mean(v * v, axis=-1, keepdims=True)
    vb = (v * lax.rsqrt(ms + EPS) * avw_ref[...]).astype(BF16)
    tril = _iota((CHUNK_A, CHUNK_A), 1) <= _iota((CHUNK_A, CHUNK_A), 0)
    r = []
    for gi in range(H_A):
        w = jnp.where(tril, wa_ref[gi], 0.0).astype(BF16)
        ws = slice(WIN_START[gi], WIN_START[gi] + WIN)
        rhs = jnp.concatenate([vb[c * CHUNK_A:(c + 1) * CHUNK_A, ws] for c in range(n_chunks)], axis=1)
        res = _dot(w, rhs)
        r.append(jnp.concatenate([res[:, c * WIN:(c + 1) * WIN] for c in range(n_chunks)], axis=0))
    mixed = _merge_windows(r)
    col_g = _head_id(_iota((CHUNK_A, D_A), 1), HD_A, H_A)
    bias = jnp.zeros((CHUNK_A, D_A), F32)
    for gi in range(H_A):
        bias = jnp.where(col_g == gi, abt_ref[:, gi:gi + 1], bias)
    bias = jnp.concatenate([bias] * n_chunks, axis=0)
    return u * (mixed + bias) * _silu(g)


def _prompt_block(proj_ref, out_ref, mk_ref, mv_ref, wa_ref, abt_ref, avw_ref, bwa_ref, bba_ref,
                  onw_ref, sbd_ref, ind_ref, o_ref):
    out_ref[:, 0:D_A] = _group_a_chunks(proj_ref, wa_ref, abt_ref, avw_ref).astype(BF16)
    yield

    q = proj_ref[:, C_BQ:C_BQ + D_BK].astype(F32)
    k = proj_ref[:, C_BK:C_BK + D_BK].astype(F32)
    vb = proj_ref[:, C_BV:C_BV + D_B]
    pre = _dot(proj_ref[:, C_BR:C_BR + LANES], bwa_ref[...]) + bba_ref[...]
    log_a = _log_sigmoid(pre) * (1.0 / GATE_TAU)
    r64 = _iota((CHUNK_B, CHUNK_B), 0)
    c64 = _iota((CHUNK_B, CHUNK_B), 1)
    tril = c64 <= r64
    tril_bf = jnp.where(tril, 1.0, 0.0).astype(BF16)
    row_pad = jnp.zeros((LANES - CHUNK_B, D_BK), F32)
    v_pad = jnp.zeros((LANES - CHUNK_B, WIN), BF16)
    win_col = _iota((CHUNK_B, WIN), 1)
    col_hq = _head_id(_iota((CHUNK_B, D_BK), 1), DK_B, H_B)
    tril_heads = (_iota((H_B * CHUNK_B, CHUNK_B), 1)
                  <= jnp.bitwise_and(_iota((H_B * CHUNK_B, CHUNK_B), 0), CHUNK_B - 1))
    for c in range(PT // CHUNK_B):
        sl = slice(c * CHUNK_B, (c + 1) * CHUNK_B)
        hi, lo = _split(log_a[sl])
        cum2 = _dot(tril_bf, jnp.concatenate([hi, lo], axis=1))
        cum = cum2[:, :D_BK] + cum2[:, D_BK:]
        tot_row = cum[CHUNK_B - 1:CHUNK_B, :]
        la_t = jnp.concatenate([log_a[sl], row_pad], axis=0).T
        q_in = q[sl] * (DK_B ** -0.5) * jnp.exp(cum)
        k_in = k[sl] * jnp.exp(-cum)
        k_out = k[sl] * jnp.exp(tot_row - cum)
        q_heads = jnp.concatenate([jnp.where(col_hq == h, q_in, 0.0) for h in range(H_B)], axis=0)
        sc = _dot_nt(q_heads.astype(BF16), k_in.astype(BF16))
        sc = jnp.where(tril_heads, sc, 0.0).astype(BF16)
        o_intra = _merge_windows(
            [_dot(sc[h * CHUNK_B:(h + 1) * CHUNK_B], vb[sl, WIN_START[h]:WIN_START[h] + WIN])
             for h in range(H_B)])
        s_old = sbd_ref[...]
        o_inter = _dot(q_in.astype(BF16), s_old.astype(BF16))
        kot = jnp.concatenate([k_out, row_pad], axis=0).T.astype(BF16)
        for h in range(H_B):
            rs = slice(h * DK_B, (h + 1) * DK_B)
            ws = slice(WIN_START[h], WIN_START[h] + WIN)
            dec = jnp.exp(jnp.sum(la_t[rs], axis=1, keepdims=True))
            lo_col = h * DV_B - WIN_START[h]
            in_head = (win_col >= lo_col) & (win_col < lo_col + DV_B)
            v_h = jnp.where(in_head, vb[sl, ws], jnp.zeros((), BF16))
            kv = _dot(kot[rs], jnp.concatenate([v_h, v_pad], axis=0))
            sbd_ref[rs, ws] = s_old[rs, ws] * dec + kv
        o_ref[sl, :] = o_intra + o_inter
        yield
    o = o_ref[...]
    ssq = _dot((o * o).astype(BF16), ind_ref[...])
    o_n = o * lax.rsqrt(ssq * (1.0 / DV_B) + EPS) * onw_ref[...]
    bg = proj_ref[:, C_BG:C_BG + D_B].astype(F32)
    out_ref[:, D_A:D_A + D_B] = (o_n * _silu(bg)).astype(BF16)
    yield

    for h in range(H_X):
        hs = slice(h * HD_X, (h + 1) * HD_X)
        qh = proj_ref[:, C_XQ + h * HD_X:C_XQ + (h + 1) * HD_X]
        s = _dot_nt(qh, mk_ref[:, hs].astype(BF16)) * (HD_X ** -0.5)
        e = jnp.exp(s - jnp.max(s, axis=-1, keepdims=True))
        den = jnp.sum(e, axis=-1, keepdims=True)
        ox = _dot(e.astype(BF16), mv_ref[:, hs].astype(BF16)) / den
        xg = proj_ref[:, C_XG + h * HD_X:C_XG + (h + 1) * HD_X].astype(F32)
        out_ref[:, D_A + D_B + h * HD_X:D_A + D_B + (h + 1) * HD_X] = (ox * _silu(xg)).astype(BF16)
        if h % 2 == 1:
            yield


def _prompt_layer_kernel(xn_ref, wi_ref, wo_ref, nw_ref, fw_ref, mk_ref, mv_ref, wa_ref, abt_ref,
                         avw_ref, bwa_ref, bba_ref, onw_ref, y_ref, st_ref,
                         pa_ref, pb_ref, xk_ref, br_ref, h_ref, sbd_ref, ind_ref, o_ref, *, nt):
    s = pl.program_id(0)
    cur = jnp.maximum(s - 1, 0)
    t = lax.rem(cur, nt)

    @pl.when(s == 0)
    def _():
        r2 = _head_id(_iota((D_B, D_B), 0), DV_B, H_B)
        c2 = _head_id(_iota((D_B, D_B), 1), DV_B, H_B)
        ind_ref[...] = jnp.where(r2 == c2, 1.0, 0.0).astype(BF16)
        pb_ref[...] = jnp.zeros_like(pb_ref)
        xk_ref[...] = jnp.zeros_like(xk_ref)

    @pl.when(t == 0)
    def _():
        sbd_ref[...] = jnp.zeros_like(sbd_ref)

    def in_proj_stages(pn_ref):
        x = xn_ref[...]
        ms = jnp.mean(x * x, axis=-1, keepdims=True)
        h_ref[...] = (x * lax.rsqrt(ms + EPS) * nw_ref[...]).astype(BF16)
        yield
        for c0 in range(0, D_PROJ_PAD, IN_BN):
            n = min(IN_BN, D_PROJ_PAD - c0)
            pn_ref[:, c0:c0 + n] = _dot(h_ref[...], wi_ref[:, c0:c0 + n]).astype(BF16)
            yield

    def out_proj_stages():
        ssq = jnp.zeros((PT, 1), F32)
        for c0 in range(0, D_MODEL, OUT_BN):
            cols = pl.ds(c0, OUT_BN)
            acc = _dot(br_ref[...], wo_ref[:, cols]) + xk_ref[:, cols]
            y_ref[:, cols] = acc
            ssq = ssq + jnp.sum(acc * acc, axis=-1, keepdims=True)
            yield
        y_ref[...] = y_ref[...] * lax.rsqrt(ssq * (1.0 / D_MODEL) + EPS) * fw_ref[...]
        yield

    def body(pn_ref, pc_ref):
        streams = {"P": in_proj_stages(pn_ref),
                   "M": _prompt_block(pc_ref, br_ref, mk_ref, mv_ref, wa_ref, abt_ref, avw_ref,
                                      bwa_ref, bba_ref, onw_ref, sbd_ref, ind_ref, o_ref),
                   "O": out_proj_stages()}
        order = "P P M P M P M P M P M P M P M P M O O O O O".split()
        for name in order:
            next(streams[name])
        for name, gen in streams.items():
            assert next(gen, "done") == "done", name
        xk_ref[...] = xn_ref[...]

    @pl.when(lax.rem(s, 2) == 0)
    def _():
        body(pa_ref, pb_ref)

    @pl.when(lax.rem(s, 2) == 1)
    def _():
        body(pb_ref, pa_ref)

    @pl.when((t == nt - 1) & (s > 0))
    def _():
        for h in range(H_B):
            off = h * DV_B - WIN_START[h]
            blk = sbd_ref[h * DK_B:(h + 1) * DK_B, WIN_START[h]:WIN_START[h] + WIN]
            if off:
                blk = pltpu.roll(blk, WIN - off, 1)
            st_ref[0, h] = blk[:, :DV_B]


def _prompt_layer(xp, w_in_bf, w_out_bf, nw, fw, memkv, wa, abt, avw, bwa, bba, onw, *, batch, seq):
    nt = seq // PT
    nblk = batch * nt
    cur = lambda s: jnp.maximum(s - 1, 0)
    const = lambda *shape: pl.BlockSpec(shape, lambda s: (0,) * len(shape))
    resident = lambda *shape: pl.BlockSpec(shape, lambda s: (0,) * len(shape),
                                           pipeline_mode=pl.Buffered(1))
    return pl.pallas_call(
        functools.partial(_prompt_layer_kernel, nt=nt),
        grid=(nblk + 1,),
        in_specs=[pl.BlockSpec((PT, D_MODEL), lambda s: (jnp.minimum(s, nblk - 1), 0)),
                  resident(D_MODEL, D_PROJ_PAD), resident(D_MODEL, D_MODEL),
                  const(1, D_MODEL), const(1, D_MODEL),
                  pl.BlockSpec((None, N_MEM, D_X), lambda s: (0, cur(s) // nt, 0)),
                  pl.BlockSpec((None, N_MEM, D_X), lambda s: (1, cur(s) // nt, 0)),
                  const(H_A, CHUNK_A, CHUNK_A), const(CHUNK_A, H_A), const(1, D_A),
                  const(LANES, D_BK), const(1, D_BK), const(1, D_B)],
        out_specs=[pl.BlockSpec((PT, D_MODEL), lambda s: (cur(s), 0)),
                   pl.BlockSpec((1, H_B, DK_B, DV_B), lambda s: (cur(s) // nt, 0, 0, 0))],
        out_shape=[jax.ShapeDtypeStruct((batch * seq, D_MODEL), F32),
                   jax.ShapeDtypeStruct((batch, H_B, DK_B, DV_B), F32)],
        scratch_shapes=[pltpu.VMEM((PT, D_PROJ_PAD), BF16),
                        pltpu.VMEM((PT, D_PROJ_PAD), BF16),
                        pltpu.VMEM((PT, D_MODEL), F32),
                        pltpu.VMEM((PT, D_MODEL), BF16),
                        pltpu.VMEM((PT, D_MODEL), BF16),
                        pltpu.VMEM((D_BK, D_B), F32),
                        pltpu.VMEM((D_B, D_B), BF16),
                        pltpu.VMEM((PT, D_B), F32)],
        compiler_params=pltpu.CompilerParams(
            dimension_semantics=("arbitrary",), vmem_limit_bytes=VMEM_LIMIT),
        name="prompt_layer",
    )(xp, w_in_bf, w_out_bf, nw.reshape(1, D_MODEL), fw.reshape(1, D_MODEL), memkv, memkv,
      wa, abt, avw, bwa, bba, onw)


NS = 8
TS = 8
SB = NS * TS
SBP = 128


def _sample_mixer_kernel(proj_ref, st_ref, ck_ref, cv_ref, wa_ref, abt_ref, avw_ref, bwa_ref,
                         bba_ref, onw_ref, out_ref, stn_ref, cvs_ref,
                         qin_ref, xq_ref, kot_ref, lat_ref, vhm_ref, ghm_ref, ohm_ref, ox_ref):
    ri = _iota((SB, SB), 0)
    ci = _iota((SB, SB), 1)
    same_seq = jnp.right_shift(ri, 3) == jnp.right_shift(ci, 3)
    causal = same_seq & (ci <= ri)

    a_br, vn = _group_a(proj_ref, wa_ref, abt_ref, avw_ref, causal)
    out_ref[:, 0:D_A] = a_br.astype(BF16)
    cvs_ref[...] = vn

    q = proj_ref[:, C_BQ:C_BQ + D_BK].astype(F32)
    k = proj_ref[:, C_BK:C_BK + D_BK].astype(F32)
    vb = proj_ref[:, C_BV:C_BV + D_B]
    bgb = proj_ref[:, C_BG:C_BG + D_B]
    pre = _dot(proj_ref[:, C_BR:C_BR + LANES], bwa_ref[...]) + bba_ref[...]
    log_a = _log_sigmoid(pre) * (1.0 / GATE_TAU)
    causal_bf = jnp.where(causal, 1.0, 0.0).astype(BF16)
    seq_bf = jnp.where(same_seq, 1.0, 0.0).astype(BF16)
    hi, lo = _split(log_a)
    cum = _dot(causal_bf, hi) + _dot(causal_bf, lo)
    tot = _dot(seq_bf, hi) + _dot(seq_bf, lo)
    q_in = q * (DK_B ** -0.5) * jnp.exp(cum)
    k_in = k * jnp.exp(-cum)
    k_out = k * jnp.exp(tot - cum)
    qin_ref[...] = q_in
    zpad = jnp.zeros((SBP - SB, D_BK), F32)
    kot_ref[...] = jnp.concatenate([k_out, zpad], axis=0).T.astype(BF16)
    lat_ref[...] = jnp.concatenate([log_a, zpad], axis=0).T
    xq_ref[...] = proj_ref[:, C_XQ:C_XQ + D_X].astype(F32)

    wins = _gla_intra_windows(q_in, k_in, vb, causal)
    sel_r = _iota((D_B, DV_B), 0)
    sel_c = _iota((D_B, DV_B), 1)
    vhm_ref[...] = jnp.zeros_like(vhm_ref)
    for h in range(H_B):
        off = h * DV_B - WIN_START[h]
        w = wins[h]
        if off:
            w = pltpu.roll(w, WIN - off, 1)
        ohm_ref[h] = w[:, :DV_B]
        sel = jnp.where(sel_r == sel_c + h * DV_B, 1.0, 0.0).astype(BF16)
        vhm_ref[h, 0:SB, :] = _dot(vb, sel)
        ghm_ref[h] = _dot(bgb, sel)

    mask_x = (jnp.right_shift(_iota((H_X * TS, N_MEM * H_X), 0), 3)
              == jnp.bitwise_and(_iota((H_X * TS, N_MEM * H_X), 1), H_X - 1))
    mask_b = jnp.right_shift(_iota((H_B * TS, D_BK), 0), 3) == _head_id(_iota((H_B * TS, D_BK), 1), DK_B, H_B)
    row_seq = jnp.right_shift(_iota((SBP, DV_B), 0), 3)
    lane_seq = jnp.right_shift(_iota((D_BK, SBP), 1), 3)

    def per_seq(s, carry):
        r0 = pl.multiple_of(s * TS, TS)
        q8 = xq_ref[pl.ds(r0, TS), :]
        q32 = jnp.concatenate([q8[:, h * HD_X:(h + 1) * HD_X] for h in range(H_X)], axis=0)
        sc = _dot_nt(q32.astype(BF16), ck_ref[s].astype(BF16)) * (HD_X ** -0.5)
        sc = jnp.where(mask_x, sc, -1e30)
        e = jnp.exp(sc - jnp.max(sc, axis=-1, keepdims=True))
        den = jnp.sum(e, axis=-1, keepdims=True)
        o = _dot(e.astype(BF16), cv_ref[s].astype(BF16)) / den
        ox_ref[pl.ds(r0, TS), :] = jnp.concatenate(
            [o[h * TS:(h + 1) * TS] for h in range(H_X)], axis=1)
        qi8 = qin_ref[pl.ds(r0, TS), :]
        qbd2 = jnp.where(mask_b, jnp.concatenate([qi8] * H_B, axis=0), 0.0).astype(BF16)
        s0 = jnp.concatenate([st_ref[s, h] for h in range(H_B)], axis=0)
        o_inter = _dot(qbd2, s0.astype(BF16))
        in_seq = row_seq == s
        for h in range(H_B):
            ohm_ref[h, pl.ds(r0, TS), :] += o_inter[h * TS:(h + 1) * TS]
        dec = jnp.exp(jnp.sum(jnp.where(lane_seq == s, lat_ref[...], 0.0), axis=1, keepdims=True))
        kot = kot_ref[...]
        kv = jnp.concatenate(
            [_dot(kot[h * DK_B:(h + 1) * DK_B],
                  jnp.where(in_seq, vhm_ref[h], 0.0).astype(BF16)) for h in range(H_B)], axis=0)
        s_new = s0 * dec + kv
        for h in range(H_B):
            stn_ref[s, h] = s_new[h * DK_B:(h + 1) * DK_B]
        return carry

    lax.fori_loop(0, NS, per_seq, 0, unroll=2)

    selt_r = _iota((DV_B, D_B), 0)
    selt_c = _iota((DV_B, D_B), 1)
    b_br = jnp.zeros((SB, D_B), F32)
    for h in range(H_B):
        o_h = ohm_ref[h]
        ms = jnp.mean(o_h * o_h, axis=-1, keepdims=True)
        ob = (o_h * lax.rsqrt(ms + EPS) * onw_ref[...] * _silu(ghm_ref[h])).astype(BF16)
        selt = jnp.where(selt_c == selt_r + h * DV_B, 1.0, 0.0).astype(BF16)
        b_br = b_br + _dot(ob, selt)
    out_ref[:, D_A:D_A + D_B] = b_br.astype(BF16)

    xg = proj_ref[:, C_XG:C_XG + D_X].astype(F32)
    out_ref[:, D_A + D_B:D_MODEL] = (ox_ref[...] * _silu(xg)).astype(BF16)


def _sample_mixer(proj, state, ck, cv, wa, abt, avw, bwa, bba, onw):
    nseq = state.shape[1]
    const = lambda *shape: pl.BlockSpec(shape, lambda i: (0,) * len(shape))
    return pl.pallas_call(
        _sample_mixer_kernel,
        grid=(nseq // NS,),
        in_specs=[pl.BlockSpec((SB, D_PROJ), lambda i: (i, 0)),
                  pl.BlockSpec((None, NS, H_B, DK_B, DV_B), lambda i: (0, i, 0, 0, 0)),
                  pl.BlockSpec((NS, N_MEM * H_X, HD_X), lambda i: (i, 0, 0)),
                  pl.BlockSpec((NS, N_MEM * H_X, HD_X), lambda i: (i, 0, 0)),
                  const(H_A, SB, SB), const(SB, H_A), const(1, D_A),
                  const(LANES, D_BK), const(1, D_BK), const(1, DV_B)],
        out_specs=[pl.BlockSpec((SB, D_MODEL), lambda i: (i, 0)),
                   pl.BlockSpec((None, NS, H_B, DK_B, DV_B), lambda i: (0, i, 0, 0, 0)),
                   pl.BlockSpec((SB, D_A), lambda i: (i, 0))],
        out_shape=[jax.ShapeDtypeStruct((nseq * TS, D_MODEL), BF16),
                   jax.ShapeDtypeStruct((1, nseq, H_B, DK_B, DV_B), F32),
                   jax.ShapeDtypeStruct((nseq * TS, D_A), F32)],
        scratch_shapes=[pltpu.VMEM((SB, D_BK), F32),
                        pltpu.VMEM((SB, D_X), F32),
                        pltpu.VMEM((D_BK, SBP), BF16),
                        pltpu.VMEM((D_BK, SBP), F32),
                        pltpu.VMEM((H_B, SBP, DV_B), F32),
                        pltpu.VMEM((H_B, SB, DV_B), F32),
                        pltpu.VMEM((H_B, SB, DV_B), F32),
                        pltpu.VMEM((SB, D_X), F32)],
        compiler_params=pltpu.CompilerParams(
            dimension_semantics=("arbitrary",), vmem_limit_bytes=VMEM_LIMIT),
        name="sample_mixer",
    )(proj, state, ck, cv, wa, abt, avw, bwa, bba, onw)


def kernel(x_prompt, x_sample, mem_prompt, state_gla, cache_mem_k, cache_mem_v, norm_w, w_in,
           a_vnorm_w, a_ws, a_bs, b_wa, b_ba, b_onorm_w, mem_norm_w, w_mem_kv, w_out, final_norm_w):
    batch, seq, _ = x_prompt.shape
    nseq, tdec, _ = x_sample.shape
    depth = w_in.shape[0]
    assert depth == 1 and tdec == TS and seq % PT == 0 and nseq % NS == 0

    w_in_bf, w_out_bf = _weight_prep(jnp.transpose(w_in[0]), w_out[0])
    last_cols = D_PROJ_PAD - (D_PROJ_PAD // IN_BN) * IN_BN
    bwa = jnp.concatenate([b_wa[0], jnp.zeros((LANES - GATE_RANK, D_BK), F32)], axis=0).astype(BF16)
    bba = b_ba[0].reshape(1, D_BK)
    avw = a_vnorm_w[0].reshape(1, D_A)
    onw_p = jnp.tile(b_onorm_w[0], H_B).reshape(1, D_B)
    onw_s = b_onorm_w[0].reshape(1, DV_B)
    wa_p = a_ws[0]
    abt_p = a_bs[0].T
    wa_s = jnp.tile(a_ws[0][:, :TS, :TS], (1, NS, NS))
    abt_s = jnp.tile(a_bs[0][:, :TS], (1, NS)).T

    xp = x_prompt.reshape(batch * seq, D_MODEL)
    xs = x_sample.reshape(nseq * TS, D_MODEL)
    mem = mem_prompt.reshape(batch * N_MEM, D_MODEL)

    memkv = _mem_kv(mem, mem_norm_w[0], w_mem_kv[0], batch=batch)
    y_p, st_p = _prompt_layer(xp, w_in_bf, w_out_bf, norm_w[0], final_norm_w, memkv, wa_p, abt_p,
                              avw, bwa, bba, onw_p, batch=batch, seq=seq)

    proj_s = _norm_matmul(xs, norm_w[0], w_in_bf, D_PROJ, bm=nseq * TS, bn=IN_BN, transposed=False,
                          out_dtype=BF16, last_cols=last_cols, name="in_proj_s")
    br_s, st_s, cvs = _sample_mixer(
        proj_s, state_gla,
        cache_mem_k.reshape(nseq, N_MEM * H_X, HD_X), cache_mem_v.reshape(nseq, N_MEM * H_X, HD_X),
        wa_s, abt_s, avw, bwa, bba, onw_s)
    y_s = _out_proj(br_s, w_out_bf, xs, final_norm_w, bm=512, name="out_proj_s")

    return (y_p.reshape(batch, seq, D_MODEL),
            y_s.reshape(nseq, TS, D_MODEL),
            memkv[0].reshape(1, batch, N_MEM, H_X, HD_X),
            memkv[1].reshape(1, batch, N_MEM, H_X, HD_X),
            st_p.reshape(1, batch, H_B, DK_B, DV_B),
            st_s,
            cvs.reshape(1, nseq, TS, D_A))
```

```python
import functools

import jax
import jax.numpy as jnp
from jax import lax
from jax.experimental import pallas as pl
from jax.experimental.pallas import tpu as pltpu

F32 = jnp.float32
BF16 = jnp.bfloat16

D_MODEL = 2048
D_A = 768
H_A = 4
HD_A = 192
CHUNK_A = 128
D_B = 768
H_B = 4
DV_B = 192
DK_B = 96
D_BK = 384
GATE_RANK = 16
GATE_TAU = 16.0
CHUNK_B = 64
D_X = 512
H_X = 4
HD_X = 128
N_MEM = 256
EPS = 1e-6

LANES = 128
MXU_COLS = 256
C_AU, C_AV, C_AG = 0, 768, 1536
C_BQ, C_BK, C_BV, C_BG = 2304, 2688, 3072, 3840
C_XQ, C_XG, C_BR = 4608, 5120, 5632
D_PROJ = 5760
D_PROJ_PAD = 5888
WIN_START = (0, 128, 384, 512)
WIN = 256
IN_BN = 768
OUT_BN = 512

VMEM_LIMIT = 60 * 1024 * 1024


def _dot(a, b):
    return jnp.dot(a, b, preferred_element_type=F32)


def _dot_nt(a, b):
    return lax.dot_general(a, b, (((1,), (1,)), ((), ())), preferred_element_type=F32)


def _dot_tn(a, b):
    return lax.dot_general(a, b, (((0,), (0,)), ((), ())), preferred_element_type=F32)


def _split(x):
    hi = x.astype(BF16)
    lo = (x - hi.astype(F32)).astype(BF16)
    return hi, lo


def _silu(x):
    return x / (1.0 + jnp.exp(-x))


def _log_sigmoid(x):
    return jnp.minimum(x, 0.0) - jnp.log1p(jnp.exp(-jnp.abs(x)))


def _head_id(idx, width, n):
    h = jnp.zeros_like(idx)
    for i in range(1, n):
        h = h + (idx >= i * width).astype(jnp.int32)
    return h


def _iota(shape, dim):
    return lax.broadcasted_iota(jnp.int32, shape, dim)


def _merge_windows(r):
    rows = r[0].shape[0]
    lo = _iota((rows, LANES), 1) < 64
    tiles = [r[0][:, :LANES], jnp.where(lo, r[0][:, LANES:], r[1][:, :LANES]), r[1][:, LANES:],
             r[2][:, :LANES], jnp.where(lo, r[2][:, LANES:], r[3][:, :LANES]), r[3][:, LANES:]]
    return jnp.concatenate(tiles, axis=1)


def _group_a(proj_ref, wa_ref, abt_ref, avw_ref, mask):
    u = proj_ref[:, C_AU:C_AU + D_A].astype(F32)
    v = proj_ref[:, C_AV:C_AV + D_A].astype(F32)
    g = proj_ref[:, C_AG:C_AG + D_A].astype(F32)
    rows = u.shape[0]
    ms = jnp.mean(v * v, axis=-1, keepdims=True)
    vn = v * lax.rsqrt(ms + EPS) * avw_ref[...]
    vb = vn.astype(BF16)
    r = []
    for gi in range(H_A):
        w = jnp.where(mask, wa_ref[gi], 0.0).astype(BF16)
        r.append(_dot(w, vb[:, WIN_START[gi]:WIN_START[gi] + WIN]))
    mixed = _merge_windows(r)
    col_g = _head_id(_iota((rows, D_A), 1), HD_A, H_A)
    bias = jnp.zeros((rows, D_A), F32)
    for gi in range(H_A):
        bias = jnp.where(col_g == gi, abt_ref[:, gi:gi + 1], bias)
    return u * (mixed + bias) * _silu(g), vn


def _gla_intra_windows(q_in, k_in, vb, mask):
    rows = q_in.shape[0]
    col_h = _head_id(_iota((rows, D_BK), 1), DK_B, H_B)
    kb = k_in.astype(BF16)
    r = []
    for h in range(H_B):
        qh = jnp.where(col_h == h, q_in, 0.0).astype(BF16)
        s = jnp.where(mask, _dot_nt(qh, kb), 0.0).astype(BF16)
        r.append(_dot(s, vb[:, WIN_START[h]:WIN_START[h] + WIN]))
    return r


def _norm_matmul_kernel(x_ref, nw_ref, w_ref, o_ref, h_ref, *, transposed, last_cols):
    j = pl.program_id(1)
    nj = pl.num_programs(1)

    @pl.when(j == 0)
    def _():
        x = x_ref[...]
        ms = jnp.mean(x * x, axis=-1, keepdims=True)
        h_ref[...] = (x * lax.rsqrt(ms + EPS) * nw_ref[...]).astype(BF16)

    mm = _dot_nt if transposed else _dot
    if last_cols is None:
        o_ref[...] = mm(h_ref[...], w_ref[...].astype(BF16)).astype(o_ref.dtype)
    else:
        @pl.when(j < nj - 1)
        def _():
            o_ref[...] = mm(h_ref[...], w_ref[...].astype(BF16)).astype(o_ref.dtype)

        @pl.when(j == nj - 1)
        def _():
            w = w_ref[:last_cols, :] if transposed else w_ref[:, :last_cols]
            o_ref[:, :last_cols] = mm(h_ref[...], w.astype(BF16)).astype(o_ref.dtype)


def _norm_matmul(x, nw, w, n_out, *, bm, bn, transposed, out_dtype, split_out=False,
                 last_cols=None, name):
    m, k = x.shape
    n = w.shape[0] if transposed else w.shape[1]
    nj = -(-n // bn)
    assert m % bm == 0
    if split_out:
        out_shape = jax.ShapeDtypeStruct((nj, m, bn), out_dtype)
        out_spec = pl.BlockSpec((None, bm, bn), lambda i, j: (j, i, 0))
    else:
        out_shape = jax.ShapeDtypeStruct((m, n_out), out_dtype)
        out_spec = pl.BlockSpec((bm, bn), lambda i, j: (i, j))
    if transposed:
        w_spec = pl.BlockSpec((bn, k), lambda i, j: (j, 0))
    else:
        w_spec = pl.BlockSpec((k, bn), lambda i, j: (0, j))
    return pl.pallas_call(
        functools.partial(_norm_matmul_kernel, transposed=transposed, last_cols=last_cols),
        grid=(m // bm, nj),
        in_specs=[pl.BlockSpec((bm, k), lambda i, j: (i, 0)),
                  pl.BlockSpec((1, k), lambda i, j: (0, 0)),
                  w_spec],
        out_specs=out_spec,
        out_shape=out_shape,
        scratch_shapes=[pltpu.VMEM((bm, k), BF16)],
        compiler_params=pltpu.CompilerParams(
            dimension_semantics=("arbitrary", "arbitrary"), vmem_limit_bytes=VMEM_LIMIT),
        name=name,
    )(x, nw.reshape(1, k), w)


def _mem_kv_kernel(x_ref, nw_ref, w_ref, o_ref):
    x = x_ref[...]
    ms = jnp.mean(x * x, axis=-1, keepdims=True)
    h = (x * lax.rsqrt(ms + EPS) * nw_ref[...]).astype(BF16)
    kv = _dot(h, w_ref[...].astype(BF16))
    o_ref[0] = kv[:, :D_X]
    o_ref[1] = kv[:, D_X:]


def _mem_kv(mem, nw, w, *, batch):
    return pl.pallas_call(
        _mem_kv_kernel,
        grid=(batch,),
        in_specs=[pl.BlockSpec((N_MEM, D_MODEL), lambda i: (i, 0)),
                  pl.BlockSpec((1, D_MODEL), lambda i: (0, 0)),
                  pl.BlockSpec((D_MODEL, 2 * D_X), lambda i: (0, 0))],
        out_specs=pl.BlockSpec((2, N_MEM, D_X), lambda i: (0, i, 0)),
        out_shape=jax.ShapeDtypeStruct((2, batch * N_MEM, D_X), F32),
        compiler_params=pltpu.CompilerParams(
            dimension_semantics=("arbitrary",), vmem_limit_bytes=VMEM_LIMIT),
        name="mem_kv",
    )(mem, nw.reshape(1, D_MODEL), w)


PREP_BN = MXU_COLS
N_MAIN_BLOCKS = C_BG // PREP_BN


GATE_BLOCK = D_PROJ_PAD // PREP_BN - 1
N_IN_BLOCKS = GATE_BLOCK + 1
N_OUT_BLOCKS = D_MODEL // PREP_BN


def _weight_prep_kernel(wt_hbm, wo_hbm, oi_ref, oo_ref, buf, sem):
    r = pl.program_id(0)
    slot = lax.rem(r, 2)

    def in_copy(rr, sl):
        start = pl.multiple_of(jnp.where(rr < N_MAIN_BLOCKS, rr * PREP_BN, rr * PREP_BN + GATE_RANK), 8)
        return pltpu.make_async_copy(wt_hbm.at[pl.ds(start, PREP_BN), :], buf.at[sl], sem.at[sl])

    def gate_copy(sl):
        return pltpu.make_async_copy(wt_hbm.at[pl.ds(C_BG, GATE_RANK), :],
                                     buf.at[sl, pl.ds(0, GATE_RANK), :], sem.at[sl])

    def out_copy(rr, sl):
        start = pl.multiple_of((rr - N_IN_BLOCKS) * PREP_BN, PREP_BN)
        return pltpu.make_async_copy(wo_hbm.at[pl.ds(start, PREP_BN), :], buf.at[sl], sem.at[sl])

    def start_fetch(rr, sl):
        @pl.when(rr < GATE_BLOCK)
        def _():
            in_copy(rr, sl).start()

        @pl.when(rr == GATE_BLOCK)
        def _():
            gate_copy(sl).start()

        @pl.when(rr > GATE_BLOCK)
        def _():
            out_copy(rr, sl).start()

    @pl.when(r == 0)
    def _():
        start_fetch(r, slot)

    @pl.when(r + 1 < pl.num_programs(0))
    def _():
        start_fetch(r + 1, 1 - slot)

    @pl.when(r < GATE_BLOCK)
    def _():
        in_copy(r, slot).wait()
        oi_ref[...] = buf[slot].T.astype(BF16)

    @pl.when(r == GATE_BLOCK)
    def _():
        gate_copy(slot).wait()
        rows = jnp.concatenate([buf[slot, 0:GATE_RANK, :],
                                jnp.zeros((PREP_BN - GATE_RANK, D_MODEL), F32)], axis=0)
        oi_ref[...] = rows.T.astype(BF16)

    @pl.when(r > GATE_BLOCK)
    def _():
        out_copy(r, slot).wait()
        oo_ref[...] = buf[slot].astype(BF16)


def _weight_prep(w_t, w_out):
    k = w_t.shape[1]
    return pl.pallas_call(
        _weight_prep_kernel,
        grid=(N_IN_BLOCKS + N_OUT_BLOCKS,),
        in_specs=[pl.BlockSpec(memory_space=pl.ANY), pl.BlockSpec(memory_space=pl.ANY)],
        out_specs=[pl.BlockSpec((k, PREP_BN), lambda r: (0, jnp.minimum(r, GATE_BLOCK))),
                   pl.BlockSpec((PREP_BN, D_MODEL), lambda r: (jnp.maximum(r - N_IN_BLOCKS, 0), 0))],
        out_shape=[jax.ShapeDtypeStruct((k, D_PROJ_PAD), BF16),
                   jax.ShapeDtypeStruct((D_MODEL, D_MODEL), BF16)],
        scratch_shapes=[pltpu.VMEM((2, PREP_BN, k), F32), pltpu.SemaphoreType.DMA((2,))],
        compiler_params=pltpu.CompilerParams(
            dimension_semantics=("arbitrary",), vmem_limit_bytes=VMEM_LIMIT),
        name="weight_prep",
    )(w_t, w_out)


def _out_proj_kernel(br_ref, w_ref, x_ref, fw_ref, y_ref):
    acc = _dot(br_ref[...], w_ref[...]) + x_ref[...]
    ms = jnp.mean(acc * acc, axis=-1, keepdims=True)
    y_ref[...] = acc * lax.rsqrt(ms + EPS) * fw_ref[...]


def _out_proj(br, w, x, fw, *, bm, name):
    m, k = br.shape
    n = w.shape[1]
    return pl.pallas_call(
        _out_proj_kernel,
        grid=(m // bm,),
        in_specs=[pl.BlockSpec((bm, k), lambda i: (i, 0)),
                  pl.BlockSpec((k, n), lambda i: (0, 0)),
                  pl.BlockSpec((bm, n), lambda i: (i, 0)),
                  pl.BlockSpec((1, n), lambda i: (0, 0))],
        out_specs=pl.BlockSpec((bm, n), lambda i: (i, 0)),
        out_shape=jax.ShapeDtypeStruct((m, n), F32),
        compiler_params=pltpu.CompilerParams(
            dimension_semantics=("arbitrary",), vmem_limit_bytes=VMEM_LIMIT),
        name=name,
    )(br, w, x, fw.reshape(1, n))


PT = 256
N_MIXER_STAGES = 2 + 4 * (PT // CHUNK_B) + 1 + 2 * H_X


def _group_a_chunks(proj_ref, wa_ref, abt_ref, avw_ref):
    u = proj_ref[:, C_AU:C_AU + D_A].astype(F32)
    v = proj_ref[:, C_AV:C_AV + D_A].astype(F32)
    g = proj_ref[:, C_AG:C_AG + D_A].astype(F32)
    n_chunks = u.shape[0] // CHUNK_A
    ms = jnp.mean(v * v, axis=-1, keepdims=True)
    vb = (v * lax.rsqrt(ms + EPS) * avw_ref[...]).astype(BF16)
    tril = _iota((CHUNK_A, CHUNK_A), 1) <= _iota((CHUNK_A, CHUNK_A), 0)
    r = []
    for gi in range(H_A):
        w = jnp.where(tril, wa_ref[gi], 0.0).astype(BF16)
        ws = slice(WIN_START[gi], WIN_START[gi] + WIN)
        rhs = jnp.concatenate([vb[c * CHUNK_A:(c + 1) * CHUNK_A, ws] for c in range(n_chunks)], axis=1)
        res = _dot(w, rhs)
        r.append(jnp.concatenate([res[:, c * WIN:(c + 1) * WIN] for c in range(n_chunks)], axis=0))
    mixed = _merge_windows(r)
    col_g = _head_id(_iota((CHUNK_A, D_A), 1), HD_A, H_A)
    bias = jnp.zeros((CHUNK_A, D_A), F32)
    for gi in range(H_A):
        bias = jnp.where(col_g == gi, abt_ref[:, gi:gi + 1], bias)
    bias = jnp.concatenate([bias] * n_chunks, axis=0)
    return u * (mixed + bias) * _silu(g)


def _prompt_block(proj_ref, out_ref, mk_ref, mv_ref, wa_ref, abt_ref, avw_ref, bwa_ref, bba_ref,
                  onw_ref, sbd_ref, o_ref):
    out_ref[:, 0:D_A] = _group_a_chunks(proj_ref, wa_ref, abt_ref, avw_ref).astype(BF16)
    yield

    q = proj_ref[:, C_BQ:C_BQ + D_BK].astype(F32)
    k = proj_ref[:, C_BK:C_BK + D_BK].astype(F32)
    vb = proj_ref[:, C_BV:C_BV + D_B]
    pre = _dot(proj_ref[:, C_BR:C_BR + LANES], bwa_ref[...]) + bba_ref[...]
    log_a = _log_sigmoid(pre) * (1.0 / GATE_TAU)
    yield
    r64 = _iota((CHUNK_B, CHUNK_B), 0)
    c64 = _iota((CHUNK_B, CHUNK_B), 1)
    tril = c64 <= r64
    tril_bf = jnp.where(tril, 1.0, 0.0).astype(BF16)
    row_pad = jnp.zeros((LANES - CHUNK_B, D_BK), F32)
    v_pad = jnp.zeros((LANES - CHUNK_B, WIN), BF16)
    win_col = _iota((CHUNK_B, WIN), 1)
    col_hq = _head_id(_iota((CHUNK_B, D_BK), 1), DK_B, H_B)
    tril_heads = (_iota((H_B * CHUNK_B, CHUNK_B), 1)
                  <= jnp.bitwise_and(_iota((H_B * CHUNK_B, CHUNK_B), 0), CHUNK_B - 1))
    for c in range(PT // CHUNK_B):
        sl = slice(c * CHUNK_B, (c + 1) * CHUNK_B)
        hi, lo = _split(log_a[sl])
        cum2 = _dot(tril_bf, jnp.concatenate([hi, lo], axis=1))
        yield
        cum = cum2[:, :D_BK] + cum2[:, D_BK:]
        tot_row = cum[CHUNK_B - 1:CHUNK_B, :]
        la_t = jnp.concatenate([log_a[sl], row_pad], axis=0).T
        q_in = q[sl] * (DK_B ** -0.5) * jnp.exp(cum)
        k_in = k[sl] * jnp.exp(-cum)
        k_out = k[sl] * jnp.exp(tot_row - cum)
        q_heads = jnp.concatenate([jnp.where(col_hq == h, q_in, 0.0) for h in range(H_B)], axis=0)
        sc = _dot_nt(q_heads.astype(BF16), k_in.astype(BF16))
        yield
        sc = jnp.where(tril_heads, sc, 0.0).astype(BF16)
        o_intra = _merge_windows(
            [_dot(sc[h * CHUNK_B:(h + 1) * CHUNK_B], vb[sl, WIN_START[h]:WIN_START[h] + WIN])
             for h in range(H_B)])
        s_old = sbd_ref[...]
        o_inter = _dot(q_in.astype(BF16), s_old.astype(BF16))
        yield
        kot = jnp.concatenate([k_out, row_pad], axis=0).T.astype(BF16)
        for h in range(H_B):
            rs = slice(h * DK_B, (h + 1) * DK_B)
            ws = slice(WIN_START[h], WIN_START[h] + WIN)
            dec = jnp.exp(jnp.sum(la_t[rs], axis=1, keepdims=True))
            lo_col = h * DV_B - WIN_START[h]
            in_head = (win_col >= lo_col) & (win_col < lo_col + DV_B)
            v_h = jnp.where(in_head, vb[sl, ws], jnp.zeros((), BF16))
            kv = _dot(kot[rs], jnp.concatenate([v_h, v_pad], axis=0))
            sbd_ref[rs, ws] = s_old[rs, ws] * dec + kv
        o_ref[sl, :] = o_intra + o_inter
        yield
    o = o_ref[...]
    o2 = o * o
    t = [o2[:, j * LANES:(j + 1) * LANES] for j in range(D_B // LANES)]
    lo = _iota((PT, LANES), 1) < 64
    t1a = jnp.where(lo, t[1], 0.0)
    t4a = jnp.where(lo, t[4], 0.0)
    sums = (t[0] + t1a, (t[1] - t1a) + t[2], t[3] + t4a, (t[4] - t4a) + t[5])
    inv = [lax.rsqrt(jnp.sum(x, axis=-1, keepdims=True) * (1.0 / DV_B) + EPS) for x in sums]
    inv = jnp.concatenate(
        [jnp.broadcast_to(inv[0], (PT, LANES)), jnp.where(lo, inv[0], inv[1]),
         jnp.broadcast_to(inv[1], (PT, LANES)), jnp.broadcast_to(inv[2], (PT, LANES)),
         jnp.where(lo, inv[2], inv[3]), jnp.broadcast_to(inv[3], (PT, LANES))], axis=1)
    o_n = o * inv * onw_ref[...]
    bg = proj_ref[:, C_BG:C_BG + D_B].astype(F32)
    out_ref[:, D_A:D_A + D_B] = (o_n * _silu(bg)).astype(BF16)
    yield

    for h in range(H_X):
        hs = slice(h * HD_X, (h + 1) * HD_X)
        qh = proj_ref[:, C_XQ + h * HD_X:C_XQ + (h + 1) * HD_X]
        s = _dot_nt(qh, mk_ref[:, hs].astype(BF16)) * (HD_X ** -0.5)
        yield
        e = jnp.exp(s - jnp.max(s, axis=-1, keepdims=True))
        den = jnp.sum(e, axis=-1, keepdims=True)
        ox = _dot(e.astype(BF16), mv_ref[:, hs].astype(BF16)) / den
        xg = proj_ref[:, C_XG + h * HD_X:C_XG + (h + 1) * HD_X].astype(F32)
        out_ref[:, D_A + D_B + h * HD_X:D_A + D_B + (h + 1) * HD_X] = (ox * _silu(xg)).astype(BF16)
        yield


def _prompt_layer_kernel(xn_ref, wi_ref, wo_ref, nw_ref, fw_ref, mk_ref, mv_ref, wa_ref, abt_ref,
                         avw_ref, bwa_ref, bba_ref, onw_ref, y_ref, st_ref,
                         pa_ref, pb_ref, xk_ref, br_ref, h_ref, sbd_ref, o_ref, *, nt):
    s = pl.program_id(0)
    cur = jnp.maximum(s - 1, 0)
    t = lax.rem(cur, nt)

    @pl.when(t == 0)
    def _():
        sbd_ref[...] = jnp.zeros_like(sbd_ref)

    def in_proj_stages(pn_ref):
        x = xn_ref[...]
        ms = jnp.mean(x * x, axis=-1, keepdims=True)
        h_ref[...] = (x * lax.rsqrt(ms + EPS) * nw_ref[...]).astype(BF16)
        yield
        for c0 in range(0, D_PROJ_PAD, MXU_COLS):
            cols = pl.ds(c0, MXU_COLS)
            pn_ref[:, cols] = _dot(h_ref[...], wi_ref[:, cols]).astype(BF16)
            yield
        xk_ref[...] = xn_ref[...]

    def out_proj_stages():
        ssq = jnp.zeros((PT, 1), F32)
        for c0 in range(0, D_MODEL, OUT_BN):
            cols = pl.ds(c0, OUT_BN)
            acc = _dot(br_ref[...], wo_ref[:, cols]) + xk_ref[:, cols]
            y_ref[:, cols] = acc
            ssq = ssq + jnp.sum(acc * acc, axis=-1, keepdims=True)
            yield
        y_ref[...] = y_ref[...] * lax.rsqrt(ssq * (1.0 / D_MODEL) + EPS) * fw_ref[...]
        yield

    def run(order, streams):
        for name in order:
            next(streams[name])
        for name, gen in streams.items():
            assert next(gen, "done") == "done", name

    def mixer_stages(pc_ref):
        return _prompt_block(pc_ref, br_ref, mk_ref, mv_ref, wa_ref, abt_ref, avw_ref, bwa_ref,
                             bba_ref, onw_ref, sbd_ref, o_ref)

    n_p = 1 + D_PROJ_PAD // MXU_COLS
    n_m = N_MIXER_STAGES
    n_o = 1 + D_MODEL // OUT_BN

    def body(pn_ref, pc_ref):
        order = ["P"]
        for i in range(max(n_p - 1, n_m)):
            order += ["M"] * (i < n_m) + ["P"] * (i < n_p - 1)
        order += ["O"] * n_o
        run(order, {"P": in_proj_stages(pn_ref), "M": mixer_stages(pc_ref), "O": out_proj_stages()})

    last = pl.num_programs(0) - 1

    @pl.when(s == 0)
    def _():
        run(["P"] * n_p, {"P": in_proj_stages(pa_ref)})

    @pl.when((lax.rem(s, 2) == 0) & (s > 0) & (s < last))
    def _():
        body(pa_ref, pb_ref)

    @pl.when(lax.rem(s, 2) == 1)
    def _():
        body(pb_ref, pa_ref)

    @pl.when(s == last)
    def _():
        run(["M"] * n_m + ["O"] * n_o, {"M": mixer_stages(pb_ref), "O": out_proj_stages()})

    @pl.when((t == nt - 1) & (s > 0))
    def _():
        for h in range(H_B):
            off = h * DV_B - WIN_START[h]
            blk = sbd_ref[h * DK_B:(h + 1) * DK_B, WIN_START[h]:WIN_START[h] + WIN]
            if off:
                blk = pltpu.roll(blk, WIN - off, 1)
            st_ref[0, h] = blk[:, :DV_B]


def _prompt_layer(xp, w_in_bf, w_out_bf, nw, fw, memkv, wa, abt, avw, bwa, bba, onw, *, batch, seq):
    nt = seq // PT
    nblk = batch * nt
    assert nblk % 2 == 0
    cur = lambda s: jnp.maximum(s - 1, 0)
    const = lambda *shape: pl.BlockSpec(shape, lambda s: (0,) * len(shape))
    resident = lambda *shape: pl.BlockSpec(shape, lambda s: (0,) * len(shape),
                                           pipeline_mode=pl.Buffered(1))
    return pl.pallas_call(
        functools.partial(_prompt_layer_kernel, nt=nt),
        grid=(nblk + 1,),
        in_specs=[pl.BlockSpec((PT, D_MODEL), lambda s: (jnp.minimum(s, nblk - 1), 0)),
                  resident(D_MODEL, D_PROJ_PAD), resident(D_MODEL, D_MODEL),
                  const(1, D_MODEL), const(1, D_MODEL),
                  pl.BlockSpec((None, N_MEM, D_X), lambda s: (0, cur(s) // nt, 0)),
                  pl.BlockSpec((None, N_MEM, D_X), lambda s: (1, cur(s) // nt, 0)),
                  const(H_A, CHUNK_A, CHUNK_A), const(CHUNK_A, H_A), const(1, D_A),
                  const(LANES, D_BK), const(1, D_BK), const(1, D_B)],
        out_specs=[pl.BlockSpec((PT, D_MODEL), lambda s: (cur(s), 0)),
                   pl.BlockSpec((1, H_B, DK_B, DV_B), lambda s: (cur(s) // nt, 0, 0, 0))],
        out_shape=[jax.ShapeDtypeStruct((batch * seq, D_MODEL), F32),
                   jax.ShapeDtypeStruct((batch, H_B, DK_B, DV_B), F32)],
        scratch_shapes=[pltpu.VMEM((PT, D_PROJ_PAD), BF16),
                        pltpu.VMEM((PT, D_PROJ_PAD), BF16),
                        pltpu.VMEM((PT, D_MODEL), F32),
                        pltpu.VMEM((PT, D_MODEL), BF16),
                        pltpu.VMEM((PT, D_MODEL), BF16),
                        pltpu.VMEM((D_BK, D_B), F32),
                        pltpu.VMEM((PT, D_B), F32)],
        compiler_params=pltpu.CompilerParams(
            dimension_semantics=("arbitrary",), vmem_limit_bytes=VMEM_LIMIT),
        name="prompt_layer",
    )(xp, w_in_bf, w_out_bf, nw.reshape(1, D_MODEL), fw.reshape(1, D_MODEL), memkv, memkv,
      wa, abt, avw, bwa, bba, onw)


NS = 8
TS = 8
SB = NS * TS
SBP = 128


def _sample_mixer_kernel(proj_ref, st_ref, ck_ref, cv_ref, wa_ref, abt_ref, avw_ref, bwa_ref,
                         bba_ref, onw_ref, out_ref, stn_ref, cvs_ref,
                         qin_ref, xq_ref, kot_ref, lat_ref, vhm_ref, ghm_ref, ohm_ref, ox_ref):
    ri = _iota((SB, SB), 0)
    ci = _iota((SB, SB), 1)
    same_seq = jnp.right_shift(ri, 3) == jnp.right_shift(ci, 3)
    causal = same_seq & (ci <= ri)

    a_br, vn = _group_a(proj_ref, wa_ref, abt_ref, avw_ref, causal)
    out_ref[:, 0:D_A] = a_br.astype(BF16)
    cvs_ref[...] = vn

    q = proj_ref[:, C_BQ:C_BQ + D_BK].astype(F32)
    k = proj_ref[:, C_BK:C_BK + D_BK].astype(F32)
    vb = proj_ref[:, C_BV:C_BV + D_B]
    bgb = proj_ref[:, C_BG:C_BG + D_B]
    pre = _dot(proj_ref[:, C_BR:C_BR + LANES], bwa_ref[...]) + bba_ref[...]
    log_a = _log_sigmoid(pre) * (1.0 / GATE_TAU)
    causal_bf = jnp.where(causal, 1.0, 0.0).astype(BF16)
    seq_bf = jnp.where(same_seq, 1.0, 0.0).astype(BF16)
    hi, lo = _split(log_a)
    cum = _dot(causal_bf, hi) + _dot(causal_bf, lo)
    tot = _dot(seq_bf, hi) + _dot(seq_bf, lo)
    q_in = q * (DK_B ** -0.5) * jnp.exp(cum)
    k_in = k * jnp.exp(-cum)
    k_out = k * jnp.exp(tot - cum)
    qin_ref[...] = q_in
    zpad = jnp.zeros((SBP - SB, D_BK), F32)
    kot_ref[...] = jnp.concatenate([k_out, zpad], axis=0).T.astype(BF16)
    lat_ref[...] = jnp.concatenate([log_a, zpad], axis=0).T
    xq_ref[...] = proj_ref[:, C_XQ:C_XQ + D_X].astype(F32)

    wins = _gla_intra_windows(q_in, k_in, vb, causal)
    sel_r = _iota((D_B, DV_B), 0)
    sel_c = _iota((D_B, DV_B), 1)
    vhm_ref[...] = jnp.zeros_like(vhm_ref)
    for h in range(H_B):
        off = h * DV_B - WIN_START[h]
        w = wins[h]
        if off:
            w = pltpu.roll(w, WIN - off, 1)
        ohm_ref[h] = w[:, :DV_B]
        sel = jnp.where(sel_r == sel_c + h * DV_B, 1.0, 0.0).astype(BF16)
        vhm_ref[h, 0:SB, :] = _dot(vb, sel)
        ghm_ref[h] = _dot(bgb, sel)

    mask_x = (jnp.right_shift(_iota((H_X * TS, N_MEM * H_X), 0), 3)
              == jnp.bitwise_and(_iota((H_X * TS, N_MEM * H_X), 1), H_X - 1))
    mask_b = jnp.right_shift(_iota((H_B * TS, D_BK), 0), 3) == _head_id(_iota((H_B * TS, D_BK), 1), DK_B, H_B)
    row_seq = jnp.right_shift(_iota((SBP, DV_B), 0), 3)
    lane_seq = jnp.right_shift(_iota((D_BK, SBP), 1), 3)

    def per_seq(s, carry):
        r0 = pl.multiple_of(s * TS, TS)
        q8 = xq_ref[pl.ds(r0, TS), :]
        q32 = jnp.concatenate([q8[:, h * HD_X:(h + 1) * HD_X] for h in range(H_X)], axis=0)
        sc = _dot_nt(q32.astype(BF16), ck_ref[s].astype(BF16)) * (HD_X ** -0.5)
        sc = jnp.where(mask_x, sc, -1e30)
        e = jnp.exp(sc - jnp.max(sc, axis=-1, keepdims=True))
        den = jnp.sum(e, axis=-1, keepdims=True)
        o = _dot(e.astype(BF16), cv_ref[s].astype(BF16)) / den
        ox_ref[pl.ds(r0, TS), :] = jnp.concatenate(
            [o[h * TS:(h + 1) * TS] for h in range(H_X)], axis=1)
        qi8 = qin_ref[pl.ds(r0, TS), :]
        qbd2 = jnp.where(mask_b, jnp.concatenate([qi8] * H_B, axis=0), 0.0).astype(BF16)
        s0 = jnp.concatenate([st_ref[s, h] for h in range(H_B)], axis=0)
        o_inter = _dot(qbd2, s0.astype(BF16))
        in_seq = row_seq == s
        for h in range(H_B):
            ohm_ref[h, pl.ds(r0, TS), :] += o_inter[h * TS:(h + 1) * TS]
        dec = jnp.exp(jnp.sum(jnp.where(lane_seq == s, lat_ref[...], 0.0), axis=1, keepdims=True))
        kot = kot_ref[...]
        kv = jnp.concatenate(
            [_dot(kot[h * DK_B:(h + 1) * DK_B],
                  jnp.where(in_seq, vhm_ref[h], 0.0).astype(BF16)) for h in range(H_B)], axis=0)
        s_new = s0 * dec + kv
        for h in range(H_B):
            stn_ref[s, h] = s_new[h * DK_B:(h + 1) * DK_B]
        return carry

    lax.fori_loop(0, NS, per_seq, 0, unroll=2)

    selt_r = _iota((DV_B, D_B), 0)
    selt_c = _iota((DV_B, D_B), 1)
    b_br = jnp.zeros((SB, D_B), F32)
    for h in range(H_B):
        o_h = ohm_ref[h]
        ms = jnp.mean(o_h * o_h, axis=-1, keepdims=True)
        ob = (o_h * lax.rsqrt(ms + EPS) * onw_ref[...] * _silu(ghm_ref[h])).astype(BF16)
        selt = jnp.where(selt_c == selt_r + h * DV_B, 1.0, 0.0).astype(BF16)
        b_br = b_br + _dot(ob, selt)
    out_ref[:, D_A:D_A + D_B] = b_br.astype(BF16)

    xg = proj_ref[:, C_XG:C_XG + D_X].astype(F32)
    out_ref[:, D_A + D_B:D_MODEL] = (ox_ref[...] * _silu(xg)).astype(BF16)


def _sample_mixer(proj, state, ck, cv, wa, abt, avw, bwa, bba, onw):
    nseq = state.shape[1]
    const = lambda *shape: pl.BlockSpec(shape, lambda i: (0,) * len(shape))
    return pl.pallas_call(
        _sample_mixer_kernel,
        grid=(nseq // NS,),
        in_specs=[pl.BlockSpec((SB, D_PROJ), lambda i: (i, 0)),
                  pl.BlockSpec((None, NS, H_B, DK_B, DV_B), lambda i: (0, i, 0, 0, 0)),
                  pl.BlockSpec((NS, N_MEM * H_X, HD_X), lambda i: (i, 0, 0)),
                  pl.BlockSpec((NS, N_MEM * H_X, HD_X), lambda i: (i, 0, 0)),
                  const(H_A, SB, SB), const(SB, H_A), const(1, D_A),
                  const(LANES, D_BK), const(1, D_BK), const(1, DV_B)],
        out_specs=[pl.BlockSpec((SB, D_MODEL), lambda i: (i, 0)),
                   pl.BlockSpec((None, NS, H_B, DK_B, DV_B), lambda i: (0, i, 0, 0, 0)),
                   pl.BlockSpec((SB, D_A), lambda i: (i, 0))],
        out_shape=[jax.ShapeDtypeStruct((nseq * TS, D_MODEL), BF16),
                   jax.ShapeDtypeStruct((1, nseq, H_B, DK_B, DV_B), F32),
                   jax.ShapeDtypeStruct((nseq * TS, D_A), F32)],
        scratch_shapes=[pltpu.VMEM((SB, D_BK), F32),
                        pltpu.VMEM((SB, D_X), F32),
                        pltpu.VMEM((D_BK, SBP), BF16),
                        pltpu.VMEM((D_BK, SBP), F32),
                        pltpu.VMEM((H_B, SBP, DV_B), F32),
                        pltpu.VMEM((H_B, SB, DV_B), F32),
                        pltpu.VMEM((H_B, SB, DV_B), F32),
                        pltpu.VMEM((SB, D_X), F32)],
        compiler_params=pltpu.CompilerParams(
            dimension_semantics=("arbitrary",), vmem_limit_bytes=VMEM_LIMIT),
        name="sample_mixer",
    )(proj, state, ck, cv, wa, abt, avw, bwa, bba, onw)


def kernel(x_prompt, x_sample, mem_prompt, state_gla, cache_mem_k, cache_mem_v, norm_w, w_in,
           a_vnorm_w, a_ws, a_bs, b_wa, b_ba, b_onorm_w, mem_norm_w, w_mem_kv, w_out, final_norm_w):
    batch, seq, _ = x_prompt.shape
    nseq, tdec, _ = x_sample.shape
    depth = w_in.shape[0]
    assert depth == 1 and tdec == TS and seq % PT == 0 and nseq % NS == 0

    w_in_bf, w_out_bf = _weight_prep(jnp.transpose(w_in[0]), w_out[0])
    last_cols = D_PROJ_PAD - (D_PROJ_PAD // IN_BN) * IN_BN
    bwa = jnp.concatenate([b_wa[0], jnp.zeros((LANES - GATE_RANK, D_BK), F32)], axis=0).astype(BF16)
    bba = b_ba[0].reshape(1, D_BK)
    avw = a_vnorm_w[0].reshape(1, D_A)
    onw_p = jnp.tile(b_onorm_w[0], H_B).reshape(1, D_B)
    onw_s = b_onorm_w[0].reshape(1, DV_B)
    wa_p = a_ws[0]
    abt_p = a_bs[0].T
    wa_s = jnp.tile(a_ws[0][:, :TS, :TS], (1, NS, NS))
    abt_s = jnp.tile(a_bs[0][:, :TS], (1, NS)).T

    xp = x_prompt.reshape(batch * seq, D_MODEL)
    xs = x_sample.reshape(nseq * TS, D_MODEL)
    mem = mem_prompt.reshape(batch * N_MEM, D_MODEL)

    memkv = _mem_kv(mem, mem_norm_w[0], w_mem_kv[0], batch=batch)
    y_p, st_p = _prompt_layer(xp, w_in_bf, w_out_bf, norm_w[0], final_norm_w, memkv, wa_p, abt_p,
                              avw, bwa, bba, onw_p, batch=batch, seq=seq)

    proj_s = _norm_matmul(xs, norm_w[0], w_in_bf, D_PROJ, bm=nseq * TS, bn=IN_BN, transposed=False,
                          out_dtype=BF16, last_cols=last_cols, name="in_proj_s")
    br_s, st_s, cvs = _sample_mixer(
        proj_s, state_gla,
        cache_mem_k.reshape(nseq, N_MEM * H_X, HD_X), cache_mem_v.reshape(nseq, N_MEM * H_X, HD_X),
        wa_s, abt_s, avw, bwa, bba, onw_s)
    y_s = _out_proj(br_s, w_out_bf, xs, final_norm_w, bm=512, name="out_proj_s")

    return (y_p.reshape(batch, seq, D_MODEL),
            y_s.reshape(nseq, TS, D_MODEL),
            memkv[0].reshape(1, batch, N_MEM, H_X, HD_X),
            memkv[1].reshape(1, batch, N_MEM, H_X, HD_X),
            st_p.reshape(1, batch, H_B, DK_B, DV_B),
            st_s,
            cvs.reshape(1, nseq, TS, D_A))
```

```python
import functools

import jax
import jax.numpy as jnp
from jax import lax
from jax.experimental import pallas as pl
from jax.experimental.pallas import tpu as pltpu

F32 = jnp.float32
BF16 = jnp.bfloat16

D_MODEL = 2048
D_A = 768
H_A = 4
HD_A = 192
CHUNK_A = 128
D_B = 768
H_B = 4
DV_B = 192
DK_B = 96
D_BK = 384
GATE_RANK = 16
GATE_TAU = 16.0
CHUNK_B = 64
D_X = 512
H_X = 4
HD_X = 128
N_MEM = 256
EPS = 1e-6

LANES = 128
MXU_COLS = 256
C_AU, C_AV, C_AG = 0, 768, 1536
C_BQ, C_BK, C_BV, C_BG = 2304, 2688, 3072, 3840
C_XQ, C_XG, C_BR = 4608, 5120, 5632
D_PROJ = 5760
D_PROJ_PAD = 5888
WIN_START = (0, 128, 384, 512)
WIN = 256
IN_BN = 768
OUT_BN = 512

VMEM_LIMIT = 60 * 1024 * 1024


def _dot(a, b):
    return jnp.dot(a, b, preferred_element_type=F32)


def _dot_nt(a, b):
    return lax.dot_general(a, b, (((1,), (1,)), ((), ())), preferred_element_type=F32)


def _dot_tn(a, b):
    return lax.dot_general(a, b, (((0,), (0,)), ((), ())), preferred_element_type=F32)


def _split(x):
    hi = x.astype(BF16)
    lo = (x - hi.astype(F32)).astype(BF16)
    return hi, lo


def _silu(x):
    return x / (1.0 + jnp.exp(-x))


def _log_sigmoid(x):
    return jnp.minimum(x, 0.0) - jnp.log1p(jnp.exp(-jnp.abs(x)))


def _head_id(idx, width, n):
    h = jnp.zeros_like(idx)
    for i in range(1, n):
        h = h + (idx >= i * width).astype(jnp.int32)
    return h


def _iota(shape, dim):
    return lax.broadcasted_iota(jnp.int32, shape, dim)


def _merge_windows(r):
    rows = r[0].shape[0]
    lo = _iota((rows, LANES), 1) < 64
    tiles = [r[0][:, :LANES], jnp.where(lo, r[0][:, LANES:], r[1][:, :LANES]), r[1][:, LANES:],
             r[2][:, :LANES], jnp.where(lo, r[2][:, LANES:], r[3][:, :LANES]), r[3][:, LANES:]]
    return jnp.concatenate(tiles, axis=1)


def _group_a(proj_ref, wa_ref, abt_ref, avw_ref, mask):
    u = proj_ref[:, C_AU:C_AU + D_A].astype(F32)
    v = proj_ref[:, C_AV:C_AV + D_A].astype(F32)
    g = proj_ref[:, C_AG:C_AG + D_A].astype(F32)
    rows = u.shape[0]
    ms = jnp.mean(v * v, axis=-1, keepdims=True)
    vn = v * lax.rsqrt(ms + EPS) * avw_ref[...]
    vb = vn.astype(BF16)
    r = []
    for gi in range(H_A):
        w = jnp.where(mask, wa_ref[gi], 0.0).astype(BF16)
        r.append(_dot(w, vb[:, WIN_START[gi]:WIN_START[gi] + WIN]))
    mixed = _merge_windows(r)
    col_g = _head_id(_iota((rows, D_A), 1), HD_A, H_A)
    bias = jnp.zeros((rows, D_A), F32)
    for gi in range(H_A):
        bias = jnp.where(col_g == gi, abt_ref[:, gi:gi + 1], bias)
    return u * (mixed + bias) * _silu(g), vn


def _gla_intra_windows(q_in, k_in, vb, mask):
    rows = q_in.shape[0]
    col_h = _head_id(_iota((rows, D_BK), 1), DK_B, H_B)
    kb = k_in.astype(BF16)
    r = []
    for h in range(H_B):
        qh = jnp.where(col_h == h, q_in, 0.0).astype(BF16)
        s = jnp.where(mask, _dot_nt(qh, kb), 0.0).astype(BF16)
        r.append(_dot(s, vb[:, WIN_START[h]:WIN_START[h] + WIN]))
    return r


def _norm_matmul_kernel(x_ref, nw_ref, w_ref, o_ref, h_ref, *, transposed, last_cols):
    j = pl.program_id(1)
    nj = pl.num_programs(1)

    @pl.when(j == 0)
    def _():
        x = x_ref[...]
        ms = jnp.mean(x * x, axis=-1, keepdims=True)
        h_ref[...] = (x * lax.rsqrt(ms + EPS) * nw_ref[...]).astype(BF16)

    mm = _dot_nt if transposed else _dot
    if last_cols is None:
        o_ref[...] = mm(h_ref[...], w_ref[...].astype(BF16)).astype(o_ref.dtype)
    else:
        @pl.when(j < nj - 1)
        def _():
            o_ref[...] = mm(h_ref[...], w_ref[...].astype(BF16)).astype(o_ref.dtype)

        @pl.when(j == nj - 1)
        def _():
            w = w_ref[:last_cols, :] if transposed else w_ref[:, :last_cols]
            o_ref[:, :last_cols] = mm(h_ref[...], w.astype(BF16)).astype(o_ref.dtype)


def _norm_matmul(x, nw, w, n_out, *, bm, bn, transposed, out_dtype, split_out=False,
                 last_cols=None, name):
    m, k = x.shape
    n = w.shape[0] if transposed else w.shape[1]
    nj = -(-n // bn)
    assert m % bm == 0
    if split_out:
        out_shape = jax.ShapeDtypeStruct((nj, m, bn), out_dtype)
        out_spec = pl.BlockSpec((None, bm, bn), lambda i, j: (j, i, 0))
    else:
        out_shape = jax.ShapeDtypeStruct((m, n_out), out_dtype)
        out_spec = pl.BlockSpec((bm, bn), lambda i, j: (i, j))
    if transposed:
        w_spec = pl.BlockSpec((bn, k), lambda i, j: (j, 0))
    else:
        w_spec = pl.BlockSpec((k, bn), lambda i, j: (0, j))
    return pl.pallas_call(
        functools.partial(_norm_matmul_kernel, transposed=transposed, last_cols=last_cols),
        grid=(m // bm, nj),
        in_specs=[pl.BlockSpec((bm, k), lambda i, j: (i, 0)),
                  pl.BlockSpec((1, k), lambda i, j: (0, 0)),
                  w_spec],
        out_specs=out_spec,
        out_shape=out_shape,
        scratch_shapes=[pltpu.VMEM((bm, k), BF16)],
        compiler_params=pltpu.CompilerParams(
            dimension_semantics=("arbitrary", "arbitrary"), vmem_limit_bytes=VMEM_LIMIT),
        name=name,
    )(x, nw.reshape(1, k), w)


def _mem_kv_kernel(x_ref, nw_ref, w_ref, o_ref):
    x = x_ref[...]
    ms = jnp.mean(x * x, axis=-1, keepdims=True)
    h = (x * lax.rsqrt(ms + EPS) * nw_ref[...]).astype(BF16)
    kv = _dot(h, w_ref[...].astype(BF16))
    o_ref[0] = kv[:, :D_X]
    o_ref[1] = kv[:, D_X:]


def _mem_kv(mem, nw, w, *, batch):
    return pl.pallas_call(
        _mem_kv_kernel,
        grid=(batch,),
        in_specs=[pl.BlockSpec((N_MEM, D_MODEL), lambda i: (i, 0)),
                  pl.BlockSpec((1, D_MODEL), lambda i: (0, 0)),
                  pl.BlockSpec((D_MODEL, 2 * D_X), lambda i: (0, 0))],
        out_specs=pl.BlockSpec((2, N_MEM, D_X), lambda i: (0, i, 0)),
        out_shape=jax.ShapeDtypeStruct((2, batch * N_MEM, D_X), F32),
        compiler_params=pltpu.CompilerParams(
            dimension_semantics=("arbitrary",), vmem_limit_bytes=VMEM_LIMIT),
        name="mem_kv",
    )(mem, nw.reshape(1, D_MODEL), w)


PREP_BN = MXU_COLS
N_MAIN_BLOCKS = C_BG // PREP_BN


GATE_BLOCK = D_PROJ_PAD // PREP_BN - 1
N_IN_BLOCKS = GATE_BLOCK + 1
N_OUT_BLOCKS = D_MODEL // PREP_BN
PREP_BUFS = 4


def _weight_prep_kernel(wt_hbm, wo_hbm, oi_ref, oo_ref, buf, sem):
    r = pl.program_id(0)
    slot = lax.rem(r, PREP_BUFS)

    def in_copy(rr, sl):
        start = pl.multiple_of(jnp.where(rr < N_MAIN_BLOCKS, rr * PREP_BN, rr * PREP_BN + GATE_RANK), 8)
        return pltpu.make_async_copy(wt_hbm.at[pl.ds(start, PREP_BN), :], buf.at[sl], sem.at[sl])

    def gate_copy(sl):
        return pltpu.make_async_copy(wt_hbm.at[pl.ds(C_BG, GATE_RANK), :],
                                     buf.at[sl, pl.ds(0, GATE_RANK), :], sem.at[sl])

    def out_copy(rr, sl):
        start = pl.multiple_of((rr - N_IN_BLOCKS) * PREP_BN, PREP_BN)
        return pltpu.make_async_copy(wo_hbm.at[pl.ds(start, PREP_BN), :], buf.at[sl], sem.at[sl])

    def start_fetch(rr, sl):
        @pl.when(rr < GATE_BLOCK)
        def _():
            in_copy(rr, sl).start()

        @pl.when(rr == GATE_BLOCK)
        def _():
            gate_copy(sl).start()

        @pl.when(rr > GATE_BLOCK)
        def _():
            out_copy(rr, sl).start()

    @pl.when(r == 0)
    def _():
        for ahead in range(PREP_BUFS - 1):
            start_fetch(r + ahead, ahead)

    nxt = r + PREP_BUFS - 1

    @pl.when(nxt < pl.num_programs(0))
    def _():
        start_fetch(nxt, lax.rem(nxt, PREP_BUFS))

    @pl.when(r < GATE_BLOCK)
    def _():
        in_copy(r, slot).wait()
        oi_ref[...] = buf[slot].T.astype(BF16)

    @pl.when(r == GATE_BLOCK)
    def _():
        gate_copy(slot).wait()
        rows = jnp.concatenate([buf[slot, 0:GATE_RANK, :],
                                jnp.zeros((PREP_BN - GATE_RANK, D_MODEL), F32)], axis=0)
        oi_ref[...] = rows.T.astype(BF16)

    @pl.when(r > GATE_BLOCK)
    def _():
        out_copy(r, slot).wait()
        oo_ref[...] = buf[slot].astype(BF16)


def _weight_prep(w_t, w_out):
    k = w_t.shape[1]
    return pl.pallas_call(
        _weight_prep_kernel,
        grid=(N_IN_BLOCKS + N_OUT_BLOCKS,),
        in_specs=[pl.BlockSpec(memory_space=pl.ANY), pl.BlockSpec(memory_space=pl.ANY)],
        out_specs=[pl.BlockSpec((k, PREP_BN), lambda r: (0, jnp.minimum(r, GATE_BLOCK))),
                   pl.BlockSpec((PREP_BN, D_MODEL), lambda r: (jnp.maximum(r - N_IN_BLOCKS, 0), 0))],
        out_shape=[jax.ShapeDtypeStruct((k, D_PROJ_PAD), BF16),
                   jax.ShapeDtypeStruct((D_MODEL, D_MODEL), BF16)],
        scratch_shapes=[pltpu.VMEM((PREP_BUFS, PREP_BN, k), F32),
                        pltpu.SemaphoreType.DMA((PREP_BUFS,))],
        compiler_params=pltpu.CompilerParams(
            dimension_semantics=("arbitrary",), vmem_limit_bytes=VMEM_LIMIT),
        name="weight_prep",
    )(w_t, w_out)


def _out_proj_kernel(br_ref, w_ref, x_ref, fw_ref, y_ref):
    acc = _dot(br_ref[...], w_ref[...]) + x_ref[...]
    ms = jnp.mean(acc * acc, axis=-1, keepdims=True)
    y_ref[...] = acc * lax.rsqrt(ms + EPS) * fw_ref[...]


def _out_proj(br, w, x, fw, *, bm, name):
    m, k = br.shape
    n = w.shape[1]
    return pl.pallas_call(
        _out_proj_kernel,
        grid=(m // bm,),
        in_specs=[pl.BlockSpec((bm, k), lambda i: (i, 0)),
                  pl.BlockSpec((k, n), lambda i: (0, 0)),
                  pl.BlockSpec((bm, n), lambda i: (i, 0)),
                  pl.BlockSpec((1, n), lambda i: (0, 0))],
        out_specs=pl.BlockSpec((bm, n), lambda i: (i, 0)),
        out_shape=jax.ShapeDtypeStruct((m, n), F32),
        compiler_params=pltpu.CompilerParams(
            dimension_semantics=("arbitrary",), vmem_limit_bytes=VMEM_LIMIT),
        name=name,
    )(br, w, x, fw.reshape(1, n))


PT = 256
N_MIXER_STAGES = 2 + 4 * (PT // CHUNK_B) + 1 + 2 * H_X


def _group_a_chunks(proj_ref, wa_ref, abt_ref, avw_ref):
    u = proj_ref[:, C_AU:C_AU + D_A].astype(F32)
    v = proj_ref[:, C_AV:C_AV + D_A].astype(F32)
    g = proj_ref[:, C_AG:C_AG + D_A].astype(F32)
    n_chunks = u.shape[0] // CHUNK_A
    ms = jnp.mean(v * v, axis=-1, keepdims=True)
    vb = (v * lax.rsqrt(ms + EPS) * avw_ref[...]).astype(BF16)
    tril = _iota((CHUNK_A, CHUNK_A), 1) <= _iota((CHUNK_A, CHUNK_A), 0)
    r = []
    for gi in range(H_A):
        w = jnp.where(tril, wa_ref[gi], 0.0).astype(BF16)
        ws = slice(WIN_START[gi], WIN_START[gi] + WIN)
        rhs = jnp.concatenate([vb[c * CHUNK_A:(c + 1) * CHUNK_A, ws] for c in range(n_chunks)], axis=1)
        res = _dot(w, rhs)
        r.append(jnp.concatenate([res[:, c * WIN:(c + 1) * WIN] for c in range(n_chunks)], axis=0))
    mixed = _merge_windows(r)
    col_g = _head_id(_iota((CHUNK_A, D_A), 1), HD_A, H_A)
    bias = jnp.zeros((CHUNK_A, D_A), F32)
    for gi in range(H_A):
        bias = jnp.where(col_g == gi, abt_ref[:, gi:gi + 1], bias)
    bias = jnp.concatenate([bias] * n_chunks, axis=0)
    return u * (mixed + bias) * _silu(g)


def _prompt_block(proj_ref, out_ref, mk_ref, mv_ref, wa_ref, abt_ref, avw_ref, bwa_ref, bba_ref,
                  onw_ref, sbd_ref, o_ref):
    out_ref[:, 0:D_A] = _group_a_chunks(proj_ref, wa_ref, abt_ref, avw_ref).astype(BF16)
    yield

    q = proj_ref[:, C_BQ:C_BQ + D_BK].astype(F32)
    k = proj_ref[:, C_BK:C_BK + D_BK].astype(F32)
    vb = proj_ref[:, C_BV:C_BV + D_B]
    pre = _dot(proj_ref[:, C_BR:C_BR + LANES], bwa_ref[...]) + bba_ref[...]
    log_a = _log_sigmoid(pre) * (1.0 / GATE_TAU)
    yield
    r64 = _iota((CHUNK_B, CHUNK_B), 0)
    c64 = _iota((CHUNK_B, CHUNK_B), 1)
    tril = c64 <= r64
    tril_bf = jnp.where(tril, 1.0, 0.0).astype(BF16)
    row_pad = jnp.zeros((LANES - CHUNK_B, D_BK), F32)
    v_pad = jnp.zeros((LANES - CHUNK_B, WIN), BF16)
    win_col = _iota((CHUNK_B, WIN), 1)
    col_hq = _head_id(_iota((CHUNK_B, D_BK), 1), DK_B, H_B)
    tril_heads = (_iota((H_B * CHUNK_B, CHUNK_B), 1)
                  <= jnp.bitwise_and(_iota((H_B * CHUNK_B, CHUNK_B), 0), CHUNK_B - 1))
    for c in range(PT // CHUNK_B):
        sl = slice(c * CHUNK_B, (c + 1) * CHUNK_B)
        hi, lo = _split(log_a[sl])
        cum2 = _dot(tril_bf, jnp.concatenate([hi, lo], axis=1))
        yield
        cum = cum2[:, :D_BK] + cum2[:, D_BK:]
        tot_row = cum[CHUNK_B - 1:CHUNK_B, :]
        la_t = jnp.concatenate([log_a[sl], row_pad], axis=0).T
        q_in = q[sl] * (DK_B ** -0.5) * jnp.exp(cum)
        k_in = k[sl] * jnp.exp(-cum)
        k_out = k[sl] * jnp.exp(tot_row - cum)
        q_heads = jnp.concatenate([jnp.where(col_hq == h, q_in, 0.0) for h in range(H_B)], axis=0)
        sc = _dot_nt(q_heads.astype(BF16), k_in.astype(BF16))
        yield
        sc = jnp.where(tril_heads, sc, 0.0).astype(BF16)
        o_intra = _merge_windows(
            [_dot(sc[h * CHUNK_B:(h + 1) * CHUNK_B], vb[sl, WIN_START[h]:WIN_START[h] + WIN])
             for h in range(H_B)])
        s_old = sbd_ref[...]
        o_inter = _dot(q_in.astype(BF16), s_old.astype(BF16))
        yield
        kot = jnp.concatenate([k_out, row_pad], axis=0).T.astype(BF16)
        for h in range(H_B):
            rs = slice(h * DK_B, (h + 1) * DK_B)
            ws = slice(WIN_START[h], WIN_START[h] + WIN)
            dec = jnp.exp(jnp.sum(la_t[rs], axis=1, keepdims=True))
            lo_col = h * DV_B - WIN_START[h]
            in_head = (win_col >= lo_col) & (win_col < lo_col + DV_B)
            v_h = jnp.where(in_head, vb[sl, ws], jnp.zeros((), BF16))
            kv = _dot(kot[rs], jnp.concatenate([v_h, v_pad], axis=0))
            sbd_ref[rs, ws] = s_old[rs, ws] * dec + kv
        o_ref[sl, :] = o_intra + o_inter
        yield
    o = o_ref[...]
    o2 = o * o
    t = [o2[:, j * LANES:(j + 1) * LANES] for j in range(D_B // LANES)]
    lo = _iota((PT, LANES), 1) < 64
    t1a = jnp.where(lo, t[1], 0.0)
    t4a = jnp.where(lo, t[4], 0.0)
    sums = (t[0] + t1a, (t[1] - t1a) + t[2], t[3] + t4a, (t[4] - t4a) + t[5])
    inv = [lax.rsqrt(jnp.sum(x, axis=-1, keepdims=True) * (1.0 / DV_B) + EPS) for x in sums]
    inv = jnp.concatenate(
        [jnp.broadcast_to(inv[0], (PT, LANES)), jnp.where(lo, inv[0], inv[1]),
         jnp.broadcast_to(inv[1], (PT, LANES)), jnp.broadcast_to(inv[2], (PT, LANES)),
         jnp.where(lo, inv[2], inv[3]), jnp.broadcast_to(inv[3], (PT, LANES))], axis=1)
    o_n = o * inv * onw_ref[...]
    bg = proj_ref[:, C_BG:C_BG + D_B].astype(F32)
    out_ref[:, D_A:D_A + D_B] = (o_n * _silu(bg)).astype(BF16)
    yield

    for h in range(H_X):
        hs = slice(h * HD_X, (h + 1) * HD_X)
        qh = proj_ref[:, C_XQ + h * HD_X:C_XQ + (h + 1) * HD_X]
        s = _dot_nt(qh, mk_ref[:, hs].astype(BF16)) * (HD_X ** -0.5)
        yield
        e = jnp.exp(s - jnp.max(s, axis=-1, keepdims=True))
        den = jnp.sum(e, axis=-1, keepdims=True)
        ox = _dot(e.astype(BF16), mv_ref[:, hs].astype(BF16)) / den
        xg = proj_ref[:, C_XG + h * HD_X:C_XG + (h + 1) * HD_X].astype(F32)
        out_ref[:, D_A + D_B + h * HD_X:D_A + D_B + (h + 1) * HD_X] = (ox * _silu(xg)).astype(BF16)
        yield


def _prompt_layer_kernel(xn_ref, wi_ref, wo_ref, nw_ref, fw_ref, mk_ref, mv_ref, wa_ref, abt_ref,
                         avw_ref, bwa_ref, bba_ref, onw_ref, y_ref, st_ref,
                         pa_ref, pb_ref, xk_ref, br_ref, h_ref, sbd_ref, o_ref, *, nt):
    s = pl.program_id(0)
    cur = jnp.maximum(s - 1, 0)
    t = lax.rem(cur, nt)

    @pl.when(t == 0)
    def _():
        sbd_ref[...] = jnp.zeros_like(sbd_ref)

    def in_proj_stages(pn_ref):
        x = xn_ref[...]
        ms = jnp.mean(x * x, axis=-1, keepdims=True)
        h_ref[...] = (x * lax.rsqrt(ms + EPS) * nw_ref[...]).astype(BF16)
        yield
        for c0 in range(0, D_PROJ_PAD, MXU_COLS):
            cols = pl.ds(c0, MXU_COLS)
            pn_ref[:, cols] = _dot(h_ref[...], wi_ref[:, cols]).astype(BF16)
            yield
        xk_ref[...] = xn_ref[...]

    def out_proj_stages():
        ssq = jnp.zeros((PT, 1), F32)
        for c0 in range(0, D_MODEL, OUT_BN):
            cols = pl.ds(c0, OUT_BN)
            acc = _dot(br_ref[...], wo_ref[:, cols]) + xk_ref[:, cols]
            y_ref[:, cols] = acc
            ssq = ssq + jnp.sum(acc * acc, axis=-1, keepdims=True)
            yield
        y_ref[...] = y_ref[...] * lax.rsqrt(ssq * (1.0 / D_MODEL) + EPS) * fw_ref[...]
        yield

    def run(order, streams):
        for name in order:
            next(streams[name])
        for name, gen in streams.items():
            assert next(gen, "done") == "done", name

    def mixer_stages(pc_ref):
        return _prompt_block(pc_ref, br_ref, mk_ref, mv_ref, wa_ref, abt_ref, avw_ref, bwa_ref,
                             bba_ref, onw_ref, sbd_ref, o_ref)

    n_p = 1 + D_PROJ_PAD // MXU_COLS
    n_m = N_MIXER_STAGES
    n_o = 1 + D_MODEL // OUT_BN

    def body(pn_ref, pc_ref):
        order = ["P"]
        for i in range(max(n_p - 1, n_m)):
            order += ["M"] * (i < n_m) + ["P"] * (i < n_p - 1)
        order += ["O"] * n_o
        run(order, {"P": in_proj_stages(pn_ref), "M": mixer_stages(pc_ref), "O": out_proj_stages()})

    last = pl.num_programs(0) - 1

    @pl.when(s == 0)
    def _():
        run(["P"] * n_p, {"P": in_proj_stages(pa_ref)})

    @pl.when((lax.rem(s, 2) == 0) & (s > 0) & (s < last))
    def _():
        body(pa_ref, pb_ref)

    @pl.when(lax.rem(s, 2) == 1)
    def _():
        body(pb_ref, pa_ref)

    @pl.when(s == last)
    def _():
        run(["M"] * n_m + ["O"] * n_o, {"M": mixer_stages(pb_ref), "O": out_proj_stages()})

    @pl.when((t == nt - 1) & (s > 0))
    def _():
        for h in range(H_B):
            off = h * DV_B - WIN_START[h]
            blk = sbd_ref[h * DK_B:(h + 1) * DK_B, WIN_START[h]:WIN_START[h] + WIN]
            if off:
                blk = pltpu.roll(blk, WIN - off, 1)
            st_ref[0, h] = blk[:, :DV_B]


def _prompt_layer(xp, w_in_bf, w_out_bf, nw, fw, memkv, wa, abt, avw, bwa, bba, onw, *, batch, seq):
    nt = seq // PT
    nblk = batch * nt
    assert nblk % 2 == 0
    cur = lambda s: jnp.maximum(s - 1, 0)
    const = lambda *shape: pl.BlockSpec(shape, lambda s: (0,) * len(shape))
    resident = lambda *shape: pl.BlockSpec(shape, lambda s: (0,) * len(shape),
                                           pipeline_mode=pl.Buffered(1))
    return pl.pallas_call(
        functools.partial(_prompt_layer_kernel, nt=nt),
        grid=(nblk + 1,),
        in_specs=[pl.BlockSpec((PT, D_MODEL), lambda s: (jnp.minimum(s, nblk - 1), 0)),
                  resident(D_MODEL, D_PROJ_PAD), resident(D_MODEL, D_MODEL),
                  const(1, D_MODEL), const(1, D_MODEL),
                  pl.BlockSpec((None, N_MEM, D_X), lambda s: (0, cur(s) // nt, 0)),
                  pl.BlockSpec((None, N_MEM, D_X), lambda s: (1, cur(s) // nt, 0)),
                  const(H_A, CHUNK_A, CHUNK_A), const(CHUNK_A, H_A), const(1, D_A),
                  const(LANES, D_BK), const(1, D_BK), const(1, D_B)],
        out_specs=[pl.BlockSpec((PT, D_MODEL), lambda s: (cur(s), 0)),
                   pl.BlockSpec((1, H_B, DK_B, DV_B), lambda s: (cur(s) // nt, 0, 0, 0))],
        out_shape=[jax.ShapeDtypeStruct((batch * seq, D_MODEL), F32),
                   jax.ShapeDtypeStruct((batch, H_B, DK_B, DV_B), F32)],
        scratch_shapes=[pltpu.VMEM((PT, D_PROJ_PAD), BF16),
                        pltpu.VMEM((PT, D_PROJ_PAD), BF16),
                        pltpu.VMEM((PT, D_MODEL), F32),
                        pltpu.VMEM((PT, D_MODEL), BF16),
                        pltpu.VMEM((PT, D_MODEL), BF16),
                        pltpu.VMEM((D_BK, D_B), F32),
                        pltpu.VMEM((PT, D_B), F32)],
        compiler_params=pltpu.CompilerParams(
            dimension_semantics=("arbitrary",), vmem_limit_bytes=VMEM_LIMIT),
        name="prompt_layer",
    )(xp, w_in_bf, w_out_bf, nw.reshape(1, D_MODEL), fw.reshape(1, D_MODEL), memkv, memkv,
      wa, abt, avw, bwa, bba, onw)


NS = 8
TS = 8
SB = NS * TS
SBP = 128


def _sample_mixer_kernel(proj_ref, st_ref, ck_ref, cv_ref, wa_ref, abt_ref, avw_ref, bwa_ref,
                         bba_ref, onw_ref, out_ref, stn_ref, cvs_ref,
                         qin_ref, xq_ref, kot_ref, lat_ref, vhm_ref, ghm_ref, ohm_ref, ox_ref):
    ri = _iota((SB, SB), 0)
    ci = _iota((SB, SB), 1)
    same_seq = jnp.right_shift(ri, 3) == jnp.right_shift(ci, 3)
    causal = same_seq & (ci <= ri)

    a_br, vn = _group_a(proj_ref, wa_ref, abt_ref, avw_ref, causal)
    out_ref[:, 0:D_A] = a_br.astype(BF16)
    cvs_ref[...] = vn

    q = proj_ref[:, C_BQ:C_BQ + D_BK].astype(F32)
    k = proj_ref[:, C_BK:C_BK + D_BK].astype(F32)
    vb = proj_ref[:, C_BV:C_BV + D_B]
    bgb = proj_ref[:, C_BG:C_BG + D_B]
    pre = _dot(proj_ref[:, C_BR:C_BR + LANES], bwa_ref[...]) + bba_ref[...]
    log_a = _log_sigmoid(pre) * (1.0 / GATE_TAU)
    causal_bf = jnp.where(causal, 1.0, 0.0).astype(BF16)
    seq_bf = jnp.where(same_seq, 1.0, 0.0).astype(BF16)
    hi, lo = _split(log_a)
    cum = _dot(causal_bf, hi) + _dot(causal_bf, lo)
    tot = _dot(seq_bf, hi) + _dot(seq_bf, lo)
    q_in = q * (DK_B ** -0.5) * jnp.exp(cum)
    k_in = k * jnp.exp(-cum)
    k_out = k * jnp.exp(tot - cum)
    qin_ref[...] = q_in
    zpad = jnp.zeros((SBP - SB, D_BK), F32)
    kot_ref[...] = jnp.concatenate([k_out, zpad], axis=0).T.astype(BF16)
    lat_ref[...] = jnp.concatenate([log_a, zpad], axis=0).T
    xq_ref[...] = proj_ref[:, C_XQ:C_XQ + D_X].astype(F32)

    wins = _gla_intra_windows(q_in, k_in, vb, causal)
    sel_r = _iota((D_B, DV_B), 0)
    sel_c = _iota((D_B, DV_B), 1)
    vhm_ref[...] = jnp.zeros_like(vhm_ref)
    for h in range(H_B):
        off = h * DV_B - WIN_START[h]
        w = wins[h]
        if off:
            w = pltpu.roll(w, WIN - off, 1)
        ohm_ref[h] = w[:, :DV_B]
        sel = jnp.where(sel_r == sel_c + h * DV_B, 1.0, 0.0).astype(BF16)
        vhm_ref[h, 0:SB, :] = _dot(vb, sel)
        ghm_ref[h] = _dot(bgb, sel)

    mask_x = (jnp.right_shift(_iota((H_X * TS, N_MEM * H_X), 0), 3)
              == jnp.bitwise_and(_iota((H_X * TS, N_MEM * H_X), 1), H_X - 1))
    mask_b = jnp.right_shift(_iota((H_B * TS, D_BK), 0), 3) == _head_id(_iota((H_B * TS, D_BK), 1), DK_B, H_B)
    row_seq = jnp.right_shift(_iota((SBP, DV_B), 0), 3)
    lane_seq = jnp.right_shift(_iota((D_BK, SBP), 1), 3)

    def per_seq(s, carry):
        r0 = pl.multiple_of(s * TS, TS)
        q8 = xq_ref[pl.ds(r0, TS), :]
        q32 = jnp.concatenate([q8[:, h * HD_X:(h + 1) * HD_X] for h in range(H_X)], axis=0)
        sc = _dot_nt(q32.astype(BF16), ck_ref[s].astype(BF16)) * (HD_X ** -0.5)
        sc = jnp.where(mask_x, sc, -1e30)
        e = jnp.exp(sc - jnp.max(sc, axis=-1, keepdims=True))
        den = jnp.sum(e, axis=-1, keepdims=True)
        o = _dot(e.astype(BF16), cv_ref[s].astype(BF16)) / den
        ox_ref[pl.ds(r0, TS), :] = jnp.concatenate(
            [o[h * TS:(h + 1) * TS] for h in range(H_X)], axis=1)
        qi8 = qin_ref[pl.ds(r0, TS), :]
        qbd2 = jnp.where(mask_b, jnp.concatenate([qi8] * H_B, axis=0), 0.0).astype(BF16)
        s0 = jnp.concatenate([st_ref[s, h] for h in range(H_B)], axis=0)
        o_inter = _dot(qbd2, s0.astype(BF16))
        in_seq = row_seq == s
        for h in range(H_B):
            ohm_ref[h, pl.ds(r0, TS), :] += o_inter[h * TS:(h + 1) * TS]
        dec = jnp.exp(jnp.sum(jnp.where(lane_seq == s, lat_ref[...], 0.0), axis=1, keepdims=True))
        kot = kot_ref[...]
        kv = jnp.concatenate(
            [_dot(kot[h * DK_B:(h + 1) * DK_B],
                  jnp.where(in_seq, vhm_ref[h], 0.0).astype(BF16)) for h in range(H_B)], axis=0)
        s_new = s0 * dec + kv
        for h in range(H_B):
            stn_ref[s, h] = s_new[h * DK_B:(h + 1) * DK_B]
        return carry

    lax.fori_loop(0, NS, per_seq, 0, unroll=2)

    selt_r = _iota((DV_B, D_B), 0)
    selt_c = _iota((DV_B, D_B), 1)
    b_br = jnp.zeros((SB, D_B), F32)
    for h in range(H_B):
        o_h = ohm_ref[h]
        ms = jnp.mean(o_h * o_h, axis=-1, keepdims=True)
        ob = (o_h * lax.rsqrt(ms + EPS) * onw_ref[...] * _silu(ghm_ref[h])).astype(BF16)
        selt = jnp.where(selt_c == selt_r + h * DV_B, 1.0, 0.0).astype(BF16)
        b_br = b_br + _dot(ob, selt)
    out_ref[:, D_A:D_A + D_B] = b_br.astype(BF16)

    xg = proj_ref[:, C_XG:C_XG + D_X].astype(F32)
    out_ref[:, D_A + D_B:D_MODEL] = (ox_ref[...] * _silu(xg)).astype(BF16)


def _sample_mixer(proj, state, ck, cv, wa, abt, avw, bwa, bba, onw):
    nseq = state.shape[1]
    const = lambda *shape: pl.BlockSpec(shape, lambda i: (0,) * len(shape))
    return pl.pallas_call(
        _sample_mixer_kernel,
        grid=(nseq // NS,),
        in_specs=[pl.BlockSpec((SB, D_PROJ), lambda i: (i, 0)),
                  pl.BlockSpec((None, NS, H_B, DK_B, DV_B), lambda i: (0, i, 0, 0, 0)),
                  pl.BlockSpec((NS, N_MEM * H_X, HD_X), lambda i: (i, 0, 0)),
                  pl.BlockSpec((NS, N_MEM * H_X, HD_X), lambda i: (i, 0, 0)),
                  const(H_A, SB, SB), const(SB, H_A), const(1, D_A),
                  const(LANES, D_BK), const(1, D_BK), const(1, DV_B)],
        out_specs=[pl.BlockSpec((SB, D_MODEL), lambda i: (i, 0)),
                   pl.BlockSpec((None, NS, H_B, DK_B, DV_B), lambda i: (0, i, 0, 0, 0)),
                   pl.BlockSpec((SB, D_A), lambda i: (i, 0))],
        out_shape=[jax.ShapeDtypeStruct((nseq * TS, D_MODEL), BF16),
                   jax.ShapeDtypeStruct((1, nseq, H_B, DK_B, DV_B), F32),
                   jax.ShapeDtypeStruct((nseq * TS, D_A), F32)],
        scratch_shapes=[pltpu.VMEM((SB, D_BK), F32),
                        pltpu.VMEM((SB, D_X), F32),
                        pltpu.VMEM((D_BK, SBP), BF16),
                        pltpu.VMEM((D_BK, SBP), F32),
                        pltpu.VMEM((H_B, SBP, DV_B), F32),
                        pltpu.VMEM((H_B, SB, DV_B), F32),
                        pltpu.VMEM((H_B, SB, DV_B), F32),
                        pltpu.VMEM((SB, D_X), F32)],
        compiler_params=pltpu.CompilerParams(
            dimension_semantics=("arbitrary",), vmem_limit_bytes=VMEM_LIMIT),
        name="sample_mixer",
    )(proj, state, ck, cv, wa, abt, avw, bwa, bba, onw)


def kernel(x_prompt, x_sample, mem_prompt, state_gla, cache_mem_k, cache_mem_v, norm_w, w_in,
           a_vnorm_w, a_ws, a_bs, b_wa, b_ba, b_onorm_w, mem_norm_w, w_mem_kv, w_out, final_norm_w):
    batch, seq, _ = x_prompt.shape
    nseq, tdec, _ = x_sample.shape
    depth = w_in.shape[0]
    assert depth == 1 and tdec == TS and seq % PT == 0 and nseq % NS == 0

    w_in_bf, w_out_bf = _weight_prep(jnp.transpose(w_in[0]), w_out[0])
    last_cols = D_PROJ_PAD - (D_PROJ_PAD // IN_BN) * IN_BN
    bwa = jnp.concatenate([b_wa[0], jnp.zeros((LANES - GATE_RANK, D_BK), F32)], axis=0).astype(BF16)
    bba = b_ba[0].reshape(1, D_BK)
    avw = a_vnorm_w[0].reshape(1, D_A)
    onw_p = jnp.tile(b_onorm_w[0], H_B).reshape(1, D_B)
    onw_s = b_onorm_w[0].reshape(1, DV_B)
    wa_p = a_ws[0]
    abt_p = a_bs[0].T
    wa_s = jnp.tile(a_ws[0][:, :TS, :TS], (1, NS, NS))
    abt_s = jnp.tile(a_bs[0][:, :TS], (1, NS)).T

    xp = x_prompt.reshape(batch * seq, D_MODEL)
    xs = x_sample.reshape(nseq * TS, D_MODEL)
    mem = mem_prompt.reshape(batch * N_MEM, D_MODEL)

    proj_s = _norm_matmul(xs, norm_w[0], w_in_bf, D_PROJ, bm=nseq * TS, bn=IN_BN, transposed=False,
                          out_dtype=BF16, last_cols=last_cols, name="in_proj_s")
    br_s, st_s, cvs = _sample_mixer(
        proj_s, state_gla,
        cache_mem_k.reshape(nseq, N_MEM * H_X, HD_X), cache_mem_v.reshape(nseq, N_MEM * H_X, HD_X),
        wa_s, abt_s, avw, bwa, bba, onw_s)
    y_s = _out_proj(br_s, w_out_bf, xs, final_norm_w, bm=512, name="out_proj_s")

    memkv = _mem_kv(mem, mem_norm_w[0], w_mem_kv[0], batch=batch)
    y_p, st_p = _prompt_layer(xp, w_in_bf, w_out_bf, norm_w[0], final_norm_w, memkv, wa_p, abt_p,
                              avw, bwa, bba, onw_p, batch=batch, seq=seq)

    return (y_p.reshape(batch, seq, D_MODEL),
            y_s.reshape(nseq, TS, D_MODEL),
            memkv[0].reshape(1, batch, N_MEM, H_X, HD_X),
            memkv[1].reshape(1, batch, N_MEM, H_X, HD_X),
            st_p.reshape(1, batch, H_B, DK_B, DV_B),
            st_s,
            cvs.reshape(1, nseq, TS, D_A))
```

```python
import functools

import jax
import jax.numpy as jnp
from jax import lax
from jax.experimental import pallas as pl
from jax.experimental.pallas import tpu as pltpu

F32 = jnp.float32
BF16 = jnp.bfloat16

D_MODEL = 2048
D_A = 768
H_A = 4
HD_A = 192
CHUNK_A = 128
D_B = 768
H_B = 4
DV_B = 192
DK_B = 96
D_BK = 384
GATE_RANK = 16
GATE_TAU = 16.0
CHUNK_B = 64
D_X = 512
H_X = 4
HD_X = 128
N_MEM = 256
EPS = 1e-6

LANES = 128
MXU_COLS = 256
C_AU, C_AV, C_AG = 0, 768, 1536
C_BQ, C_BK, C_BV, C_BG = 2304, 2688, 3072, 3840
C_XQ, C_XG, C_BR = 4608, 5120, 5632
D_PROJ = 5760
D_PROJ_PAD = 5888
WIN_START = (0, 128, 384, 512)
WIN = 256
IN_BN = 768
OUT_BN = 512

VMEM_LIMIT = 60 * 1024 * 1024


def _dot(a, b):
    return jnp.dot(a, b, preferred_element_type=F32)


def _dot_nt(a, b):
    return lax.dot_general(a, b, (((1,), (1,)), ((), ())), preferred_element_type=F32)


def _dot_tn(a, b):
    return lax.dot_general(a, b, (((0,), (0,)), ((), ())), preferred_element_type=F32)


def _split(x):
    hi = x.astype(BF16)
    lo = (x - hi.astype(F32)).astype(BF16)
    return hi, lo


def _silu(x):
    return x / (1.0 + jnp.exp(-x))


def _log_sigmoid(x):
    return jnp.minimum(x, 0.0) - jnp.log1p(jnp.exp(-jnp.abs(x)))


def _head_id(idx, width, n):
    h = jnp.zeros_like(idx)
    for i in range(1, n):
        h = h + (idx >= i * width).astype(jnp.int32)
    return h


def _iota(shape, dim):
    return lax.broadcasted_iota(jnp.int32, shape, dim)


def _merge_windows(r):
    rows = r[0].shape[0]
    lo = _iota((rows, LANES), 1) < 64
    tiles = [r[0][:, :LANES], jnp.where(lo, r[0][:, LANES:], r[1][:, :LANES]), r[1][:, LANES:],
             r[2][:, :LANES], jnp.where(lo, r[2][:, LANES:], r[3][:, :LANES]), r[3][:, LANES:]]
    return jnp.concatenate(tiles, axis=1)


def _group_a(proj_ref, wa_ref, abt_ref, avw_ref, mask):
    u = proj_ref[:, C_AU:C_AU + D_A].astype(F32)
    v = proj_ref[:, C_AV:C_AV + D_A].astype(F32)
    g = proj_ref[:, C_AG:C_AG + D_A].astype(F32)
    rows = u.shape[0]
    ms = jnp.mean(v * v, axis=-1, keepdims=True)
    vn = v * lax.rsqrt(ms + EPS) * avw_ref[...]
    vb = vn.astype(BF16)
    r = []
    for gi in range(H_A):
        w = jnp.where(mask, wa_ref[gi], 0.0).astype(BF16)
        r.append(_dot(w, vb[:, WIN_START[gi]:WIN_START[gi] + WIN]))
    mixed = _merge_windows(r)
    col_g = _head_id(_iota((rows, D_A), 1), HD_A, H_A)
    bias = jnp.zeros((rows, D_A), F32)
    for gi in range(H_A):
        bias = jnp.where(col_g == gi, abt_ref[:, gi:gi + 1], bias)
    return u * (mixed + bias) * _silu(g), vn


def _gla_intra_windows(q_in, k_in, vb, mask):
    rows = q_in.shape[0]
    col_h = _head_id(_iota((rows, D_BK), 1), DK_B, H_B)
    kb = k_in.astype(BF16)
    r = []
    for h in range(H_B):
        qh = jnp.where(col_h == h, q_in, 0.0).astype(BF16)
        s = jnp.where(mask, _dot_nt(qh, kb), 0.0).astype(BF16)
        r.append(_dot(s, vb[:, WIN_START[h]:WIN_START[h] + WIN]))
    return r


def _norm_matmul_kernel(x_ref, nw_ref, w_ref, o_ref, h_ref, *, transposed, last_cols):
    j = pl.program_id(1)
    nj = pl.num_programs(1)

    @pl.when(j == 0)
    def _():
        x = x_ref[...]
        ms = jnp.mean(x * x, axis=-1, keepdims=True)
        h_ref[...] = (x * lax.rsqrt(ms + EPS) * nw_ref[...]).astype(BF16)

    mm = _dot_nt if transposed else _dot
    if last_cols is None:
        o_ref[...] = mm(h_ref[...], w_ref[...].astype(BF16)).astype(o_ref.dtype)
    else:
        @pl.when(j < nj - 1)
        def _():
            o_ref[...] = mm(h_ref[...], w_ref[...].astype(BF16)).astype(o_ref.dtype)

        @pl.when(j == nj - 1)
        def _():
            w = w_ref[:last_cols, :] if transposed else w_ref[:, :last_cols]
            o_ref[:, :last_cols] = mm(h_ref[...], w.astype(BF16)).astype(o_ref.dtype)


def _norm_matmul(x, nw, w, n_out, *, bm, bn, transposed, out_dtype, split_out=False,
                 last_cols=None, name):
    m, k = x.shape
    n = w.shape[0] if transposed else w.shape[1]
    nj = -(-n // bn)
    assert m % bm == 0
    if split_out:
        out_shape = jax.ShapeDtypeStruct((nj, m, bn), out_dtype)
        out_spec = pl.BlockSpec((None, bm, bn), lambda i, j: (j, i, 0))
    else:
        out_shape = jax.ShapeDtypeStruct((m, n_out), out_dtype)
        out_spec = pl.BlockSpec((bm, bn), lambda i, j: (i, j))
    if transposed:
        w_spec = pl.BlockSpec((bn, k), lambda i, j: (j, 0))
    else:
        w_spec = pl.BlockSpec((k, bn), lambda i, j: (0, j))
    return pl.pallas_call(
        functools.partial(_norm_matmul_kernel, transposed=transposed, last_cols=last_cols),
        grid=(m // bm, nj),
        in_specs=[pl.BlockSpec((bm, k), lambda i, j: (i, 0)),
                  pl.BlockSpec((1, k), lambda i, j: (0, 0)),
                  w_spec],
        out_specs=out_spec,
        out_shape=out_shape,
        scratch_shapes=[pltpu.VMEM((bm, k), BF16)],
        compiler_params=pltpu.CompilerParams(
            dimension_semantics=("arbitrary", "arbitrary"), vmem_limit_bytes=VMEM_LIMIT),
        name=name,
    )(x, nw.reshape(1, k), w)


def _mem_kv_kernel(x_ref, nw_ref, w_ref, o_ref, ok_ref, ov_ref):
    x = x_ref[...]
    ms = jnp.mean(x * x, axis=-1, keepdims=True)
    h = (x * lax.rsqrt(ms + EPS) * nw_ref[...]).astype(BF16)
    kv = _dot(h, w_ref[...].astype(BF16))
    o_ref[0] = kv[:, :D_X]
    o_ref[1] = kv[:, D_X:]
    for hd in range(H_X):
        ok_ref[pl.ds(hd, N_MEM, stride=H_X), :] = kv[:, hd * HD_X:(hd + 1) * HD_X]
        ov_ref[pl.ds(hd, N_MEM, stride=H_X), :] = kv[:, D_X + hd * HD_X:D_X + (hd + 1) * HD_X]


def _mem_kv(mem, nw, w, *, batch):
    return pl.pallas_call(
        _mem_kv_kernel,
        grid=(batch,),
        in_specs=[pl.BlockSpec((N_MEM, D_MODEL), lambda i: (i, 0)),
                  pl.BlockSpec((1, D_MODEL), lambda i: (0, 0)),
                  pl.BlockSpec((D_MODEL, 2 * D_X), lambda i: (0, 0))],
        out_specs=[pl.BlockSpec((2, N_MEM, D_X), lambda i: (0, i, 0)),
                   pl.BlockSpec((None, N_MEM * H_X, HD_X), lambda i: (i, 0, 0)),
                   pl.BlockSpec((None, N_MEM * H_X, HD_X), lambda i: (i, 0, 0))],
        out_shape=[jax.ShapeDtypeStruct((2, batch * N_MEM, D_X), F32),
                   jax.ShapeDtypeStruct((batch, N_MEM * H_X, HD_X), F32),
                   jax.ShapeDtypeStruct((batch, N_MEM * H_X, HD_X), F32)],
        compiler_params=pltpu.CompilerParams(
            dimension_semantics=("arbitrary",), vmem_limit_bytes=VMEM_LIMIT),
        name="mem_kv",
    )(mem, nw.reshape(1, D_MODEL), w)


PREP_BN = MXU_COLS
N_MAIN_BLOCKS = C_BG // PREP_BN


GATE_BLOCK = D_PROJ_PAD // PREP_BN - 1
N_IN_BLOCKS = GATE_BLOCK + 1
N_OUT_BLOCKS = D_MODEL // PREP_BN
PREP_BUFS = 4


def _weight_prep_kernel(wt_hbm, wo_hbm, oi_ref, oo_ref, buf, sem):
    r = pl.program_id(0)
    slot = lax.rem(r, PREP_BUFS)

    def in_copy(rr, sl):
        start = pl.multiple_of(jnp.where(rr < N_MAIN_BLOCKS, rr * PREP_BN, rr * PREP_BN + GATE_RANK), 8)
        return pltpu.make_async_copy(wt_hbm.at[pl.ds(start, PREP_BN), :], buf.at[sl], sem.at[sl])

    def gate_copy(sl):
        return pltpu.make_async_copy(wt_hbm.at[pl.ds(C_BG, GATE_RANK), :],
                                     buf.at[sl, pl.ds(0, GATE_RANK), :], sem.at[sl])

    def out_copy(rr, sl):
        start = pl.multiple_of((rr - N_IN_BLOCKS) * PREP_BN, PREP_BN)
        return pltpu.make_async_copy(wo_hbm.at[pl.ds(start, PREP_BN), :], buf.at[sl], sem.at[sl])

    def start_fetch(rr, sl):
        @pl.when(rr < GATE_BLOCK)
        def _():
            in_copy(rr, sl).start()

        @pl.when(rr == GATE_BLOCK)
        def _():
            gate_copy(sl).start()

        @pl.when(rr > GATE_BLOCK)
        def _():
            out_copy(rr, sl).start()

    @pl.when(r == 0)
    def _():
        for ahead in range(PREP_BUFS - 1):
            start_fetch(r + ahead, ahead)

    nxt = r + PREP_BUFS - 1

    @pl.when(nxt < pl.num_programs(0))
    def _():
        start_fetch(nxt, lax.rem(nxt, PREP_BUFS))

    @pl.when(r < GATE_BLOCK)
    def _():
        in_copy(r, slot).wait()
        oi_ref[...] = buf[slot].T.astype(BF16)

    @pl.when(r == GATE_BLOCK)
    def _():
        gate_copy(slot).wait()
        rows = jnp.concatenate([buf[slot, 0:GATE_RANK, :],
                                jnp.zeros((PREP_BN - GATE_RANK, D_MODEL), F32)], axis=0)
        oi_ref[...] = rows.T.astype(BF16)

    @pl.when(r > GATE_BLOCK)
    def _():
        out_copy(r, slot).wait()
        oo_ref[...] = buf[slot].astype(BF16)


def _weight_prep(w_t, w_out):
    k = w_t.shape[1]
    return pl.pallas_call(
        _weight_prep_kernel,
        grid=(N_IN_BLOCKS + N_OUT_BLOCKS,),
        in_specs=[pl.BlockSpec(memory_space=pl.ANY), pl.BlockSpec(memory_space=pl.ANY)],
        out_specs=[pl.BlockSpec((k, PREP_BN), lambda r: (0, jnp.minimum(r, GATE_BLOCK))),
                   pl.BlockSpec((PREP_BN, D_MODEL), lambda r: (jnp.maximum(r - N_IN_BLOCKS, 0), 0))],
        out_shape=[jax.ShapeDtypeStruct((k, D_PROJ_PAD), BF16),
                   jax.ShapeDtypeStruct((D_MODEL, D_MODEL), BF16)],
        scratch_shapes=[pltpu.VMEM((PREP_BUFS, PREP_BN, k), F32),
                        pltpu.SemaphoreType.DMA((PREP_BUFS,))],
        compiler_params=pltpu.CompilerParams(
            dimension_semantics=("arbitrary",), vmem_limit_bytes=VMEM_LIMIT),
        name="weight_prep",
    )(w_t, w_out)


def _out_proj_kernel(br_ref, w_ref, x_ref, fw_ref, y_ref):
    acc = _dot(br_ref[...], w_ref[...]) + x_ref[...]
    ms = jnp.mean(acc * acc, axis=-1, keepdims=True)
    y_ref[...] = acc * lax.rsqrt(ms + EPS) * fw_ref[...]


def _out_proj(br, w, x, fw, *, bm, name):
    m, k = br.shape
    n = w.shape[1]
    return pl.pallas_call(
        _out_proj_kernel,
        grid=(m // bm,),
        in_specs=[pl.BlockSpec((bm, k), lambda i: (i, 0)),
                  pl.BlockSpec((k, n), lambda i: (0, 0)),
                  pl.BlockSpec((bm, n), lambda i: (i, 0)),
                  pl.BlockSpec((1, n), lambda i: (0, 0))],
        out_specs=pl.BlockSpec((bm, n), lambda i: (i, 0)),
        out_shape=jax.ShapeDtypeStruct((m, n), F32),
        compiler_params=pltpu.CompilerParams(
            dimension_semantics=("arbitrary",), vmem_limit_bytes=VMEM_LIMIT),
        name=name,
    )(br, w, x, fw.reshape(1, n))


PT = 256
N_MIXER_STAGES = 2 + 4 * (PT // CHUNK_B) + 1 + 2 * H_X


def _group_a_chunks(proj_ref, wa_ref, abt_ref, avw_ref):
    u = proj_ref[:, C_AU:C_AU + D_A].astype(F32)
    v = proj_ref[:, C_AV:C_AV + D_A].astype(F32)
    g = proj_ref[:, C_AG:C_AG + D_A].astype(F32)
    n_chunks = u.shape[0] // CHUNK_A
    ms = jnp.mean(v * v, axis=-1, keepdims=True)
    vb = (v * lax.rsqrt(ms + EPS) * avw_ref[...]).astype(BF16)
    tril = _iota((CHUNK_A, CHUNK_A), 1) <= _iota((CHUNK_A, CHUNK_A), 0)
    r = []
    for gi in range(H_A):
        w = jnp.where(tril, wa_ref[gi], 0.0).astype(BF16)
        ws = slice(WIN_START[gi], WIN_START[gi] + WIN)
        rhs = jnp.concatenate([vb[c * CHUNK_A:(c + 1) * CHUNK_A, ws] for c in range(n_chunks)], axis=1)
        res = _dot(w, rhs)
        r.append(jnp.concatenate([res[:, c * WIN:(c + 1) * WIN] for c in range(n_chunks)], axis=0))
    mixed = _merge_windows(r)
    col_g = _head_id(_iota((CHUNK_A, D_A), 1), HD_A, H_A)
    bias = jnp.zeros((CHUNK_A, D_A), F32)
    for gi in range(H_A):
        bias = jnp.where(col_g == gi, abt_ref[:, gi:gi + 1], bias)
    bias = jnp.concatenate([bias] * n_chunks, axis=0)
    return u * (mixed + bias) * _silu(g)


def _prompt_block(proj_ref, out_ref, mk_ref, mv_ref, wa_ref, abt_ref, avw_ref, bwa_ref, bba_ref,
                  onw_ref, sbd_ref, o_ref):
    out_ref[:, 0:D_A] = _group_a_chunks(proj_ref, wa_ref, abt_ref, avw_ref).astype(BF16)
    yield

    q = proj_ref[:, C_BQ:C_BQ + D_BK].astype(F32)
    k = proj_ref[:, C_BK:C_BK + D_BK].astype(F32)
    vb = proj_ref[:, C_BV:C_BV + D_B]
    pre = _dot(proj_ref[:, C_BR:C_BR + LANES], bwa_ref[...]) + bba_ref[...]
    log_a = _log_sigmoid(pre) * (1.0 / GATE_TAU)
    yield
    r64 = _iota((CHUNK_B, CHUNK_B), 0)
    c64 = _iota((CHUNK_B, CHUNK_B), 1)
    tril = c64 <= r64
    tril_bf = jnp.where(tril, 1.0, 0.0).astype(BF16)
    row_pad = jnp.zeros((LANES - CHUNK_B, D_BK), F32)
    v_pad = jnp.zeros((LANES - CHUNK_B, WIN), BF16)
    win_col = _iota((CHUNK_B, WIN), 1)
    col_hq = _head_id(_iota((CHUNK_B, D_BK), 1), DK_B, H_B)
    tril_heads = (_iota((H_B * CHUNK_B, CHUNK_B), 1)
                  <= jnp.bitwise_and(_iota((H_B * CHUNK_B, CHUNK_B), 0), CHUNK_B - 1))
    for c in range(PT // CHUNK_B):
        sl = slice(c * CHUNK_B, (c + 1) * CHUNK_B)
        hi, lo = _split(log_a[sl])
        cum2 = _dot(tril_bf, jnp.concatenate([hi, lo], axis=1))
        yield
        cum = cum2[:, :D_BK] + cum2[:, D_BK:]
        tot_row = cum[CHUNK_B - 1:CHUNK_B, :]
        la_t = jnp.concatenate([log_a[sl], row_pad], axis=0).T
        q_in = q[sl] * (DK_B ** -0.5) * jnp.exp(cum)
        k_in = k[sl] * jnp.exp(-cum)
        k_out = k[sl] * jnp.exp(tot_row - cum)
        q_heads = jnp.concatenate([jnp.where(col_hq == h, q_in, 0.0) for h in range(H_B)], axis=0)
        sc = _dot_nt(q_heads.astype(BF16), k_in.astype(BF16))
        yield
        sc = jnp.where(tril_heads, sc, 0.0).astype(BF16)
        o_intra = _merge_windows(
            [_dot(sc[h * CHUNK_B:(h + 1) * CHUNK_B], vb[sl, WIN_START[h]:WIN_START[h] + WIN])
             for h in range(H_B)])
        s_old = sbd_ref[...]
        o_inter = _dot(q_in.astype(BF16), s_old.astype(BF16))
        yield
        kot = jnp.concatenate([k_out, row_pad], axis=0).T.astype(BF16)
        for h in range(H_B):
            rs = slice(h * DK_B, (h + 1) * DK_B)
            ws = slice(WIN_START[h], WIN_START[h] + WIN)
            dec = jnp.exp(jnp.sum(la_t[rs], axis=1, keepdims=True))
            lo_col = h * DV_B - WIN_START[h]
            in_head = (win_col >= lo_col) & (win_col < lo_col + DV_B)
            v_h = jnp.where(in_head, vb[sl, ws], jnp.zeros((), BF16))
            kv = _dot(kot[rs], jnp.concatenate([v_h, v_pad], axis=0))
            sbd_ref[rs, ws] = s_old[rs, ws] * dec + kv
        o_ref[sl, :] = o_intra + o_inter
        yield
    o = o_ref[...]
    o2 = o * o
    t = [o2[:, j * LANES:(j + 1) * LANES] for j in range(D_B // LANES)]
    lo = _iota((PT, LANES), 1) < 64
    t1a = jnp.where(lo, t[1], 0.0)
    t4a = jnp.where(lo, t[4], 0.0)
    sums = (t[0] + t1a, (t[1] - t1a) + t[2], t[3] + t4a, (t[4] - t4a) + t[5])
    inv = [lax.rsqrt(jnp.sum(x, axis=-1, keepdims=True) * (1.0 / DV_B) + EPS) for x in sums]
    inv = jnp.concatenate(
        [jnp.broadcast_to(inv[0], (PT, LANES)), jnp.where(lo, inv[0], inv[1]),
         jnp.broadcast_to(inv[1], (PT, LANES)), jnp.broadcast_to(inv[2], (PT, LANES)),
         jnp.where(lo, inv[2], inv[3]), jnp.broadcast_to(inv[3], (PT, LANES))], axis=1)
    o_n = o * inv * onw_ref[...]
    bg = proj_ref[:, C_BG:C_BG + D_B].astype(F32)
    out_ref[:, D_A:D_A + D_B] = (o_n * _silu(bg)).astype(BF16)
    yield

    for h in range(H_X):
        hs = slice(h * HD_X, (h + 1) * HD_X)
        qh = proj_ref[:, C_XQ + h * HD_X:C_XQ + (h + 1) * HD_X]
        s = _dot_nt(qh, mk_ref[:, hs].astype(BF16)) * (HD_X ** -0.5)
        yield
        e = jnp.exp(s - jnp.max(s, axis=-1, keepdims=True))
        den = jnp.sum(e, axis=-1, keepdims=True)
        ox = _dot(e.astype(BF16), mv_ref[:, hs].astype(BF16)) / den
        xg = proj_ref[:, C_XG + h * HD_X:C_XG + (h + 1) * HD_X].astype(F32)
        out_ref[:, D_A + D_B + h * HD_X:D_A + D_B + (h + 1) * HD_X] = (ox * _silu(xg)).astype(BF16)
        yield


def _prompt_layer_kernel(xn_ref, wi_ref, wo_ref, nw_ref, fw_ref, mk_ref, mv_ref, wa_ref, abt_ref,
                         avw_ref, bwa_ref, bba_ref, onw_ref, y_ref, st_ref,
                         pa_ref, pb_ref, xk_ref, br_ref, h_ref, sbd_ref, o_ref, *, nt):
    s = pl.program_id(0)
    cur = jnp.maximum(s - 1, 0)
    t = lax.rem(cur, nt)

    @pl.when(t == 0)
    def _():
        sbd_ref[...] = jnp.zeros_like(sbd_ref)

    def in_proj_stages(pn_ref):
        x = xn_ref[...]
        ms = jnp.mean(x * x, axis=-1, keepdims=True)
        h_ref[...] = (x * lax.rsqrt(ms + EPS) * nw_ref[...]).astype(BF16)
        yield
        for c0 in range(0, D_PROJ_PAD, MXU_COLS):
            cols = pl.ds(c0, MXU_COLS)
            pn_ref[:, cols] = _dot(h_ref[...], wi_ref[:, cols]).astype(BF16)
            yield
        xk_ref[...] = xn_ref[...]

    def out_proj_stages():
        ssq = jnp.zeros((PT, 1), F32)
        for c0 in range(0, D_MODEL, OUT_BN):
            cols = pl.ds(c0, OUT_BN)
            acc = _dot(br_ref[...], wo_ref[:, cols]) + xk_ref[:, cols]
            y_ref[:, cols] = acc
            ssq = ssq + jnp.sum(acc * acc, axis=-1, keepdims=True)
            yield
        y_ref[...] = y_ref[...] * lax.rsqrt(ssq * (1.0 / D_MODEL) + EPS) * fw_ref[...]
        yield

    def run(order, streams):
        for name in order:
            next(streams[name])
        for name, gen in streams.items():
            assert next(gen, "done") == "done", name

    def mixer_stages(pc_ref):
        return _prompt_block(pc_ref, br_ref, mk_ref, mv_ref, wa_ref, abt_ref, avw_ref, bwa_ref,
                             bba_ref, onw_ref, sbd_ref, o_ref)

    n_p = 1 + D_PROJ_PAD // MXU_COLS
    n_m = N_MIXER_STAGES
    n_o = 1 + D_MODEL // OUT_BN

    def body(pn_ref, pc_ref):
        order = ["P"]
        for i in range(max(n_p - 1, n_m)):
            order += ["M"] * (i < n_m) + ["P"] * (i < n_p - 1)
        order += ["O"] * n_o
        run(order, {"P": in_proj_stages(pn_ref), "M": mixer_stages(pc_ref), "O": out_proj_stages()})

    last = pl.num_programs(0) - 1

    @pl.when(s == 0)
    def _():
        run(["P"] * n_p, {"P": in_proj_stages(pa_ref)})

    @pl.when((lax.rem(s, 2) == 0) & (s > 0) & (s < last))
    def _():
        body(pa_ref, pb_ref)

    @pl.when(lax.rem(s, 2) == 1)
    def _():
        body(pb_ref, pa_ref)

    @pl.when(s == last)
    def _():
        run(["M"] * n_m + ["O"] * n_o, {"M": mixer_stages(pb_ref), "O": out_proj_stages()})

    @pl.when((t == nt - 1) & (s > 0))
    def _():
        for h in range(H_B):
            off = h * DV_B - WIN_START[h]
            blk = sbd_ref[h * DK_B:(h + 1) * DK_B, WIN_START[h]:WIN_START[h] + WIN]
            if off:
                blk = pltpu.roll(blk, WIN - off, 1)
            st_ref[0, h] = blk[:, :DV_B]


def _prompt_layer(xp, w_in_bf, w_out_bf, nw, fw, memkv, wa, abt, avw, bwa, bba, onw, *, batch, seq):
    nt = seq // PT
    nblk = batch * nt
    assert nblk % 2 == 0
    cur = lambda s: jnp.maximum(s - 1, 0)
    const = lambda *shape: pl.BlockSpec(shape, lambda s: (0,) * len(shape))
    resident = lambda *shape: pl.BlockSpec(shape, lambda s: (0,) * len(shape),
                                           pipeline_mode=pl.Buffered(1))
    return pl.pallas_call(
        functools.partial(_prompt_layer_kernel, nt=nt),
        grid=(nblk + 1,),
        in_specs=[pl.BlockSpec((PT, D_MODEL), lambda s: (jnp.minimum(s, nblk - 1), 0)),
                  resident(D_MODEL, D_PROJ_PAD), resident(D_MODEL, D_MODEL),
                  const(1, D_MODEL), const(1, D_MODEL),
                  pl.BlockSpec((None, N_MEM, D_X), lambda s: (0, cur(s) // nt, 0)),
                  pl.BlockSpec((None, N_MEM, D_X), lambda s: (1, cur(s) // nt, 0)),
                  const(H_A, CHUNK_A, CHUNK_A), const(CHUNK_A, H_A), const(1, D_A),
                  const(LANES, D_BK), const(1, D_BK), const(1, D_B)],
        out_specs=[pl.BlockSpec((PT, D_MODEL), lambda s: (cur(s), 0)),
                   pl.BlockSpec((1, H_B, DK_B, DV_B), lambda s: (cur(s) // nt, 0, 0, 0))],
        out_shape=[jax.ShapeDtypeStruct((batch * seq, D_MODEL), F32),
                   jax.ShapeDtypeStruct((batch, H_B, DK_B, DV_B), F32)],
        scratch_shapes=[pltpu.VMEM((PT, D_PROJ_PAD), BF16),
                        pltpu.VMEM((PT, D_PROJ_PAD), BF16),
                        pltpu.VMEM((PT, D_MODEL), F32),
                        pltpu.VMEM((PT, D_MODEL), BF16),
                        pltpu.VMEM((PT, D_MODEL), BF16),
                        pltpu.VMEM((D_BK, D_B), F32),
                        pltpu.VMEM((PT, D_B), F32)],
        compiler_params=pltpu.CompilerParams(
            dimension_semantics=("arbitrary",), vmem_limit_bytes=VMEM_LIMIT),
        name="prompt_layer",
    )(xp, w_in_bf, w_out_bf, nw.reshape(1, D_MODEL), fw.reshape(1, D_MODEL), memkv, memkv,
      wa, abt, avw, bwa, bba, onw)


NS = 8
TS = 8
SB = NS * TS
SBP = 128


def _sample_mixer_kernel(proj_ref, st_ref, ck_ref, cv_ref, wa_ref, abt_ref, avw_ref, bwa_ref,
                         bba_ref, onw_ref, out_ref, stn_ref, cvs_ref,
                         qin_ref, xq_ref, kot_ref, lat_ref, vhm_ref, ghm_ref, ohm_ref, ox_ref):
    ri = _iota((SB, SB), 0)
    ci = _iota((SB, SB), 1)
    same_seq = jnp.right_shift(ri, 3) == jnp.right_shift(ci, 3)
    causal = same_seq & (ci <= ri)

    a_br, vn = _group_a(proj_ref, wa_ref, abt_ref, avw_ref, causal)
    out_ref[:, 0:D_A] = a_br.astype(BF16)
    cvs_ref[...] = vn

    q = proj_ref[:, C_BQ:C_BQ + D_BK].astype(F32)
    k = proj_ref[:, C_BK:C_BK + D_BK].astype(F32)
    vb = proj_ref[:, C_BV:C_BV + D_B]
    bgb = proj_ref[:, C_BG:C_BG + D_B]
    pre = _dot(proj_ref[:, C_BR:C_BR + LANES], bwa_ref[...]) + bba_ref[...]
    log_a = _log_sigmoid(pre) * (1.0 / GATE_TAU)
    causal_bf = jnp.where(causal, 1.0, 0.0).astype(BF16)
    seq_bf = jnp.where(same_seq, 1.0, 0.0).astype(BF16)
    hi, lo = _split(log_a)
    cum = _dot(causal_bf, hi) + _dot(causal_bf, lo)
    tot = _dot(seq_bf, hi) + _dot(seq_bf, lo)
    q_in = q * (DK_B ** -0.5) * jnp.exp(cum)
    k_in = k * jnp.exp(-cum)
    k_out = k * jnp.exp(tot - cum)
    qin_ref[...] = q_in
    zpad = jnp.zeros((SBP - SB, D_BK), F32)
    kot_ref[...] = jnp.concatenate([k_out, zpad], axis=0).T.astype(BF16)
    lat_ref[...] = jnp.concatenate([log_a, zpad], axis=0).T
    xq_ref[...] = proj_ref[:, C_XQ:C_XQ + D_X].astype(F32)

    wins = _gla_intra_windows(q_in, k_in, vb, causal)
    sel_r = _iota((D_B, DV_B), 0)
    sel_c = _iota((D_B, DV_B), 1)
    vhm_ref[...] = jnp.zeros_like(vhm_ref)
    for h in range(H_B):
        off = h * DV_B - WIN_START[h]
        w = wins[h]
        if off:
            w = pltpu.roll(w, WIN - off, 1)
        ohm_ref[h] = w[:, :DV_B]
        sel = jnp.where(sel_r == sel_c + h * DV_B, 1.0, 0.0).astype(BF16)
        vhm_ref[h, 0:SB, :] = _dot(vb, sel).astype(BF16)
        ghm_ref[h] = _dot(bgb, sel)

    mask_x = (jnp.right_shift(_iota((H_X * TS, N_MEM * H_X), 0), 3)
              == jnp.bitwise_and(_iota((H_X * TS, N_MEM * H_X), 1), H_X - 1))
    mask_b = jnp.right_shift(_iota((H_B * TS, D_BK), 0), 3) == _head_id(_iota((H_B * TS, D_BK), 1), DK_B, H_B)
    lane_seq = jnp.right_shift(_iota((D_BK, SBP), 1), 3)

    def per_seq(s, carry):
        r0 = pl.multiple_of(s * TS, TS)
        q8 = xq_ref[pl.ds(r0, TS), :]
        q32 = jnp.concatenate([q8[:, h * HD_X:(h + 1) * HD_X] for h in range(H_X)], axis=0)
        sc = _dot_nt(q32.astype(BF16), ck_ref[s].astype(BF16)) * (HD_X ** -0.5)
        sc = jnp.where(mask_x, sc, -1e30)
        e = jnp.exp(sc - jnp.max(sc, axis=-1, keepdims=True))
        den = jnp.sum(e, axis=-1, keepdims=True)
        o = _dot(e.astype(BF16), cv_ref[s].astype(BF16)) / den
        ox_ref[pl.ds(r0, TS), :] = jnp.concatenate(
            [o[h * TS:(h + 1) * TS] for h in range(H_X)], axis=1)
        qi8 = qin_ref[pl.ds(r0, TS), :]
        qbd2 = jnp.where(mask_b, jnp.concatenate([qi8] * H_B, axis=0), 0.0).astype(BF16)
        s0 = jnp.concatenate([st_ref[s, h] for h in range(H_B)], axis=0)
        o_inter = _dot(qbd2, s0.astype(BF16))
        for h in range(H_B):
            ohm_ref[h, pl.ds(r0, TS), :] += o_inter[h * TS:(h + 1) * TS]
        dec = jnp.exp(jnp.sum(jnp.where(lane_seq == s, lat_ref[...], 0.0), axis=1, keepdims=True))
        kot = jnp.where(lane_seq == s, kot_ref[...], jnp.zeros((), BF16))
        kv = jnp.concatenate(
            [_dot(kot[h * DK_B:(h + 1) * DK_B], vhm_ref[h]) for h in range(H_B)], axis=0)
        s_new = s0 * dec + kv
        for h in range(H_B):
            stn_ref[s, h] = s_new[h * DK_B:(h + 1) * DK_B]
        return carry

    lax.fori_loop(0, NS, per_seq, 0, unroll=2)

    selt_r = _iota((DV_B, D_B), 0)
    selt_c = _iota((DV_B, D_B), 1)
    b_br = jnp.zeros((SB, D_B), F32)
    for h in range(H_B):
        o_h = ohm_ref[h]
        ms = jnp.mean(o_h * o_h, axis=-1, keepdims=True)
        ob = (o_h * lax.rsqrt(ms + EPS) * onw_ref[...] * _silu(ghm_ref[h])).astype(BF16)
        selt = jnp.where(selt_c == selt_r + h * DV_B, 1.0, 0.0).astype(BF16)
        b_br = b_br + _dot(ob, selt)
    out_ref[:, D_A:D_A + D_B] = b_br.astype(BF16)

    xg = proj_ref[:, C_XG:C_XG + D_X].astype(F32)
    out_ref[:, D_A + D_B:D_MODEL] = (ox_ref[...] * _silu(xg)).astype(BF16)


def _sample_mixer(proj, state, ck, cv, wa, abt, avw, bwa, bba, onw):
    nseq = state.shape[1]
    const = lambda *shape: pl.BlockSpec(shape, lambda i: (0,) * len(shape))
    return pl.pallas_call(
        _sample_mixer_kernel,
        grid=(nseq // NS,),
        in_specs=[pl.BlockSpec((SB, D_PROJ), lambda i: (i, 0)),
                  pl.BlockSpec((None, NS, H_B, DK_B, DV_B), lambda i: (0, i, 0, 0, 0)),
                  pl.BlockSpec((NS, N_MEM * H_X, HD_X), lambda i: (i, 0, 0)),
                  pl.BlockSpec((NS, N_MEM * H_X, HD_X), lambda i: (i, 0, 0)),
                  const(H_A, SB, SB), const(SB, H_A), const(1, D_A),
                  const(LANES, D_BK), const(1, D_BK), const(1, DV_B)],
        out_specs=[pl.BlockSpec((SB, D_MODEL), lambda i: (i, 0)),
                   pl.BlockSpec((None, NS, H_B, DK_B, DV_B), lambda i: (0, i, 0, 0, 0)),
                   pl.BlockSpec((SB, D_A), lambda i: (i, 0))],
        out_shape=[jax.ShapeDtypeStruct((nseq * TS, D_MODEL), BF16),
                   jax.ShapeDtypeStruct((1, nseq, H_B, DK_B, DV_B), F32),
                   jax.ShapeDtypeStruct((nseq * TS, D_A), F32)],
        scratch_shapes=[pltpu.VMEM((SB, D_BK), F32),
                        pltpu.VMEM((SB, D_X), F32),
                        pltpu.VMEM((D_BK, SBP), BF16),
                        pltpu.VMEM((D_BK, SBP), F32),
                        pltpu.VMEM((H_B, SBP, DV_B), BF16),
                        pltpu.VMEM((H_B, SB, DV_B), F32),
                        pltpu.VMEM((H_B, SB, DV_B), F32),
                        pltpu.VMEM((SB, D_X), F32)],
        compiler_params=pltpu.CompilerParams(
            dimension_semantics=("arbitrary",), vmem_limit_bytes=VMEM_LIMIT),
        name="sample_mixer",
    )(proj, state, ck, cv, wa, abt, avw, bwa, bba, onw)


def kernel(x_prompt, x_sample, mem_prompt, state_gla, cache_mem_k, cache_mem_v, norm_w, w_in,
           a_vnorm_w, a_ws, a_bs, b_wa, b_ba, b_onorm_w, mem_norm_w, w_mem_kv, w_out, final_norm_w):
    batch, seq, _ = x_prompt.shape
    nseq, tdec, _ = x_sample.shape
    depth = w_in.shape[0]
    assert depth == 1 and tdec == TS and seq % PT == 0 and nseq % NS == 0

    w_in_bf, w_out_bf = _weight_prep(jnp.transpose(w_in[0]), w_out[0])
    last_cols = D_PROJ_PAD - (D_PROJ_PAD // IN_BN) * IN_BN
    bwa = jnp.concatenate([b_wa[0], jnp.zeros((LANES - GATE_RANK, D_BK), F32)], axis=0).astype(BF16)
    bba = b_ba[0].reshape(1, D_BK)
    avw = a_vnorm_w[0].reshape(1, D_A)
    onw_p = jnp.tile(b_onorm_w[0], H_B).reshape(1, D_B)
    onw_s = b_onorm_w[0].reshape(1, DV_B)
    wa_p = a_ws[0]
    abt_p = a_bs[0].T
    wa_s = jnp.tile(a_ws[0][:, :TS, :TS], (1, NS, NS))
    abt_s = jnp.tile(a_bs[0][:, :TS], (1, NS)).T

    xp = x_prompt.reshape(batch * seq, D_MODEL)
    xs = x_sample.reshape(nseq * TS, D_MODEL)
    mem = mem_prompt.reshape(batch * N_MEM, D_MODEL)

    proj_s = _norm_matmul(xs, norm_w[0], w_in_bf, D_PROJ, bm=nseq * TS, bn=IN_BN, transposed=False,
                          out_dtype=BF16, last_cols=last_cols, name="in_proj_s")
    br_s, st_s, cvs = _sample_mixer(
        proj_s, state_gla,
        cache_mem_k.reshape(nseq, N_MEM * H_X, HD_X), cache_mem_v.reshape(nseq, N_MEM * H_X, HD_X),
        wa_s, abt_s, avw, bwa, bba, onw_s)
    y_s = _out_proj(br_s, w_out_bf, xs, final_norm_w, bm=512, name="out_proj_s")

    memkv, mem_k, mem_v = _mem_kv(mem, mem_norm_w[0], w_mem_kv[0], batch=batch)
    y_p, st_p = _prompt_layer(xp, w_in_bf, w_out_bf, norm_w[0], final_norm_w, memkv, wa_p, abt_p,
                              avw, bwa, bba, onw_p, batch=batch, seq=seq)

    return (y_p.reshape(batch, seq, D_MODEL),
            y_s.reshape(nseq, TS, D_MODEL),
            mem_k.reshape(1, batch, N_MEM, H_X, HD_X),
            mem_v.reshape(1, batch, N_MEM, H_X, HD_X),
            st_p.reshape(1, batch, H_B, DK_B, DV_B),
            st_s,
            cvs.reshape(1, nseq, TS, D_A))
```

```python
import functools

import jax
import jax.numpy as jnp
from jax import lax
from jax.experimental import pallas as pl
from jax.experimental.pallas import tpu as pltpu

F32 = jnp.float32
BF16 = jnp.bfloat16

D_MODEL = 2048
D_A = 768
H_A = 4
HD_A = 192
CHUNK_A = 128
D_B = 768
H_B = 4
DV_B = 192
DK_B = 96
D_BK = 384
GATE_RANK = 16
GATE_TAU = 16.0
CHUNK_B = 64
D_X = 512
H_X = 4
HD_X = 128
N_MEM = 256
EPS = 1e-6

LANES = 128
MXU_COLS = 256
C_AU, C_AV, C_AG = 0, 768, 1536
C_BQ, C_BK, C_BV, C_BG = 2304, 2688, 3072, 3840
C_XQ, C_XG, C_BR = 4608, 5120, 5632
D_PROJ = 5760
D_PROJ_PAD = 5888
WIN_START = (0, 128, 384, 512)
WIN = 256
IN_BN = 768
OUT_BN = 512

VMEM_LIMIT = 60 * 1024 * 1024


def _dot(a, b):
    return jnp.dot(a, b, preferred_element_type=F32)


def _dot_nt(a, b):
    return lax.dot_general(a, b, (((1,), (1,)), ((), ())), preferred_element_type=F32)


def _dot_tn(a, b):
    return lax.dot_general(a, b, (((0,), (0,)), ((), ())), preferred_element_type=F32)


def _split(x):
    hi = x.astype(BF16)
    lo = (x - hi.astype(F32)).astype(BF16)
    return hi, lo


def _silu(x):
    return x / (1.0 + jnp.exp(-x))


def _log_sigmoid(x):
    return jnp.minimum(x, 0.0) - jnp.log1p(jnp.exp(-jnp.abs(x)))


def _head_id(idx, width, n):
    h = jnp.zeros_like(idx)
    for i in range(1, n):
        h = h + (idx >= i * width).astype(jnp.int32)
    return h


def _iota(shape, dim):
    return lax.broadcasted_iota(jnp.int32, shape, dim)


def _merge_windows(r):
    rows = r[0].shape[0]
    lo = _iota((rows, LANES), 1) < 64
    tiles = [r[0][:, :LANES], jnp.where(lo, r[0][:, LANES:], r[1][:, :LANES]), r[1][:, LANES:],
             r[2][:, :LANES], jnp.where(lo, r[2][:, LANES:], r[3][:, :LANES]), r[3][:, LANES:]]
    return jnp.concatenate(tiles, axis=1)


def _group_a(proj_ref, wa_ref, abt_ref, avw_ref, mask):
    u = proj_ref[:, C_AU:C_AU + D_A].astype(F32)
    v = proj_ref[:, C_AV:C_AV + D_A].astype(F32)
    g = proj_ref[:, C_AG:C_AG + D_A].astype(F32)
    rows = u.shape[0]
    ms = jnp.mean(v * v, axis=-1, keepdims=True)
    vn = v * lax.rsqrt(ms + EPS) * avw_ref[...]
    vb = vn.astype(BF16)
    r = []
    for gi in range(H_A):
        w = jnp.where(mask, wa_ref[gi], 0.0).astype(BF16)
        r.append(_dot(w, vb[:, WIN_START[gi]:WIN_START[gi] + WIN]))
    mixed = _merge_windows(r)
    col_g = _head_id(_iota((rows, D_A), 1), HD_A, H_A)
    bias = jnp.zeros((rows, D_A), F32)
    for gi in range(H_A):
        bias = jnp.where(col_g == gi, abt_ref[:, gi:gi + 1], bias)
    return u * (mixed + bias) * _silu(g), vn


def _gla_intra_windows(q_in, k_in, vb, mask):
    rows = q_in.shape[0]
    col_h = _head_id(_iota((rows, D_BK), 1), DK_B, H_B)
    kb = k_in.astype(BF16)
    r = []
    for h in range(H_B):
        qh = jnp.where(col_h == h, q_in, 0.0).astype(BF16)
        s = jnp.where(mask, _dot_nt(qh, kb), 0.0).astype(BF16)
        r.append(_dot(s, vb[:, WIN_START[h]:WIN_START[h] + WIN]))
    return r


def _norm_matmul_kernel(x_ref, nw_ref, w_ref, o_ref, h_ref, *, transposed, last_cols):
    j = pl.program_id(1)
    nj = pl.num_programs(1)

    @pl.when(j == 0)
    def _():
        x = x_ref[...]
        ms = jnp.mean(x * x, axis=-1, keepdims=True)
        h_ref[...] = (x * lax.rsqrt(ms + EPS) * nw_ref[...]).astype(BF16)

    mm = _dot_nt if transposed else _dot
    if last_cols is None:
        o_ref[...] = mm(h_ref[...], w_ref[...].astype(BF16)).astype(o_ref.dtype)
    else:
        @pl.when(j < nj - 1)
        def _():
            o_ref[...] = mm(h_ref[...], w_ref[...].astype(BF16)).astype(o_ref.dtype)

        @pl.when(j == nj - 1)
        def _():
            w = w_ref[:last_cols, :] if transposed else w_ref[:, :last_cols]
            o_ref[:, :last_cols] = mm(h_ref[...], w.astype(BF16)).astype(o_ref.dtype)


def _norm_matmul(x, nw, w, n_out, *, bm, bn, transposed, out_dtype, split_out=False,
                 last_cols=None, name):
    m, k = x.shape
    n = w.shape[0] if transposed else w.shape[1]
    nj = -(-n // bn)
    assert m % bm == 0
    if split_out:
        out_shape = jax.ShapeDtypeStruct((nj, m, bn), out_dtype)
        out_spec = pl.BlockSpec((None, bm, bn), lambda i, j: (j, i, 0))
    else:
        out_shape = jax.ShapeDtypeStruct((m, n_out), out_dtype)
        out_spec = pl.BlockSpec((bm, bn), lambda i, j: (i, j))
    if transposed:
        w_spec = pl.BlockSpec((bn, k), lambda i, j: (j, 0))
    else:
        w_spec = pl.BlockSpec((k, bn), lambda i, j: (0, j))
    return pl.pallas_call(
        functools.partial(_norm_matmul_kernel, transposed=transposed, last_cols=last_cols),
        grid=(m // bm, nj),
        in_specs=[pl.BlockSpec((bm, k), lambda i, j: (i, 0)),
                  pl.BlockSpec((1, k), lambda i, j: (0, 0)),
                  w_spec],
        out_specs=out_spec,
        out_shape=out_shape,
        scratch_shapes=[pltpu.VMEM((bm, k), BF16)],
        compiler_params=pltpu.CompilerParams(
            dimension_semantics=("arbitrary", "arbitrary"), vmem_limit_bytes=VMEM_LIMIT),
        name=name,
    )(x, nw.reshape(1, k), w)


def _mem_kv_kernel(x_ref, nw_ref, w_ref, o_ref, ok_ref, ov_ref):
    x = x_ref[...]
    ms = jnp.mean(x * x, axis=-1, keepdims=True)
    h = (x * lax.rsqrt(ms + EPS) * nw_ref[...]).astype(BF16)
    kv = _dot(h, w_ref[...].astype(BF16))
    o_ref[0] = kv[:, :D_X]
    o_ref[1] = kv[:, D_X:]
    for hd in range(H_X):
        ok_ref[pl.ds(hd, N_MEM, stride=H_X), :] = kv[:, hd * HD_X:(hd + 1) * HD_X]
        ov_ref[pl.ds(hd, N_MEM, stride=H_X), :] = kv[:, D_X + hd * HD_X:D_X + (hd + 1) * HD_X]


def _mem_kv(mem, nw, w, *, batch):
    return pl.pallas_call(
        _mem_kv_kernel,
        grid=(batch,),
        in_specs=[pl.BlockSpec((N_MEM, D_MODEL), lambda i: (i, 0)),
                  pl.BlockSpec((1, D_MODEL), lambda i: (0, 0)),
                  pl.BlockSpec((D_MODEL, 2 * D_X), lambda i: (0, 0))],
        out_specs=[pl.BlockSpec((2, N_MEM, D_X), lambda i: (0, i, 0)),
                   pl.BlockSpec((None, N_MEM * H_X, HD_X), lambda i: (i, 0, 0)),
                   pl.BlockSpec((None, N_MEM * H_X, HD_X), lambda i: (i, 0, 0))],
        out_shape=[jax.ShapeDtypeStruct((2, batch * N_MEM, D_X), F32),
                   jax.ShapeDtypeStruct((batch, N_MEM * H_X, HD_X), F32),
                   jax.ShapeDtypeStruct((batch, N_MEM * H_X, HD_X), F32)],
        compiler_params=pltpu.CompilerParams(
            dimension_semantics=("arbitrary",), vmem_limit_bytes=VMEM_LIMIT),
        name="mem_kv",
    )(mem, nw.reshape(1, D_MODEL), w)


PREP_BN = MXU_COLS
N_MAIN_BLOCKS = C_BG // PREP_BN


GATE_BLOCK = D_PROJ_PAD // PREP_BN - 1
N_IN_BLOCKS = GATE_BLOCK + 1
N_OUT_BLOCKS = D_MODEL // PREP_BN
PREP_BUFS = 4


def _weight_prep_kernel(wt_hbm, wo_hbm, oi_ref, oo_ref, buf, sem):
    r = pl.program_id(0)
    slot = lax.rem(r, PREP_BUFS)

    def in_copy(rr, sl):
        start = pl.multiple_of(jnp.where(rr < N_MAIN_BLOCKS, rr * PREP_BN, rr * PREP_BN + GATE_RANK), 8)
        return pltpu.make_async_copy(wt_hbm.at[pl.ds(start, PREP_BN), :], buf.at[sl], sem.at[sl])

    def gate_copy(sl):
        return pltpu.make_async_copy(wt_hbm.at[pl.ds(C_BG, GATE_RANK), :],
                                     buf.at[sl, pl.ds(0, GATE_RANK), :], sem.at[sl])

    def out_copy(rr, sl):
        start = pl.multiple_of((rr - N_IN_BLOCKS) * PREP_BN, PREP_BN)
        return pltpu.make_async_copy(wo_hbm.at[pl.ds(start, PREP_BN), :], buf.at[sl], sem.at[sl])

    def start_fetch(rr, sl):
        @pl.when(rr < GATE_BLOCK)
        def _():
            in_copy(rr, sl).start()

        @pl.when(rr == GATE_BLOCK)
        def _():
            gate_copy(sl).start()

        @pl.when(rr > GATE_BLOCK)
        def _():
            out_copy(rr, sl).start()

    @pl.when(r == 0)
    def _():
        for ahead in range(PREP_BUFS - 1):
            start_fetch(r + ahead, ahead)

    nxt = r + PREP_BUFS - 1

    @pl.when(nxt < pl.num_programs(0))
    def _():
        start_fetch(nxt, lax.rem(nxt, PREP_BUFS))

    @pl.when(r < GATE_BLOCK)
    def _():
        in_copy(r, slot).wait()
        oi_ref[...] = buf[slot].T.astype(BF16)

    @pl.when(r == GATE_BLOCK)
    def _():
        gate_copy(slot).wait()
        rows = jnp.concatenate([buf[slot, 0:GATE_RANK, :],
                                jnp.zeros((PREP_BN - GATE_RANK, D_MODEL), F32)], axis=0)
        oi_ref[...] = rows.T.astype(BF16)

    @pl.when(r > GATE_BLOCK)
    def _():
        out_copy(r, slot).wait()
        oo_ref[...] = buf[slot].astype(BF16)


def _weight_prep(w_t, w_out):
    k = w_t.shape[1]
    return pl.pallas_call(
        _weight_prep_kernel,
        grid=(N_IN_BLOCKS + N_OUT_BLOCKS,),
        in_specs=[pl.BlockSpec(memory_space=pl.ANY), pl.BlockSpec(memory_space=pl.ANY)],
        out_specs=[pl.BlockSpec((k, PREP_BN), lambda r: (0, jnp.minimum(r, GATE_BLOCK))),
                   pl.BlockSpec((PREP_BN, D_MODEL), lambda r: (jnp.maximum(r - N_IN_BLOCKS, 0), 0))],
        out_shape=[jax.ShapeDtypeStruct((k, D_PROJ_PAD), BF16),
                   jax.ShapeDtypeStruct((D_MODEL, D_MODEL), BF16)],
        scratch_shapes=[pltpu.VMEM((PREP_BUFS, PREP_BN, k), F32),
                        pltpu.SemaphoreType.DMA((PREP_BUFS,))],
        compiler_params=pltpu.CompilerParams(
            dimension_semantics=("arbitrary",), vmem_limit_bytes=VMEM_LIMIT),
        name="weight_prep",
    )(w_t, w_out)


def _out_proj_kernel(br_ref, w_ref, x_ref, fw_ref, y_ref):
    acc = _dot(br_ref[...], w_ref[...]) + x_ref[...]
    ms = jnp.mean(acc * acc, axis=-1, keepdims=True)
    y_ref[...] = acc * lax.rsqrt(ms + EPS) * fw_ref[...]


def _out_proj(br, w, x, fw, *, bm, name):
    m, k = br.shape
    n = w.shape[1]
    return pl.pallas_call(
        _out_proj_kernel,
        grid=(m // bm,),
        in_specs=[pl.BlockSpec((bm, k), lambda i: (i, 0)),
                  pl.BlockSpec((k, n), lambda i: (0, 0)),
                  pl.BlockSpec((bm, n), lambda i: (i, 0)),
                  pl.BlockSpec((1, n), lambda i: (0, 0))],
        out_specs=pl.BlockSpec((bm, n), lambda i: (i, 0)),
        out_shape=jax.ShapeDtypeStruct((m, n), F32),
        compiler_params=pltpu.CompilerParams(
            dimension_semantics=("arbitrary",), vmem_limit_bytes=VMEM_LIMIT),
        name=name,
    )(br, w, x, fw.reshape(1, n))


PT = 256
GLA_ROWS = 128
N_MIXER_STAGES = 2 + 4 * (PT // GLA_ROWS) + 1 + 2 * H_X


def _group_a_chunks(proj_ref, wa_ref, abt_ref, avw_ref):
    v = proj_ref[:, C_AV:C_AV + D_A].astype(F32)
    n_chunks = v.shape[0] // CHUNK_A
    ms = jnp.mean(v * v, axis=-1, keepdims=True)
    vb = (v * lax.rsqrt(ms + EPS) * avw_ref[...]).astype(BF16)
    tril = _iota((CHUNK_A, CHUNK_A), 1) <= _iota((CHUNK_A, CHUNK_A), 0)
    r = []
    for gi in range(H_A):
        w = jnp.where(tril, wa_ref[gi], 0.0).astype(BF16)
        ws = slice(WIN_START[gi], WIN_START[gi] + WIN)
        rhs = jnp.concatenate([vb[c * CHUNK_A:(c + 1) * CHUNK_A, ws] for c in range(n_chunks)], axis=1)
        res = _dot(w, rhs)
        r.append(jnp.concatenate([res[:, c * WIN:(c + 1) * WIN] for c in range(n_chunks)], axis=0))
    mixed = _merge_windows(r)
    col_g = _head_id(_iota((CHUNK_A, D_A), 1), HD_A, H_A)
    bias = jnp.zeros((CHUNK_A, D_A), F32)
    for gi in range(H_A):
        bias = jnp.where(col_g == gi, abt_ref[:, gi:gi + 1], bias)
    bias = jnp.concatenate([bias] * n_chunks, axis=0)
    u = proj_ref[:, C_AU:C_AU + D_A].astype(F32)
    g = proj_ref[:, C_AG:C_AG + D_A].astype(F32)
    return u * (mixed + bias) * _silu(g)


def _prompt_block(proj_ref, out_ref, mk_ref, mv_ref, wa_ref, abt_ref, avw_ref, bwa_ref, bba_ref,
                  onw_ref, sbd_ref, o_ref, la_ref):
    out_ref[:, 0:D_A] = _group_a_chunks(proj_ref, wa_ref, abt_ref, avw_ref).astype(BF16)
    yield

    pre = _dot(proj_ref[:, C_BR:C_BR + LANES], bwa_ref[...]) + bba_ref[...]
    la_ref[...] = _log_sigmoid(pre) * (1.0 / GATE_TAU)
    yield
    gl = GLA_ROWS
    tril_bf = jnp.where(_iota((gl, gl), 1) <= _iota((gl, gl), 0), 1.0, 0.0).astype(BF16)
    win_col = _iota((gl, WIN), 1)
    col_hq = _head_id(_iota((gl, D_BK), 1), DK_B, H_B)
    tril_heads = (_iota((H_B * gl, gl), 1) <= jnp.bitwise_and(_iota((H_B * gl, gl), 0), gl - 1))
    for c in range(PT // gl):
        sl = slice(c * gl, (c + 1) * gl)
        log_a = la_ref[sl, :]
        q = proj_ref[sl, C_BQ:C_BQ + D_BK].astype(F32)
        k = proj_ref[sl, C_BK:C_BK + D_BK].astype(F32)
        vb = proj_ref[sl, C_BV:C_BV + D_B]
        hi, lo = _split(log_a)
        cum2 = _dot(tril_bf, jnp.concatenate([hi, lo], axis=1))
        yield
        cum = cum2[:, :D_BK] + cum2[:, D_BK:]
        tot_row = cum[gl - 1:gl, :]
        mid_row = cum[CHUNK_B - 1:CHUNK_B, :]
        la_t = log_a.T
        q_s = q * (DK_B ** -0.5)
        q_dec = q_s * jnp.exp(cum)
        q_in = q_s * jnp.exp(cum - mid_row)
        k_in = k * jnp.exp(mid_row - cum)
        k_out = k * jnp.exp(tot_row - cum)
        q_heads = jnp.concatenate([jnp.where(col_hq == h, q_in, 0.0) for h in range(H_B)], axis=0)
        sc = _dot_nt(q_heads.astype(BF16), k_in.astype(BF16))
        yield
        sc = jnp.where(tril_heads, sc, 0.0).astype(BF16)
        o_intra = _merge_windows(
            [_dot(sc[h * gl:(h + 1) * gl], vb[:, WIN_START[h]:WIN_START[h] + WIN])
             for h in range(H_B)])
        s_old = sbd_ref[...]
        o_inter = _dot(q_dec.astype(BF16), s_old.astype(BF16))
        yield
        kot = k_out.T.astype(BF16)
        for h in range(H_B):
            rs = slice(h * DK_B, (h + 1) * DK_B)
            ws = slice(WIN_START[h], WIN_START[h] + WIN)
            dec = jnp.exp(jnp.sum(la_t[rs], axis=1, keepdims=True))
            lo_col = h * DV_B - WIN_START[h]
            in_head = (win_col >= lo_col) & (win_col < lo_col + DV_B)
            v_h = jnp.where(in_head, vb[:, ws], jnp.zeros((), BF16))
            kv = _dot(kot[rs], v_h)
            sbd_ref[rs, ws] = s_old[rs, ws] * dec + kv
        o_ref[sl, :] = o_intra + o_inter
        yield
    o = o_ref[...]
    o2 = o * o
    t = [o2[:, j * LANES:(j + 1) * LANES] for j in range(D_B // LANES)]
    lo = _iota((PT, LANES), 1) < 64
    t1a = jnp.where(lo, t[1], 0.0)
    t4a = jnp.where(lo, t[4], 0.0)
    sums = (t[0] + t1a, (t[1] - t1a) + t[2], t[3] + t4a, (t[4] - t4a) + t[5])
    inv = [lax.rsqrt(jnp.sum(x, axis=-1, keepdims=True) * (1.0 / DV_B) + EPS) for x in sums]
    inv = jnp.concatenate(
        [jnp.broadcast_to(inv[0], (PT, LANES)), jnp.where(lo, inv[0], inv[1]),
         jnp.broadcast_to(inv[1], (PT, LANES)), jnp.broadcast_to(inv[2], (PT, LANES)),
         jnp.where(lo, inv[2], inv[3]), jnp.broadcast_to(inv[3], (PT, LANES))], axis=1)
    o_n = o * inv * onw_ref[...]
    bg = proj_ref[:, C_BG:C_BG + D_B].astype(F32)
    out_ref[:, D_A:D_A + D_B] = (o_n * _silu(bg)).astype(BF16)
    yield

    for h in range(H_X):
        hs = slice(h * HD_X, (h + 1) * HD_X)
        qh = proj_ref[:, C_XQ + h * HD_X:C_XQ + (h + 1) * HD_X]
        s = _dot_nt(qh, mk_ref[:, hs].astype(BF16)) * (HD_X ** -0.5)
        yield
        e = jnp.exp(s - jnp.max(s, axis=-1, keepdims=True))
        den = jnp.sum(e, axis=-1, keepdims=True)
        ox = _dot(e.astype(BF16), mv_ref[:, hs].astype(BF16)) / den
        xg = proj_ref[:, C_XG + h * HD_X:C_XG + (h + 1) * HD_X].astype(F32)
        out_ref[:, D_A + D_B + h * HD_X:D_A + D_B + (h + 1) * HD_X] = (ox * _silu(xg)).astype(BF16)
        yield


def _prompt_layer_kernel(xn_ref, wi_ref, wo_ref, nw_ref, fw_ref, mk_ref, mv_ref, wa_ref, abt_ref,
                         avw_ref, bwa_ref, bba_ref, onw_ref, y_ref, st_ref,
                         pa_ref, pb_ref, xk_ref, br_ref, h_ref, sbd_ref, o_ref, la_ref, *, nt):
    s = pl.program_id(0)
    cur = jnp.maximum(s - 1, 0)
    t = lax.rem(cur, nt)

    @pl.when(t == 0)
    def _():
        sbd_ref[...] = jnp.zeros_like(sbd_ref)

    def in_proj_stages(pn_ref):
        x = xn_ref[...]
        ms = jnp.mean(x * x, axis=-1, keepdims=True)
        h_ref[...] = (x * lax.rsqrt(ms + EPS) * nw_ref[...]).astype(BF16)
        yield
        for c0 in range(0, D_PROJ_PAD, MXU_COLS):
            cols = pl.ds(c0, MXU_COLS)
            pn_ref[:, cols] = _dot(h_ref[...], wi_ref[:, cols]).astype(BF16)
            yield
        xk_ref[...] = xn_ref[...]

    def out_proj_stages():
        ssq = jnp.zeros((PT, 1), F32)
        for c0 in range(0, D_MODEL, OUT_BN):
            cols = pl.ds(c0, OUT_BN)
            acc = _dot(br_ref[...], wo_ref[:, cols]) + xk_ref[:, cols]
            y_ref[:, cols] = acc
            ssq = ssq + jnp.sum(acc * acc, axis=-1, keepdims=True)
            yield
        y_ref[...] = y_ref[...] * lax.rsqrt(ssq * (1.0 / D_MODEL) + EPS) * fw_ref[...]
        yield

    def run(order, streams):
        for name in order:
            next(streams[name])
        for name, gen in streams.items():
            assert next(gen, "done") == "done", name

    def mixer_stages(pc_ref):
        return _prompt_block(pc_ref, br_ref, mk_ref, mv_ref, wa_ref, abt_ref, avw_ref, bwa_ref,
                             bba_ref, onw_ref, sbd_ref, o_ref, la_ref)

    n_p = 1 + D_PROJ_PAD // MXU_COLS
    n_m = N_MIXER_STAGES
    n_o = 1 + D_MODEL // OUT_BN

    def body(pn_ref, pc_ref):
        order = ["P"]
        for i in range(max(n_p - 1, n_m)):
            order += ["M"] * (i < n_m) + ["P"] * (i < n_p - 1)
        order += ["O"] * n_o
        run(order, {"P": in_proj_stages(pn_ref), "M": mixer_stages(pc_ref), "O": out_proj_stages()})

    last = pl.num_programs(0) - 1

    @pl.when(s == 0)
    def _():
        run(["P"] * n_p, {"P": in_proj_stages(pa_ref)})

    @pl.when((lax.rem(s, 2) == 0) & (s > 0) & (s < last))
    def _():
        body(pa_ref, pb_ref)

    @pl.when(lax.rem(s, 2) == 1)
    def _():
        body(pb_ref, pa_ref)

    @pl.when(s == last)
    def _():
        run(["M"] * n_m + ["O"] * n_o, {"M": mixer_stages(pb_ref), "O": out_proj_stages()})

    @pl.when((t == nt - 1) & (s > 0))
    def _():
        for h in range(H_B):
            off = h * DV_B - WIN_START[h]
            blk = sbd_ref[h * DK_B:(h + 1) * DK_B, WIN_START[h]:WIN_START[h] + WIN]
            if off:
                blk = pltpu.roll(blk, WIN - off, 1)
            st_ref[0, h] = blk[:, :DV_B]


def _prompt_layer(xp, w_in_bf, w_out_bf, nw, fw, memkv, wa, abt, avw, bwa, bba, onw, *, batch, seq):
    nt = seq // PT
    nblk = batch * nt
    assert nblk % 2 == 0
    cur = lambda s: jnp.maximum(s - 1, 0)
    const = lambda *shape: pl.BlockSpec(shape, lambda s: (0,) * len(shape))
    resident = lambda *shape: pl.BlockSpec(shape, lambda s: (0,) * len(shape),
                                           pipeline_mode=pl.Buffered(1))
    return pl.pallas_call(
        functools.partial(_prompt_layer_kernel, nt=nt),
        grid=(nblk + 1,),
        in_specs=[pl.BlockSpec((PT, D_MODEL), lambda s: (jnp.minimum(s, nblk - 1), 0)),
                  resident(D_MODEL, D_PROJ_PAD), resident(D_MODEL, D_MODEL),
                  const(1, D_MODEL), const(1, D_MODEL),
                  pl.BlockSpec((None, N_MEM, D_X), lambda s: (0, cur(s) // nt, 0)),
                  pl.BlockSpec((None, N_MEM, D_X), lambda s: (1, cur(s) // nt, 0)),
                  const(H_A, CHUNK_A, CHUNK_A), const(CHUNK_A, H_A), const(1, D_A),
                  const(LANES, D_BK), const(1, D_BK), const(1, D_B)],
        out_specs=[pl.BlockSpec((PT, D_MODEL), lambda s: (cur(s), 0)),
                   pl.BlockSpec((1, H_B, DK_B, DV_B), lambda s: (cur(s) // nt, 0, 0, 0))],
        out_shape=[jax.ShapeDtypeStruct((batch * seq, D_MODEL), F32),
                   jax.ShapeDtypeStruct((batch, H_B, DK_B, DV_B), F32)],
        scratch_shapes=[pltpu.VMEM((PT, D_PROJ_PAD), BF16),
                        pltpu.VMEM((PT, D_PROJ_PAD), BF16),
                        pltpu.VMEM((PT, D_MODEL), F32),
                        pltpu.VMEM((PT, D_MODEL), BF16),
                        pltpu.VMEM((PT, D_MODEL), BF16),
                        pltpu.VMEM((D_BK, D_B), F32),
                        pltpu.VMEM((PT, D_B), F32),
                        pltpu.VMEM((PT, D_BK), F32)],
        compiler_params=pltpu.CompilerParams(
            dimension_semantics=("arbitrary",), vmem_limit_bytes=VMEM_LIMIT),
        name="prompt_layer",
    )(xp, w_in_bf, w_out_bf, nw.reshape(1, D_MODEL), fw.reshape(1, D_MODEL), memkv, memkv,
      wa, abt, avw, bwa, bba, onw)


NS = 8
TS = 8
SB = NS * TS
SBP = 128


def _sample_mixer_kernel(proj_ref, st_ref, ck_ref, cv_ref, wa_ref, abt_ref, avw_ref, bwa_ref,
                         bba_ref, onw_ref, out_ref, stn_ref, cvs_ref,
                         qin_ref, xq_ref, kot_ref, lat_ref, vhm_ref, ghm_ref, ohm_ref, ox_ref):
    ri = _iota((SB, SB), 0)
    ci = _iota((SB, SB), 1)
    same_seq = jnp.right_shift(ri, 3) == jnp.right_shift(ci, 3)
    causal = same_seq & (ci <= ri)

    a_br, vn = _group_a(proj_ref, wa_ref, abt_ref, avw_ref, causal)
    out_ref[:, 0:D_A] = a_br.astype(BF16)
    cvs_ref[...] = vn

    q = proj_ref[:, C_BQ:C_BQ + D_BK].astype(F32)
    k = proj_ref[:, C_BK:C_BK + D_BK].astype(F32)
    vb = proj_ref[:, C_BV:C_BV + D_B]
    bgb = proj_ref[:, C_BG:C_BG + D_B]
    pre = _dot(proj_ref[:, C_BR:C_BR + LANES], bwa_ref[...]) + bba_ref[...]
    log_a = _log_sigmoid(pre) * (1.0 / GATE_TAU)
    causal_bf = jnp.where(causal, 1.0, 0.0).astype(BF16)
    seq_bf = jnp.where(same_seq, 1.0, 0.0).astype(BF16)
    hi, lo = _split(log_a)
    cum = _dot(causal_bf, hi) + _dot(causal_bf, lo)
    tot = _dot(seq_bf, hi) + _dot(seq_bf, lo)
    q_in = q * (DK_B ** -0.5) * jnp.exp(cum)
    k_in = k * jnp.exp(-cum)
    k_out = k * jnp.exp(tot - cum)
    qin_ref[...] = q_in
    zpad = jnp.zeros((SBP - SB, D_BK), F32)
    kot_ref[...] = jnp.concatenate([k_out, zpad], axis=0).T.astype(BF16)
    lat_ref[...] = jnp.concatenate([log_a, zpad], axis=0).T
    xq_ref[...] = proj_ref[:, C_XQ:C_XQ + D_X].astype(F32)

    wins = _gla_intra_windows(q_in, k_in, vb, causal)
    sel_r = _iota((D_B, DV_B), 0)
    sel_c = _iota((D_B, DV_B), 1)
    vhm_ref[...] = jnp.zeros_like(vhm_ref)
    for h in range(H_B):
        off = h * DV_B - WIN_START[h]
        w = wins[h]
        if off:
            w = pltpu.roll(w, WIN - off, 1)
        ohm_ref[h] = w[:, :DV_B]
        sel = jnp.where(sel_r == sel_c + h * DV_B, 1.0, 0.0).astype(BF16)
        vhm_ref[h, 0:SB, :] = _dot(vb, sel).astype(BF16)
        ghm_ref[h] = _dot(bgb, sel)

    mask_x = (jnp.right_shift(_iota((H_X * TS, N_MEM * H_X), 0), 3)
              == jnp.bitwise_and(_iota((H_X * TS, N_MEM * H_X), 1), H_X - 1))
    mask_b = jnp.right_shift(_iota((H_B * TS, D_BK), 0), 3) == _head_id(_iota((H_B * TS, D_BK), 1), DK_B, H_B)
    lane_seq = jnp.right_shift(_iota((D_BK, SBP), 1), 3)

    def per_seq(s, carry):
        r0 = pl.multiple_of(s * TS, TS)
        q8 = xq_ref[pl.ds(r0, TS), :]
        q32 = jnp.concatenate([q8[:, h * HD_X:(h + 1) * HD_X] for h in range(H_X)], axis=0)
        sc = _dot_nt(q32.astype(BF16), ck_ref[s].astype(BF16)) * (HD_X ** -0.5)
        sc = jnp.where(mask_x, sc, -1e30)
        e = jnp.exp(sc - jnp.max(sc, axis=-1, keepdims=True))
        den = jnp.sum(e, axis=-1, keepdims=True)
        o = _dot(e.astype(BF16), cv_ref[s].astype(BF16)) / den
        ox_ref[pl.ds(r0, TS), :] = jnp.concatenate(
            [o[h * TS:(h + 1) * TS] for h in range(H_X)], axis=1)
        qi8 = qin_ref[pl.ds(r0, TS), :]
        qbd2 = jnp.where(mask_b, jnp.concatenate([qi8] * H_B, axis=0), 0.0).astype(BF16)
        s0 = jnp.concatenate([st_ref[s, h] for h in range(H_B)], axis=0)
        o_inter = _dot(qbd2, s0.astype(BF16))
        for h in range(H_B):
            ohm_ref[h, pl.ds(r0, TS), :] += o_inter[h * TS:(h + 1) * TS]
        dec = jnp.exp(jnp.sum(jnp.where(lane_seq == s, lat_ref[...], 0.0), axis=1, keepdims=True))
        kot = jnp.where(lane_seq == s, kot_ref[...], jnp.zeros((), BF16))
        kv = jnp.concatenate(
            [_dot(kot[h * DK_B:(h + 1) * DK_B], vhm_ref[h]) for h in range(H_B)], axis=0)
        s_new = s0 * dec + kv
        for h in range(H_B):
            stn_ref[s, h] = s_new[h * DK_B:(h + 1) * DK_B]
        return carry

    lax.fori_loop(0, NS, per_seq, 0, unroll=4)

    selt_r = _iota((DV_B, D_B), 0)
    selt_c = _iota((DV_B, D_B), 1)
    b_br = jnp.zeros((SB, D_B), F32)
    for h in range(H_B):
        o_h = ohm_ref[h]
        ms = jnp.mean(o_h * o_h, axis=-1, keepdims=True)
        ob = (o_h * lax.rsqrt(ms + EPS) * onw_ref[...] * _silu(ghm_ref[h])).astype(BF16)
        selt = jnp.where(selt_c == selt_r + h * DV_B, 1.0, 0.0).astype(BF16)
        b_br = b_br + _dot(ob, selt)
    out_ref[:, D_A:D_A + D_B] = b_br.astype(BF16)

    xg = proj_ref[:, C_XG:C_XG + D_X].astype(F32)
    out_ref[:, D_A + D_B:D_MODEL] = (ox_ref[...] * _silu(xg)).astype(BF16)


def _sample_mixer(proj, state, ck, cv, wa, abt, avw, bwa, bba, onw):
    nseq = state.shape[1]
    const = lambda *shape: pl.BlockSpec(shape, lambda i: (0,) * len(shape))
    return pl.pallas_call(
        _sample_mixer_kernel,
        grid=(nseq // NS,),
        in_specs=[pl.BlockSpec((SB, D_PROJ), lambda i: (i, 0)),
                  pl.BlockSpec((None, NS, H_B, DK_B, DV_B), lambda i: (0, i, 0, 0, 0)),
                  pl.BlockSpec((NS, N_MEM * H_X, HD_X), lambda i: (i, 0, 0)),
                  pl.BlockSpec((NS, N_MEM * H_X, HD_X), lambda i: (i, 0, 0)),
                  const(H_A, SB, SB), const(SB, H_A), const(1, D_A),
                  const(LANES, D_BK), const(1, D_BK), const(1, DV_B)],
        out_specs=[pl.BlockSpec((SB, D_MODEL), lambda i: (i, 0)),
                   pl.BlockSpec((None, NS, H_B, DK_B, DV_B), lambda i: (0, i, 0, 0, 0)),
                   pl.BlockSpec((SB, D_A), lambda i: (i, 0))],
        out_shape=[jax.ShapeDtypeStruct((nseq * TS, D_MODEL), BF16),
                   jax.ShapeDtypeStruct((1, nseq, H_B, DK_B, DV_B), F32),
                   jax.ShapeDtypeStruct((nseq * TS, D_A), F32)],
        scratch_shapes=[pltpu.VMEM((SB, D_BK), F32),
                        pltpu.VMEM((SB, D_X), F32),
                        pltpu.VMEM((D_BK, SBP), BF16),
                        pltpu.VMEM((D_BK, SBP), F32),
                        pltpu.VMEM((H_B, SBP, DV_B), BF16),
                        pltpu.VMEM((H_B, SB, DV_B), F32),
                        pltpu.VMEM((H_B, SB, DV_B), F32),
                        pltpu.VMEM((SB, D_X), F32)],
        compiler_params=pltpu.CompilerParams(
            dimension_semantics=("arbitrary",), vmem_limit_bytes=VMEM_LIMIT),
        name="sample_mixer",
    )(proj, state, ck, cv, wa, abt, avw, bwa, bba, onw)


def kernel(x_prompt, x_sample, mem_prompt, state_gla, cache_mem_k, cache_mem_v, norm_w, w_in,
           a_vnorm_w, a_ws, a_bs, b_wa, b_ba, b_onorm_w, mem_norm_w, w_mem_kv, w_out, final_norm_w):
    batch, seq, _ = x_prompt.shape
    nseq, tdec, _ = x_sample.shape
    depth = w_in.shape[0]
    assert depth == 1 and tdec == TS and seq % PT == 0 and nseq % NS == 0

    w_in_bf, w_out_bf = _weight_prep(jnp.transpose(w_in[0]), w_out[0])
    last_cols = D_PROJ_PAD - (D_PROJ_PAD // IN_BN) * IN_BN
    bwa = jnp.concatenate([b_wa[0], jnp.zeros((LANES - GATE_RANK, D_BK), F32)], axis=0).astype(BF16)
    bba = b_ba[0].reshape(1, D_BK)
    avw = a_vnorm_w[0].reshape(1, D_A)
    onw_p = jnp.tile(b_onorm_w[0], H_B).reshape(1, D_B)
    onw_s = b_onorm_w[0].reshape(1, DV_B)
    wa_p = a_ws[0]
    abt_p = a_bs[0].T
    wa_s = jnp.tile(a_ws[0][:, :TS, :TS], (1, NS, NS))
    abt_s = jnp.tile(a_bs[0][:, :TS], (1, NS)).T

    xp = x_prompt.reshape(batch * seq, D_MODEL)
    xs = x_sample.reshape(nseq * TS, D_MODEL)
    mem = mem_prompt.reshape(batch * N_MEM, D_MODEL)

    proj_s = _norm_matmul(xs, norm_w[0], w_in_bf, D_PROJ, bm=nseq * TS, bn=IN_BN, transposed=False,
                          out_dtype=BF16, last_cols=last_cols, name="in_proj_s")
    br_s, st_s, cvs = _sample_mixer(
        proj_s, state_gla,
        cache_mem_k.reshape(nseq, N_MEM * H_X, HD_X), cache_mem_v.reshape(nseq, N_MEM * H_X, HD_X),
        wa_s, abt_s, avw, bwa, bba, onw_s)
    y_s = _out_proj(br_s, w_out_bf, xs, final_norm_w, bm=512, name="out_proj_s")

    memkv, mem_k, mem_v = _mem_kv(mem, mem_norm_w[0], w_mem_kv[0], batch=batch)
    y_p, st_p = _prompt_layer(xp, w_in_bf, w_out_bf, norm_w[0], final_norm_w, memkv, wa_p, abt_p,
                              avw, bwa, bba, onw_p, batch=batch, seq=seq)

    return (y_p.reshape(batch, seq, D_MODEL),
            y_s.reshape(nseq, TS, D_MODEL),
            mem_k.reshape(1, batch, N_MEM, H_X, HD_X),
            mem_v.reshape(1, batch, N_MEM, H_X, HD_X),
            st_p.reshape(1, batch, H_B, DK_B, DV_B),
            st_s,
            cvs.reshape(1, nseq, TS, D_A))
```

```python
import functools

import jax
import jax.numpy as jnp
from jax import lax
from jax.experimental import pallas as pl
from jax.experimental.pallas import tpu as pltpu

F32 = jnp.float32
BF16 = jnp.bfloat16

D_MODEL = 2048
D_A = 768
H_A = 4
HD_A = 192
CHUNK_A = 128
D_B = 768
H_B = 4
DV_B = 192
DK_B = 96
D_BK = 384
GATE_RANK = 16
GATE_TAU = 16.0
CHUNK_B = 64
D_X = 512
H_X = 4
HD_X = 128
N_MEM = 256
EPS = 1e-6

LANES = 128
MXU_COLS = 256
C_AU, C_AV, C_AG = 0, 768, 1536
C_BQ, C_BK, C_BV, C_BG = 2304, 2688, 3072, 3840
C_XQ, C_XG, C_BR = 4608, 5120, 5632
D_PROJ = 5760
D_PROJ_PAD = 5888
WIN_START = (0, 128, 384, 512)
WIN = 256
IN_BN = 768
IN_BM_S = 512
OUT_BN = 512

VMEM_LIMIT = 60 * 1024 * 1024


def _dot(a, b):
    return jnp.dot(a, b, preferred_element_type=F32)


def _dot_nt(a, b):
    return lax.dot_general(a, b, (((1,), (1,)), ((), ())), preferred_element_type=F32)


def _dot_tn(a, b):
    return lax.dot_general(a, b, (((0,), (0,)), ((), ())), preferred_element_type=F32)


def _split(x):
    hi = x.astype(BF16)
    lo = (x - hi.astype(F32)).astype(BF16)
    return hi, lo


def _silu(x):
    return x / (1.0 + jnp.exp(-x))


def _log_sigmoid(x):
    return jnp.minimum(x, 0.0) - jnp.log1p(jnp.exp(-jnp.abs(x)))


def _head_id(idx, width, n):
    h = jnp.zeros_like(idx)
    for i in range(1, n):
        h = h + (idx >= i * width).astype(jnp.int32)
    return h


def _iota(shape, dim):
    return lax.broadcasted_iota(jnp.int32, shape, dim)


def _merge_windows(r):
    rows = r[0].shape[0]
    lo = _iota((rows, LANES), 1) < 64
    tiles = [r[0][:, :LANES], jnp.where(lo, r[0][:, LANES:], r[1][:, :LANES]), r[1][:, LANES:],
             r[2][:, :LANES], jnp.where(lo, r[2][:, LANES:], r[3][:, :LANES]), r[3][:, LANES:]]
    return jnp.concatenate(tiles, axis=1)


def _group_a(proj_ref, wa_ref, abt_ref, avw_ref, mask):
    u = proj_ref[:, C_AU:C_AU + D_A].astype(F32)
    v = proj_ref[:, C_AV:C_AV + D_A].astype(F32)
    g = proj_ref[:, C_AG:C_AG + D_A].astype(F32)
    rows = u.shape[0]
    ms = jnp.mean(v * v, axis=-1, keepdims=True)
    vn = v * lax.rsqrt(ms + EPS) * avw_ref[...]
    vb = vn.astype(BF16)
    r = []
    for gi in range(H_A):
        w = jnp.where(mask, wa_ref[gi], 0.0).astype(BF16)
        r.append(_dot(w, vb[:, WIN_START[gi]:WIN_START[gi] + WIN]))
    mixed = _merge_windows(r)
    col_g = _head_id(_iota((rows, D_A), 1), HD_A, H_A)
    bias = jnp.zeros((rows, D_A), F32)
    for gi in range(H_A):
        bias = jnp.where(col_g == gi, abt_ref[:, gi:gi + 1], bias)
    return u * (mixed + bias) * _silu(g), vn


def _gla_intra_windows(q_in, k_in, vb, mask):
    rows = q_in.shape[0]
    col_h = _head_id(_iota((rows, D_BK), 1), DK_B, H_B)
    kb = k_in.astype(BF16)
    r = []
    for h in range(H_B):
        qh = jnp.where(col_h == h, q_in, 0.0).astype(BF16)
        s = jnp.where(mask, _dot_nt(qh, kb), 0.0).astype(BF16)
        r.append(_dot(s, vb[:, WIN_START[h]:WIN_START[h] + WIN]))
    return r


def _norm_matmul_kernel(x_ref, nw_ref, w_ref, o_ref, h_ref, *, transposed, last_cols):
    j = pl.program_id(1)
    nj = pl.num_programs(1)

    @pl.when(j == 0)
    def _():
        x = x_ref[...]
        ms = jnp.mean(x * x, axis=-1, keepdims=True)
        h_ref[...] = (x * lax.rsqrt(ms + EPS) * nw_ref[...]).astype(BF16)

    mm = _dot_nt if transposed else _dot
    if last_cols is None:
        o_ref[...] = mm(h_ref[...], w_ref[...].astype(BF16)).astype(o_ref.dtype)
    else:
        @pl.when(j < nj - 1)
        def _():
            o_ref[...] = mm(h_ref[...], w_ref[...].astype(BF16)).astype(o_ref.dtype)

        @pl.when(j == nj - 1)
        def _():
            w = w_ref[:last_cols, :] if transposed else w_ref[:, :last_cols]
            o_ref[:, :last_cols] = mm(h_ref[...], w.astype(BF16)).astype(o_ref.dtype)


def _norm_matmul(x, nw, w, n_out, *, bm, bn, transposed, out_dtype, split_out=False,
                 last_cols=None, name):
    m, k = x.shape
    n = w.shape[0] if transposed else w.shape[1]
    nj = -(-n // bn)
    assert m % bm == 0
    if split_out:
        out_shape = jax.ShapeDtypeStruct((nj, m, bn), out_dtype)
        out_spec = pl.BlockSpec((None, bm, bn), lambda i, j: (j, i, 0))
    else:
        out_shape = jax.ShapeDtypeStruct((m, n_out), out_dtype)
        out_spec = pl.BlockSpec((bm, bn), lambda i, j: (i, j))
    if transposed:
        w_spec = pl.BlockSpec((bn, k), lambda i, j: (j, 0))
    else:
        w_spec = pl.BlockSpec((k, bn), lambda i, j: (0, j))
    return pl.pallas_call(
        functools.partial(_norm_matmul_kernel, transposed=transposed, last_cols=last_cols),
        grid=(m // bm, nj),
        in_specs=[pl.BlockSpec((bm, k), lambda i, j: (i, 0)),
                  pl.BlockSpec((1, k), lambda i, j: (0, 0)),
                  w_spec],
        out_specs=out_spec,
        out_shape=out_shape,
        scratch_shapes=[pltpu.VMEM((bm, k), BF16)],
        compiler_params=pltpu.CompilerParams(
            dimension_semantics=("arbitrary", "arbitrary"), vmem_limit_bytes=VMEM_LIMIT),
        name=name,
    )(x, nw.reshape(1, k), w)


def _mem_kv_kernel(x_ref, nw_ref, w_ref, o_ref, ok_ref, ov_ref, wb_ref):
    @pl.when(pl.program_id(0) == 0)
    def _():
        wb_ref[...] = w_ref[...].astype(BF16)

    x = x_ref[...]
    ms = jnp.mean(x * x, axis=-1, keepdims=True)
    h = (x * lax.rsqrt(ms + EPS) * nw_ref[...]).astype(BF16)
    kv = _dot(h, wb_ref[...])
    o_ref[0] = kv[:, :D_X]
    o_ref[1] = kv[:, D_X:]
    for hd in range(H_X):
        ok_ref[pl.ds(hd, N_MEM, stride=H_X), :] = kv[:, hd * HD_X:(hd + 1) * HD_X]
        ov_ref[pl.ds(hd, N_MEM, stride=H_X), :] = kv[:, D_X + hd * HD_X:D_X + (hd + 1) * HD_X]


def _mem_kv(mem, nw, w, *, batch):
    return pl.pallas_call(
        _mem_kv_kernel,
        grid=(batch,),
        in_specs=[pl.BlockSpec((N_MEM, D_MODEL), lambda i: (i, 0)),
                  pl.BlockSpec((1, D_MODEL), lambda i: (0, 0)),
                  pl.BlockSpec((D_MODEL, 2 * D_X), lambda i: (0, 0))],
        out_specs=[pl.BlockSpec((2, N_MEM, D_X), lambda i: (0, i, 0)),
                   pl.BlockSpec((None, N_MEM * H_X, HD_X), lambda i: (i, 0, 0)),
                   pl.BlockSpec((None, N_MEM * H_X, HD_X), lambda i: (i, 0, 0))],
        out_shape=[jax.ShapeDtypeStruct((2, batch * N_MEM, D_X), F32),
                   jax.ShapeDtypeStruct((batch, N_MEM * H_X, HD_X), F32),
                   jax.ShapeDtypeStruct((batch, N_MEM * H_X, HD_X), F32)],
        scratch_shapes=[pltpu.VMEM((D_MODEL, 2 * D_X), BF16)],
        compiler_params=pltpu.CompilerParams(
            dimension_semantics=("arbitrary",), vmem_limit_bytes=VMEM_LIMIT),
        name="mem_kv",
    )(mem, nw.reshape(1, D_MODEL), w)


PREP_BN = MXU_COLS
N_MAIN_BLOCKS = C_BG // PREP_BN


GATE_BLOCK = D_PROJ_PAD // PREP_BN - 1
N_IN_BLOCKS = GATE_BLOCK + 1
N_OUT_BLOCKS = D_MODEL // PREP_BN
PREP_BUFS = 4


def _weight_prep_kernel(wt_hbm, wo_hbm, oi_ref, oo_ref, buf, sem):
    r = pl.program_id(0)
    slot = lax.rem(r, PREP_BUFS)

    def in_copy(rr, sl):
        start = pl.multiple_of(jnp.where(rr < N_MAIN_BLOCKS, rr * PREP_BN, rr * PREP_BN + GATE_RANK), 8)
        return pltpu.make_async_copy(wt_hbm.at[pl.ds(start, PREP_BN), :], buf.at[sl], sem.at[sl])

    def gate_copy(sl):
        return pltpu.make_async_copy(wt_hbm.at[pl.ds(C_BG, GATE_RANK), :],
                                     buf.at[sl, pl.ds(0, GATE_RANK), :], sem.at[sl])

    def out_copy(rr, sl):
        start = pl.multiple_of((rr - N_IN_BLOCKS) * PREP_BN, PREP_BN)
        return pltpu.make_async_copy(wo_hbm.at[pl.ds(start, PREP_BN), :], buf.at[sl], sem.at[sl])

    def start_fetch(rr, sl):
        @pl.when(rr < GATE_BLOCK)
        def _():
            in_copy(rr, sl).start()

        @pl.when(rr == GATE_BLOCK)
        def _():
            gate_copy(sl).start()

        @pl.when(rr > GATE_BLOCK)
        def _():
            out_copy(rr, sl).start()

    @pl.when(r == 0)
    def _():
        for ahead in range(PREP_BUFS - 1):
            start_fetch(r + ahead, ahead)

    nxt = r + PREP_BUFS - 1

    @pl.when(nxt < pl.num_programs(0))
    def _():
        start_fetch(nxt, lax.rem(nxt, PREP_BUFS))

    @pl.when(r < GATE_BLOCK)
    def _():
        in_copy(r, slot).wait()
        oi_ref[...] = buf[slot].T.astype(BF16)

    @pl.when(r == GATE_BLOCK)
    def _():
        gate_copy(slot).wait()
        rows = jnp.concatenate([buf[slot, 0:GATE_RANK, :],
                                jnp.zeros((PREP_BN - GATE_RANK, D_MODEL), F32)], axis=0)
        oi_ref[...] = rows.T.astype(BF16)

    @pl.when(r > GATE_BLOCK)
    def _():
        out_copy(r, slot).wait()
        oo_ref[...] = buf[slot].astype(BF16)


def _weight_prep(w_t, w_out):
    k = w_t.shape[1]
    return pl.pallas_call(
        _weight_prep_kernel,
        grid=(N_IN_BLOCKS + N_OUT_BLOCKS,),
        in_specs=[pl.BlockSpec(memory_space=pl.ANY), pl.BlockSpec(memory_space=pl.ANY)],
        out_specs=[pl.BlockSpec((k, PREP_BN), lambda r: (0, jnp.minimum(r, GATE_BLOCK))),
                   pl.BlockSpec((PREP_BN, D_MODEL), lambda r: (jnp.maximum(r - N_IN_BLOCKS, 0), 0))],
        out_shape=[jax.ShapeDtypeStruct((k, D_PROJ_PAD), BF16),
                   jax.ShapeDtypeStruct((D_MODEL, D_MODEL), BF16)],
        scratch_shapes=[pltpu.VMEM((PREP_BUFS, PREP_BN, k), F32),
                        pltpu.SemaphoreType.DMA((PREP_BUFS,))],
        compiler_params=pltpu.CompilerParams(
            dimension_semantics=("arbitrary",), vmem_limit_bytes=VMEM_LIMIT),
        name="weight_prep",
    )(w_t, w_out)


def _out_proj_kernel(br_ref, w_ref, x_ref, fw_ref, y_ref):
    acc = _dot(br_ref[...], w_ref[...]) + x_ref[...]
    ms = jnp.mean(acc * acc, axis=-1, keepdims=True)
    y_ref[...] = acc * lax.rsqrt(ms + EPS) * fw_ref[...]


def _out_proj(br, w, x, fw, *, bm, name):
    m, k = br.shape
    n = w.shape[1]
    return pl.pallas_call(
        _out_proj_kernel,
        grid=(m // bm,),
        in_specs=[pl.BlockSpec((bm, k), lambda i: (i, 0)),
                  pl.BlockSpec((k, n), lambda i: (0, 0)),
                  pl.BlockSpec((bm, n), lambda i: (i, 0)),
                  pl.BlockSpec((1, n), lambda i: (0, 0))],
        out_specs=pl.BlockSpec((bm, n), lambda i: (i, 0)),
        out_shape=jax.ShapeDtypeStruct((m, n), F32),
        compiler_params=pltpu.CompilerParams(
            dimension_semantics=("arbitrary",), vmem_limit_bytes=VMEM_LIMIT),
        name=name,
    )(br, w, x, fw.reshape(1, n))


PT = 256
GLA_ROWS = 128
N_MIXER_STAGES = 2 + 4 * (PT // GLA_ROWS) + 1 + 2 * H_X


def _group_a_chunks(proj_ref, wa_ref, abt_ref, avw_ref):
    v = proj_ref[:, C_AV:C_AV + D_A].astype(F32)
    n_chunks = v.shape[0] // CHUNK_A
    ms = jnp.mean(v * v, axis=-1, keepdims=True)
    vb = (v * lax.rsqrt(ms + EPS) * avw_ref[...]).astype(BF16)
    tril = _iota((CHUNK_A, CHUNK_A), 1) <= _iota((CHUNK_A, CHUNK_A), 0)
    r = []
    for gi in range(H_A):
        w = jnp.where(tril, wa_ref[gi], 0.0).astype(BF16)
        ws = slice(WIN_START[gi], WIN_START[gi] + WIN)
        rhs = jnp.concatenate([vb[c * CHUNK_A:(c + 1) * CHUNK_A, ws] for c in range(n_chunks)], axis=1)
        res = _dot(w, rhs)
        r.append(jnp.concatenate([res[:, c * WIN:(c + 1) * WIN] for c in range(n_chunks)], axis=0))
    mixed = _merge_windows(r)
    col_g = _head_id(_iota((CHUNK_A, D_A), 1), HD_A, H_A)
    bias = jnp.zeros((CHUNK_A, D_A), F32)
    for gi in range(H_A):
        bias = jnp.where(col_g == gi, abt_ref[:, gi:gi + 1], bias)
    bias = jnp.concatenate([bias] * n_chunks, axis=0)
    u = proj_ref[:, C_AU:C_AU + D_A].astype(F32)
    g = proj_ref[:, C_AG:C_AG + D_A].astype(F32)
    return u * (mixed + bias) * _silu(g)


def _prompt_block(proj_ref, out_ref, mk_ref, mv_ref, wa_ref, abt_ref, avw_ref, bwa_ref, bba_ref,
                  onw_ref, sbd_ref, o_ref, la_ref):
    out_ref[:, 0:D_A] = _group_a_chunks(proj_ref, wa_ref, abt_ref, avw_ref).astype(BF16)
    yield

    pre = _dot(proj_ref[:, C_BR:C_BR + LANES], bwa_ref[...]) + bba_ref[...]
    la_ref[...] = _log_sigmoid(pre) * (1.0 / GATE_TAU)
    yield
    gl = GLA_ROWS
    tril_bf = jnp.where(_iota((gl, gl), 1) <= _iota((gl, gl), 0), 1.0, 0.0).astype(BF16)
    win_col = _iota((gl, WIN), 1)
    col_hq = _head_id(_iota((gl, D_BK), 1), DK_B, H_B)
    tril_heads = (_iota((H_B * gl, gl), 1) <= jnp.bitwise_and(_iota((H_B * gl, gl), 0), gl - 1))
    for c in range(PT // gl):
        sl = slice(c * gl, (c + 1) * gl)
        log_a = la_ref[sl, :]
        q = proj_ref[sl, C_BQ:C_BQ + D_BK].astype(F32)
        k = proj_ref[sl, C_BK:C_BK + D_BK].astype(F32)
        vb = proj_ref[sl, C_BV:C_BV + D_B]
        hi, lo = _split(log_a)
        cum2 = _dot(tril_bf, jnp.concatenate([hi, lo], axis=1))
        yield
        cum = cum2[:, :D_BK] + cum2[:, D_BK:]
        tot_row = cum[gl - 1:gl, :]
        mid_row = cum[CHUNK_B - 1:CHUNK_B, :]
        la_t = log_a.T
        q_s = q * (DK_B ** -0.5)
        q_dec = q_s * jnp.exp(cum)
        q_in = q_s * jnp.exp(cum - mid_row)
        k_in = k * jnp.exp(mid_row - cum)
        k_out = k * jnp.exp(tot_row - cum)
        q_heads = jnp.concatenate([jnp.where(col_hq == h, q_in, 0.0) for h in range(H_B)], axis=0)
        sc = _dot_nt(q_heads.astype(BF16), k_in.astype(BF16))
        yield
        sc = jnp.where(tril_heads, sc, 0.0).astype(BF16)
        o_intra = _merge_windows(
            [_dot(sc[h * gl:(h + 1) * gl], vb[:, WIN_START[h]:WIN_START[h] + WIN])
             for h in range(H_B)])
        s_old = sbd_ref[...]
        o_inter = _dot(q_dec.astype(BF16), s_old.astype(BF16))
        yield
        kot = k_out.T.astype(BF16)
        for h in range(H_B):
            rs = slice(h * DK_B, (h + 1) * DK_B)
            ws = slice(WIN_START[h], WIN_START[h] + WIN)
            dec = jnp.exp(jnp.sum(la_t[rs], axis=1, keepdims=True))
            lo_col = h * DV_B - WIN_START[h]
            in_head = (win_col >= lo_col) & (win_col < lo_col + DV_B)
            v_h = jnp.where(in_head, vb[:, ws], jnp.zeros((), BF16))
            kv = _dot(kot[rs], v_h)
            sbd_ref[rs, ws] = s_old[rs, ws] * dec + kv
        o_ref[sl, :] = o_intra + o_inter
        yield
    o = o_ref[...]
    o2 = o * o
    t = [o2[:, j * LANES:(j + 1) * LANES] for j in range(D_B // LANES)]
    lo = _iota((PT, LANES), 1) < 64
    t1a = jnp.where(lo, t[1], 0.0)
    t4a = jnp.where(lo, t[4], 0.0)
    sums = (t[0] + t1a, (t[1] - t1a) + t[2], t[3] + t4a, (t[4] - t4a) + t[5])
    inv = [lax.rsqrt(jnp.sum(x, axis=-1, keepdims=True) * (1.0 / DV_B) + EPS) for x in sums]
    inv = jnp.concatenate(
        [jnp.broadcast_to(inv[0], (PT, LANES)), jnp.where(lo, inv[0], inv[1]),
         jnp.broadcast_to(inv[1], (PT, LANES)), jnp.broadcast_to(inv[2], (PT, LANES)),
         jnp.where(lo, inv[2], inv[3]), jnp.broadcast_to(inv[3], (PT, LANES))], axis=1)
    o_n = o * inv * onw_ref[...]
    bg = proj_ref[:, C_BG:C_BG + D_B].astype(F32)
    out_ref[:, D_A:D_A + D_B] = (o_n * _silu(bg)).astype(BF16)
    yield

    for h in range(H_X):
        hs = slice(h * HD_X, (h + 1) * HD_X)
        qh = proj_ref[:, C_XQ + h * HD_X:C_XQ + (h + 1) * HD_X]
        s = _dot_nt(qh, mk_ref[:, hs].astype(BF16)) * (HD_X ** -0.5)
        yield
        e = jnp.exp(s - jnp.max(s, axis=-1, keepdims=True))
        den = jnp.sum(e, axis=-1, keepdims=True)
        ox = _dot(e.astype(BF16), mv_ref[:, hs].astype(BF16)) / den
        xg = proj_ref[:, C_XG + h * HD_X:C_XG + (h + 1) * HD_X].astype(F32)
        out_ref[:, D_A + D_B + h * HD_X:D_A + D_B + (h + 1) * HD_X] = (ox * _silu(xg)).astype(BF16)
        yield


def _prompt_layer_kernel(xn_ref, wi_ref, wo_ref, nw_ref, fw_ref, mk_ref, mv_ref, wa_ref, abt_ref,
                         avw_ref, bwa_ref, bba_ref, onw_ref, y_ref, st_ref,
                         pa_ref, pb_ref, xk_ref, br_ref, h_ref, sbd_ref, o_ref, la_ref, *, nt):
    s = pl.program_id(0)
    cur = jnp.maximum(s - 1, 0)
    t = lax.rem(cur, nt)

    @pl.when(t == 0)
    def _():
        sbd_ref[...] = jnp.zeros_like(sbd_ref)

    def in_proj_stages(pn_ref):
        x = xn_ref[...]
        ms = jnp.mean(x * x, axis=-1, keepdims=True)
        h_ref[...] = (x * lax.rsqrt(ms + EPS) * nw_ref[...]).astype(BF16)
        yield
        for c0 in range(0, D_PROJ_PAD, MXU_COLS):
            cols = pl.ds(c0, MXU_COLS)
            pn_ref[:, cols] = _dot(h_ref[...], wi_ref[:, cols]).astype(BF16)
            yield
        xk_ref[...] = xn_ref[...]

    def out_proj_stages():
        ssq = jnp.zeros((PT, 1), F32)
        for c0 in range(0, D_MODEL, OUT_BN):
            cols = pl.ds(c0, OUT_BN)
            acc = _dot(br_ref[...], wo_ref[:, cols]) + xk_ref[:, cols]
            y_ref[:, cols] = acc
            ssq = ssq + jnp.sum(acc * acc, axis=-1, keepdims=True)
            yield
        y_ref[...] = y_ref[...] * lax.rsqrt(ssq * (1.0 / D_MODEL) + EPS) * fw_ref[...]
        yield

    def run(order, streams):
        for name in order:
            next(streams[name])
        for name, gen in streams.items():
            assert next(gen, "done") == "done", name

    def mixer_stages(pc_ref):
        return _prompt_block(pc_ref, br_ref, mk_ref, mv_ref, wa_ref, abt_ref, avw_ref, bwa_ref,
                             bba_ref, onw_ref, sbd_ref, o_ref, la_ref)

    n_p = 1 + D_PROJ_PAD // MXU_COLS
    n_m = N_MIXER_STAGES
    n_o = 1 + D_MODEL // OUT_BN

    def body(pn_ref, pc_ref):
        order = ["P"]
        for i in range(max(n_p - 1, n_m)):
            order += ["M"] * (i < n_m) + ["P"] * (i < n_p - 1)
        order += ["O"] * n_o
        run(order, {"P": in_proj_stages(pn_ref), "M": mixer_stages(pc_ref), "O": out_proj_stages()})

    last = pl.num_programs(0) - 1

    @pl.when(s == 0)
    def _():
        run(["P"] * n_p, {"P": in_proj_stages(pa_ref)})

    @pl.when((lax.rem(s, 2) == 0) & (s > 0) & (s < last))
    def _():
        body(pa_ref, pb_ref)

    @pl.when(lax.rem(s, 2) == 1)
    def _():
        body(pb_ref, pa_ref)

    @pl.when(s == last)
    def _():
        run(["M"] * n_m + ["O"] * n_o, {"M": mixer_stages(pb_ref), "O": out_proj_stages()})

    @pl.when((t == nt - 1) & (s > 0))
    def _():
        for h in range(H_B):
            off = h * DV_B - WIN_START[h]
            blk = sbd_ref[h * DK_B:(h + 1) * DK_B, WIN_START[h]:WIN_START[h] + WIN]
            if off:
                blk = pltpu.roll(blk, WIN - off, 1)
            st_ref[0, h] = blk[:, :DV_B]


def _prompt_layer(xp, w_in_bf, w_out_bf, nw, fw, memkv, wa, abt, avw, bwa, bba, onw, *, batch, seq):
    nt = seq // PT
    nblk = batch * nt
    assert nblk % 2 == 0
    cur = lambda s: jnp.maximum(s - 1, 0)
    const = lambda *shape: pl.BlockSpec(shape, lambda s: (0,) * len(shape))
    resident = lambda *shape: pl.BlockSpec(shape, lambda s: (0,) * len(shape),
                                           pipeline_mode=pl.Buffered(1))
    return pl.pallas_call(
        functools.partial(_prompt_layer_kernel, nt=nt),
        grid=(nblk + 1,),
        in_specs=[pl.BlockSpec((PT, D_MODEL), lambda s: (jnp.minimum(s, nblk - 1), 0)),
                  resident(D_MODEL, D_PROJ_PAD), resident(D_MODEL, D_MODEL),
                  const(1, D_MODEL), const(1, D_MODEL),
                  pl.BlockSpec((None, N_MEM, D_X), lambda s: (0, cur(s) // nt, 0)),
                  pl.BlockSpec((None, N_MEM, D_X), lambda s: (1, cur(s) // nt, 0)),
                  const(H_A, CHUNK_A, CHUNK_A), const(CHUNK_A, H_A), const(1, D_A),
                  const(LANES, D_BK), const(1, D_BK), const(1, D_B)],
        out_specs=[pl.BlockSpec((PT, D_MODEL), lambda s: (cur(s), 0)),
                   pl.BlockSpec((1, H_B, DK_B, DV_B), lambda s: (cur(s) // nt, 0, 0, 0))],
        out_shape=[jax.ShapeDtypeStruct((batch * seq, D_MODEL), F32),
                   jax.ShapeDtypeStruct((batch, H_B, DK_B, DV_B), F32)],
        scratch_shapes=[pltpu.VMEM((PT, D_PROJ_PAD), BF16),
                        pltpu.VMEM((PT, D_PROJ_PAD), BF16),
                        pltpu.VMEM((PT, D_MODEL), F32),
                        pltpu.VMEM((PT, D_MODEL), BF16),
                        pltpu.VMEM((PT, D_MODEL), BF16),
                        pltpu.VMEM((D_BK, D_B), F32),
                        pltpu.VMEM((PT, D_B), F32),
                        pltpu.VMEM((PT, D_BK), F32)],
        compiler_params=pltpu.CompilerParams(
            dimension_semantics=("arbitrary",), vmem_limit_bytes=VMEM_LIMIT),
        name="prompt_layer",
    )(xp, w_in_bf, w_out_bf, nw.reshape(1, D_MODEL), fw.reshape(1, D_MODEL), memkv, memkv,
      wa, abt, avw, bwa, bba, onw)


NS = 8
TS = 8
SB = NS * TS
SBP = 128


def _sample_mixer_kernel(proj_ref, st_ref, ck_ref, cv_ref, wa_ref, abt_ref, avw_ref, bwa_ref,
                         bba_ref, onw_ref, out_ref, stn_ref, cvs_ref,
                         qin_ref, xq_ref, kot_ref, lat_ref, vhm_ref, ghm_ref, ohm_ref, ox_ref,
                         sel_ref, selt_ref):
    @pl.when(pl.program_id(0) == 0)
    def _():
        for h in range(H_B):
            sel_ref[h] = jnp.where(_iota((D_B, DV_B), 0) == _iota((D_B, DV_B), 1) + h * DV_B,
                                   1.0, 0.0).astype(BF16)
            selt_ref[h] = jnp.where(_iota((DV_B, D_B), 1) == _iota((DV_B, D_B), 0) + h * DV_B,
                                    1.0, 0.0).astype(BF16)

    ri = _iota((SB, SB), 0)
    ci = _iota((SB, SB), 1)
    same_seq = jnp.right_shift(ri, 3) == jnp.right_shift(ci, 3)
    causal = same_seq & (ci <= ri)

    a_br, vn = _group_a(proj_ref, wa_ref, abt_ref, avw_ref, causal)
    out_ref[:, 0:D_A] = a_br.astype(BF16)
    cvs_ref[...] = vn

    q = proj_ref[:, C_BQ:C_BQ + D_BK].astype(F32)
    k = proj_ref[:, C_BK:C_BK + D_BK].astype(F32)
    vb = proj_ref[:, C_BV:C_BV + D_B]
    bgb = proj_ref[:, C_BG:C_BG + D_B]
    pre = _dot(proj_ref[:, C_BR:C_BR + LANES], bwa_ref[...]) + bba_ref[...]
    log_a = _log_sigmoid(pre) * (1.0 / GATE_TAU)
    causal_bf = jnp.where(causal, 1.0, 0.0).astype(BF16)
    seq_bf = jnp.where(same_seq, 1.0, 0.0).astype(BF16)
    hi, lo = _split(log_a)
    cum = _dot(causal_bf, hi) + _dot(causal_bf, lo)
    tot = _dot(seq_bf, hi) + _dot(seq_bf, lo)
    q_in = q * (DK_B ** -0.5) * jnp.exp(cum)
    k_in = k * jnp.exp(-cum)
    k_out = k * jnp.exp(tot - cum)
    qin_ref[...] = q_in
    zpad = jnp.zeros((SBP - SB, D_BK), F32)
    kot_ref[...] = jnp.concatenate([k_out, zpad], axis=0).T.astype(BF16)
    lat_ref[...] = jnp.concatenate([log_a, zpad], axis=0).T
    xq_ref[...] = proj_ref[:, C_XQ:C_XQ + D_X].astype(F32)

    wins = _gla_intra_windows(q_in, k_in, vb, causal)
    vhm_ref[...] = jnp.zeros_like(vhm_ref)
    for h in range(H_B):
        off = h * DV_B - WIN_START[h]
        w = wins[h]
        if off:
            w = pltpu.roll(w, WIN - off, 1)
        ohm_ref[h] = w[:, :DV_B]
        vhm_ref[h, 0:SB, :] = _dot(vb, sel_ref[h]).astype(BF16)
        ghm_ref[h] = _dot(bgb, sel_ref[h])

    mask_x = (jnp.right_shift(_iota((H_X * TS, N_MEM * H_X), 0), 3)
              == jnp.bitwise_and(_iota((H_X * TS, N_MEM * H_X), 1), H_X - 1))
    mask_b = jnp.right_shift(_iota((H_B * TS, D_BK), 0), 3) == _head_id(_iota((H_B * TS, D_BK), 1), DK_B, H_B)
    lane_seq = jnp.right_shift(_iota((D_BK, SBP), 1), 3)

    def per_seq(s, carry):
        r0 = pl.multiple_of(s * TS, TS)
        q8 = xq_ref[pl.ds(r0, TS), :]
        q32 = jnp.concatenate([q8[:, h * HD_X:(h + 1) * HD_X] for h in range(H_X)], axis=0)
        sc = _dot_nt(q32.astype(BF16), ck_ref[s].astype(BF16)) * (HD_X ** -0.5)
        sc = jnp.where(mask_x, sc, -1e30)
        e = jnp.exp(sc - jnp.max(sc, axis=-1, keepdims=True))
        den = jnp.sum(e, axis=-1, keepdims=True)
        o = _dot(e.astype(BF16), cv_ref[s].astype(BF16)) / den
        ox_ref[pl.ds(r0, TS), :] = jnp.concatenate(
            [o[h * TS:(h + 1) * TS] for h in range(H_X)], axis=1)
        qi8 = qin_ref[pl.ds(r0, TS), :]
        qbd2 = jnp.where(mask_b, jnp.concatenate([qi8] * H_B, axis=0), 0.0).astype(BF16)
        s0 = jnp.concatenate([st_ref[s, h] for h in range(H_B)], axis=0)
        o_inter = _dot(qbd2, s0.astype(BF16))
        for h in range(H_B):
            ohm_ref[h, pl.ds(r0, TS), :] += o_inter[h * TS:(h + 1) * TS]
        dec = jnp.exp(jnp.sum(jnp.where(lane_seq == s, lat_ref[...], 0.0), axis=1, keepdims=True))
        kot = jnp.where(lane_seq == s, kot_ref[...], jnp.zeros((), BF16))
        kv = jnp.concatenate(
            [_dot(kot[h * DK_B:(h + 1) * DK_B], vhm_ref[h]) for h in range(H_B)], axis=0)
        s_new = s0 * dec + kv
        for h in range(H_B):
            stn_ref[s, h] = s_new[h * DK_B:(h + 1) * DK_B]
        return carry

    lax.fori_loop(0, NS, per_seq, 0, unroll=4)

    b_br = jnp.zeros((SB, D_B), F32)
    for h in range(H_B):
        o_h = ohm_ref[h]
        ms = jnp.mean(o_h * o_h, axis=-1, keepdims=True)
        ob = (o_h * lax.rsqrt(ms + EPS) * onw_ref[...] * _silu(ghm_ref[h])).astype(BF16)
        b_br = b_br + _dot(ob, selt_ref[h])
    out_ref[:, D_A:D_A + D_B] = b_br.astype(BF16)

    xg = proj_ref[:, C_XG:C_XG + D_X].astype(F32)
    out_ref[:, D_A + D_B:D_MODEL] = (ox_ref[...] * _silu(xg)).astype(BF16)


def _sample_mixer(proj, state, ck, cv, wa, abt, avw, bwa, bba, onw):
    nseq = state.shape[1]
    const = lambda *shape: pl.BlockSpec(shape, lambda i: (0,) * len(shape))
    return pl.pallas_call(
        _sample_mixer_kernel,
        grid=(nseq // NS,),
        in_specs=[pl.BlockSpec((SB, D_PROJ), lambda i: (i, 0)),
                  pl.BlockSpec((None, NS, H_B, DK_B, DV_B), lambda i: (0, i, 0, 0, 0)),
                  pl.BlockSpec((NS, N_MEM * H_X, HD_X), lambda i: (i, 0, 0)),
                  pl.BlockSpec((NS, N_MEM * H_X, HD_X), lambda i: (i, 0, 0)),
                  const(H_A, SB, SB), const(SB, H_A), const(1, D_A),
                  const(LANES, D_BK), const(1, D_BK), const(1, DV_B)],
        out_specs=[pl.BlockSpec((SB, D_MODEL), lambda i: (i, 0)),
                   pl.BlockSpec((None, NS, H_B, DK_B, DV_B), lambda i: (0, i, 0, 0, 0)),
                   pl.BlockSpec((SB, D_A), lambda i: (i, 0))],
        out_shape=[jax.ShapeDtypeStruct((nseq * TS, D_MODEL), BF16),
                   jax.ShapeDtypeStruct((1, nseq, H_B, DK_B, DV_B), F32),
                   jax.ShapeDtypeStruct((nseq * TS, D_A), F32)],
        scratch_shapes=[pltpu.VMEM((SB, D_BK), F32),
                        pltpu.VMEM((SB, D_X), F32),
                        pltpu.VMEM((D_BK, SBP), BF16),
                        pltpu.VMEM((D_BK, SBP), F32),
                        pltpu.VMEM((H_B, SBP, DV_B), BF16),
                        pltpu.VMEM((H_B, SB, DV_B), F32),
                        pltpu.VMEM((H_B, SB, DV_B), F32),
                        pltpu.VMEM((SB, D_X), F32),
                        pltpu.VMEM((H_B, D_B, DV_B), BF16),
                        pltpu.VMEM((H_B, DV_B, D_B), BF16)],
        compiler_params=pltpu.CompilerParams(
            dimension_semantics=("arbitrary",), vmem_limit_bytes=VMEM_LIMIT),
        name="sample_mixer",
    )(proj, state, ck, cv, wa, abt, avw, bwa, bba, onw)


def kernel(x_prompt, x_sample, mem_prompt, state_gla, cache_mem_k, cache_mem_v, norm_w, w_in,
           a_vnorm_w, a_ws, a_bs, b_wa, b_ba, b_onorm_w, mem_norm_w, w_mem_kv, w_out, final_norm_w):
    batch, seq, _ = x_prompt.shape
    nseq, tdec, _ = x_sample.shape
    depth = w_in.shape[0]
    assert depth == 1 and tdec == TS and seq % PT == 0 and nseq % NS == 0

    w_in_bf, w_out_bf = _weight_prep(jnp.transpose(w_in[0]), w_out[0])
    last_cols = D_PROJ_PAD - (D_PROJ_PAD // IN_BN) * IN_BN
    bwa = jnp.concatenate([b_wa[0], jnp.zeros((LANES - GATE_RANK, D_BK), F32)], axis=0).astype(BF16)
    bba = b_ba[0].reshape(1, D_BK)
    avw = a_vnorm_w[0].reshape(1, D_A)
    onw_p = jnp.tile(b_onorm_w[0], H_B).reshape(1, D_B)
    onw_s = b_onorm_w[0].reshape(1, DV_B)
    wa_p = a_ws[0]
    abt_p = a_bs[0].T
    wa_s = jnp.tile(a_ws[0][:, :TS, :TS], (1, NS, NS))
    abt_s = jnp.tile(a_bs[0][:, :TS], (1, NS)).T

    xp = x_prompt.reshape(batch * seq, D_MODEL)
    xs = x_sample.reshape(nseq * TS, D_MODEL)
    mem = mem_prompt.reshape(batch * N_MEM, D_MODEL)

    proj_s = _norm_matmul(xs, norm_w[0], w_in_bf, D_PROJ, bm=IN_BM_S, bn=IN_BN, transposed=False,
                          out_dtype=BF16, last_cols=last_cols, name="in_proj_s")
    br_s, st_s, cvs = _sample_mixer(
        proj_s, state_gla,
        cache_mem_k.reshape(nseq, N_MEM * H_X, HD_X), cache_mem_v.reshape(nseq, N_MEM * H_X, HD_X),
        wa_s, abt_s, avw, bwa, bba, onw_s)
    y_s = _out_proj(br_s, w_out_bf, xs, final_norm_w, bm=256, name="out_proj_s")

    memkv, mem_k, mem_v = _mem_kv(mem, mem_norm_w[0], w_mem_kv[0], batch=batch)
    y_p, st_p = _prompt_layer(xp, w_in_bf, w_out_bf, norm_w[0], final_norm_w, memkv, wa_p, abt_p,
                              avw, bwa, bba, onw_p, batch=batch, seq=seq)

    return (y_p.reshape(batch, seq, D_MODEL),
            y_s.reshape(nseq, TS, D_MODEL),
            mem_k.reshape(1, batch, N_MEM, H_X, HD_X),
            mem_v.reshape(1, batch, N_MEM, H_X, HD_X),
            st_p.reshape(1, batch, H_B, DK_B, DV_B),
            st_s,
            cvs.reshape(1, nseq, TS, D_A))
```

```python
import functools

import jax
import jax.numpy as jnp
from jax import lax
from jax.experimental import pallas as pl
from jax.experimental.pallas import tpu as pltpu

F32 = jnp.float32
BF16 = jnp.bfloat16

D_MODEL = 2048
D_A = 768
H_A = 4
HD_A = 192
CHUNK_A = 128
D_B = 768
H_B = 4
DV_B = 192
DK_B = 96
D_BK = 384
GATE_RANK = 16
GATE_TAU = 16.0
CHUNK_B = 64
D_X = 512
H_X = 4
HD_X = 128
N_MEM = 256
EPS = 1e-6

LANES = 128
MXU_COLS = 256
C_AU, C_AV, C_AG = 0, 768, 1536
C_BQ, C_BK, C_BV, C_BG = 2304, 2688, 3072, 3840
C_XQ, C_XG, C_BR = 4608, 5120, 5632
D_PROJ = 5760
D_PROJ_PAD = 5888
WIN_START = (0, 128, 384, 512)
WIN = 256
IN_BN = 768
IN_BM_S = 1024
OUT_BN = 512

VMEM_LIMIT = 60 * 1024 * 1024


def _dot(a, b):
    return jnp.dot(a, b, preferred_element_type=F32)


def _dot_nt(a, b):
    return lax.dot_general(a, b, (((1,), (1,)), ((), ())), preferred_element_type=F32)


def _dot_tn(a, b):
    return lax.dot_general(a, b, (((0,), (0,)), ((), ())), preferred_element_type=F32)


def _split(x):
    hi = x.astype(BF16)
    lo = (x - hi.astype(F32)).astype(BF16)
    return hi, lo


def _silu(x):
    return x / (1.0 + jnp.exp(-x))


def _log_sigmoid(x):
    return jnp.minimum(x, 0.0) - jnp.log1p(jnp.exp(-jnp.abs(x)))


def _head_id(idx, width, n):
    h = jnp.zeros_like(idx)
    for i in range(1, n):
        h = h + (idx >= i * width).astype(jnp.int32)
    return h


def _iota(shape, dim):
    return lax.broadcasted_iota(jnp.int32, shape, dim)


def _merge_windows(r):
    rows = r[0].shape[0]
    lo = _iota((rows, LANES), 1) < 64
    tiles = [r[0][:, :LANES], jnp.where(lo, r[0][:, LANES:], r[1][:, :LANES]), r[1][:, LANES:],
             r[2][:, :LANES], jnp.where(lo, r[2][:, LANES:], r[3][:, :LANES]), r[3][:, LANES:]]
    return jnp.concatenate(tiles, axis=1)


def _group_a(proj_ref, wa_ref, abt_ref, avw_ref, mask):
    u = proj_ref[:, C_AU:C_AU + D_A].astype(F32)
    v = proj_ref[:, C_AV:C_AV + D_A].astype(F32)
    g = proj_ref[:, C_AG:C_AG + D_A].astype(F32)
    rows = u.shape[0]
    ms = jnp.mean(v * v, axis=-1, keepdims=True)
    vn = v * lax.rsqrt(ms + EPS) * avw_ref[...]
    vb = vn.astype(BF16)
    r = []
    for gi in range(H_A):
        w = jnp.where(mask, wa_ref[gi], 0.0).astype(BF16)
        r.append(_dot(w, vb[:, WIN_START[gi]:WIN_START[gi] + WIN]))
    mixed = _merge_windows(r)
    col_g = _head_id(_iota((rows, D_A), 1), HD_A, H_A)
    bias = jnp.zeros((rows, D_A), F32)
    for gi in range(H_A):
        bias = jnp.where(col_g == gi, abt_ref[:, gi:gi + 1], bias)
    return u * (mixed + bias) * _silu(g), vn


def _gla_intra_windows(q_in, k_in, vb, mask):
    rows = q_in.shape[0]
    col_h = _head_id(_iota((rows, D_BK), 1), DK_B, H_B)
    kb = k_in.astype(BF16)
    r = []
    for h in range(H_B):
        qh = jnp.where(col_h == h, q_in, 0.0).astype(BF16)
        s = jnp.where(mask, _dot_nt(qh, kb), 0.0).astype(BF16)
        r.append(_dot(s, vb[:, WIN_START[h]:WIN_START[h] + WIN]))
    return r


def _norm_matmul_kernel(x_ref, nw_ref, w_ref, o_ref, h_ref, *, transposed, last_cols):
    j = pl.program_id(1)
    nj = pl.num_programs(1)

    @pl.when(j == 0)
    def _():
        x = x_ref[...]
        ms = jnp.mean(x * x, axis=-1, keepdims=True)
        h_ref[...] = (x * lax.rsqrt(ms + EPS) * nw_ref[...]).astype(BF16)

    mm = _dot_nt if transposed else _dot
    if last_cols is None:
        o_ref[...] = mm(h_ref[...], w_ref[...].astype(BF16)).astype(o_ref.dtype)
    else:
        @pl.when(j < nj - 1)
        def _():
            o_ref[...] = mm(h_ref[...], w_ref[...].astype(BF16)).astype(o_ref.dtype)

        @pl.when(j == nj - 1)
        def _():
            w = w_ref[:last_cols, :] if transposed else w_ref[:, :last_cols]
            o_ref[:, :last_cols] = mm(h_ref[...], w.astype(BF16)).astype(o_ref.dtype)


def _norm_matmul(x, nw, w, n_out, *, bm, bn, transposed, out_dtype, split_out=False,
                 last_cols=None, name):
    m, k = x.shape
    n = w.shape[0] if transposed else w.shape[1]
    nj = -(-n // bn)
    assert m % bm == 0
    if split_out:
        out_shape = jax.ShapeDtypeStruct((nj, m, bn), out_dtype)
        out_spec = pl.BlockSpec((None, bm, bn), lambda i, j: (j, i, 0))
    else:
        out_shape = jax.ShapeDtypeStruct((m, n_out), out_dtype)
        out_spec = pl.BlockSpec((bm, bn), lambda i, j: (i, j))
    if transposed:
        w_spec = pl.BlockSpec((bn, k), lambda i, j: (j, 0))
    else:
        w_spec = pl.BlockSpec((k, bn), lambda i, j: (0, j))
    return pl.pallas_call(
        functools.partial(_norm_matmul_kernel, transposed=transposed, last_cols=last_cols),
        grid=(m // bm, nj),
        in_specs=[pl.BlockSpec((bm, k), lambda i, j: (i, 0)),
                  pl.BlockSpec((1, k), lambda i, j: (0, 0)),
                  w_spec],
        out_specs=out_spec,
        out_shape=out_shape,
        scratch_shapes=[pltpu.VMEM((bm, k), BF16)],
        compiler_params=pltpu.CompilerParams(
            dimension_semantics=("arbitrary", "arbitrary"), vmem_limit_bytes=VMEM_LIMIT),
        name=name,
    )(x, nw.reshape(1, k), w)


def _mem_kv_kernel(x_ref, nw_ref, w_ref, o_ref, ok_ref, ov_ref):
    x = x_ref[...]
    ms = jnp.mean(x * x, axis=-1, keepdims=True)
    h = (x * lax.rsqrt(ms + EPS) * nw_ref[...]).astype(BF16)
    kv = _dot(h, w_ref[...].astype(BF16))
    o_ref[0] = kv[:, :D_X]
    o_ref[1] = kv[:, D_X:]
    for hd in range(H_X):
        ok_ref[pl.ds(hd, N_MEM, stride=H_X), :] = kv[:, hd * HD_X:(hd + 1) * HD_X]
        ov_ref[pl.ds(hd, N_MEM, stride=H_X), :] = kv[:, D_X + hd * HD_X:D_X + (hd + 1) * HD_X]


def _mem_kv(mem, nw, w, *, batch):
    return pl.pallas_call(
        _mem_kv_kernel,
        grid=(batch,),
        in_specs=[pl.BlockSpec((N_MEM, D_MODEL), lambda i: (i, 0)),
                  pl.BlockSpec((1, D_MODEL), lambda i: (0, 0)),
                  pl.BlockSpec((D_MODEL, 2 * D_X), lambda i: (0, 0))],
        out_specs=[pl.BlockSpec((2, N_MEM, D_X), lambda i: (0, i, 0)),
                   pl.BlockSpec((None, N_MEM * H_X, HD_X), lambda i: (i, 0, 0)),
                   pl.BlockSpec((None, N_MEM * H_X, HD_X), lambda i: (i, 0, 0))],
        out_shape=[jax.ShapeDtypeStruct((2, batch * N_MEM, D_X), F32),
                   jax.ShapeDtypeStruct((batch, N_MEM * H_X, HD_X), F32),
                   jax.ShapeDtypeStruct((batch, N_MEM * H_X, HD_X), F32)],
        compiler_params=pltpu.CompilerParams(
            dimension_semantics=("arbitrary",), vmem_limit_bytes=VMEM_LIMIT),
        name="mem_kv",
    )(mem, nw.reshape(1, D_MODEL), w)


PREP_BN = MXU_COLS
N_MAIN_BLOCKS = C_BG // PREP_BN


GATE_BLOCK = D_PROJ_PAD // PREP_BN - 1
N_IN_BLOCKS = GATE_BLOCK + 1
N_OUT_BLOCKS = D_MODEL // PREP_BN
PREP_BUFS = 4


def _weight_prep_kernel(wt_hbm, wo_hbm, xs_ref, nw_ref, oi_ref, oo_ref, ps_ref, buf, sem, hs_ref):
    r = pl.program_id(0)
    slot = lax.rem(r, PREP_BUFS)

    def in_copy(rr, sl):
        start = pl.multiple_of(jnp.where(rr < N_MAIN_BLOCKS, rr * PREP_BN, rr * PREP_BN + GATE_RANK), 8)
        return pltpu.make_async_copy(wt_hbm.at[pl.ds(start, PREP_BN), :], buf.at[sl], sem.at[sl])

    def gate_copy(sl):
        return pltpu.make_async_copy(wt_hbm.at[pl.ds(C_BG, GATE_RANK), :],
                                     buf.at[sl, pl.ds(0, GATE_RANK), :], sem.at[sl])

    def out_copy(rr, sl):
        start = pl.multiple_of((rr - N_IN_BLOCKS) * PREP_BN, PREP_BN)
        return pltpu.make_async_copy(wo_hbm.at[pl.ds(start, PREP_BN), :], buf.at[sl], sem.at[sl])

    def start_fetch(rr, sl):
        @pl.when(rr < GATE_BLOCK)
        def _():
            in_copy(rr, sl).start()

        @pl.when(rr == GATE_BLOCK)
        def _():
            gate_copy(sl).start()

        @pl.when(rr > GATE_BLOCK)
        def _():
            out_copy(rr, sl).start()

    @pl.when(r == 0)
    def _():
        for ahead in range(PREP_BUFS - 1):
            start_fetch(r + ahead, ahead)

    nxt = r + PREP_BUFS - 1

    @pl.when(nxt < pl.num_programs(0))
    def _():
        start_fetch(nxt, lax.rem(nxt, PREP_BUFS))

    @pl.when(r == 0)
    def _():
        x = xs_ref[...]
        ms = jnp.mean(x * x, axis=-1, keepdims=True)
        hs_ref[...] = (x * lax.rsqrt(ms + EPS) * nw_ref[...]).astype(BF16)

    @pl.when(r < GATE_BLOCK)
    def _():
        in_copy(r, slot).wait()
        oi_ref[...] = buf[slot].T.astype(BF16)
        ps_ref[...] = _dot(hs_ref[...], oi_ref[...]).astype(BF16)

    @pl.when(r == GATE_BLOCK)
    def _():
        gate_copy(slot).wait()
        rows = jnp.concatenate([buf[slot, 0:GATE_RANK, :],
                                jnp.zeros((PREP_BN - GATE_RANK, D_MODEL), F32)], axis=0)
        oi_ref[...] = rows.T.astype(BF16)
        ps_ref[...] = _dot(hs_ref[...], oi_ref[...]).astype(BF16)

    @pl.when(r > GATE_BLOCK)
    def _():
        out_copy(r, slot).wait()
        oo_ref[...] = buf[slot].astype(BF16)


def _weight_prep(w_t, w_out, xs, nw):
    k = w_t.shape[1]
    m = xs.shape[0]
    in_block = lambda r: (0, jnp.minimum(r, GATE_BLOCK))
    return pl.pallas_call(
        _weight_prep_kernel,
        grid=(N_IN_BLOCKS + N_OUT_BLOCKS,),
        in_specs=[pl.BlockSpec(memory_space=pl.ANY), pl.BlockSpec(memory_space=pl.ANY),
                  pl.BlockSpec((m, k), lambda r: (0, 0), pipeline_mode=pl.Buffered(1)),
                  pl.BlockSpec((1, k), lambda r: (0, 0))],
        out_specs=[pl.BlockSpec((k, PREP_BN), in_block),
                   pl.BlockSpec((PREP_BN, D_MODEL), lambda r: (jnp.maximum(r - N_IN_BLOCKS, 0), 0)),
                   pl.BlockSpec((m, PREP_BN), in_block)],
        out_shape=[jax.ShapeDtypeStruct((k, D_PROJ_PAD), BF16),
                   jax.ShapeDtypeStruct((D_MODEL, D_MODEL), BF16),
                   jax.ShapeDtypeStruct((m, D_PROJ_PAD), BF16)],
        scratch_shapes=[pltpu.VMEM((PREP_BUFS, PREP_BN, k), F32),
                        pltpu.SemaphoreType.DMA((PREP_BUFS,)),
                        pltpu.VMEM((m, k), BF16)],
        compiler_params=pltpu.CompilerParams(
            dimension_semantics=("arbitrary",), vmem_limit_bytes=VMEM_LIMIT),
        name="weight_prep",
    )(w_t, w_out, xs, nw.reshape(1, k))


def _out_proj_kernel(br_ref, w_ref, x_ref, fw_ref, y_ref):
    acc = _dot(br_ref[...], w_ref[...]) + x_ref[...]
    ms = jnp.mean(acc * acc, axis=-1, keepdims=True)
    y_ref[...] = acc * lax.rsqrt(ms + EPS) * fw_ref[...]


def _out_proj(br, w, x, fw, *, bm, name):
    m, k = br.shape
    n = w.shape[1]
    return pl.pallas_call(
        _out_proj_kernel,
        grid=(m // bm,),
        in_specs=[pl.BlockSpec((bm, k), lambda i: (i, 0)),
                  pl.BlockSpec((k, n), lambda i: (0, 0)),
                  pl.BlockSpec((bm, n), lambda i: (i, 0)),
                  pl.BlockSpec((1, n), lambda i: (0, 0))],
        out_specs=pl.BlockSpec((bm, n), lambda i: (i, 0)),
        out_shape=jax.ShapeDtypeStruct((m, n), F32),
        compiler_params=pltpu.CompilerParams(
            dimension_semantics=("arbitrary",), vmem_limit_bytes=VMEM_LIMIT),
        name=name,
    )(br, w, x, fw.reshape(1, n))


PT = 256
GLA_ROWS = 128
N_MIXER_STAGES = 2 + 4 * (PT // GLA_ROWS) + 1 + 2 * H_X


def _group_a_chunks(proj_ref, wa_ref, abt_ref, avw_ref):
    v = proj_ref[:, C_AV:C_AV + D_A].astype(F32)
    n_chunks = v.shape[0] // CHUNK_A
    ms = jnp.mean(v * v, axis=-1, keepdims=True)
    vb = (v * lax.rsqrt(ms + EPS) * avw_ref[...]).astype(BF16)
    tril = _iota((CHUNK_A, CHUNK_A), 1) <= _iota((CHUNK_A, CHUNK_A), 0)
    r = []
    for gi in range(H_A):
        w = jnp.where(tril, wa_ref[gi], 0.0).astype(BF16)
        ws = slice(WIN_START[gi], WIN_START[gi] + WIN)
        rhs = jnp.concatenate([vb[c * CHUNK_A:(c + 1) * CHUNK_A, ws] for c in range(n_chunks)], axis=1)
        res = _dot(w, rhs)
        r.append(jnp.concatenate([res[:, c * WIN:(c + 1) * WIN] for c in range(n_chunks)], axis=0))
    mixed = _merge_windows(r)
    col_g = _head_id(_iota((CHUNK_A, D_A), 1), HD_A, H_A)
    bias = jnp.zeros((CHUNK_A, D_A), F32)
    for gi in range(H_A):
        bias = jnp.where(col_g == gi, abt_ref[:, gi:gi + 1], bias)
    bias = jnp.concatenate([bias] * n_chunks, axis=0)
    u = proj_ref[:, C_AU:C_AU + D_A].astype(F32)
    g = proj_ref[:, C_AG:C_AG + D_A].astype(F32)
    return u * (mixed + bias) * _silu(g)


def _prompt_block(proj_ref, out_ref, mk_ref, mv_ref, wa_ref, abt_ref, avw_ref, bwa_ref, bba_ref,
                  onw_ref, sbd_ref, o_ref, la_ref):
    out_ref[:, 0:D_A] = _group_a_chunks(proj_ref, wa_ref, abt_ref, avw_ref).astype(BF16)
    yield

    pre = _dot(proj_ref[:, C_BR:C_BR + LANES], bwa_ref[...]) + bba_ref[...]
    la_ref[...] = _log_sigmoid(pre) * (1.0 / GATE_TAU)
    yield
    gl = GLA_ROWS
    tril_bf = jnp.where(_iota((gl, gl), 1) <= _iota((gl, gl), 0), 1.0, 0.0).astype(BF16)
    win_col = _iota((gl, WIN), 1)
    col_hq = _head_id(_iota((gl, D_BK), 1), DK_B, H_B)
    tril_heads = (_iota((H_B * gl, gl), 1) <= jnp.bitwise_and(_iota((H_B * gl, gl), 0), gl - 1))
    for c in range(PT // gl):
        sl = slice(c * gl, (c + 1) * gl)
        log_a = la_ref[sl, :]
        q = proj_ref[sl, C_BQ:C_BQ + D_BK].astype(F32)
        k = proj_ref[sl, C_BK:C_BK + D_BK].astype(F32)
        vb = proj_ref[sl, C_BV:C_BV + D_B]
        hi, lo = _split(log_a)
        cum2 = _dot(tril_bf, jnp.concatenate([hi, lo], axis=1))
        yield
        cum = cum2[:, :D_BK] + cum2[:, D_BK:]
        tot_row = cum[gl - 1:gl, :]
        mid_row = cum[CHUNK_B - 1:CHUNK_B, :]
        la_t = log_a.T
        q_s = q * (DK_B ** -0.5)
        q_dec = q_s * jnp.exp(cum)
        q_in = q_s * jnp.exp(cum - mid_row)
        k_in = k * jnp.exp(mid_row - cum)
        k_out = k * jnp.exp(tot_row - cum)
        q_heads = jnp.concatenate([jnp.where(col_hq == h, q_in, 0.0) for h in range(H_B)], axis=0)
        sc = _dot_nt(q_heads.astype(BF16), k_in.astype(BF16))
        yield
        sc = jnp.where(tril_heads, sc, 0.0).astype(BF16)
        o_intra = _merge_windows(
            [_dot(sc[h * gl:(h + 1) * gl], vb[:, WIN_START[h]:WIN_START[h] + WIN])
             for h in range(H_B)])
        s_old = sbd_ref[...]
        o_inter = _dot(q_dec.astype(BF16), s_old.astype(BF16))
        yield
        kot = k_out.T.astype(BF16)
        for h in range(H_B):
            rs = slice(h * DK_B, (h + 1) * DK_B)
            ws = slice(WIN_START[h], WIN_START[h] + WIN)
            dec = jnp.exp(jnp.sum(la_t[rs], axis=1, keepdims=True))
            lo_col = h * DV_B - WIN_START[h]
            in_head = (win_col >= lo_col) & (win_col < lo_col + DV_B)
            v_h = jnp.where(in_head, vb[:, ws], jnp.zeros((), BF16))
            kv = _dot(kot[rs], v_h)
            sbd_ref[rs, ws] = s_old[rs, ws] * dec + kv
        o_ref[sl, :] = o_intra + o_inter
        yield
    o = o_ref[...]
    o2 = o * o
    t = [o2[:, j * LANES:(j + 1) * LANES] for j in range(D_B // LANES)]
    lo = _iota((PT, LANES), 1) < 64
    t1a = jnp.where(lo, t[1], 0.0)
    t4a = jnp.where(lo, t[4], 0.0)
    sums = (t[0] + t1a, (t[1] - t1a) + t[2], t[3] + t4a, (t[4] - t4a) + t[5])
    inv = [lax.rsqrt(jnp.sum(x, axis=-1, keepdims=True) * (1.0 / DV_B) + EPS) for x in sums]
    inv = jnp.concatenate(
        [jnp.broadcast_to(inv[0], (PT, LANES)), jnp.where(lo, inv[0], inv[1]),
         jnp.broadcast_to(inv[1], (PT, LANES)), jnp.broadcast_to(inv[2], (PT, LANES)),
         jnp.where(lo, inv[2], inv[3]), jnp.broadcast_to(inv[3], (PT, LANES))], axis=1)
    o_n = o * inv * onw_ref[...]
    bg = proj_ref[:, C_BG:C_BG + D_B].astype(F32)
    out_ref[:, D_A:D_A + D_B] = (o_n * _silu(bg)).astype(BF16)
    yield

    for h in range(H_X):
        hs = slice(h * HD_X, (h + 1) * HD_X)
        qh = proj_ref[:, C_XQ + h * HD_X:C_XQ + (h + 1) * HD_X]
        s = _dot_nt(qh, mk_ref[:, hs].astype(BF16)) * (HD_X ** -0.5)
        yield
        e = jnp.exp(s - jnp.max(s, axis=-1, keepdims=True))
        den = jnp.sum(e, axis=-1, keepdims=True)
        ox = _dot(e.astype(BF16), mv_ref[:, hs].astype(BF16)) / den
        xg = proj_ref[:, C_XG + h * HD_X:C_XG + (h + 1) * HD_X].astype(F32)
        out_ref[:, D_A + D_B + h * HD_X:D_A + D_B + (h + 1) * HD_X] = (ox * _silu(xg)).astype(BF16)
        yield


def _prompt_layer_kernel(xn_ref, wi_ref, wo_ref, nw_ref, fw_ref, mk_ref, mv_ref, wa_ref, abt_ref,
                         avw_ref, bwa_ref, bba_ref, onw_ref, y_ref, st_ref,
                         pa_ref, pb_ref, xk_ref, br_ref, h_ref, sbd_ref, o_ref, la_ref, *, nt):
    s = pl.program_id(0)
    cur = jnp.maximum(s - 1, 0)
    t = lax.rem(cur, nt)

    @pl.when(t == 0)
    def _():
        sbd_ref[...] = jnp.zeros_like(sbd_ref)

    def in_proj_stages(pn_ref):
        x = xn_ref[...]
        ms = jnp.mean(x * x, axis=-1, keepdims=True)
        h_ref[...] = (x * lax.rsqrt(ms + EPS) * nw_ref[...]).astype(BF16)
        yield
        for c0 in range(0, D_PROJ_PAD, MXU_COLS):
            cols = pl.ds(c0, MXU_COLS)
            pn_ref[:, cols] = _dot(h_ref[...], wi_ref[:, cols]).astype(BF16)
            yield
        xk_ref[...] = xn_ref[...]

    def out_proj_stages():
        ssq = jnp.zeros((PT, 1), F32)
        for c0 in range(0, D_MODEL, OUT_BN):
            cols = pl.ds(c0, OUT_BN)
            acc = _dot(br_ref[...], wo_ref[:, cols]) + xk_ref[:, cols]
            y_ref[:, cols] = acc
            ssq = ssq + jnp.sum(acc * acc, axis=-1, keepdims=True)
            yield
        y_ref[...] = y_ref[...] * lax.rsqrt(ssq * (1.0 / D_MODEL) + EPS) * fw_ref[...]
        yield

    def run(order, streams):
        for name in order:
            next(streams[name])
        for name, gen in streams.items():
            assert next(gen, "done") == "done", name

    def mixer_stages(pc_ref):
        return _prompt_block(pc_ref, br_ref, mk_ref, mv_ref, wa_ref, abt_ref, avw_ref, bwa_ref,
                             bba_ref, onw_ref, sbd_ref, o_ref, la_ref)

    n_p = 1 + D_PROJ_PAD // MXU_COLS
    n_m = N_MIXER_STAGES
    n_o = 1 + D_MODEL // OUT_BN

    def body(pn_ref, pc_ref):
        order = ["P"]
        for i in range(max(n_p - 1, n_m)):
            order += ["M"] * (i < n_m) + ["P"] * (i < n_p - 1)
        order += ["O"] * n_o
        run(order, {"P": in_proj_stages(pn_ref), "M": mixer_stages(pc_ref), "O": out_proj_stages()})

    last = pl.num_programs(0) - 1

    @pl.when(s == 0)
    def _():
        run(["P"] * n_p, {"P": in_proj_stages(pa_ref)})

    @pl.when((lax.rem(s, 2) == 0) & (s > 0) & (s < last))
    def _():
        body(pa_ref, pb_ref)

    @pl.when(lax.rem(s, 2) == 1)
    def _():
        body(pb_ref, pa_ref)

    @pl.when(s == last)
    def _():
        run(["M"] * n_m + ["O"] * n_o, {"M": mixer_stages(pb_ref), "O": out_proj_stages()})

    @pl.when((t == nt - 1) & (s > 0))
    def _():
        for h in range(H_B):
            off = h * DV_B - WIN_START[h]
            blk = sbd_ref[h * DK_B:(h + 1) * DK_B, WIN_START[h]:WIN_START[h] + WIN]
            if off:
                blk = pltpu.roll(blk, WIN - off, 1)
            st_ref[0, h] = blk[:, :DV_B]


def _prompt_layer(xp, w_in_bf, w_out_bf, nw, fw, memkv, wa, abt, avw, bwa, bba, onw, *, batch, seq):
    nt = seq // PT
    nblk = batch * nt
    assert nblk % 2 == 0
    cur = lambda s: jnp.maximum(s - 1, 0)
    const = lambda *shape: pl.BlockSpec(shape, lambda s: (0,) * len(shape))
    resident = lambda *shape: pl.BlockSpec(shape, lambda s: (0,) * len(shape),
                                           pipeline_mode=pl.Buffered(1))
    return pl.pallas_call(
        functools.partial(_prompt_layer_kernel, nt=nt),
        grid=(nblk + 1,),
        in_specs=[pl.BlockSpec((PT, D_MODEL), lambda s: (jnp.minimum(s, nblk - 1), 0)),
                  resident(D_MODEL, D_PROJ_PAD), resident(D_MODEL, D_MODEL),
                  const(1, D_MODEL), const(1, D_MODEL),
                  pl.BlockSpec((None, N_MEM, D_X), lambda s: (0, cur(s) // nt, 0)),
                  pl.BlockSpec((None, N_MEM, D_X), lambda s: (1, cur(s) // nt, 0)),
                  const(H_A, CHUNK_A, CHUNK_A), const(CHUNK_A, H_A), const(1, D_A),
                  const(LANES, D_BK), const(1, D_BK), const(1, D_B)],
        out_specs=[pl.BlockSpec((PT, D_MODEL), lambda s: (cur(s), 0)),
                   pl.BlockSpec((1, H_B, DK_B, DV_B), lambda s: (cur(s) // nt, 0, 0, 0))],
        out_shape=[jax.ShapeDtypeStruct((batch * seq, D_MODEL), F32),
                   jax.ShapeDtypeStruct((batch, H_B, DK_B, DV_B), F32)],
        scratch_shapes=[pltpu.VMEM((PT, D_PROJ_PAD), BF16),
                        pltpu.VMEM((PT, D_PROJ_PAD), BF16),
                        pltpu.VMEM((PT, D_MODEL), F32),
                        pltpu.VMEM((PT, D_MODEL), BF16),
                        pltpu.VMEM((PT, D_MODEL), BF16),
                        pltpu.VMEM((D_BK, D_B), F32),
                        pltpu.VMEM((PT, D_B), F32),
                        pltpu.VMEM((PT, D_BK), F32)],
        compiler_params=pltpu.CompilerParams(
            dimension_semantics=("arbitrary",), vmem_limit_bytes=VMEM_LIMIT),
        name="prompt_layer",
    )(xp, w_in_bf, w_out_bf, nw.reshape(1, D_MODEL), fw.reshape(1, D_MODEL), memkv, memkv,
      wa, abt, avw, bwa, bba, onw)


NS = 8
TS = 8
SB = NS * TS
SBP = 128


def _sample_mixer_kernel(proj_ref, st_ref, ck_ref, cv_ref, wa_ref, abt_ref, avw_ref, bwa_ref,
                         bba_ref, onw_ref, out_ref, stn_ref, cvs_ref,
                         qin_ref, xq_ref, kot_ref, lat_ref, vhm_ref, ghm_ref, ohm_ref, ox_ref):
    ri = _iota((SB, SB), 0)
    ci = _iota((SB, SB), 1)
    same_seq = jnp.right_shift(ri, 3) == jnp.right_shift(ci, 3)
    causal = same_seq & (ci <= ri)

    a_br, vn = _group_a(proj_ref, wa_ref, abt_ref, avw_ref, causal)
    out_ref[:, 0:D_A] = a_br.astype(BF16)
    cvs_ref[...] = vn

    q = proj_ref[:, C_BQ:C_BQ + D_BK].astype(F32)
    k = proj_ref[:, C_BK:C_BK + D_BK].astype(F32)
    vb = proj_ref[:, C_BV:C_BV + D_B]
    bgb = proj_ref[:, C_BG:C_BG + D_B]
    pre = _dot(proj_ref[:, C_BR:C_BR + LANES], bwa_ref[...]) + bba_ref[...]
    log_a = _log_sigmoid(pre) * (1.0 / GATE_TAU)
    causal_bf = jnp.where(causal, 1.0, 0.0).astype(BF16)
    seq_bf = jnp.where(same_seq, 1.0, 0.0).astype(BF16)
    hi, lo = _split(log_a)
    cum = _dot(causal_bf, hi) + _dot(causal_bf, lo)
    tot = _dot(seq_bf, hi) + _dot(seq_bf, lo)
    q_in = q * (DK_B ** -0.5) * jnp.exp(cum)
    k_in = k * jnp.exp(-cum)
    k_out = k * jnp.exp(tot - cum)
    qin_ref[...] = q_in
    zpad = jnp.zeros((SBP - SB, D_BK), F32)
    kot_ref[...] = jnp.concatenate([k_out, zpad], axis=0).T.astype(BF16)
    lat_ref[...] = jnp.concatenate([log_a, zpad], axis=0).T
    xq_ref[...] = proj_ref[:, C_XQ:C_XQ + D_X].astype(F32)

    wins = _gla_intra_windows(q_in, k_in, vb, causal)
    sel_r = _iota((D_B, DV_B), 0)
    sel_c = _iota((D_B, DV_B), 1)
    vhm_ref[...] = jnp.zeros_like(vhm_ref)
    for h in range(H_B):
        off = h * DV_B - WIN_START[h]
        w = wins[h]
        if off:
            w = pltpu.roll(w, WIN - off, 1)
        ohm_ref[h] = w[:, :DV_B]
        sel = jnp.where(sel_r == sel_c + h * DV_B, 1.0, 0.0).astype(BF16)
        vhm_ref[h, 0:SB, :] = _dot(vb, sel).astype(BF16)
        ghm_ref[h] = _dot(bgb, sel)

    mask_x = (jnp.right_shift(_iota((H_X * TS, N_MEM * H_X), 0), 3)
              == jnp.bitwise_and(_iota((H_X * TS, N_MEM * H_X), 1), H_X - 1))
    mask_b = jnp.right_shift(_iota((H_B * TS, D_BK), 0), 3) == _head_id(_iota((H_B * TS, D_BK), 1), DK_B, H_B)
    lane_seq = jnp.right_shift(_iota((D_BK, SBP), 1), 3)

    def per_seq(s, carry):
        r0 = pl.multiple_of(s * TS, TS)
        q8 = xq_ref[pl.ds(r0, TS), :]
        q32 = jnp.concatenate([q8[:, h * HD_X:(h + 1) * HD_X] for h in range(H_X)], axis=0)
        sc = _dot_nt(q32.astype(BF16), ck_ref[s].astype(BF16)) * (HD_X ** -0.5)
        sc = jnp.where(mask_x, sc, -1e30)
        e = jnp.exp(sc - jnp.max(sc, axis=-1, keepdims=True))
        den = jnp.sum(e, axis=-1, keepdims=True)
        o = _dot(e.astype(BF16), cv_ref[s].astype(BF16)) / den
        ox_ref[pl.ds(r0, TS), :] = jnp.concatenate(
            [o[h * TS:(h + 1) * TS] for h in range(H_X)], axis=1)
        qi8 = qin_ref[pl.ds(r0, TS), :]
        qbd2 = jnp.where(mask_b, jnp.concatenate([qi8] * H_B, axis=0), 0.0).astype(BF16)
        s0 = jnp.concatenate([st_ref[s, h] for h in range(H_B)], axis=0)
        o_inter = _dot(qbd2, s0.astype(BF16))
        for h in range(H_B):
            ohm_ref[h, pl.ds(r0, TS), :] += o_inter[h * TS:(h + 1) * TS]
        dec = jnp.exp(jnp.sum(jnp.where(lane_seq == s, lat_ref[...], 0.0), axis=1, keepdims=True))
        kot = jnp.where(lane_seq == s, kot_ref[...], jnp.zeros((), BF16))
        kv = jnp.concatenate(
            [_dot(kot[h * DK_B:(h + 1) * DK_B], vhm_ref[h]) for h in range(H_B)], axis=0)
        s_new = s0 * dec + kv
        for h in range(H_B):
            stn_ref[s, h] = s_new[h * DK_B:(h + 1) * DK_B]
        return carry

    lax.fori_loop(0, NS, per_seq, 0, unroll=4)

    selt_r = _iota((DV_B, D_B), 0)
    selt_c = _iota((DV_B, D_B), 1)
    b_br = jnp.zeros((SB, D_B), F32)
    for h in range(H_B):
        o_h = ohm_ref[h]
        ms = jnp.mean(o_h * o_h, axis=-1, keepdims=True)
        ob = (o_h * lax.rsqrt(ms + EPS) * onw_ref[...] * _silu(ghm_ref[h])).astype(BF16)
        selt = jnp.where(selt_c == selt_r + h * DV_B, 1.0, 0.0).astype(BF16)
        b_br = b_br + _dot(ob, selt)
    out_ref[:, D_A:D_A + D_B] = b_br.astype(BF16)

    xg = proj_ref[:, C_XG:C_XG + D_X].astype(F32)
    out_ref[:, D_A + D_B:D_MODEL] = (ox_ref[...] * _silu(xg)).astype(BF16)


def _sample_mixer(proj, state, ck, cv, wa, abt, avw, bwa, bba, onw):
    nseq = state.shape[1]
    const = lambda *shape: pl.BlockSpec(shape, lambda i: (0,) * len(shape))
    return pl.pallas_call(
        _sample_mixer_kernel,
        grid=(nseq // NS,),
        in_specs=[pl.BlockSpec((SB, D_PROJ), lambda i: (i, 0)),
                  pl.BlockSpec((None, NS, H_B, DK_B, DV_B), lambda i: (0, i, 0, 0, 0)),
                  pl.BlockSpec((NS, N_MEM * H_X, HD_X), lambda i: (i, 0, 0)),
                  pl.BlockSpec((NS, N_MEM * H_X, HD_X), lambda i: (i, 0, 0)),
                  const(H_A, SB, SB), const(SB, H_A), const(1, D_A),
                  const(LANES, D_BK), const(1, D_BK), const(1, DV_B)],
        out_specs=[pl.BlockSpec((SB, D_MODEL), lambda i: (i, 0)),
                   pl.BlockSpec((None, NS, H_B, DK_B, DV_B), lambda i: (0, i, 0, 0, 0)),
                   pl.BlockSpec((SB, D_A), lambda i: (i, 0))],
        out_shape=[jax.ShapeDtypeStruct((nseq * TS, D_MODEL), BF16),
                   jax.ShapeDtypeStruct((1, nseq, H_B, DK_B, DV_B), F32),
                   jax.ShapeDtypeStruct((nseq * TS, D_A), F32)],
        scratch_shapes=[pltpu.VMEM((SB, D_BK), F32),
                        pltpu.VMEM((SB, D_X), F32),
                        pltpu.VMEM((D_BK, SBP), BF16),
                        pltpu.VMEM((D_BK, SBP), F32),
                        pltpu.VMEM((H_B, SBP, DV_B), BF16),
                        pltpu.VMEM((H_B, SB, DV_B), F32),
                        pltpu.VMEM((H_B, SB, DV_B), F32),
                        pltpu.VMEM((SB, D_X), F32)],
        compiler_params=pltpu.CompilerParams(
            dimension_semantics=("arbitrary",), vmem_limit_bytes=VMEM_LIMIT),
        name="sample_mixer",
    )(proj, state, ck, cv, wa, abt, avw, bwa, bba, onw)


def kernel(x_prompt, x_sample, mem_prompt, state_gla, cache_mem_k, cache_mem_v, norm_w, w_in,
           a_vnorm_w, a_ws, a_bs, b_wa, b_ba, b_onorm_w, mem_norm_w, w_mem_kv, w_out, final_norm_w):
    batch, seq, _ = x_prompt.shape
    nseq, tdec, _ = x_sample.shape
    depth = w_in.shape[0]
    assert depth == 1 and tdec == TS and seq % PT == 0 and nseq % NS == 0

    xp = x_prompt.reshape(batch * seq, D_MODEL)
    xs = x_sample.reshape(nseq * TS, D_MODEL)
    mem = mem_prompt.reshape(batch * N_MEM, D_MODEL)
    w_in_bf, w_out_bf, proj_s = _weight_prep(jnp.transpose(w_in[0]), w_out[0], xs, norm_w[0])
    bwa = jnp.concatenate([b_wa[0], jnp.zeros((LANES - GATE_RANK, D_BK), F32)], axis=0).astype(BF16)
    bba = b_ba[0].reshape(1, D_BK)
    avw = a_vnorm_w[0].reshape(1, D_A)
    onw_p = jnp.tile(b_onorm_w[0], H_B).reshape(1, D_B)
    onw_s = b_onorm_w[0].reshape(1, DV_B)
    wa_p = a_ws[0]
    abt_p = a_bs[0].T
    wa_s = jnp.tile(a_ws[0][:, :TS, :TS], (1, NS, NS))
    abt_s = jnp.tile(a_bs[0][:, :TS], (1, NS)).T

    br_s, st_s, cvs = _sample_mixer(
        proj_s, state_gla,
        cache_mem_k.reshape(nseq, N_MEM * H_X, HD_X), cache_mem_v.reshape(nseq, N_MEM * H_X, HD_X),
        wa_s, abt_s, avw, bwa, bba, onw_s)
    y_s = _out_proj(br_s, w_out_bf, xs, final_norm_w, bm=256, name="out_proj_s")

    memkv, mem_k, mem_v = _mem_kv(mem, mem_norm_w[0], w_mem_kv[0], batch=batch)
    y_p, st_p = _prompt_layer(xp, w_in_bf, w_out_bf, norm_w[0], final_norm_w, memkv, wa_p, abt_p,
                              avw, bwa, bba, onw_p, batch=batch, seq=seq)

    return (y_p.reshape(batch, seq, D_MODEL),
            y_s.reshape(nseq, TS, D_MODEL),
            mem_k.reshape(1, batch, N_MEM, H_X, HD_X),
            mem_v.reshape(1, batch, N_MEM, H_X, HD_X),
            st_p.reshape(1, batch, H_B, DK_B, DV_B),
            st_s,
            cvs.reshape(1, nseq, TS, D_A))
```

```python
import functools

import jax
import jax.numpy as jnp
from jax import lax
from jax.experimental import pallas as pl
from jax.experimental.pallas import tpu as pltpu

F32 = jnp.float32
BF16 = jnp.bfloat16

D_MODEL = 2048
D_A = 768
H_A = 4
HD_A = 192
CHUNK_A = 128
D_B = 768
H_B = 4
DV_B = 192
DK_B = 96
D_BK = 384
GATE_RANK = 16
GATE_TAU = 16.0
CHUNK_B = 64
D_X = 512
H_X = 4
HD_X = 128
N_MEM = 256
EPS = 1e-6

LANES = 128
MXU_COLS = 256
C_AU, C_AV, C_AG = 0, 768, 1536
C_BQ, C_BK, C_BV, C_BG = 2304, 2688, 3072, 3840
C_XQ, C_XG, C_BR = 4608, 5120, 5632
D_PROJ = 5760
D_PROJ_PAD = 5888
WIN_START = (0, 128, 384, 512)
WIN = 256
IN_BN = 768
IN_BM_S = 1024
OUT_BN = 512

VMEM_LIMIT = 60 * 1024 * 1024


def _dot(a, b):
    return jnp.dot(a, b, preferred_element_type=F32)


def _dot_nt(a, b):
    return lax.dot_general(a, b, (((1,), (1,)), ((), ())), preferred_element_type=F32)


def _dot_tn(a, b):
    return lax.dot_general(a, b, (((0,), (0,)), ((), ())), preferred_element_type=F32)


def _split(x):
    hi = x.astype(BF16)
    lo = (x - hi.astype(F32)).astype(BF16)
    return hi, lo


def _silu(x):
    return x / (1.0 + jnp.exp(-x))


def _log_sigmoid(x):
    return jnp.minimum(x, 0.0) - jnp.log1p(jnp.exp(-jnp.abs(x)))


def _head_id(idx, width, n):
    h = jnp.zeros_like(idx)
    for i in range(1, n):
        h = h + (idx >= i * width).astype(jnp.int32)
    return h


def _iota(shape, dim):
    return lax.broadcasted_iota(jnp.int32, shape, dim)


def _merge_windows(r):
    rows = r[0].shape[0]
    lo = _iota((rows, LANES), 1) < 64
    tiles = [r[0][:, :LANES], jnp.where(lo, r[0][:, LANES:], r[1][:, :LANES]), r[1][:, LANES:],
             r[2][:, :LANES], jnp.where(lo, r[2][:, LANES:], r[3][:, :LANES]), r[3][:, LANES:]]
    return jnp.concatenate(tiles, axis=1)


def _group_a(proj_ref, wa_ref, abt_ref, avw_ref, mask):
    u = proj_ref[:, C_AU:C_AU + D_A].astype(F32)
    v = proj_ref[:, C_AV:C_AV + D_A].astype(F32)
    g = proj_ref[:, C_AG:C_AG + D_A].astype(F32)
    rows = u.shape[0]
    ms = jnp.mean(v * v, axis=-1, keepdims=True)
    vn = v * lax.rsqrt(ms + EPS) * avw_ref[...]
    vb = vn.astype(BF16)
    r = []
    for gi in range(H_A):
        w = jnp.where(mask, wa_ref[gi], 0.0).astype(BF16)
        r.append(_dot(w, vb[:, WIN_START[gi]:WIN_START[gi] + WIN]))
    mixed = _merge_windows(r)
    col_g = _head_id(_iota((rows, D_A), 1), HD_A, H_A)
    bias = jnp.zeros((rows, D_A), F32)
    for gi in range(H_A):
        bias = jnp.where(col_g == gi, abt_ref[:, gi:gi + 1], bias)
    return u * (mixed + bias) * _silu(g), vn


def _gla_intra_windows(q_in, k_in, vb, mask):
    rows = q_in.shape[0]
    col_h = _head_id(_iota((rows, D_BK), 1), DK_B, H_B)
    kb = k_in.astype(BF16)
    r = []
    for h in range(H_B):
        qh = jnp.where(col_h == h, q_in, 0.0).astype(BF16)
        s = jnp.where(mask, _dot_nt(qh, kb), 0.0).astype(BF16)
        r.append(_dot(s, vb[:, WIN_START[h]:WIN_START[h] + WIN]))
    return r


def _norm_matmul_kernel(x_ref, nw_ref, w_ref, o_ref, h_ref, *, transposed, last_cols):
    j = pl.program_id(1)
    nj = pl.num_programs(1)

    @pl.when(j == 0)
    def _():
        x = x_ref[...]
        ms = jnp.mean(x * x, axis=-1, keepdims=True)
        h_ref[...] = (x * lax.rsqrt(ms + EPS) * nw_ref[...]).astype(BF16)

    mm = _dot_nt if transposed else _dot
    if last_cols is None:
        o_ref[...] = mm(h_ref[...], w_ref[...].astype(BF16)).astype(o_ref.dtype)
    else:
        @pl.when(j < nj - 1)
        def _():
            o_ref[...] = mm(h_ref[...], w_ref[...].astype(BF16)).astype(o_ref.dtype)

        @pl.when(j == nj - 1)
        def _():
            w = w_ref[:last_cols, :] if transposed else w_ref[:, :last_cols]
            o_ref[:, :last_cols] = mm(h_ref[...], w.astype(BF16)).astype(o_ref.dtype)


def _norm_matmul(x, nw, w, n_out, *, bm, bn, transposed, out_dtype, split_out=False,
                 last_cols=None, name):
    m, k = x.shape
    n = w.shape[0] if transposed else w.shape[1]
    nj = -(-n // bn)
    assert m % bm == 0
    if split_out:
        out_shape = jax.ShapeDtypeStruct((nj, m, bn), out_dtype)
        out_spec = pl.BlockSpec((None, bm, bn), lambda i, j: (j, i, 0))
    else:
        out_shape = jax.ShapeDtypeStruct((m, n_out), out_dtype)
        out_spec = pl.BlockSpec((bm, bn), lambda i, j: (i, j))
    if transposed:
        w_spec = pl.BlockSpec((bn, k), lambda i, j: (j, 0))
    else:
        w_spec = pl.BlockSpec((k, bn), lambda i, j: (0, j))
    return pl.pallas_call(
        functools.partial(_norm_matmul_kernel, transposed=transposed, last_cols=last_cols),
        grid=(m // bm, nj),
        in_specs=[pl.BlockSpec((bm, k), lambda i, j: (i, 0)),
                  pl.BlockSpec((1, k), lambda i, j: (0, 0)),
                  w_spec],
        out_specs=out_spec,
        out_shape=out_shape,
        scratch_shapes=[pltpu.VMEM((bm, k), BF16)],
        compiler_params=pltpu.CompilerParams(
            dimension_semantics=("arbitrary", "arbitrary"), vmem_limit_bytes=VMEM_LIMIT),
        name=name,
    )(x, nw.reshape(1, k), w)


def _mem_kv_kernel(x_ref, nw_ref, w_ref, o_ref, ok_ref, ov_ref):
    x = x_ref[...]
    ms = jnp.mean(x * x, axis=-1, keepdims=True)
    h = (x * lax.rsqrt(ms + EPS) * nw_ref[...]).astype(BF16)
    kv = _dot(h, w_ref[...].astype(BF16))
    o_ref[0] = kv[:, :D_X]
    o_ref[1] = kv[:, D_X:]
    for hd in range(H_X):
        ok_ref[pl.ds(hd, N_MEM, stride=H_X), :] = kv[:, hd * HD_X:(hd + 1) * HD_X]
        ov_ref[pl.ds(hd, N_MEM, stride=H_X), :] = kv[:, D_X + hd * HD_X:D_X + (hd + 1) * HD_X]


def _mem_kv(mem, nw, w, *, batch):
    return pl.pallas_call(
        _mem_kv_kernel,
        grid=(batch,),
        in_specs=[pl.BlockSpec((N_MEM, D_MODEL), lambda i: (i, 0)),
                  pl.BlockSpec((1, D_MODEL), lambda i: (0, 0)),
                  pl.BlockSpec((D_MODEL, 2 * D_X), lambda i: (0, 0))],
        out_specs=[pl.BlockSpec((2, N_MEM, D_X), lambda i: (0, i, 0)),
                   pl.BlockSpec((None, N_MEM * H_X, HD_X), lambda i: (i, 0, 0)),
                   pl.BlockSpec((None, N_MEM * H_X, HD_X), lambda i: (i, 0, 0))],
        out_shape=[jax.ShapeDtypeStruct((2, batch * N_MEM, D_X), F32),
                   jax.ShapeDtypeStruct((batch, N_MEM * H_X, HD_X), F32),
                   jax.ShapeDtypeStruct((batch, N_MEM * H_X, HD_X), F32)],
        compiler_params=pltpu.CompilerParams(
            dimension_semantics=("arbitrary",), vmem_limit_bytes=VMEM_LIMIT),
        name="mem_kv",
    )(mem, nw.reshape(1, D_MODEL), w)


PREP_BN = MXU_COLS
N_MAIN_BLOCKS = C_BG // PREP_BN


GATE_BLOCK = D_PROJ_PAD // PREP_BN - 1
N_IN_BLOCKS = GATE_BLOCK + 1
N_OUT_BLOCKS = D_MODEL // PREP_BN
PREP_BUFS = 4


def _weight_prep_kernel(wt_hbm, wo_hbm, wm_hbm, xs_ref, nw_ref, mem_ref, mnw_ref,
                        oi_ref, oo_ref, ps_ref, mkv_ref, mk_ref, mv_ref,
                        buf, sem, hs_ref, hm_ref, wm_buf, wm_sem):
    r = pl.program_id(0)
    slot = lax.rem(r, PREP_BUFS)
    mem_step0 = N_IN_BLOCKS + N_OUT_BLOCKS

    def mem_copy(j):
        return pltpu.make_async_copy(wm_hbm.at[:, pl.ds(j * PREP_BN, PREP_BN)], wm_buf.at[j],
                                     wm_sem.at[j])

    for j in range(N_MEM_BLOCKS):
        @pl.when(r == N_IN_BLOCKS + j)
        def _(j=j):
            mem_copy(j).start()

    @pl.when(r == N_IN_BLOCKS)
    def _():
        x = mem_ref[...]
        ms = jnp.mean(x * x, axis=-1, keepdims=True)
        hm_ref[...] = (x * lax.rsqrt(ms + EPS) * mnw_ref[...]).astype(BF16)

    for j in range(N_MEM_BLOCKS):
        @pl.when(r == mem_step0 + j)
        def _(j=j):
            mem_copy(j).wait()
            kv = _dot(hm_ref[...], wm_buf[j].astype(BF16))
            mkv_ref[...] = kv
            dst = mk_ref if j * PREP_BN < D_X else mv_ref
            for hh in range(PREP_BN // HD_X):
                head = (j * PREP_BN % D_X) // HD_X + hh
                for b in range(kv.shape[0] // N_MEM):
                    dst[b, pl.ds(head, N_MEM, stride=H_X), :] = (
                        kv[b * N_MEM:(b + 1) * N_MEM, hh * HD_X:(hh + 1) * HD_X])

    def in_copy(rr, sl):
        start = pl.multiple_of(jnp.where(rr < N_MAIN_BLOCKS, rr * PREP_BN, rr * PREP_BN + GATE_RANK), 8)
        return pltpu.make_async_copy(wt_hbm.at[pl.ds(start, PREP_BN), :], buf.at[sl], sem.at[sl])

    def gate_copy(sl):
        return pltpu.make_async_copy(wt_hbm.at[pl.ds(C_BG, GATE_RANK), :],
                                     buf.at[sl, pl.ds(0, GATE_RANK), :], sem.at[sl])

    def out_copy(rr, sl):
        start = pl.multiple_of((rr - N_IN_BLOCKS) * PREP_BN, PREP_BN)
        return pltpu.make_async_copy(wo_hbm.at[pl.ds(start, PREP_BN), :], buf.at[sl], sem.at[sl])

    def start_fetch(rr, sl):
        @pl.when(rr < GATE_BLOCK)
        def _():
            in_copy(rr, sl).start()

        @pl.when(rr == GATE_BLOCK)
        def _():
            gate_copy(sl).start()

        @pl.when(rr > GATE_BLOCK)
        def _():
            out_copy(rr, sl).start()

    @pl.when(r == 0)
    def _():
        for ahead in range(PREP_BUFS - 1):
            start_fetch(r + ahead, ahead)

    nxt = r + PREP_BUFS - 1

    @pl.when(nxt < mem_step0)
    def _():
        start_fetch(nxt, lax.rem(nxt, PREP_BUFS))

    @pl.when(r == 0)
    def _():
        x = xs_ref[...]
        ms = jnp.mean(x * x, axis=-1, keepdims=True)
        hs_ref[...] = (x * lax.rsqrt(ms + EPS) * nw_ref[...]).astype(BF16)

    @pl.when(r < GATE_BLOCK)
    def _():
        in_copy(r, slot).wait()
        oi_ref[...] = buf[slot].T.astype(BF16)
        ps_ref[...] = _dot(hs_ref[...], oi_ref[...]).astype(BF16)

    @pl.when(r == GATE_BLOCK)
    def _():
        gate_copy(slot).wait()
        rows = jnp.concatenate([buf[slot, 0:GATE_RANK, :],
                                jnp.zeros((PREP_BN - GATE_RANK, D_MODEL), F32)], axis=0)
        oi_ref[...] = rows.T.astype(BF16)
        ps_ref[...] = _dot(hs_ref[...], oi_ref[...]).astype(BF16)

    @pl.when((r > GATE_BLOCK) & (r < mem_step0))
    def _():
        out_copy(r, slot).wait()
        oo_ref[...] = buf[slot].astype(BF16)


N_MEM_BLOCKS = 2 * D_X // PREP_BN


def _weight_prep(w_t, w_out, w_mem, xs, nw, mem, mnw):
    k = w_t.shape[1]
    m = xs.shape[0]
    mm = mem.shape[0]
    batch = mm // N_MEM
    mem_step0 = N_IN_BLOCKS + N_OUT_BLOCKS
    in_block = lambda r: (0, jnp.minimum(r, GATE_BLOCK))
    mem_j = lambda r: jnp.maximum(r - mem_step0, 0)
    per_d_x = D_X // PREP_BN
    resident = lambda *shape: pl.BlockSpec(shape, lambda r: (0,) * len(shape),
                                           pipeline_mode=pl.Buffered(1))
    return pl.pallas_call(
        _weight_prep_kernel,
        grid=(mem_step0 + N_MEM_BLOCKS,),
        in_specs=[pl.BlockSpec(memory_space=pl.ANY), pl.BlockSpec(memory_space=pl.ANY),
                  pl.BlockSpec(memory_space=pl.ANY),
                  resident(m, k), pl.BlockSpec((1, k), lambda r: (0, 0)),
                  resident(mm, k), pl.BlockSpec((1, k), lambda r: (0, 0))],
        out_specs=[pl.BlockSpec((k, PREP_BN), in_block),
                   pl.BlockSpec((PREP_BN, D_MODEL),
                                lambda r: (jnp.clip(r - N_IN_BLOCKS, 0, N_OUT_BLOCKS - 1), 0)),
                   pl.BlockSpec((m, PREP_BN), in_block),
                   pl.BlockSpec((None, mm, PREP_BN),
                                lambda r: (mem_j(r) // per_d_x, 0, lax.rem(mem_j(r), per_d_x))),
                   pl.BlockSpec((batch, N_MEM * H_X, HD_X), lambda r: (0, 0, 0)),
                   pl.BlockSpec((batch, N_MEM * H_X, HD_X), lambda r: (0, 0, 0))],
        out_shape=[jax.ShapeDtypeStruct((k, D_PROJ_PAD), BF16),
                   jax.ShapeDtypeStruct((D_MODEL, D_MODEL), BF16),
                   jax.ShapeDtypeStruct((m, D_PROJ_PAD), BF16),
                   jax.ShapeDtypeStruct((2, mm, D_X), F32),
                   jax.ShapeDtypeStruct((batch, N_MEM * H_X, HD_X), F32),
                   jax.ShapeDtypeStruct((batch, N_MEM * H_X, HD_X), F32)],
        scratch_shapes=[pltpu.VMEM((PREP_BUFS, PREP_BN, k), F32),
                        pltpu.SemaphoreType.DMA((PREP_BUFS,)),
                        pltpu.VMEM((m, k), BF16),
                        pltpu.VMEM((mm, k), BF16),
                        pltpu.VMEM((N_MEM_BLOCKS, k, PREP_BN), F32),
                        pltpu.SemaphoreType.DMA((N_MEM_BLOCKS,))],
        compiler_params=pltpu.CompilerParams(
            dimension_semantics=("arbitrary",), vmem_limit_bytes=VMEM_LIMIT),
        name="weight_prep",
    )(w_t, w_out, w_mem, xs, nw.reshape(1, k), mem, mnw.reshape(1, k))


def _out_proj_kernel(br_ref, w_ref, x_ref, fw_ref, y_ref):
    acc = _dot(br_ref[...], w_ref[...]) + x_ref[...]
    ms = jnp.mean(acc * acc, axis=-1, keepdims=True)
    y_ref[...] = acc * lax.rsqrt(ms + EPS) * fw_ref[...]


def _out_proj(br, w, x, fw, *, bm, name):
    m, k = br.shape
    n = w.shape[1]
    return pl.pallas_call(
        _out_proj_kernel,
        grid=(m // bm,),
        in_specs=[pl.BlockSpec((bm, k), lambda i: (i, 0)),
                  pl.BlockSpec((k, n), lambda i: (0, 0)),
                  pl.BlockSpec((bm, n), lambda i: (i, 0)),
                  pl.BlockSpec((1, n), lambda i: (0, 0))],
        out_specs=pl.BlockSpec((bm, n), lambda i: (i, 0)),
        out_shape=jax.ShapeDtypeStruct((m, n), F32),
        compiler_params=pltpu.CompilerParams(
            dimension_semantics=("arbitrary",), vmem_limit_bytes=VMEM_LIMIT),
        name=name,
    )(br, w, x, fw.reshape(1, n))


PT = 256
GLA_ROWS = 128
N_MIXER_STAGES = 2 + 4 * (PT // GLA_ROWS) + 1 + 2 * H_X


def _group_a_chunks(proj_ref, wa_ref, abt_ref, avw_ref):
    v = proj_ref[:, C_AV:C_AV + D_A].astype(F32)
    n_chunks = v.shape[0] // CHUNK_A
    ms = jnp.mean(v * v, axis=-1, keepdims=True)
    vb = (v * lax.rsqrt(ms + EPS) * avw_ref[...]).astype(BF16)
    tril = _iota((CHUNK_A, CHUNK_A), 1) <= _iota((CHUNK_A, CHUNK_A), 0)
    r = []
    for gi in range(H_A):
        w = jnp.where(tril, wa_ref[gi], 0.0).astype(BF16)
        ws = slice(WIN_START[gi], WIN_START[gi] + WIN)
        rhs = jnp.concatenate([vb[c * CHUNK_A:(c + 1) * CHUNK_A, ws] for c in range(n_chunks)], axis=1)
        res = _dot(w, rhs)
        r.append(jnp.concatenate([res[:, c * WIN:(c + 1) * WIN] for c in range(n_chunks)], axis=0))
    mixed = _merge_windows(r)
    col_g = _head_id(_iota((CHUNK_A, D_A), 1), HD_A, H_A)
    bias = jnp.zeros((CHUNK_A, D_A), F32)
    for gi in range(H_A):
        bias = jnp.where(col_g == gi, abt_ref[:, gi:gi + 1], bias)
    bias = jnp.concatenate([bias] * n_chunks, axis=0)
    u = proj_ref[:, C_AU:C_AU + D_A].astype(F32)
    g = proj_ref[:, C_AG:C_AG + D_A].astype(F32)
    return u * (mixed + bias) * _silu(g)


def _prompt_block(proj_ref, out_ref, mk_ref, mv_ref, wa_ref, abt_ref, avw_ref, bwa_ref, bba_ref,
                  onw_ref, sbd_ref, o_ref, la_ref):
    out_ref[:, 0:D_A] = _group_a_chunks(proj_ref, wa_ref, abt_ref, avw_ref).astype(BF16)
    yield

    pre = _dot(proj_ref[:, C_BR:C_BR + LANES], bwa_ref[...]) + bba_ref[...]
    la_ref[...] = _log_sigmoid(pre) * (1.0 / GATE_TAU)
    yield
    gl = GLA_ROWS
    tril_bf = jnp.where(_iota((gl, gl), 1) <= _iota((gl, gl), 0), 1.0, 0.0).astype(BF16)
    win_col = _iota((gl, WIN), 1)
    col_hq = _head_id(_iota((gl, D_BK), 1), DK_B, H_B)
    tril_heads = (_iota((H_B * gl, gl), 1) <= jnp.bitwise_and(_iota((H_B * gl, gl), 0), gl - 1))
    for c in range(PT // gl):
        sl = slice(c * gl, (c + 1) * gl)
        log_a = la_ref[sl, :]
        q = proj_ref[sl, C_BQ:C_BQ + D_BK].astype(F32)
        k = proj_ref[sl, C_BK:C_BK + D_BK].astype(F32)
        vb = proj_ref[sl, C_BV:C_BV + D_B]
        hi, lo = _split(log_a)
        cum2 = _dot(tril_bf, jnp.concatenate([hi, lo], axis=1))
        yield
        cum = cum2[:, :D_BK] + cum2[:, D_BK:]
        tot_row = cum[gl - 1:gl, :]
        mid_row = cum[CHUNK_B - 1:CHUNK_B, :]
        la_t = log_a.T
        q_s = q * (DK_B ** -0.5)
        q_dec = q_s * jnp.exp(cum)
        q_in = q_s * jnp.exp(cum - mid_row)
        k_in = k * jnp.exp(mid_row - cum)
        k_out = k * jnp.exp(tot_row - cum)
        q_heads = jnp.concatenate([jnp.where(col_hq == h, q_in, 0.0) for h in range(H_B)], axis=0)
        sc = _dot_nt(q_heads.astype(BF16), k_in.astype(BF16))
        yield
        sc = jnp.where(tril_heads, sc, 0.0).astype(BF16)
        o_intra = _merge_windows(
            [_dot(sc[h * gl:(h + 1) * gl], vb[:, WIN_START[h]:WIN_START[h] + WIN])
             for h in range(H_B)])
        s_old = sbd_ref[...]
        o_inter = _dot(q_dec.astype(BF16), s_old.astype(BF16))
        yield
        kot = k_out.T.astype(BF16)
        for h in range(H_B):
            rs = slice(h * DK_B, (h + 1) * DK_B)
            ws = slice(WIN_START[h], WIN_START[h] + WIN)
            dec = jnp.exp(jnp.sum(la_t[rs], axis=1, keepdims=True))
            lo_col = h * DV_B - WIN_START[h]
            in_head = (win_col >= lo_col) & (win_col < lo_col + DV_B)
            v_h = jnp.where(in_head, vb[:, ws], jnp.zeros((), BF16))
            kv = _dot(kot[rs], v_h)
            sbd_ref[rs, ws] = s_old[rs, ws] * dec + kv
        o_ref[sl, :] = o_intra + o_inter
        yield
    o = o_ref[...]
    o2 = o * o
    t = [o2[:, j * LANES:(j + 1) * LANES] for j in range(D_B // LANES)]
    lo = _iota((PT, LANES), 1) < 64
    t1a = jnp.where(lo, t[1], 0.0)
    t4a = jnp.where(lo, t[4], 0.0)
    sums = (t[0] + t1a, (t[1] - t1a) + t[2], t[3] + t4a, (t[4] - t4a) + t[5])
    inv = [lax.rsqrt(jnp.sum(x, axis=-1, keepdims=True) * (1.0 / DV_B) + EPS) for x in sums]
    inv = jnp.concatenate(
        [jnp.broadcast_to(inv[0], (PT, LANES)), jnp.where(lo, inv[0], inv[1]),
         jnp.broadcast_to(inv[1], (PT, LANES)), jnp.broadcast_to(inv[2], (PT, LANES)),
         jnp.where(lo, inv[2], inv[3]), jnp.broadcast_to(inv[3], (PT, LANES))], axis=1)
    o_n = o * inv * onw_ref[...]
    bg = proj_ref[:, C_BG:C_BG + D_B].astype(F32)
    out_ref[:, D_A:D_A + D_B] = (o_n * _silu(bg)).astype(BF16)
    yield

    for h in range(H_X):
        hs = slice(h * HD_X, (h + 1) * HD_X)
        qh = proj_ref[:, C_XQ + h * HD_X:C_XQ + (h + 1) * HD_X]
        s = _dot_nt(qh, mk_ref[:, hs].astype(BF16)) * (HD_X ** -0.5)
        yield
        e = jnp.exp(s - jnp.max(s, axis=-1, keepdims=True))
        den = jnp.sum(e, axis=-1, keepdims=True)
        ox = _dot(e.astype(BF16), mv_ref[:, hs].astype(BF16)) / den
        xg = proj_ref[:, C_XG + h * HD_X:C_XG + (h + 1) * HD_X].astype(F32)
        out_ref[:, D_A + D_B + h * HD_X:D_A + D_B + (h + 1) * HD_X] = (ox * _silu(xg)).astype(BF16)
        yield


def _prompt_layer_kernel(xn_ref, wi_ref, wo_ref, nw_ref, fw_ref, mk_ref, mv_ref, wa_ref, abt_ref,
                         avw_ref, bwa_ref, bba_ref, onw_ref, y_ref, st_ref,
                         pa_ref, pb_ref, xk_ref, br_ref, h_ref, sbd_ref, o_ref, la_ref, *, nt):
    s = pl.program_id(0)
    cur = jnp.maximum(s - 1, 0)
    t = lax.rem(cur, nt)

    @pl.when(t == 0)
    def _():
        sbd_ref[...] = jnp.zeros_like(sbd_ref)

    def in_proj_stages(pn_ref):
        x = xn_ref[...]
        ms = jnp.mean(x * x, axis=-1, keepdims=True)
        h_ref[...] = (x * lax.rsqrt(ms + EPS) * nw_ref[...]).astype(BF16)
        yield
        for c0 in range(0, D_PROJ_PAD, MXU_COLS):
            cols = pl.ds(c0, MXU_COLS)
            pn_ref[:, cols] = _dot(h_ref[...], wi_ref[:, cols]).astype(BF16)
            yield
        xk_ref[...] = xn_ref[...]

    def out_proj_stages():
        ssq = jnp.zeros((PT, 1), F32)
        for c0 in range(0, D_MODEL, OUT_BN):
            cols = pl.ds(c0, OUT_BN)
            acc = _dot(br_ref[...], wo_ref[:, cols]) + xk_ref[:, cols]
            y_ref[:, cols] = acc
            ssq = ssq + jnp.sum(acc * acc, axis=-1, keepdims=True)
            yield
        y_ref[...] = y_ref[...] * lax.rsqrt(ssq * (1.0 / D_MODEL) + EPS) * fw_ref[...]
        yield

    def run(order, streams):
        for name in order:
            next(streams[name])
        for name, gen in streams.items():
            assert next(gen, "done") == "done", name

    def mixer_stages(pc_ref):
        return _prompt_block(pc_ref, br_ref, mk_ref, mv_ref, wa_ref, abt_ref, avw_ref, bwa_ref,
                             bba_ref, onw_ref, sbd_ref, o_ref, la_ref)

    n_p = 1 + D_PROJ_PAD // MXU_COLS
    n_m = N_MIXER_STAGES
    n_o = 1 + D_MODEL // OUT_BN

    def body(pn_ref, pc_ref):
        order = ["P"]
        for i in range(max(n_p - 1, n_m)):
            order += ["M"] * (i < n_m) + ["P"] * (i < n_p - 1)
        order += ["O"] * n_o
        run(order, {"P": in_proj_stages(pn_ref), "M": mixer_stages(pc_ref), "O": out_proj_stages()})

    last = pl.num_programs(0) - 1

    @pl.when(s == 0)
    def _():
        run(["P"] * n_p, {"P": in_proj_stages(pa_ref)})

    @pl.when((lax.rem(s, 2) == 0) & (s > 0) & (s < last))
    def _():
        body(pa_ref, pb_ref)

    @pl.when(lax.rem(s, 2) == 1)
    def _():
        body(pb_ref, pa_ref)

    @pl.when(s == last)
    def _():
        run(["M"] * n_m + ["O"] * n_o, {"M": mixer_stages(pb_ref), "O": out_proj_stages()})

    @pl.when((t == nt - 1) & (s > 0))
    def _():
        for h in range(H_B):
            off = h * DV_B - WIN_START[h]
            blk = sbd_ref[h * DK_B:(h + 1) * DK_B, WIN_START[h]:WIN_START[h] + WIN]
            if off:
                blk = pltpu.roll(blk, WIN - off, 1)
            st_ref[0, h] = blk[:, :DV_B]


def _prompt_layer(xp, w_in_bf, w_out_bf, nw, fw, memkv, wa, abt, avw, bwa, bba, onw, *, batch, seq):
    nt = seq // PT
    nblk = batch * nt
    assert nblk % 2 == 0
    cur = lambda s: jnp.maximum(s - 1, 0)
    const = lambda *shape: pl.BlockSpec(shape, lambda s: (0,) * len(shape))
    resident = lambda *shape: pl.BlockSpec(shape, lambda s: (0,) * len(shape),
                                           pipeline_mode=pl.Buffered(1))
    return pl.pallas_call(
        functools.partial(_prompt_layer_kernel, nt=nt),
        grid=(nblk + 1,),
        in_specs=[pl.BlockSpec((PT, D_MODEL), lambda s: (jnp.minimum(s, nblk - 1), 0)),
                  resident(D_MODEL, D_PROJ_PAD), resident(D_MODEL, D_MODEL),
                  const(1, D_MODEL), const(1, D_MODEL),
                  pl.BlockSpec((None, N_MEM, D_X), lambda s: (0, cur(s) // nt, 0)),
                  pl.BlockSpec((None, N_MEM, D_X), lambda s: (1, cur(s) // nt, 0)),
                  const(H_A, CHUNK_A, CHUNK_A), const(CHUNK_A, H_A), const(1, D_A),
                  const(LANES, D_BK), const(1, D_BK), const(1, D_B)],
        out_specs=[pl.BlockSpec((PT, D_MODEL), lambda s: (cur(s), 0)),
                   pl.BlockSpec((1, H_B, DK_B, DV_B), lambda s: (cur(s) // nt, 0, 0, 0))],
        out_shape=[jax.ShapeDtypeStruct((batch * seq, D_MODEL), F32),
                   jax.ShapeDtypeStruct((batch, H_B, DK_B, DV_B), F32)],
        scratch_shapes=[pltpu.VMEM((PT, D_PROJ_PAD), BF16),
                        pltpu.VMEM((PT, D_PROJ_PAD), BF16),
                        pltpu.VMEM((PT, D_MODEL), F32),
                        pltpu.VMEM((PT, D_MODEL), BF16),
                        pltpu.VMEM((PT, D_MODEL), BF16),
                        pltpu.VMEM((D_BK, D_B), F32),
                        pltpu.VMEM((PT, D_B), F32),
                        pltpu.VMEM((PT, D_BK), F32)],
        compiler_params=pltpu.CompilerParams(
            dimension_semantics=("arbitrary",), vmem_limit_bytes=VMEM_LIMIT),
        name="prompt_layer",
    )(xp, w_in_bf, w_out_bf, nw.reshape(1, D_MODEL), fw.reshape(1, D_MODEL), memkv, memkv,
      wa, abt, avw, bwa, bba, onw)


NS = 8
TS = 8
SB = NS * TS
SBP = 128


def _sample_mixer_kernel(proj_ref, st_ref, ck_ref, cv_ref, wa_ref, abt_ref, avw_ref, bwa_ref,
                         bba_ref, onw_ref, out_ref, stn_ref, cvs_ref,
                         qin_ref, xq_ref, kot_ref, lat_ref, vhm_ref, ghm_ref, ohm_ref, ox_ref):
    ri = _iota((SB, SB), 0)
    ci = _iota((SB, SB), 1)
    same_seq = jnp.right_shift(ri, 3) == jnp.right_shift(ci, 3)
    causal = same_seq & (ci <= ri)

    a_br, vn = _group_a(proj_ref, wa_ref, abt_ref, avw_ref, causal)
    out_ref[:, 0:D_A] = a_br.astype(BF16)
    cvs_ref[...] = vn

    q = proj_ref[:, C_BQ:C_BQ + D_BK].astype(F32)
    k = proj_ref[:, C_BK:C_BK + D_BK].astype(F32)
    vb = proj_ref[:, C_BV:C_BV + D_B]
    bgb = proj_ref[:, C_BG:C_BG + D_B]
    pre = _dot(proj_ref[:, C_BR:C_BR + LANES], bwa_ref[...]) + bba_ref[...]
    log_a = _log_sigmoid(pre) * (1.0 / GATE_TAU)
    causal_bf = jnp.where(causal, 1.0, 0.0).astype(BF16)
    seq_bf = jnp.where(same_seq, 1.0, 0.0).astype(BF16)
    hi, lo = _split(log_a)
    cum = _dot(causal_bf, hi) + _dot(causal_bf, lo)
    tot = _dot(seq_bf, hi) + _dot(seq_bf, lo)
    q_in = q * (DK_B ** -0.5) * jnp.exp(cum)
    k_in = k * jnp.exp(-cum)
    k_out = k * jnp.exp(tot - cum)
    qin_ref[...] = q_in
    zpad = jnp.zeros((SBP - SB, D_BK), F32)
    kot_ref[...] = jnp.concatenate([k_out, zpad], axis=0).T.astype(BF16)
    lat_ref[...] = jnp.concatenate([log_a, zpad], axis=0).T
    xq_ref[...] = proj_ref[:, C_XQ:C_XQ + D_X].astype(F32)

    wins = _gla_intra_windows(q_in, k_in, vb, causal)
    sel_r = _iota((D_B, DV_B), 0)
    sel_c = _iota((D_B, DV_B), 1)
    vhm_ref[...] = jnp.zeros_like(vhm_ref)
    for h in range(H_B):
        off = h * DV_B - WIN_START[h]
        w = wins[h]
        if off:
            w = pltpu.roll(w, WIN - off, 1)
        ohm_ref[h] = w[:, :DV_B]
        sel = jnp.where(sel_r == sel_c + h * DV_B, 1.0, 0.0).astype(BF16)
        vhm_ref[h, 0:SB, :] = _dot(vb, sel).astype(BF16)
        ghm_ref[h] = _dot(bgb, sel)

    mask_x = (jnp.right_shift(_iota((H_X * TS, N_MEM * H_X), 0), 3)
              == jnp.bitwise_and(_iota((H_X * TS, N_MEM * H_X), 1), H_X - 1))
    mask_b = jnp.right_shift(_iota((H_B * TS, D_BK), 0), 3) == _head_id(_iota((H_B * TS, D_BK), 1), DK_B, H_B)
    lane_seq = jnp.right_shift(_iota((D_BK, SBP), 1), 3)

    def per_seq(s, carry):
        r0 = pl.multiple_of(s * TS, TS)
        q8 = xq_ref[pl.ds(r0, TS), :]
        q32 = jnp.concatenate([q8[:, h * HD_X:(h + 1) * HD_X] for h in range(H_X)], axis=0)
        sc = _dot_nt(q32.astype(BF16), ck_ref[s].astype(BF16)) * (HD_X ** -0.5)
        sc = jnp.where(mask_x, sc, -1e30)
        e = jnp.exp(sc - jnp.max(sc, axis=-1, keepdims=True))
        den = jnp.sum(e, axis=-1, keepdims=True)
        o = _dot(e.astype(BF16), cv_ref[s].astype(BF16)) / den
        ox_ref[pl.ds(r0, TS), :] = jnp.concatenate(
            [o[h * TS:(h + 1) * TS] for h in range(H_X)], axis=1)
        qi8 = qin_ref[pl.ds(r0, TS), :]
        qbd2 = jnp.where(mask_b, jnp.concatenate([qi8] * H_B, axis=0), 0.0).astype(BF16)
        s0 = jnp.concatenate([st_ref[s, h] for h in range(H_B)], axis=0)
        o_inter = _dot(qbd2, s0.astype(BF16))
        for h in range(H_B):
            ohm_ref[h, pl.ds(r0, TS), :] += o_inter[h * TS:(h + 1) * TS]
        dec = jnp.exp(jnp.sum(jnp.where(lane_seq == s, lat_ref[...], 0.0), axis=1, keepdims=True))
        kot = jnp.where(lane_seq == s, kot_ref[...], jnp.zeros((), BF16))
        kv = jnp.concatenate(
            [_dot(kot[h * DK_B:(h + 1) * DK_B], vhm_ref[h]) for h in range(H_B)], axis=0)
        s_new = s0 * dec + kv
        for h in range(H_B):
            stn_ref[s, h] = s_new[h * DK_B:(h + 1) * DK_B]
        return carry

    lax.fori_loop(0, NS, per_seq, 0, unroll=4)

    selt_r = _iota((DV_B, D_B), 0)
    selt_c = _iota((DV_B, D_B), 1)
    b_br = jnp.zeros((SB, D_B), F32)
    for h in range(H_B):
        o_h = ohm_ref[h]
        ms = jnp.mean(o_h * o_h, axis=-1, keepdims=True)
        ob = (o_h * lax.rsqrt(ms + EPS) * onw_ref[...] * _silu(ghm_ref[h])).astype(BF16)
        selt = jnp.where(selt_c == selt_r + h * DV_B, 1.0, 0.0).astype(BF16)
        b_br = b_br + _dot(ob, selt)
    out_ref[:, D_A:D_A + D_B] = b_br.astype(BF16)

    xg = proj_ref[:, C_XG:C_XG + D_X].astype(F32)
    out_ref[:, D_A + D_B:D_MODEL] = (ox_ref[...] * _silu(xg)).astype(BF16)


def _sample_mixer(proj, state, ck, cv, wa, abt, avw, bwa, bba, onw):
    nseq = state.shape[1]
    const = lambda *shape: pl.BlockSpec(shape, lambda i: (0,) * len(shape))
    return pl.pallas_call(
        _sample_mixer_kernel,
        grid=(nseq // NS,),
        in_specs=[pl.BlockSpec((SB, D_PROJ), lambda i: (i, 0)),
                  pl.BlockSpec((None, NS, H_B, DK_B, DV_B), lambda i: (0, i, 0, 0, 0)),
                  pl.BlockSpec((NS, N_MEM * H_X, HD_X), lambda i: (i, 0, 0)),
                  pl.BlockSpec((NS, N_MEM * H_X, HD_X), lambda i: (i, 0, 0)),
                  const(H_A, SB, SB), const(SB, H_A), const(1, D_A),
                  const(LANES, D_BK), const(1, D_BK), const(1, DV_B)],
        out_specs=[pl.BlockSpec((SB, D_MODEL), lambda i: (i, 0)),
                   pl.BlockSpec((None, NS, H_B, DK_B, DV_B), lambda i: (0, i, 0, 0, 0)),
                   pl.BlockSpec((SB, D_A), lambda i: (i, 0))],
        out_shape=[jax.ShapeDtypeStruct((nseq * TS, D_MODEL), BF16),
                   jax.ShapeDtypeStruct((1, nseq, H_B, DK_B, DV_B), F32),
                   jax.ShapeDtypeStruct((nseq * TS, D_A), F32)],
        scratch_shapes=[pltpu.VMEM((SB, D_BK), F32),
                        pltpu.VMEM((SB, D_X), F32),
                        pltpu.VMEM((D_BK, SBP), BF16),
                        pltpu.VMEM((D_BK, SBP), F32),
                        pltpu.VMEM((H_B, SBP, DV_B), BF16),
                        pltpu.VMEM((H_B, SB, DV_B), F32),
                        pltpu.VMEM((H_B, SB, DV_B), F32),
                        pltpu.VMEM((SB, D_X), F32)],
        compiler_params=pltpu.CompilerParams(
            dimension_semantics=("arbitrary",), vmem_limit_bytes=VMEM_LIMIT),
        name="sample_mixer",
    )(proj, state, ck, cv, wa, abt, avw, bwa, bba, onw)


def kernel(x_prompt, x_sample, mem_prompt, state_gla, cache_mem_k, cache_mem_v, norm_w, w_in,
           a_vnorm_w, a_ws, a_bs, b_wa, b_ba, b_onorm_w, mem_norm_w, w_mem_kv, w_out, final_norm_w):
    batch, seq, _ = x_prompt.shape
    nseq, tdec, _ = x_sample.shape
    depth = w_in.shape[0]
    assert depth == 1 and tdec == TS and seq % PT == 0 and nseq % NS == 0

    xp = x_prompt.reshape(batch * seq, D_MODEL)
    xs = x_sample.reshape(nseq * TS, D_MODEL)
    mem = mem_prompt.reshape(batch * N_MEM, D_MODEL)
    w_in_bf, w_out_bf, proj_s, memkv, mem_k, mem_v = _weight_prep(
        jnp.transpose(w_in[0]), w_out[0], w_mem_kv[0], xs, norm_w[0], mem, mem_norm_w[0])
    bwa = jnp.concatenate([b_wa[0], jnp.zeros((LANES - GATE_RANK, D_BK), F32)], axis=0).astype(BF16)
    bba = b_ba[0].reshape(1, D_BK)
    avw = a_vnorm_w[0].reshape(1, D_A)
    onw_p = jnp.tile(b_onorm_w[0], H_B).reshape(1, D_B)
    onw_s = b_onorm_w[0].reshape(1, DV_B)
    wa_p = a_ws[0]
    abt_p = a_bs[0].T
    wa_s = jnp.tile(a_ws[0][:, :TS, :TS], (1, NS, NS))
    abt_s = jnp.tile(a_bs[0][:, :TS], (1, NS)).T

    br_s, st_s, cvs = _sample_mixer(
        proj_s, state_gla,
        cache_mem_k.reshape(nseq, N_MEM * H_X, HD_X), cache_mem_v.reshape(nseq, N_MEM * H_X, HD_X),
        wa_s, abt_s, avw, bwa, bba, onw_s)
    y_s = _out_proj(br_s, w_out_bf, xs, final_norm_w, bm=256, name="out_proj_s")

    y_p, st_p = _prompt_layer(xp, w_in_bf, w_out_bf, norm_w[0], final_norm_w, memkv, wa_p, abt_p,
                              avw, bwa, bba, onw_p, batch=batch, seq=seq)

    return (y_p.reshape(batch, seq, D_MODEL),
            y_s.reshape(nseq, TS, D_MODEL),
            mem_k.reshape(1, batch, N_MEM, H_X, HD_X),
            mem_v.reshape(1, batch, N_MEM, H_X, HD_X),
            st_p.reshape(1, batch, H_B, DK_B, DV_B),
            st_s,
            cvs.reshape(1, nseq, TS, D_A))
```

```python
import functools

import jax
import jax.numpy as jnp
from jax import lax
from jax.experimental import pallas as pl
from jax.experimental.pallas import tpu as pltpu

F32 = jnp.float32
BF16 = jnp.bfloat16

D_MODEL = 2048
D_A = 768
H_A = 4
HD_A = 192
CHUNK_A = 128
D_B = 768
H_B = 4
DV_B = 192
DK_B = 96
D_BK = 384
GATE_RANK = 16
GATE_TAU = 16.0
CHUNK_B = 64
D_X = 512
H_X = 4
HD_X = 128
N_MEM = 256
EPS = 1e-6

LANES = 128
MXU_COLS = 256
C_AU, C_AV, C_AG = 0, 768, 1536
C_BQ, C_BK, C_BV, C_BG = 2304, 2688, 3072, 3840
C_XQ, C_XG, C_BR = 4608, 5120, 5632
D_PROJ = 5760
D_PROJ_PAD = 5888
WIN_START = (0, 128, 384, 512)
WIN = 256
IN_BN = 768
IN_BM_S = 1024
OUT_BN = 512

VMEM_LIMIT = 60 * 1024 * 1024


def _dot(a, b):
    return jnp.dot(a, b, preferred_element_type=F32)


def _dot_nt(a, b):
    return lax.dot_general(a, b, (((1,), (1,)), ((), ())), preferred_element_type=F32)


def _dot_tn(a, b):
    return lax.dot_general(a, b, (((0,), (0,)), ((), ())), preferred_element_type=F32)


def _split(x):
    hi = x.astype(BF16)
    lo = (x - hi.astype(F32)).astype(BF16)
    return hi, lo


def _silu(x):
    return x / (1.0 + jnp.exp(-x))


def _log_sigmoid(x):
    return jnp.minimum(x, 0.0) - jnp.log1p(jnp.exp(-jnp.abs(x)))


def _head_id(idx, width, n):
    h = jnp.zeros_like(idx)
    for i in range(1, n):
        h = h + (idx >= i * width).astype(jnp.int32)
    return h


def _iota(shape, dim):
    return lax.broadcasted_iota(jnp.int32, shape, dim)


def _merge_windows(r):
    rows = r[0].shape[0]
    lo = _iota((rows, LANES), 1) < 64
    tiles = [r[0][:, :LANES], jnp.where(lo, r[0][:, LANES:], r[1][:, :LANES]), r[1][:, LANES:],
             r[2][:, :LANES], jnp.where(lo, r[2][:, LANES:], r[3][:, :LANES]), r[3][:, LANES:]]
    return jnp.concatenate(tiles, axis=1)


def _group_a(proj_ref, wa_ref, abt_ref, avw_ref, mask):
    u = proj_ref[:, C_AU:C_AU + D_A].astype(F32)
    v = proj_ref[:, C_AV:C_AV + D_A].astype(F32)
    g = proj_ref[:, C_AG:C_AG + D_A].astype(F32)
    rows = u.shape[0]
    ms = jnp.mean(v * v, axis=-1, keepdims=True)
    vn = v * lax.rsqrt(ms + EPS) * avw_ref[...]
    vb = vn.astype(BF16)
    r = []
    for gi in range(H_A):
        w = jnp.where(mask, wa_ref[gi], 0.0).astype(BF16)
        r.append(_dot(w, vb[:, WIN_START[gi]:WIN_START[gi] + WIN]))
    mixed = _merge_windows(r)
    col_g = _head_id(_iota((rows, D_A), 1), HD_A, H_A)
    bias = jnp.zeros((rows, D_A), F32)
    for gi in range(H_A):
        bias = jnp.where(col_g == gi, abt_ref[:, gi:gi + 1], bias)
    return u * (mixed + bias) * _silu(g), vn


def _gla_intra_windows(q_in, k_in, vb, mask):
    rows = q_in.shape[0]
    col_h = _head_id(_iota((rows, D_BK), 1), DK_B, H_B)
    kb = k_in.astype(BF16)
    r = []
    for h in range(H_B):
        qh = jnp.where(col_h == h, q_in, 0.0).astype(BF16)
        s = jnp.where(mask, _dot_nt(qh, kb), 0.0).astype(BF16)
        r.append(_dot(s, vb[:, WIN_START[h]:WIN_START[h] + WIN]))
    return r


def _norm_matmul_kernel(x_ref, nw_ref, w_ref, o_ref, h_ref, *, transposed, last_cols):
    j = pl.program_id(1)
    nj = pl.num_programs(1)

    @pl.when(j == 0)
    def _():
        x = x_ref[...]
        ms = jnp.mean(x * x, axis=-1, keepdims=True)
        h_ref[...] = (x * lax.rsqrt(ms + EPS) * nw_ref[...]).astype(BF16)

    mm = _dot_nt if transposed else _dot
    if last_cols is None:
        o_ref[...] = mm(h_ref[...], w_ref[...].astype(BF16)).astype(o_ref.dtype)
    else:
        @pl.when(j < nj - 1)
        def _():
            o_ref[...] = mm(h_ref[...], w_ref[...].astype(BF16)).astype(o_ref.dtype)

        @pl.when(j == nj - 1)
        def _():
            w = w_ref[:last_cols, :] if transposed else w_ref[:, :last_cols]
            o_ref[:, :last_cols] = mm(h_ref[...], w.astype(BF16)).astype(o_ref.dtype)


def _norm_matmul(x, nw, w, n_out, *, bm, bn, transposed, out_dtype, split_out=False,
                 last_cols=None, name):
    m, k = x.shape
    n = w.shape[0] if transposed else w.shape[1]
    nj = -(-n // bn)
    assert m % bm == 0
    if split_out:
        out_shape = jax.ShapeDtypeStruct((nj, m, bn), out_dtype)
        out_spec = pl.BlockSpec((None, bm, bn), lambda i, j: (j, i, 0))
    else:
        out_shape = jax.ShapeDtypeStruct((m, n_out), out_dtype)
        out_spec = pl.BlockSpec((bm, bn), lambda i, j: (i, j))
    if transposed:
        w_spec = pl.BlockSpec((bn, k), lambda i, j: (j, 0))
    else:
        w_spec = pl.BlockSpec((k, bn), lambda i, j: (0, j))
    return pl.pallas_call(
        functools.partial(_norm_matmul_kernel, transposed=transposed, last_cols=last_cols),
        grid=(m // bm, nj),
        in_specs=[pl.BlockSpec((bm, k), lambda i, j: (i, 0)),
                  pl.BlockSpec((1, k), lambda i, j: (0, 0)),
                  w_spec],
        out_specs=out_spec,
        out_shape=out_shape,
        scratch_shapes=[pltpu.VMEM((bm, k), BF16)],
        compiler_params=pltpu.CompilerParams(
            dimension_semantics=("arbitrary", "arbitrary"), vmem_limit_bytes=VMEM_LIMIT),
        name=name,
    )(x, nw.reshape(1, k), w)


def _mem_kv_kernel(x_ref, nw_ref, w_ref, o_ref, ok_ref, ov_ref):
    x = x_ref[...]
    ms = jnp.mean(x * x, axis=-1, keepdims=True)
    h = (x * lax.rsqrt(ms + EPS) * nw_ref[...]).astype(BF16)
    kv = _dot(h, w_ref[...].astype(BF16))
    o_ref[0] = kv[:, :D_X]
    o_ref[1] = kv[:, D_X:]
    for hd in range(H_X):
        ok_ref[pl.ds(hd, N_MEM, stride=H_X), :] = kv[:, hd * HD_X:(hd + 1) * HD_X]
        ov_ref[pl.ds(hd, N_MEM, stride=H_X), :] = kv[:, D_X + hd * HD_X:D_X + (hd + 1) * HD_X]


def _mem_kv(mem, nw, w, *, batch):
    return pl.pallas_call(
        _mem_kv_kernel,
        grid=(batch,),
        in_specs=[pl.BlockSpec((N_MEM, D_MODEL), lambda i: (i, 0)),
                  pl.BlockSpec((1, D_MODEL), lambda i: (0, 0)),
                  pl.BlockSpec((D_MODEL, 2 * D_X), lambda i: (0, 0))],
        out_specs=[pl.BlockSpec((2, N_MEM, D_X), lambda i: (0, i, 0)),
                   pl.BlockSpec((None, N_MEM * H_X, HD_X), lambda i: (i, 0, 0)),
                   pl.BlockSpec((None, N_MEM * H_X, HD_X), lambda i: (i, 0, 0))],
        out_shape=[jax.ShapeDtypeStruct((2, batch * N_MEM, D_X), F32),
                   jax.ShapeDtypeStruct((batch, N_MEM * H_X, HD_X), F32),
                   jax.ShapeDtypeStruct((batch, N_MEM * H_X, HD_X), F32)],
        compiler_params=pltpu.CompilerParams(
            dimension_semantics=("arbitrary",), vmem_limit_bytes=VMEM_LIMIT),
        name="mem_kv",
    )(mem, nw.reshape(1, D_MODEL), w)


PREP_BN = MXU_COLS
N_MAIN_BLOCKS = C_BG // PREP_BN


GATE_BLOCK = D_PROJ_PAD // PREP_BN - 1
N_IN_BLOCKS = GATE_BLOCK + 1
N_OUT_BLOCKS = D_MODEL // PREP_BN
PREP_BUFS = 4


def _weight_prep_kernel(wt_hbm, wo_hbm, wm_hbm, xs_ref, nw_ref, mem_ref, mnw_ref,
                        oi_ref, oo_ref, ps_ref, mkv_ref, mk_ref, mv_ref,
                        buf, sem, hs_ref, hm_ref, wm_buf, wm_sem):
    r = pl.program_id(0)
    slot = lax.rem(r, PREP_BUFS)
    mem_step0 = N_IN_BLOCKS + N_OUT_BLOCKS

    def mem_copy(j):
        return pltpu.make_async_copy(wm_hbm.at[:, pl.ds(j * PREP_BN, PREP_BN)], wm_buf.at[j],
                                     wm_sem.at[j])

    for j in range(N_MEM_BLOCKS):
        @pl.when(r == N_IN_BLOCKS + j)
        def _(j=j):
            mem_copy(j).start()

    @pl.when(r == N_IN_BLOCKS)
    def _():
        x = mem_ref[...]
        ms = jnp.mean(x * x, axis=-1, keepdims=True)
        hm_ref[...] = (x * lax.rsqrt(ms + EPS) * mnw_ref[...]).astype(BF16)

    for j in range(N_MEM_BLOCKS):
        @pl.when(r == mem_step0 + j)
        def _(j=j):
            mem_copy(j).wait()
            kv = _dot(hm_ref[...], wm_buf[j].astype(BF16))
            mkv_ref[...] = kv.astype(BF16)
            dst = mk_ref if j * PREP_BN < D_X else mv_ref
            for hh in range(PREP_BN // HD_X):
                head = (j * PREP_BN % D_X) // HD_X + hh
                for b in range(kv.shape[0] // N_MEM):
                    dst[b, pl.ds(head, N_MEM, stride=H_X), :] = (
                        kv[b * N_MEM:(b + 1) * N_MEM, hh * HD_X:(hh + 1) * HD_X])

    def in_copy(rr, sl):
        start = pl.multiple_of(jnp.where(rr < N_MAIN_BLOCKS, rr * PREP_BN, rr * PREP_BN + GATE_RANK), 8)
        return pltpu.make_async_copy(wt_hbm.at[pl.ds(start, PREP_BN), :], buf.at[sl], sem.at[sl])

    def gate_copy(sl):
        return pltpu.make_async_copy(wt_hbm.at[pl.ds(C_BG, GATE_RANK), :],
                                     buf.at[sl, pl.ds(0, GATE_RANK), :], sem.at[sl])

    def out_copy(rr, sl):
        start = pl.multiple_of((rr - N_IN_BLOCKS) * PREP_BN, PREP_BN)
        return pltpu.make_async_copy(wo_hbm.at[pl.ds(start, PREP_BN), :], buf.at[sl], sem.at[sl])

    def start_fetch(rr, sl):
        @pl.when(rr < GATE_BLOCK)
        def _():
            in_copy(rr, sl).start()

        @pl.when(rr == GATE_BLOCK)
        def _():
            gate_copy(sl).start()

        @pl.when(rr > GATE_BLOCK)
        def _():
            out_copy(rr, sl).start()

    @pl.when(r == 0)
    def _():
        for ahead in range(PREP_BUFS - 1):
            start_fetch(r + ahead, ahead)

    nxt = r + PREP_BUFS - 1

    @pl.when(nxt < mem_step0)
    def _():
        start_fetch(nxt, lax.rem(nxt, PREP_BUFS))

    @pl.when(r == 0)
    def _():
        x = xs_ref[...]
        ms = jnp.mean(x * x, axis=-1, keepdims=True)
        hs_ref[...] = (x * lax.rsqrt(ms + EPS) * nw_ref[...]).astype(BF16)

    @pl.when(r < GATE_BLOCK)
    def _():
        in_copy(r, slot).wait()
        oi_ref[...] = buf[slot].T.astype(BF16)
        ps_ref[...] = _dot(hs_ref[...], oi_ref[...]).astype(BF16)

    @pl.when(r == GATE_BLOCK)
    def _():
        gate_copy(slot).wait()
        rows = jnp.concatenate([buf[slot, 0:GATE_RANK, :],
                                jnp.zeros((PREP_BN - GATE_RANK, D_MODEL), F32)], axis=0)
        oi_ref[...] = rows.T.astype(BF16)
        ps_ref[...] = _dot(hs_ref[...], oi_ref[...]).astype(BF16)

    @pl.when((r > GATE_BLOCK) & (r < mem_step0))
    def _():
        out_copy(r, slot).wait()
        oo_ref[...] = buf[slot].astype(BF16)


N_MEM_BLOCKS = 2 * D_X // PREP_BN


def _weight_prep(w_t, w_out, w_mem, xs, nw, mem, mnw):
    k = w_t.shape[1]
    m = xs.shape[0]
    mm = mem.shape[0]
    batch = mm // N_MEM
    mem_step0 = N_IN_BLOCKS + N_OUT_BLOCKS
    in_block = lambda r: (0, jnp.minimum(r, GATE_BLOCK))
    mem_j = lambda r: jnp.maximum(r - mem_step0, 0)
    per_d_x = D_X // PREP_BN
    resident = lambda *shape: pl.BlockSpec(shape, lambda r: (0,) * len(shape),
                                           pipeline_mode=pl.Buffered(1))
    return pl.pallas_call(
        _weight_prep_kernel,
        grid=(mem_step0 + N_MEM_BLOCKS,),
        in_specs=[pl.BlockSpec(memory_space=pl.ANY), pl.BlockSpec(memory_space=pl.ANY),
                  pl.BlockSpec(memory_space=pl.ANY),
                  resident(m, k), pl.BlockSpec((1, k), lambda r: (0, 0)),
                  resident(mm, k), pl.BlockSpec((1, k), lambda r: (0, 0))],
        out_specs=[pl.BlockSpec((k, PREP_BN), in_block),
                   pl.BlockSpec((PREP_BN, D_MODEL),
                                lambda r: (jnp.clip(r - N_IN_BLOCKS, 0, N_OUT_BLOCKS - 1), 0)),
                   pl.BlockSpec((m, PREP_BN), in_block),
                   pl.BlockSpec((None, mm, PREP_BN),
                                lambda r: (mem_j(r) // per_d_x, 0, lax.rem(mem_j(r), per_d_x))),
                   pl.BlockSpec((batch, N_MEM * H_X, HD_X), lambda r: (0, 0, 0)),
                   pl.BlockSpec((batch, N_MEM * H_X, HD_X), lambda r: (0, 0, 0))],
        out_shape=[jax.ShapeDtypeStruct((k, D_PROJ_PAD), BF16),
                   jax.ShapeDtypeStruct((D_MODEL, D_MODEL), BF16),
                   jax.ShapeDtypeStruct((m, D_PROJ_PAD), BF16),
                   jax.ShapeDtypeStruct((2, mm, D_X), BF16),
                   jax.ShapeDtypeStruct((batch, N_MEM * H_X, HD_X), F32),
                   jax.ShapeDtypeStruct((batch, N_MEM * H_X, HD_X), F32)],
        scratch_shapes=[pltpu.VMEM((PREP_BUFS, PREP_BN, k), F32),
                        pltpu.SemaphoreType.DMA((PREP_BUFS,)),
                        pltpu.VMEM((m, k), BF16),
                        pltpu.VMEM((mm, k), BF16),
                        pltpu.VMEM((N_MEM_BLOCKS, k, PREP_BN), F32),
                        pltpu.SemaphoreType.DMA((N_MEM_BLOCKS,))],
        compiler_params=pltpu.CompilerParams(
            dimension_semantics=("arbitrary",), vmem_limit_bytes=VMEM_LIMIT),
        name="weight_prep",
    )(w_t, w_out, w_mem, xs, nw.reshape(1, k), mem, mnw.reshape(1, k))


def _out_proj_kernel(br_ref, w_ref, x_ref, fw_ref, y_ref):
    acc = _dot(br_ref[...], w_ref[...]) + x_ref[...]
    ms = jnp.mean(acc * acc, axis=-1, keepdims=True)
    y_ref[...] = acc * lax.rsqrt(ms + EPS) * fw_ref[...]


def _out_proj(br, w, x, fw, *, bm, name):
    m, k = br.shape
    n = w.shape[1]
    return pl.pallas_call(
        _out_proj_kernel,
        grid=(m // bm,),
        in_specs=[pl.BlockSpec((bm, k), lambda i: (i, 0)),
                  pl.BlockSpec((k, n), lambda i: (0, 0)),
                  pl.BlockSpec((bm, n), lambda i: (i, 0)),
                  pl.BlockSpec((1, n), lambda i: (0, 0))],
        out_specs=pl.BlockSpec((bm, n), lambda i: (i, 0)),
        out_shape=jax.ShapeDtypeStruct((m, n), F32),
        compiler_params=pltpu.CompilerParams(
            dimension_semantics=("arbitrary",), vmem_limit_bytes=VMEM_LIMIT),
        name=name,
    )(br, w, x, fw.reshape(1, n))


PT = 256
GLA_ROWS = 128
RELAYOUT_ROWS = 12
RELAYOUT_SLOTS = RELAYOUT_ROWS
VMEM_LIMIT_FUSED = 62 * 1024 * 1024
N_MIXER_STAGES = 2 + 4 * (PT // GLA_ROWS) + 1 + 2 * H_X


def _group_a_chunks(proj_ref, wa_ref, abt_ref, avw_ref):
    v = proj_ref[:, C_AV:C_AV + D_A].astype(F32)
    n_chunks = v.shape[0] // CHUNK_A
    ms = jnp.mean(v * v, axis=-1, keepdims=True)
    vb = (v * lax.rsqrt(ms + EPS) * avw_ref[...]).astype(BF16)
    tril = _iota((CHUNK_A, CHUNK_A), 1) <= _iota((CHUNK_A, CHUNK_A), 0)
    r = []
    for gi in range(H_A):
        w = jnp.where(tril, wa_ref[gi], 0.0).astype(BF16)
        ws = slice(WIN_START[gi], WIN_START[gi] + WIN)
        rhs = jnp.concatenate([vb[c * CHUNK_A:(c + 1) * CHUNK_A, ws] for c in range(n_chunks)], axis=1)
        res = _dot(w, rhs)
        r.append(jnp.concatenate([res[:, c * WIN:(c + 1) * WIN] for c in range(n_chunks)], axis=0))
    mixed = _merge_windows(r)
    col_g = _head_id(_iota((CHUNK_A, D_A), 1), HD_A, H_A)
    bias = jnp.zeros((CHUNK_A, D_A), F32)
    for gi in range(H_A):
        bias = jnp.where(col_g == gi, abt_ref[:, gi:gi + 1], bias)
    bias = jnp.concatenate([bias] * n_chunks, axis=0)
    u = proj_ref[:, C_AU:C_AU + D_A].astype(F32)
    g = proj_ref[:, C_AG:C_AG + D_A].astype(F32)
    return u * (mixed + bias) * _silu(g)


def _prompt_block(proj_ref, out_ref, mk_ref, mv_ref, wa_ref, abt_ref, avw_ref, bwa_ref, bba_ref,
                  onw_ref, sbd_ref, o_ref, la_ref):
    out_ref[:, 0:D_A] = _group_a_chunks(proj_ref, wa_ref, abt_ref, avw_ref).astype(BF16)
    yield

    pre = _dot(proj_ref[:, C_BR:C_BR + LANES], bwa_ref[...]) + bba_ref[...]
    la_ref[...] = _log_sigmoid(pre) * (1.0 / GATE_TAU)
    yield
    gl = GLA_ROWS
    tril_bf = jnp.where(_iota((gl, gl), 1) <= _iota((gl, gl), 0), 1.0, 0.0).astype(BF16)
    win_col = _iota((gl, WIN), 1)
    col_hq = _head_id(_iota((gl, D_BK), 1), DK_B, H_B)
    tril_heads = (_iota((H_B * gl, gl), 1) <= jnp.bitwise_and(_iota((H_B * gl, gl), 0), gl - 1))
    for c in range(PT // gl):
        sl = slice(c * gl, (c + 1) * gl)
        log_a = la_ref[sl, :]
        q = proj_ref[sl, C_BQ:C_BQ + D_BK].astype(F32)
        k = proj_ref[sl, C_BK:C_BK + D_BK].astype(F32)
        vb = proj_ref[sl, C_BV:C_BV + D_B]
        hi, lo = _split(log_a)
        cum2 = _dot(tril_bf, jnp.concatenate([hi, lo], axis=1))
        yield
        cum = cum2[:, :D_BK] + cum2[:, D_BK:]
        tot_row = cum[gl - 1:gl, :]
        mid_row = cum[CHUNK_B - 1:CHUNK_B, :]
        la_t = log_a.T
        q_s = q * (DK_B ** -0.5)
        q_dec = q_s * jnp.exp(cum)
        q_in = q_s * jnp.exp(cum - mid_row)
        k_in = k * jnp.exp(mid_row - cum)
        k_out = k * jnp.exp(tot_row - cum)
        q_heads = jnp.concatenate([jnp.where(col_hq == h, q_in, 0.0) for h in range(H_B)], axis=0)
        sc = _dot_nt(q_heads.astype(BF16), k_in.astype(BF16))
        yield
        sc = jnp.where(tril_heads, sc, 0.0).astype(BF16)
        o_intra = _merge_windows(
            [_dot(sc[h * gl:(h + 1) * gl], vb[:, WIN_START[h]:WIN_START[h] + WIN])
             for h in range(H_B)])
        s_old = sbd_ref[...]
        o_inter = _dot(q_dec.astype(BF16), s_old.astype(BF16))
        yield
        kot = k_out.T.astype(BF16)
        for h in range(H_B):
            rs = slice(h * DK_B, (h + 1) * DK_B)
            ws = slice(WIN_START[h], WIN_START[h] + WIN)
            dec = jnp.exp(jnp.sum(la_t[rs], axis=1, keepdims=True))
            lo_col = h * DV_B - WIN_START[h]
            in_head = (win_col >= lo_col) & (win_col < lo_col + DV_B)
            v_h = jnp.where(in_head, vb[:, ws], jnp.zeros((), BF16))
            kv = _dot(kot[rs], v_h)
            sbd_ref[rs, ws] = s_old[rs, ws] * dec + kv
        o_ref[sl, :] = o_intra + o_inter
        yield
    o = o_ref[...]
    o2 = o * o
    t = [o2[:, j * LANES:(j + 1) * LANES] for j in range(D_B // LANES)]
    lo = _iota((PT, LANES), 1) < 64
    t1a = jnp.where(lo, t[1], 0.0)
    t4a = jnp.where(lo, t[4], 0.0)
    sums = (t[0] + t1a, (t[1] - t1a) + t[2], t[3] + t4a, (t[4] - t4a) + t[5])
    inv = [lax.rsqrt(jnp.sum(x, axis=-1, keepdims=True) * (1.0 / DV_B) + EPS) for x in sums]
    inv = jnp.concatenate(
        [jnp.broadcast_to(inv[0], (PT, LANES)), jnp.where(lo, inv[0], inv[1]),
         jnp.broadcast_to(inv[1], (PT, LANES)), jnp.broadcast_to(inv[2], (PT, LANES)),
         jnp.where(lo, inv[2], inv[3]), jnp.broadcast_to(inv[3], (PT, LANES))], axis=1)
    o_n = o * inv * onw_ref[...]
    bg = proj_ref[:, C_BG:C_BG + D_B].astype(F32)
    out_ref[:, D_A:D_A + D_B] = (o_n * _silu(bg)).astype(BF16)
    yield

    for h in range(H_X):
        hs = slice(h * HD_X, (h + 1) * HD_X)
        qh = proj_ref[:, C_XQ + h * HD_X:C_XQ + (h + 1) * HD_X]
        s = _dot_nt(qh, mk_ref[:, hs]) * (HD_X ** -0.5)
        yield
        e = jnp.exp(s - jnp.max(s, axis=-1, keepdims=True))
        den = jnp.sum(e, axis=-1, keepdims=True)
        ox = _dot(e.astype(BF16), mv_ref[:, hs]) / den
        xg = proj_ref[:, C_XG + h * HD_X:C_XG + (h + 1) * HD_X].astype(F32)
        out_ref[:, D_A + D_B + h * HD_X:D_A + D_B + (h + 1) * HD_X] = (ox * _silu(xg)).astype(BF16)
        yield


def _prompt_layer_kernel(xn_ref, wi_ref, wo_ref, nw_ref, fw_ref, mk_ref, mv_ref, wa_ref, abt_ref,
                         avw_ref, bwa_ref, bba_ref, onw_ref, sti_ref, y_ref, st_ref, stb_hbm,
                         pa_ref, pb_ref, xk_ref, br_ref, h_ref, sbd_ref, o_ref, la_ref,
                         stg_ref, rsem, *, nt):
    s = pl.program_id(0)
    cur = jnp.maximum(s - 1, 0)
    t = lax.rem(cur, nt)

    @pl.when(t == 0)
    def _():
        sbd_ref[...] = jnp.zeros_like(sbd_ref)

    def in_proj_stages(pn_ref):
        x = xn_ref[...]
        ms = jnp.mean(x * x, axis=-1, keepdims=True)
        h_ref[...] = (x * lax.rsqrt(ms + EPS) * nw_ref[...]).astype(BF16)
        yield
        for c0 in range(0, D_PROJ_PAD, MXU_COLS):
            cols = pl.ds(c0, MXU_COLS)
            pn_ref[:, cols] = _dot(h_ref[...], wi_ref[:, cols]).astype(BF16)
            yield
        xk_ref[...] = xn_ref[...]

    def out_proj_stages():
        ssq = jnp.zeros((PT, 1), F32)
        for c0 in range(0, D_MODEL, OUT_BN):
            cols = pl.ds(c0, OUT_BN)
            acc = _dot(br_ref[...], wo_ref[:, cols]) + xk_ref[:, cols]
            y_ref[:, cols] = acc
            ssq = ssq + jnp.sum(acc * acc, axis=-1, keepdims=True)
            yield
        y_ref[...] = y_ref[...] * lax.rsqrt(ssq * (1.0 / D_MODEL) + EPS) * fw_ref[...]
        yield

    def run(order, streams):
        for name in order:
            next(streams[name])
        for name, gen in streams.items():
            assert next(gen, "done") == "done", name

    def mixer_stages(pc_ref):
        return _prompt_block(pc_ref, br_ref, mk_ref, mv_ref, wa_ref, abt_ref, avw_ref, bwa_ref,
                             bba_ref, onw_ref, sbd_ref, o_ref, la_ref)

    n_p = 1 + D_PROJ_PAD // MXU_COLS
    n_m = N_MIXER_STAGES
    n_o = 1 + D_MODEL // OUT_BN

    def relayout_copy(slot, row):
        return pltpu.make_async_copy(stg_ref.at[slot], stb_hbm.at[:, row, :], rsem.at[slot])

    def relayout_stages(first_step):
        base = s * RELAYOUT_ROWS
        if not first_step:
            for i in range(RELAYOUT_ROWS):
                relayout_copy(i, base + i).wait()
        yield
        for i in range(RELAYOUT_ROWS):
            x = sti_ref[i]
            stg_ref[i, :, 0:LANES] = x[0:LANES, :].T
            tail = jnp.concatenate([x[LANES:DV_B, :], jnp.zeros((2 * LANES - DV_B, LANES), F32)],
                                   axis=0).T
            stg_ref[i, :, LANES:DV_B] = tail[:, 0:DV_B - LANES]
            yield
        for i in range(RELAYOUT_ROWS):
            relayout_copy(i, base + i).start()

    def relayout_drain():
        for i in range(RELAYOUT_ROWS):
            relayout_copy(i, 0).wait()
        yield

    n_r = 1 + RELAYOUT_ROWS

    def interleave(counts):
        order = []
        for i in range(max(counts.values())):
            order += [name for name, n in counts.items() if i < n]
        return order

    def body(pn_ref, pc_ref):
        order = ["R", "P"] + interleave({"M": n_m, "P": n_p - 1, "R": n_r - 1}) + ["O"] * n_o
        run(order, {"P": in_proj_stages(pn_ref), "M": mixer_stages(pc_ref), "O": out_proj_stages(),
                    "R": relayout_stages(False)})

    last = pl.num_programs(0) - 1

    @pl.when(s == 0)
    def _():
        run(["P"] + interleave({"P": n_p - 1, "R": n_r}),
            {"P": in_proj_stages(pa_ref), "R": relayout_stages(True)})

    @pl.when((lax.rem(s, 2) == 0) & (s > 0) & (s < last))
    def _():
        body(pa_ref, pb_ref)

    @pl.when(lax.rem(s, 2) == 1)
    def _():
        body(pb_ref, pa_ref)

    @pl.when(s == last)
    def _():
        run(["R"] + ["M"] * n_m + ["O"] * n_o,
            {"M": mixer_stages(pb_ref), "O": out_proj_stages(), "R": relayout_drain()})

    @pl.when((t == nt - 1) & (s > 0))
    def _():
        for h in range(H_B):
            off = h * DV_B - WIN_START[h]
            blk = sbd_ref[h * DK_B:(h + 1) * DK_B, WIN_START[h]:WIN_START[h] + WIN]
            if off:
                blk = pltpu.roll(blk, WIN - off, 1)
            st_ref[0, h] = blk[:, :DV_B]


def _prompt_layer(xp, w_in_bf, w_out_bf, nw, fw, memkv, wa, abt, avw, bwa, bba, onw, state_t, *,
                  batch, seq):
    nt = seq // PT
    nblk = batch * nt
    assert nblk % 2 == 0
    n_rows, _, n_seq = state_t.shape
    assert n_rows == nblk * RELAYOUT_ROWS and n_seq == LANES
    cur = lambda s: jnp.maximum(s - 1, 0)
    const = lambda *shape: pl.BlockSpec(shape, lambda s: (0,) * len(shape))
    resident = lambda *shape: pl.BlockSpec(shape, lambda s: (0,) * len(shape),
                                           pipeline_mode=pl.Buffered(1))
    return pl.pallas_call(
        functools.partial(_prompt_layer_kernel, nt=nt),
        grid=(nblk + 1,),
        in_specs=[pl.BlockSpec((PT, D_MODEL), lambda s: (jnp.minimum(s, nblk - 1), 0)),
                  resident(D_MODEL, D_PROJ_PAD), resident(D_MODEL, D_MODEL),
                  const(1, D_MODEL), const(1, D_MODEL),
                  pl.BlockSpec((None, N_MEM, D_X), lambda s: (0, cur(s) // nt, 0),
                               pipeline_mode=pl.Buffered(1)),
                  pl.BlockSpec((None, N_MEM, D_X), lambda s: (1, cur(s) // nt, 0),
                               pipeline_mode=pl.Buffered(1)),
                  const(H_A, CHUNK_A, CHUNK_A), const(CHUNK_A, H_A), const(1, D_A),
                  const(LANES, D_BK), const(1, D_BK), const(1, D_B),
                  pl.BlockSpec((RELAYOUT_ROWS, DV_B, LANES),
                               lambda s: (jnp.minimum(s, nblk - 1), 0, 0))],
        out_specs=[pl.BlockSpec((PT, D_MODEL), lambda s: (cur(s), 0)),
                   pl.BlockSpec((1, H_B, DK_B, DV_B), lambda s: (cur(s) // nt, 0, 0, 0)),
                   pl.BlockSpec(memory_space=pl.ANY)],
        out_shape=[jax.ShapeDtypeStruct((batch * seq, D_MODEL), F32),
                   jax.ShapeDtypeStruct((batch, H_B, DK_B, DV_B), F32),
                   jax.ShapeDtypeStruct((n_seq, n_rows, DV_B), F32)],
        scratch_shapes=[pltpu.VMEM((PT, D_PROJ_PAD), BF16),
                        pltpu.VMEM((PT, D_PROJ_PAD), BF16),
                        pltpu.VMEM((PT, D_MODEL), F32),
                        pltpu.VMEM((PT, D_MODEL), BF16),
                        pltpu.VMEM((PT, D_MODEL), BF16),
                        pltpu.VMEM((D_BK, D_B), F32),
                        pltpu.VMEM((PT, D_B), F32),
                        pltpu.VMEM((PT, D_BK), F32),
                        pltpu.VMEM((RELAYOUT_SLOTS, LANES, DV_B), F32),
                        pltpu.SemaphoreType.DMA((RELAYOUT_SLOTS,))],
        compiler_params=pltpu.CompilerParams(
            dimension_semantics=("arbitrary",), vmem_limit_bytes=VMEM_LIMIT_FUSED),
        name="prompt_layer",
    )(xp, w_in_bf, w_out_bf, nw.reshape(1, D_MODEL), fw.reshape(1, D_MODEL), memkv, memkv,
      wa, abt, avw, bwa, bba, onw, state_t)


NS = 8
TS = 8
SB = NS * TS
SBP = 128


def _sample_mixer_kernel(proj_ref, st_ref, ck_ref, cv_ref, wa_ref, abt_ref, avw_ref, bwa_ref,
                         bba_ref, onw_ref, out_ref, stn_ref, cvs_ref,
                         qin_ref, xq_ref, kot_ref, lat_ref, vhm_ref, ghm_ref, ohm_ref, ox_ref):
    ri = _iota((SB, SB), 0)
    ci = _iota((SB, SB), 1)
    same_seq = jnp.right_shift(ri, 3) == jnp.right_shift(ci, 3)
    causal = same_seq & (ci <= ri)

    a_br, vn = _group_a(proj_ref, wa_ref, abt_ref, avw_ref, causal)
    out_ref[:, 0:D_A] = a_br.astype(BF16)
    cvs_ref[...] = vn

    q = proj_ref[:, C_BQ:C_BQ + D_BK].astype(F32)
    k = proj_ref[:, C_BK:C_BK + D_BK].astype(F32)
    vb = proj_ref[:, C_BV:C_BV + D_B]
    bgb = proj_ref[:, C_BG:C_BG + D_B]
    pre = _dot(proj_ref[:, C_BR:C_BR + LANES], bwa_ref[...]) + bba_ref[...]
    log_a = _log_sigmoid(pre) * (1.0 / GATE_TAU)
    causal_bf = jnp.where(causal, 1.0, 0.0).astype(BF16)
    seq_bf = jnp.where(same_seq, 1.0, 0.0).astype(BF16)
    hi, lo = _split(log_a)
    cum = _dot(causal_bf, hi) + _dot(causal_bf, lo)
    tot = _dot(seq_bf, hi) + _dot(seq_bf, lo)
    q_in = q * (DK_B ** -0.5) * jnp.exp(cum)
    k_in = k * jnp.exp(-cum)
    k_out = k * jnp.exp(tot - cum)
    qin_ref[...] = q_in
    zpad = jnp.zeros((SBP - SB, D_BK), F32)
    kot_ref[...] = jnp.concatenate([k_out, zpad], axis=0).T.astype(BF16)
    lat_ref[...] = jnp.concatenate([log_a, zpad], axis=0).T
    xq_ref[...] = proj_ref[:, C_XQ:C_XQ + D_X].astype(F32)

    wins = _gla_intra_windows(q_in, k_in, vb, causal)
    sel_r = _iota((D_B, DV_B), 0)
    sel_c = _iota((D_B, DV_B), 1)
    vhm_ref[...] = jnp.zeros_like(vhm_ref)
    for h in range(H_B):
        off = h * DV_B - WIN_START[h]
        w = wins[h]
        if off:
            w = pltpu.roll(w, WIN - off, 1)
        ohm_ref[h] = w[:, :DV_B]
        sel = jnp.where(sel_r == sel_c + h * DV_B, 1.0, 0.0).astype(BF16)
        vhm_ref[h, 0:SB, :] = _dot(vb, sel).astype(BF16)
        ghm_ref[h] = _dot(bgb, sel)

    mask_x = (jnp.right_shift(_iota((H_X * TS, N_MEM * H_X), 0), 3)
              == jnp.bitwise_and(_iota((H_X * TS, N_MEM * H_X), 1), H_X - 1))
    mask_b = jnp.right_shift(_iota((H_B * TS, D_BK), 0), 3) == _head_id(_iota((H_B * TS, D_BK), 1), DK_B, H_B)
    lane_seq = jnp.right_shift(_iota((D_BK, SBP), 1), 3)

    def per_seq(s, carry):
        r0 = pl.multiple_of(s * TS, TS)
        q8 = xq_ref[pl.ds(r0, TS), :]
        q32 = jnp.concatenate([q8[:, h * HD_X:(h + 1) * HD_X] for h in range(H_X)], axis=0)
        sc = _dot_nt(q32.astype(BF16), ck_ref[s].astype(BF16)) * (HD_X ** -0.5)
        sc = jnp.where(mask_x, sc, -1e30)
        e = jnp.exp(sc - jnp.max(sc, axis=-1, keepdims=True))
        den = jnp.sum(e, axis=-1, keepdims=True)
        o = _dot(e.astype(BF16), cv_ref[s].astype(BF16)) / den
        ox_ref[pl.ds(r0, TS), :] = jnp.concatenate(
            [o[h * TS:(h + 1) * TS] for h in range(H_X)], axis=1)
        qi8 = qin_ref[pl.ds(r0, TS), :]
        qbd2 = jnp.where(mask_b, jnp.concatenate([qi8] * H_B, axis=0), 0.0).astype(BF16)
        s0 = jnp.concatenate([st_ref[s, h] for h in range(H_B)], axis=0)
        o_inter = _dot(qbd2, s0.astype(BF16))
        for h in range(H_B):
            ohm_ref[h, pl.ds(r0, TS), :] += o_inter[h * TS:(h + 1) * TS]
        dec = jnp.exp(jnp.sum(jnp.where(lane_seq == s, lat_ref[...], 0.0), axis=1, keepdims=True))
        kot = jnp.where(lane_seq == s, kot_ref[...], jnp.zeros((), BF16))
        kv = jnp.concatenate(
            [_dot(kot[h * DK_B:(h + 1) * DK_B], vhm_ref[h]) for h in range(H_B)], axis=0)
        s_new = s0 * dec + kv
        for h in range(H_B):
            stn_ref[s, h] = s_new[h * DK_B:(h + 1) * DK_B]
        return carry

    lax.fori_loop(0, NS, per_seq, 0, unroll=4)

    selt_r = _iota((DV_B, D_B), 0)
    selt_c = _iota((DV_B, D_B), 1)
    b_br = jnp.zeros((SB, D_B), F32)
    for h in range(H_B):
        o_h = ohm_ref[h]
        ms = jnp.mean(o_h * o_h, axis=-1, keepdims=True)
        ob = (o_h * lax.rsqrt(ms + EPS) * onw_ref[...] * _silu(ghm_ref[h])).astype(BF16)
        selt = jnp.where(selt_c == selt_r + h * DV_B, 1.0, 0.0).astype(BF16)
        b_br = b_br + _dot(ob, selt)
    out_ref[:, D_A:D_A + D_B] = b_br.astype(BF16)

    xg = proj_ref[:, C_XG:C_XG + D_X].astype(F32)
    out_ref[:, D_A + D_B:D_MODEL] = (ox_ref[...] * _silu(xg)).astype(BF16)


def _sample_mixer(proj, state, ck, cv, wa, abt, avw, bwa, bba, onw):
    nseq = state.shape[1]
    const = lambda *shape: pl.BlockSpec(shape, lambda i: (0,) * len(shape))
    return pl.pallas_call(
        _sample_mixer_kernel,
        grid=(nseq // NS,),
        in_specs=[pl.BlockSpec((SB, D_PROJ), lambda i: (i, 0)),
                  pl.BlockSpec((None, NS, H_B, DK_B, DV_B), lambda i: (0, i, 0, 0, 0)),
                  pl.BlockSpec((NS, N_MEM * H_X, HD_X), lambda i: (i, 0, 0)),
                  pl.BlockSpec((NS, N_MEM * H_X, HD_X), lambda i: (i, 0, 0)),
                  const(H_A, SB, SB), const(SB, H_A), const(1, D_A),
                  const(LANES, D_BK), const(1, D_BK), const(1, DV_B)],
        out_specs=[pl.BlockSpec((SB, D_MODEL), lambda i: (i, 0)),
                   pl.BlockSpec((None, NS, H_B, DK_B, DV_B), lambda i: (0, i, 0, 0, 0)),
                   pl.BlockSpec((SB, D_A), lambda i: (i, 0))],
        out_shape=[jax.ShapeDtypeStruct((nseq * TS, D_MODEL), BF16),
                   jax.ShapeDtypeStruct((1, nseq, H_B, DK_B, DV_B), F32),
                   jax.ShapeDtypeStruct((nseq * TS, D_A), F32)],
        scratch_shapes=[pltpu.VMEM((SB, D_BK), F32),
                        pltpu.VMEM((SB, D_X), F32),
                        pltpu.VMEM((D_BK, SBP), BF16),
                        pltpu.VMEM((D_BK, SBP), F32),
                        pltpu.VMEM((H_B, SBP, DV_B), BF16),
                        pltpu.VMEM((H_B, SB, DV_B), F32),
                        pltpu.VMEM((H_B, SB, DV_B), F32),
                        pltpu.VMEM((SB, D_X), F32)],
        compiler_params=pltpu.CompilerParams(
            dimension_semantics=("arbitrary",), vmem_limit_bytes=VMEM_LIMIT),
        name="sample_mixer",
    )(proj, state, ck, cv, wa, abt, avw, bwa, bba, onw)


def kernel(x_prompt, x_sample, mem_prompt, state_gla, cache_mem_k, cache_mem_v, norm_w, w_in,
           a_vnorm_w, a_ws, a_bs, b_wa, b_ba, b_onorm_w, mem_norm_w, w_mem_kv, w_out, final_norm_w):
    batch, seq, _ = x_prompt.shape
    nseq, tdec, _ = x_sample.shape
    depth = w_in.shape[0]
    assert depth == 1 and tdec == TS and seq % PT == 0 and nseq % NS == 0

    xp = x_prompt.reshape(batch * seq, D_MODEL)
    xs = x_sample.reshape(nseq * TS, D_MODEL)
    mem = mem_prompt.reshape(batch * N_MEM, D_MODEL)
    w_in_bf, w_out_bf, proj_s, memkv, mem_k, mem_v = _weight_prep(
        jnp.transpose(w_in[0]), w_out[0], w_mem_kv[0], xs, norm_w[0], mem, mem_norm_w[0])
    bwa = jnp.concatenate([b_wa[0], jnp.zeros((LANES - GATE_RANK, D_BK), F32)], axis=0).astype(BF16)
    bba = b_ba[0].reshape(1, D_BK)
    avw = a_vnorm_w[0].reshape(1, D_A)
    onw_p = jnp.tile(b_onorm_w[0], H_B).reshape(1, D_B)
    onw_s = b_onorm_w[0].reshape(1, DV_B)
    wa_p = a_ws[0]
    abt_p = a_bs[0].T
    wa_s = jnp.tile(a_ws[0][:, :TS, :TS], (1, NS, NS))
    abt_s = jnp.tile(a_bs[0][:, :TS], (1, NS)).T

    state_t = jnp.transpose(state_gla[0], (1, 2, 3, 0)).reshape(H_B * DK_B, DV_B, nseq)
    y_p, st_p, state_bm = _prompt_layer(xp, w_in_bf, w_out_bf, norm_w[0], final_norm_w, memkv,
                                        wa_p, abt_p, avw, bwa, bba, onw_p, state_t,
                                        batch=batch, seq=seq)

    br_s, st_s, cvs = _sample_mixer(
        proj_s, state_bm.reshape(1, nseq, H_B, DK_B, DV_B),
        cache_mem_k.reshape(nseq, N_MEM * H_X, HD_X), cache_mem_v.reshape(nseq, N_MEM * H_X, HD_X),
        wa_s, abt_s, avw, bwa, bba, onw_s)
    y_s = _out_proj(br_s, w_out_bf, xs, final_norm_w, bm=256, name="out_proj_s")

    return (y_p.reshape(batch, seq, D_MODEL),
            y_s.reshape(nseq, TS, D_MODEL),
            mem_k.reshape(1, batch, N_MEM, H_X, HD_X),
            mem_v.reshape(1, batch, N_MEM, H_X, HD_X),
            st_p.reshape(1, batch, H_B, DK_B, DV_B),
            st_s,
            cvs.reshape(1, nseq, TS, D_A))
```

```python
import functools

import jax
import jax.numpy as jnp
from jax import lax
from jax.experimental import pallas as pl
from jax.experimental.pallas import tpu as pltpu

F32 = jnp.float32
BF16 = jnp.bfloat16

D_MODEL = 2048
D_A = 768
H_A = 4
HD_A = 192
CHUNK_A = 128
D_B = 768
H_B = 4
DV_B = 192
DK_B = 96
D_BK = 384
GATE_RANK = 16
GATE_TAU = 16.0
CHUNK_B = 64
D_X = 512
H_X = 4
HD_X = 128
N_MEM = 256
EPS = 1e-6

LANES = 128
MXU_COLS = 256
C_AU, C_AV, C_AG = 0, 768, 1536
C_BQ, C_BK, C_BV, C_BG = 2304, 2688, 3072, 3840
C_XQ, C_XG, C_BR = 4608, 5120, 5632
D_PROJ = 5760
D_PROJ_PAD = 5888
WIN_START = (0, 128, 384, 512)
WIN = 256
OUT_BN = 512

VMEM_LIMIT = 60 * 1024 * 1024


def _dot(a, b):
    return jnp.dot(a, b, preferred_element_type=F32)


def _dot_nt(a, b):
    return lax.dot_general(a, b, (((1,), (1,)), ((), ())), preferred_element_type=F32)


def _split(x):
    hi = x.astype(BF16)
    lo = (x - hi.astype(F32)).astype(BF16)
    return hi, lo


def _silu(x):
    return x / (1.0 + jnp.exp(-x))


def _log_sigmoid(x):
    return jnp.minimum(x, 0.0) - jnp.log1p(jnp.exp(-jnp.abs(x)))


def _head_id(idx, width, n):
    h = jnp.zeros_like(idx)
    for i in range(1, n):
        h = h + (idx >= i * width).astype(jnp.int32)
    return h


def _iota(shape, dim):
    return lax.broadcasted_iota(jnp.int32, shape, dim)


def _merge_windows(r):
    rows = r[0].shape[0]
    lo = _iota((rows, LANES), 1) < 64
    tiles = [r[0][:, :LANES], jnp.where(lo, r[0][:, LANES:], r[1][:, :LANES]), r[1][:, LANES:],
             r[2][:, :LANES], jnp.where(lo, r[2][:, LANES:], r[3][:, :LANES]), r[3][:, LANES:]]
    return jnp.concatenate(tiles, axis=1)


def _group_a(proj_ref, wa_ref, abt_ref, avw_ref, mask):
    u = proj_ref[:, C_AU:C_AU + D_A].astype(F32)
    v = proj_ref[:, C_AV:C_AV + D_A].astype(F32)
    g = proj_ref[:, C_AG:C_AG + D_A].astype(F32)
    rows = u.shape[0]
    ms = jnp.mean(v * v, axis=-1, keepdims=True)
    vn = v * lax.rsqrt(ms + EPS) * avw_ref[...]
    vb = vn.astype(BF16)
    r = []
    for gi in range(H_A):
        w = jnp.where(mask, wa_ref[gi], 0.0).astype(BF16)
        r.append(_dot(w, vb[:, WIN_START[gi]:WIN_START[gi] + WIN]))
    mixed = _merge_windows(r)
    col_g = _head_id(_iota((rows, D_A), 1), HD_A, H_A)
    bias = jnp.zeros((rows, D_A), F32)
    for gi in range(H_A):
        bias = jnp.where(col_g == gi, abt_ref[:, gi:gi + 1], bias)
    return u * (mixed + bias) * _silu(g), vn


def _gla_intra_windows(q_in, k_in, vb, mask):
    rows = q_in.shape[0]
    col_h = _head_id(_iota((rows, D_BK), 1), DK_B, H_B)
    kb = k_in.astype(BF16)
    r = []
    for h in range(H_B):
        qh = jnp.where(col_h == h, q_in, 0.0).astype(BF16)
        s = jnp.where(mask, _dot_nt(qh, kb), 0.0).astype(BF16)
        r.append(_dot(s, vb[:, WIN_START[h]:WIN_START[h] + WIN]))
    return r


PREP_BN = MXU_COLS
N_MAIN_BLOCKS = C_BG // PREP_BN


GATE_BLOCK = D_PROJ_PAD // PREP_BN - 1
N_IN_BLOCKS = GATE_BLOCK + 1
N_OUT_BLOCKS = D_MODEL // PREP_BN
PREP_BUFS = 4


def _weight_prep_kernel(wt_hbm, wo_hbm, wm_hbm, xs_ref, nw_ref, mem_ref, mnw_ref,
                        oi_ref, oo_ref, ps_ref, mkv_ref, mk_ref, mv_ref,
                        buf, sem, hs_ref, hm_ref, wm_buf, wm_sem):
    r = pl.program_id(0)
    slot = lax.rem(r, PREP_BUFS)
    mem_step0 = N_IN_BLOCKS + N_OUT_BLOCKS

    def mem_copy(j):
        return pltpu.make_async_copy(wm_hbm.at[:, pl.ds(j * PREP_BN, PREP_BN)], wm_buf.at[j],
                                     wm_sem.at[j])

    for j in range(N_MEM_BLOCKS):
        @pl.when(r == N_IN_BLOCKS + j)
        def _(j=j):
            mem_copy(j).start()

    @pl.when(r == N_IN_BLOCKS)
    def _():
        x = mem_ref[...]
        ms = jnp.mean(x * x, axis=-1, keepdims=True)
        hm_ref[...] = (x * lax.rsqrt(ms + EPS) * mnw_ref[...]).astype(BF16)

    for j in range(N_MEM_BLOCKS):
        @pl.when(r == mem_step0 + j)
        def _(j=j):
            mem_copy(j).wait()
            kv = _dot(hm_ref[...], wm_buf[j].astype(BF16))
            mkv_ref[...] = kv.astype(BF16)
            dst = mk_ref if j * PREP_BN < D_X else mv_ref
            for hh in range(PREP_BN // HD_X):
                head = (j * PREP_BN % D_X) // HD_X + hh
                for b in range(kv.shape[0] // N_MEM):
                    dst[b, pl.ds(head, N_MEM, stride=H_X), :] = (
                        kv[b * N_MEM:(b + 1) * N_MEM, hh * HD_X:(hh + 1) * HD_X])

    def in_copy(rr, sl):
        start = pl.multiple_of(jnp.where(rr < N_MAIN_BLOCKS, rr * PREP_BN, rr * PREP_BN + GATE_RANK), 8)
        return pltpu.make_async_copy(wt_hbm.at[pl.ds(start, PREP_BN), :], buf.at[sl], sem.at[sl])

    def gate_copy(sl):
        return pltpu.make_async_copy(wt_hbm.at[pl.ds(C_BG, GATE_RANK), :],
                                     buf.at[sl, pl.ds(0, GATE_RANK), :], sem.at[sl])

    def out_copy(rr, sl):
        start = pl.multiple_of((rr - N_IN_BLOCKS) * PREP_BN, PREP_BN)
        return pltpu.make_async_copy(wo_hbm.at[pl.ds(start, PREP_BN), :], buf.at[sl], sem.at[sl])

    def start_fetch(rr, sl):
        @pl.when(rr < GATE_BLOCK)
        def _():
            in_copy(rr, sl).start()

        @pl.when(rr == GATE_BLOCK)
        def _():
            gate_copy(sl).start()

        @pl.when(rr > GATE_BLOCK)
        def _():
            out_copy(rr, sl).start()

    @pl.when(r == 0)
    def _():
        for ahead in range(PREP_BUFS - 1):
            start_fetch(r + ahead, ahead)

    nxt = r + PREP_BUFS - 1

    @pl.when(nxt < mem_step0)
    def _():
        start_fetch(nxt, lax.rem(nxt, PREP_BUFS))

    @pl.when(r == 0)
    def _():
        x = xs_ref[...]
        ms = jnp.mean(x * x, axis=-1, keepdims=True)
        hs_ref[...] = (x * lax.rsqrt(ms + EPS) * nw_ref[...]).astype(BF16)

    @pl.when(r < GATE_BLOCK)
    def _():
        in_copy(r, slot).wait()
        oi_ref[...] = buf[slot].T.astype(BF16)
        ps_ref[...] = _dot(hs_ref[...], oi_ref[...]).astype(BF16)

    @pl.when(r == GATE_BLOCK)
    def _():
        gate_copy(slot).wait()
        rows = jnp.concatenate([buf[slot, 0:GATE_RANK, :],
                                jnp.zeros((PREP_BN - GATE_RANK, D_MODEL), F32)], axis=0)
        oi_ref[...] = rows.T.astype(BF16)
        ps_ref[...] = _dot(hs_ref[...], oi_ref[...]).astype(BF16)

    @pl.when((r > GATE_BLOCK) & (r < mem_step0))
    def _():
        out_copy(r, slot).wait()
        oo_ref[...] = buf[slot].astype(BF16)


N_MEM_BLOCKS = 2 * D_X // PREP_BN


def _weight_prep(w_t, w_out, w_mem, xs, nw, mem, mnw):
    k = w_t.shape[1]
    m = xs.shape[0]
    mm = mem.shape[0]
    batch = mm // N_MEM
    mem_step0 = N_IN_BLOCKS + N_OUT_BLOCKS
    in_block = lambda r: (0, jnp.minimum(r, GATE_BLOCK))
    mem_j = lambda r: jnp.maximum(r - mem_step0, 0)
    per_d_x = D_X // PREP_BN
    resident = lambda *shape: pl.BlockSpec(shape, lambda r: (0,) * len(shape),
                                           pipeline_mode=pl.Buffered(1))
    return pl.pallas_call(
        _weight_prep_kernel,
        grid=(mem_step0 + N_MEM_BLOCKS,),
        in_specs=[pl.BlockSpec(memory_space=pl.ANY), pl.BlockSpec(memory_space=pl.ANY),
                  pl.BlockSpec(memory_space=pl.ANY),
                  resident(m, k), pl.BlockSpec((1, k), lambda r: (0, 0)),
                  resident(mm, k), pl.BlockSpec((1, k), lambda r: (0, 0))],
        out_specs=[pl.BlockSpec((k, PREP_BN), in_block),
                   pl.BlockSpec((PREP_BN, D_MODEL),
                                lambda r: (jnp.clip(r - N_IN_BLOCKS, 0, N_OUT_BLOCKS - 1), 0)),
                   pl.BlockSpec((m, PREP_BN), in_block),
                   pl.BlockSpec((None, mm, PREP_BN),
                                lambda r: (mem_j(r) // per_d_x, 0, lax.rem(mem_j(r), per_d_x))),
                   pl.BlockSpec((batch, N_MEM * H_X, HD_X), lambda r: (0, 0, 0)),
                   pl.BlockSpec((batch, N_MEM * H_X, HD_X), lambda r: (0, 0, 0))],
        out_shape=[jax.ShapeDtypeStruct((k, D_PROJ_PAD), BF16),
                   jax.ShapeDtypeStruct((D_MODEL, D_MODEL), BF16),
                   jax.ShapeDtypeStruct((m, D_PROJ_PAD), BF16),
                   jax.ShapeDtypeStruct((2, mm, D_X), BF16),
                   jax.ShapeDtypeStruct((batch, N_MEM * H_X, HD_X), F32),
                   jax.ShapeDtypeStruct((batch, N_MEM * H_X, HD_X), F32)],
        scratch_shapes=[pltpu.VMEM((PREP_BUFS, PREP_BN, k), F32),
                        pltpu.SemaphoreType.DMA((PREP_BUFS,)),
                        pltpu.VMEM((m, k), BF16),
                        pltpu.VMEM((mm, k), BF16),
                        pltpu.VMEM((N_MEM_BLOCKS, k, PREP_BN), F32),
                        pltpu.SemaphoreType.DMA((N_MEM_BLOCKS,))],
        compiler_params=pltpu.CompilerParams(
            dimension_semantics=("arbitrary",), vmem_limit_bytes=VMEM_LIMIT),
        name="weight_prep",
    )(w_t, w_out, w_mem, xs, nw.reshape(1, k), mem, mnw.reshape(1, k))


def _out_proj_kernel(br_ref, w_ref, x_ref, fw_ref, y_ref):
    acc = _dot(br_ref[...], w_ref[...]) + x_ref[...]
    ms = jnp.mean(acc * acc, axis=-1, keepdims=True)
    y_ref[...] = acc * lax.rsqrt(ms + EPS) * fw_ref[...]


def _out_proj(br, w, x, fw, *, bm, name):
    m, k = br.shape
    n = w.shape[1]
    return pl.pallas_call(
        _out_proj_kernel,
        grid=(m // bm,),
        in_specs=[pl.BlockSpec((bm, k), lambda i: (i, 0)),
                  pl.BlockSpec((k, n), lambda i: (0, 0)),
                  pl.BlockSpec((bm, n), lambda i: (i, 0)),
                  pl.BlockSpec((1, n), lambda i: (0, 0))],
        out_specs=pl.BlockSpec((bm, n), lambda i: (i, 0)),
        out_shape=jax.ShapeDtypeStruct((m, n), F32),
        compiler_params=pltpu.CompilerParams(
            dimension_semantics=("arbitrary",), vmem_limit_bytes=VMEM_LIMIT),
        name=name,
    )(br, w, x, fw.reshape(1, n))


PT = 256
GLA_ROWS = 128
N_MIXER_STAGES = 2 + 4 * (PT // GLA_ROWS) + 1 + 2 * H_X


def _group_a_chunks(proj_ref, wa_ref, abt_ref, avw_ref):
    v = proj_ref[:, C_AV:C_AV + D_A].astype(F32)
    n_chunks = v.shape[0] // CHUNK_A
    ms = jnp.mean(v * v, axis=-1, keepdims=True)
    vb = (v * lax.rsqrt(ms + EPS) * avw_ref[...]).astype(BF16)
    tril = _iota((CHUNK_A, CHUNK_A), 1) <= _iota((CHUNK_A, CHUNK_A), 0)
    r = []
    for gi in range(H_A):
        w = jnp.where(tril, wa_ref[gi], 0.0).astype(BF16)
        ws = slice(WIN_START[gi], WIN_START[gi] + WIN)
        rhs = jnp.concatenate([vb[c * CHUNK_A:(c + 1) * CHUNK_A, ws] for c in range(n_chunks)], axis=1)
        res = _dot(w, rhs)
        r.append(jnp.concatenate([res[:, c * WIN:(c + 1) * WIN] for c in range(n_chunks)], axis=0))
    mixed = _merge_windows(r)
    col_g = _head_id(_iota((CHUNK_A, D_A), 1), HD_A, H_A)
    bias = jnp.zeros((CHUNK_A, D_A), F32)
    for gi in range(H_A):
        bias = jnp.where(col_g == gi, abt_ref[:, gi:gi + 1], bias)
    bias = jnp.concatenate([bias] * n_chunks, axis=0)
    u = proj_ref[:, C_AU:C_AU + D_A].astype(F32)
    g = proj_ref[:, C_AG:C_AG + D_A].astype(F32)
    return u * (mixed + bias) * _silu(g)


def _prompt_block(proj_ref, out_ref, mk_ref, mv_ref, wa_ref, abt_ref, avw_ref, bwa_ref, bba_ref,
                  onw_ref, sbd_ref, o_ref, la_ref):
    out_ref[:, 0:D_A] = _group_a_chunks(proj_ref, wa_ref, abt_ref, avw_ref).astype(BF16)
    yield

    pre = _dot(proj_ref[:, C_BR:C_BR + LANES], bwa_ref[...]) + bba_ref[...]
    la_ref[...] = _log_sigmoid(pre) * (1.0 / GATE_TAU)
    yield
    gl = GLA_ROWS
    tril_bf = jnp.where(_iota((gl, gl), 1) <= _iota((gl, gl), 0), 1.0, 0.0).astype(BF16)
    win_col = _iota((gl, WIN), 1)
    col_hq = _head_id(_iota((gl, D_BK), 1), DK_B, H_B)
    tril_heads = (_iota((H_B * gl, gl), 1) <= jnp.bitwise_and(_iota((H_B * gl, gl), 0), gl - 1))
    for c in range(PT // gl):
        sl = slice(c * gl, (c + 1) * gl)
        log_a = la_ref[sl, :]
        q = proj_ref[sl, C_BQ:C_BQ + D_BK].astype(F32)
        k = proj_ref[sl, C_BK:C_BK + D_BK].astype(F32)
        vb = proj_ref[sl, C_BV:C_BV + D_B]
        hi, lo = _split(log_a)
        cum2 = _dot(tril_bf, jnp.concatenate([hi, lo], axis=1))
        yield
        cum = cum2[:, :D_BK] + cum2[:, D_BK:]
        tot_row = cum[gl - 1:gl, :]
        mid_row = cum[CHUNK_B - 1:CHUNK_B, :]
        la_t = log_a.T
        q_s = q * (DK_B ** -0.5)
        q_dec = q_s * jnp.exp(cum)
        q_in = q_s * jnp.exp(cum - mid_row)
        k_in = k * jnp.exp(mid_row - cum)
        k_out = k * jnp.exp(tot_row - cum)
        q_heads = jnp.concatenate([jnp.where(col_hq == h, q_in, 0.0) for h in range(H_B)], axis=0)
        sc = _dot_nt(q_heads.astype(BF16), k_in.astype(BF16))
        yield
        sc = jnp.where(tril_heads, sc, 0.0).astype(BF16)
        o_intra = _merge_windows(
            [_dot(sc[h * gl:(h + 1) * gl], vb[:, WIN_START[h]:WIN_START[h] + WIN])
             for h in range(H_B)])
        s_old = sbd_ref[...]
        o_inter = _dot(q_dec.astype(BF16), s_old.astype(BF16))
        yield
        kot = k_out.T.astype(BF16)
        for h in range(H_B):
            rs = slice(h * DK_B, (h + 1) * DK_B)
            ws = slice(WIN_START[h], WIN_START[h] + WIN)
            dec = jnp.exp(jnp.sum(la_t[rs], axis=1, keepdims=True))
            lo_col = h * DV_B - WIN_START[h]
            in_head = (win_col >= lo_col) & (win_col < lo_col + DV_B)
            v_h = jnp.where(in_head, vb[:, ws], jnp.zeros((), BF16))
            kv = _dot(kot[rs], v_h)
            sbd_ref[rs, ws] = s_old[rs, ws] * dec + kv
        o_ref[sl, :] = o_intra + o_inter
        yield
    o = o_ref[...]
    o2 = o * o
    t = [o2[:, j * LANES:(j + 1) * LANES] for j in range(D_B // LANES)]
    lo = _iota((PT, LANES), 1) < 64
    t1a = jnp.where(lo, t[1], 0.0)
    t4a = jnp.where(lo, t[4], 0.0)
    sums = (t[0] + t1a, (t[1] - t1a) + t[2], t[3] + t4a, (t[4] - t4a) + t[5])
    inv = [lax.rsqrt(jnp.sum(x, axis=-1, keepdims=True) * (1.0 / DV_B) + EPS) for x in sums]
    inv = jnp.concatenate(
        [jnp.broadcast_to(inv[0], (PT, LANES)), jnp.where(lo, inv[0], inv[1]),
         jnp.broadcast_to(inv[1], (PT, LANES)), jnp.broadcast_to(inv[2], (PT, LANES)),
         jnp.where(lo, inv[2], inv[3]), jnp.broadcast_to(inv[3], (PT, LANES))], axis=1)
    o_n = o * inv * onw_ref[...]
    bg = proj_ref[:, C_BG:C_BG + D_B].astype(F32)
    out_ref[:, D_A:D_A + D_B] = (o_n * _silu(bg)).astype(BF16)
    yield

    for h in range(H_X):
        hs = slice(h * HD_X, (h + 1) * HD_X)
        qh = proj_ref[:, C_XQ + h * HD_X:C_XQ + (h + 1) * HD_X]
        s = _dot_nt(qh, mk_ref[:, hs]) * (HD_X ** -0.5)
        yield
        e = jnp.exp(s - jnp.max(s, axis=-1, keepdims=True))
        den = jnp.sum(e, axis=-1, keepdims=True)
        ox = _dot(e.astype(BF16), mv_ref[:, hs]) / den
        xg = proj_ref[:, C_XG + h * HD_X:C_XG + (h + 1) * HD_X].astype(F32)
        out_ref[:, D_A + D_B + h * HD_X:D_A + D_B + (h + 1) * HD_X] = (ox * _silu(xg)).astype(BF16)
        yield


def _prompt_layer_kernel(xn_ref, wi_ref, wo_ref, nw_ref, fw_ref, mk_ref, mv_ref, wa_ref, abt_ref,
                         avw_ref, bwa_ref, bba_ref, onw_ref, y_ref, st_ref,
                         pa_ref, pb_ref, xk_ref, br_ref, h_ref, sbd_ref, o_ref, la_ref, *, nt):
    s = pl.program_id(0)
    cur = jnp.maximum(s - 1, 0)
    t = lax.rem(cur, nt)

    @pl.when(t == 0)
    def _():
        sbd_ref[...] = jnp.zeros_like(sbd_ref)

    def in_proj_stages(pn_ref):
        x = xn_ref[...]
        ms = jnp.mean(x * x, axis=-1, keepdims=True)
        h_ref[...] = (x * lax.rsqrt(ms + EPS) * nw_ref[...]).astype(BF16)
        yield
        for c0 in range(0, D_PROJ_PAD, MXU_COLS):
            cols = pl.ds(c0, MXU_COLS)
            pn_ref[:, cols] = _dot(h_ref[...], wi_ref[:, cols]).astype(BF16)
            yield
        xk_ref[...] = xn_ref[...]

    def out_proj_stages():
        ssq = jnp.zeros((PT, 1), F32)
        for c0 in range(0, D_MODEL, OUT_BN):
            cols = pl.ds(c0, OUT_BN)
            acc = _dot(br_ref[...], wo_ref[:, cols]) + xk_ref[:, cols]
            y_ref[:, cols] = acc
            ssq = ssq + jnp.sum(acc * acc, axis=-1, keepdims=True)
            yield
        y_ref[...] = y_ref[...] * lax.rsqrt(ssq * (1.0 / D_MODEL) + EPS) * fw_ref[...]
        yield

    def run(order, streams):
        for name in order:
            next(streams[name])
        for name, gen in streams.items():
            assert next(gen, "done") == "done", name

    def mixer_stages(pc_ref):
        return _prompt_block(pc_ref, br_ref, mk_ref, mv_ref, wa_ref, abt_ref, avw_ref, bwa_ref,
                             bba_ref, onw_ref, sbd_ref, o_ref, la_ref)

    n_p = 1 + D_PROJ_PAD // MXU_COLS
    n_m = N_MIXER_STAGES
    n_o = 1 + D_MODEL // OUT_BN

    def interleave(counts):
        order = []
        for i in range(max(counts.values())):
            order += [name for name, n in counts.items() if i < n]
        return order

    def body(pn_ref, pc_ref):
        order = ["P"] + interleave({"M": n_m, "P": n_p - 1}) + ["O"] * n_o
        run(order, {"P": in_proj_stages(pn_ref), "M": mixer_stages(pc_ref), "O": out_proj_stages()})

    last = pl.num_programs(0) - 1

    @pl.when(s == 0)
    def _():
        run(["P"] * n_p, {"P": in_proj_stages(pa_ref)})

    @pl.when((lax.rem(s, 2) == 0) & (s > 0) & (s < last))
    def _():
        body(pa_ref, pb_ref)

    @pl.when(lax.rem(s, 2) == 1)
    def _():
        body(pb_ref, pa_ref)

    @pl.when(s == last)
    def _():
        run(["M"] * n_m + ["O"] * n_o, {"M": mixer_stages(pb_ref), "O": out_proj_stages()})

    @pl.when((t == nt - 1) & (s > 0))
    def _():
        for h in range(H_B):
            off = h * DV_B - WIN_START[h]
            blk = sbd_ref[h * DK_B:(h + 1) * DK_B, WIN_START[h]:WIN_START[h] + WIN]
            if off:
                blk = pltpu.roll(blk, WIN - off, 1)
            st_ref[0, h] = blk[:, :DV_B]


def _prompt_layer(xp, w_in_bf, w_out_bf, nw, fw, memkv, wa, abt, avw, bwa, bba, onw, *, batch, seq):
    nt = seq // PT
    nblk = batch * nt
    assert nblk % 2 == 0
    cur = lambda s: jnp.maximum(s - 1, 0)
    const = lambda *shape: pl.BlockSpec(shape, lambda s: (0,) * len(shape))
    resident = lambda *shape: pl.BlockSpec(shape, lambda s: (0,) * len(shape),
                                           pipeline_mode=pl.Buffered(1))
    return pl.pallas_call(
        functools.partial(_prompt_layer_kernel, nt=nt),
        grid=(nblk + 1,),
        in_specs=[pl.BlockSpec((PT, D_MODEL), lambda s: (jnp.minimum(s, nblk - 1), 0)),
                  resident(D_MODEL, D_PROJ_PAD), resident(D_MODEL, D_MODEL),
                  const(1, D_MODEL), const(1, D_MODEL),
                  pl.BlockSpec((None, N_MEM, D_X), lambda s: (0, cur(s) // nt, 0),
                               pipeline_mode=pl.Buffered(1)),
                  pl.BlockSpec((None, N_MEM, D_X), lambda s: (1, cur(s) // nt, 0),
                               pipeline_mode=pl.Buffered(1)),
                  const(H_A, CHUNK_A, CHUNK_A), const(CHUNK_A, H_A), const(1, D_A),
                  const(LANES, D_BK), const(1, D_BK), const(1, D_B)],
        out_specs=[pl.BlockSpec((PT, D_MODEL), lambda s: (cur(s), 0)),
                   pl.BlockSpec((1, H_B, DK_B, DV_B), lambda s: (cur(s) // nt, 0, 0, 0))],
        out_shape=[jax.ShapeDtypeStruct((batch * seq, D_MODEL), F32),
                   jax.ShapeDtypeStruct((batch, H_B, DK_B, DV_B), F32)],
        scratch_shapes=[pltpu.VMEM((PT, D_PROJ_PAD), BF16),
                        pltpu.VMEM((PT, D_PROJ_PAD), BF16),
                        pltpu.VMEM((PT, D_MODEL), F32),
                        pltpu.VMEM((PT, D_MODEL), BF16),
                        pltpu.VMEM((PT, D_MODEL), BF16),
                        pltpu.VMEM((D_BK, D_B), F32),
                        pltpu.VMEM((PT, D_B), F32),
                        pltpu.VMEM((PT, D_BK), F32)],
        compiler_params=pltpu.CompilerParams(
            dimension_semantics=("arbitrary",), vmem_limit_bytes=VMEM_LIMIT),
        name="prompt_layer",
    )(xp, w_in_bf, w_out_bf, nw.reshape(1, D_MODEL), fw.reshape(1, D_MODEL), memkv, memkv,
      wa, abt, avw, bwa, bba, onw)


NS = 8
TS = 8
SB = NS * TS
SBP = 128


def _sample_mixer_kernel(proj_ref, st_ref, ck_ref, cv_ref, wa_ref, abt_ref, avw_ref, bwa_ref,
                         bba_ref, onw_ref, out_ref, stn_ref, cvs_ref,
                         qin_ref, xq_ref, kot_ref, lat_ref, vhm_ref, ghm_ref, ohm_ref, ox_ref):
    ri = _iota((SB, SB), 0)
    ci = _iota((SB, SB), 1)
    same_seq = jnp.right_shift(ri, 3) == jnp.right_shift(ci, 3)
    causal = same_seq & (ci <= ri)

    a_br, vn = _group_a(proj_ref, wa_ref, abt_ref, avw_ref, causal)
    out_ref[:, 0:D_A] = a_br.astype(BF16)
    cvs_ref[...] = vn

    q = proj_ref[:, C_BQ:C_BQ + D_BK].astype(F32)
    k = proj_ref[:, C_BK:C_BK + D_BK].astype(F32)
    vb = proj_ref[:, C_BV:C_BV + D_B]
    bgb = proj_ref[:, C_BG:C_BG + D_B]
    pre = _dot(proj_ref[:, C_BR:C_BR + LANES], bwa_ref[...]) + bba_ref[...]
    log_a = _log_sigmoid(pre) * (1.0 / GATE_TAU)
    causal_bf = jnp.where(causal, 1.0, 0.0).astype(BF16)
    seq_bf = jnp.where(same_seq, 1.0, 0.0).astype(BF16)
    hi, lo = _split(log_a)
    cum = _dot(causal_bf, hi) + _dot(causal_bf, lo)
    tot = _dot(seq_bf, hi) + _dot(seq_bf, lo)
    q_in = q * (DK_B ** -0.5) * jnp.exp(cum)
    k_in = k * jnp.exp(-cum)
    k_out = k * jnp.exp(tot - cum)
    qin_ref[...] = q_in
    zpad = jnp.zeros((SBP - SB, D_BK), F32)
    kot_ref[...] = jnp.concatenate([k_out, zpad], axis=0).T.astype(BF16)
    lat_ref[...] = jnp.concatenate([log_a, zpad], axis=0).T
    xq_ref[...] = proj_ref[:, C_XQ:C_XQ + D_X].astype(F32)

    wins = _gla_intra_windows(q_in, k_in, vb, causal)
    sel_r = _iota((D_B, DV_B), 0)
    sel_c = _iota((D_B, DV_B), 1)
    vhm_ref[...] = jnp.zeros_like(vhm_ref)
    for h in range(H_B):
        off = h * DV_B - WIN_START[h]
        w = wins[h]
        if off:
            w = pltpu.roll(w, WIN - off, 1)
        ohm_ref[h] = w[:, :DV_B]
        sel = jnp.where(sel_r == sel_c + h * DV_B, 1.0, 0.0).astype(BF16)
        vhm_ref[h, 0:SB, :] = _dot(vb, sel).astype(BF16)
        ghm_ref[h] = _dot(bgb, sel)

    mask_x = (jnp.right_shift(_iota((H_X * TS, N_MEM * H_X), 0), 3)
              == jnp.bitwise_and(_iota((H_X * TS, N_MEM * H_X), 1), H_X - 1))
    mask_b = jnp.right_shift(_iota((H_B * TS, D_BK), 0), 3) == _head_id(_iota((H_B * TS, D_BK), 1), DK_B, H_B)
    lane_seq = jnp.right_shift(_iota((D_BK, SBP), 1), 3)

    def per_seq(s, carry):
        r0 = pl.multiple_of(s * TS, TS)
        q8 = xq_ref[pl.ds(r0, TS), :]
        q32 = jnp.concatenate([q8[:, h * HD_X:(h + 1) * HD_X] for h in range(H_X)], axis=0)
        sc = _dot_nt(q32.astype(BF16), ck_ref[s].astype(BF16)) * (HD_X ** -0.5)
        sc = jnp.where(mask_x, sc, -1e30)
        e = jnp.exp(sc - jnp.max(sc, axis=-1, keepdims=True))
        den = jnp.sum(e, axis=-1, keepdims=True)
        o = _dot(e.astype(BF16), cv_ref[s].astype(BF16)) / den
        ox_ref[pl.ds(r0, TS), :] = jnp.concatenate(
            [o[h * TS:(h + 1) * TS] for h in range(H_X)], axis=1)
        qi8 = qin_ref[pl.ds(r0, TS), :]
        qbd2 = jnp.where(mask_b, jnp.concatenate([qi8] * H_B, axis=0), 0.0).astype(BF16)
        s0 = jnp.concatenate([st_ref[s, h] for h in range(H_B)], axis=0)
        o_inter = _dot(qbd2, s0.astype(BF16))
        for h in range(H_B):
            ohm_ref[h, pl.ds(r0, TS), :] += o_inter[h * TS:(h + 1) * TS]
        dec = jnp.exp(jnp.sum(jnp.where(lane_seq == s, lat_ref[...], 0.0), axis=1, keepdims=True))
        kot = jnp.where(lane_seq == s, kot_ref[...], jnp.zeros((), BF16))
        kv = jnp.concatenate(
            [_dot(kot[h * DK_B:(h + 1) * DK_B], vhm_ref[h]) for h in range(H_B)], axis=0)
        s_new = s0 * dec + kv
        for h in range(H_B):
            stn_ref[s, h] = s_new[h * DK_B:(h + 1) * DK_B]
        return carry

    lax.fori_loop(0, NS, per_seq, 0, unroll=4)

    selt_r = _iota((DV_B, D_B), 0)
    selt_c = _iota((DV_B, D_B), 1)
    b_br = jnp.zeros((SB, D_B), F32)
    for h in range(H_B):
        o_h = ohm_ref[h]
        ms = jnp.mean(o_h * o_h, axis=-1, keepdims=True)
        ob = (o_h * lax.rsqrt(ms + EPS) * onw_ref[...] * _silu(ghm_ref[h])).astype(BF16)
        selt = jnp.where(selt_c == selt_r + h * DV_B, 1.0, 0.0).astype(BF16)
        b_br = b_br + _dot(ob, selt)
    out_ref[:, D_A:D_A + D_B] = b_br.astype(BF16)

    xg = proj_ref[:, C_XG:C_XG + D_X].astype(F32)
    out_ref[:, D_A + D_B:D_MODEL] = (ox_ref[...] * _silu(xg)).astype(BF16)


def _sample_mixer(proj, state, ck, cv, wa, abt, avw, bwa, bba, onw):
    nseq = state.shape[1]
    const = lambda *shape: pl.BlockSpec(shape, lambda i: (0,) * len(shape))
    return pl.pallas_call(
        _sample_mixer_kernel,
        grid=(nseq // NS,),
        in_specs=[pl.BlockSpec((SB, D_PROJ), lambda i: (i, 0)),
                  pl.BlockSpec((None, NS, H_B, DK_B, DV_B), lambda i: (0, i, 0, 0, 0)),
                  pl.BlockSpec((NS, N_MEM * H_X, HD_X), lambda i: (i, 0, 0)),
                  pl.BlockSpec((NS, N_MEM * H_X, HD_X), lambda i: (i, 0, 0)),
                  const(H_A, SB, SB), const(SB, H_A), const(1, D_A),
                  const(LANES, D_BK), const(1, D_BK), const(1, DV_B)],
        out_specs=[pl.BlockSpec((SB, D_MODEL), lambda i: (i, 0)),
                   pl.BlockSpec((None, NS, H_B, DK_B, DV_B), lambda i: (0, i, 0, 0, 0)),
                   pl.BlockSpec((SB, D_A), lambda i: (i, 0))],
        out_shape=[jax.ShapeDtypeStruct((nseq * TS, D_MODEL), BF16),
                   jax.ShapeDtypeStruct((1, nseq, H_B, DK_B, DV_B), F32),
                   jax.ShapeDtypeStruct((nseq * TS, D_A), F32)],
        scratch_shapes=[pltpu.VMEM((SB, D_BK), F32),
                        pltpu.VMEM((SB, D_X), F32),
                        pltpu.VMEM((D_BK, SBP), BF16),
                        pltpu.VMEM((D_BK, SBP), F32),
                        pltpu.VMEM((H_B, SBP, DV_B), BF16),
                        pltpu.VMEM((H_B, SB, DV_B), F32),
                        pltpu.VMEM((H_B, SB, DV_B), F32),
                        pltpu.VMEM((SB, D_X), F32)],
        compiler_params=pltpu.CompilerParams(
            dimension_semantics=("arbitrary",), vmem_limit_bytes=VMEM_LIMIT),
        name="sample_mixer",
    )(proj, state, ck, cv, wa, abt, avw, bwa, bba, onw)


def kernel(x_prompt, x_sample, mem_prompt, state_gla, cache_mem_k, cache_mem_v, norm_w, w_in,
           a_vnorm_w, a_ws, a_bs, b_wa, b_ba, b_onorm_w, mem_norm_w, w_mem_kv, w_out, final_norm_w):
    batch, seq, _ = x_prompt.shape
    nseq, tdec, _ = x_sample.shape
    depth = w_in.shape[0]
    assert depth == 1 and tdec == TS and seq % PT == 0 and nseq % NS == 0

    xp = x_prompt.reshape(batch * seq, D_MODEL)
    xs = x_sample.reshape(nseq * TS, D_MODEL)
    mem = mem_prompt.reshape(batch * N_MEM, D_MODEL)
    w_in_bf, w_out_bf, proj_s, memkv, mem_k, mem_v = _weight_prep(
        jnp.transpose(w_in[0]), w_out[0], w_mem_kv[0], xs, norm_w[0], mem, mem_norm_w[0])
    bwa = jnp.concatenate([b_wa[0], jnp.zeros((LANES - GATE_RANK, D_BK), F32)], axis=0).astype(BF16)
    bba = b_ba[0].reshape(1, D_BK)
    avw = a_vnorm_w[0].reshape(1, D_A)
    onw_p = jnp.tile(b_onorm_w[0], H_B).reshape(1, D_B)
    onw_s = b_onorm_w[0].reshape(1, DV_B)
    wa_p = a_ws[0]
    abt_p = a_bs[0].T
    wa_s = jnp.tile(a_ws[0][:, :TS, :TS], (1, NS, NS))
    abt_s = jnp.tile(a_bs[0][:, :TS], (1, NS)).T

    br_s, st_s, cvs = _sample_mixer(
        proj_s, state_gla,
        cache_mem_k.reshape(nseq, N_MEM * H_X, HD_X), cache_mem_v.reshape(nseq, N_MEM * H_X, HD_X),
        wa_s, abt_s, avw, bwa, bba, onw_s)
    y_s = _out_proj(br_s, w_out_bf, xs, final_norm_w, bm=256, name="out_proj_s")

    y_p, st_p = _prompt_layer(xp, w_in_bf, w_out_bf, norm_w[0], final_norm_w, memkv, wa_p, abt_p,
                              avw, bwa, bba, onw_p, batch=batch, seq=seq)

    return (y_p.reshape(batch, seq, D_MODEL),
            y_s.reshape(nseq, TS, D_MODEL),
            mem_k.reshape(1, batch, N_MEM, H_X, HD_X),
            mem_v.reshape(1, batch, N_MEM, H_X, HD_X),
            st_p.reshape(1, batch, H_B, DK_B, DV_B),
            st_s,
            cvs.reshape(1, nseq, TS, D_A))
```

```python
import functools

import jax
import jax.numpy as jnp
from jax import lax
from jax.experimental import pallas as pl
from jax.experimental.pallas import tpu as pltpu

F32 = jnp.float32
BF16 = jnp.bfloat16

D_MODEL = 2048
D_A = 768
H_A = 4
HD_A = 192
CHUNK_A = 128
D_B = 768
H_B = 4
DV_B = 192
DK_B = 96
D_BK = 384
GATE_RANK = 16
GATE_TAU = 16.0
CHUNK_B = 64
D_X = 512
H_X = 4
HD_X = 128
N_MEM = 256
EPS = 1e-6

LANES = 128
MXU_COLS = 256
C_AU, C_AV, C_AG = 0, 768, 1536
C_BQ, C_BK, C_BV, C_BG = 2304, 2688, 3072, 3840
C_XQ, C_XG, C_BR = 4608, 5120, 5632
D_PROJ = 5760
D_PROJ_PAD = 5888
WIN_START = (0, 128, 384, 512)
WIN = 256
HEAD_SPLIT = DV_B - LANES
OUT_BN = 512

VMEM_LIMIT = 60 * 1024 * 1024


def _dot(a, b):
    return jnp.dot(a, b, preferred_element_type=F32)


def _dot_nt(a, b):
    return lax.dot_general(a, b, (((1,), (1,)), ((), ())), preferred_element_type=F32)


def _split(x):
    hi = x.astype(BF16)
    lo = (x - hi.astype(F32)).astype(BF16)
    return hi, lo


def _silu(x):
    return x / (1.0 + jnp.exp(-x))


def _log_sigmoid(x):
    return jnp.minimum(x, 0.0) - jnp.log1p(jnp.exp(-jnp.abs(x)))


def _head_id(idx, width, n):
    h = jnp.zeros_like(idx)
    for i in range(1, n):
        h = h + (idx >= i * width).astype(jnp.int32)
    return h


def _iota(shape, dim):
    return lax.broadcasted_iota(jnp.int32, shape, dim)


def _merge_windows(r):
    rows = r[0].shape[0]
    lo = _iota((rows, LANES), 1) < HEAD_SPLIT
    tiles = [r[0][:, :LANES], jnp.where(lo, r[0][:, LANES:], r[1][:, :LANES]), r[1][:, LANES:],
             r[2][:, :LANES], jnp.where(lo, r[2][:, LANES:], r[3][:, :LANES]), r[3][:, LANES:]]
    return jnp.concatenate(tiles, axis=1)


def _group_a(proj_ref, wa_ref, abt_ref, avw_ref, mask):
    u = proj_ref[:, C_AU:C_AU + D_A].astype(F32)
    v = proj_ref[:, C_AV:C_AV + D_A].astype(F32)
    g = proj_ref[:, C_AG:C_AG + D_A].astype(F32)
    rows = u.shape[0]
    ms = jnp.mean(v * v, axis=-1, keepdims=True)
    vn = v * lax.rsqrt(ms + EPS) * avw_ref[...]
    vb = vn.astype(BF16)
    r = []
    for gi in range(H_A):
        w = jnp.where(mask, wa_ref[gi], 0.0).astype(BF16)
        r.append(_dot(w, vb[:, WIN_START[gi]:WIN_START[gi] + WIN]))
    mixed = _merge_windows(r)
    col_g = _head_id(_iota((rows, D_A), 1), HD_A, H_A)
    bias = jnp.zeros((rows, D_A), F32)
    for gi in range(H_A):
        bias = jnp.where(col_g == gi, abt_ref[:, gi:gi + 1], bias)
    return u * (mixed + bias) * _silu(g), vn


def _gla_intra_windows(q_in, k_in, vb, mask):
    rows = q_in.shape[0]
    col_h = _head_id(_iota((rows, D_BK), 1), DK_B, H_B)
    kb = k_in.astype(BF16)
    r = []
    for h in range(H_B):
        qh = jnp.where(col_h == h, q_in, 0.0).astype(BF16)
        s = jnp.where(mask, _dot_nt(qh, kb), 0.0).astype(BF16)
        r.append(_dot(s, vb[:, WIN_START[h]:WIN_START[h] + WIN]))
    return r


PREP_BN = MXU_COLS
N_MAIN_BLOCKS = C_BG // PREP_BN


GATE_BLOCK = D_PROJ_PAD // PREP_BN - 1
N_IN_BLOCKS = GATE_BLOCK + 1
N_OUT_BLOCKS = D_MODEL // PREP_BN
PREP_BUFS = 4


def _weight_prep_kernel(wt_hbm, wo_hbm, wm_hbm, xs_ref, nw_ref, mem_ref, mnw_ref,
                        oi_ref, oo_ref, ps_ref, mkv_ref, mk_ref, mv_ref,
                        buf, sem, hs_ref, hm_ref, wm_buf, wm_sem):
    r = pl.program_id(0)
    slot = lax.rem(r, PREP_BUFS)
    mem_step0 = N_IN_BLOCKS + N_OUT_BLOCKS

    def mem_copy(j):
        return pltpu.make_async_copy(wm_hbm.at[:, pl.ds(j * PREP_BN, PREP_BN)], wm_buf.at[j],
                                     wm_sem.at[j])

    for j in range(N_MEM_BLOCKS):
        @pl.when(r == N_IN_BLOCKS + j)
        def _(j=j):
            mem_copy(j).start()

    @pl.when(r == N_IN_BLOCKS)
    def _():
        x = mem_ref[...]
        ms = jnp.mean(x * x, axis=-1, keepdims=True)
        hm_ref[...] = (x * lax.rsqrt(ms + EPS) * mnw_ref[...]).astype(BF16)

    for j in range(N_MEM_BLOCKS):
        @pl.when(r == mem_step0 + j)
        def _(j=j):
            mem_copy(j).wait()
            kv = _dot(hm_ref[...], wm_buf[j].astype(BF16))
            mkv_ref[...] = kv
            dst = mk_ref if j * PREP_BN < D_X else mv_ref
            for hh in range(PREP_BN // HD_X):
                head = (j * PREP_BN % D_X) // HD_X + hh
                for b in range(kv.shape[0] // N_MEM):
                    dst[b, pl.ds(head, N_MEM, stride=H_X), :] = (
                        kv[b * N_MEM:(b + 1) * N_MEM, hh * HD_X:(hh + 1) * HD_X])

    def in_copy(rr, sl):
        start = pl.multiple_of(jnp.where(rr < N_MAIN_BLOCKS, rr * PREP_BN, rr * PREP_BN + GATE_RANK), 8)
        return pltpu.make_async_copy(wt_hbm.at[pl.ds(start, PREP_BN), :], buf.at[sl], sem.at[sl])

    def gate_copy(sl):
        return pltpu.make_async_copy(wt_hbm.at[pl.ds(C_BG, GATE_RANK), :],
                                     buf.at[sl, pl.ds(0, GATE_RANK), :], sem.at[sl])

    def out_copy(rr, sl):
        start = pl.multiple_of((rr - N_IN_BLOCKS) * PREP_BN, PREP_BN)
        return pltpu.make_async_copy(wo_hbm.at[pl.ds(start, PREP_BN), :], buf.at[sl], sem.at[sl])

    def start_fetch(rr, sl):
        @pl.when(rr < GATE_BLOCK)
        def _():
            in_copy(rr, sl).start()

        @pl.when(rr == GATE_BLOCK)
        def _():
            gate_copy(sl).start()

        @pl.when(rr > GATE_BLOCK)
        def _():
            out_copy(rr, sl).start()

    @pl.when(r == 0)
    def _():
        for ahead in range(PREP_BUFS - 1):
            start_fetch(r + ahead, ahead)

    nxt = r + PREP_BUFS - 1

    @pl.when(nxt < mem_step0)
    def _():
        start_fetch(nxt, lax.rem(nxt, PREP_BUFS))

    @pl.when(r == 0)
    def _():
        x = xs_ref[...]
        ms = jnp.mean(x * x, axis=-1, keepdims=True)
        hs_ref[...] = (x * lax.rsqrt(ms + EPS) * nw_ref[...]).astype(BF16)

    @pl.when(r < GATE_BLOCK)
    def _():
        in_copy(r, slot).wait()
        oi_ref[...] = buf[slot].T.astype(BF16)
        ps_ref[...] = _dot(hs_ref[...], oi_ref[...]).astype(BF16)

    @pl.when(r == GATE_BLOCK)
    def _():
        gate_copy(slot).wait()
        rows = jnp.concatenate([buf[slot, 0:GATE_RANK, :],
                                jnp.zeros((PREP_BN - GATE_RANK, D_MODEL), F32)], axis=0)
        oi_ref[...] = rows.T.astype(BF16)
        ps_ref[...] = _dot(hs_ref[...], oi_ref[...]).astype(BF16)

    @pl.when((r > GATE_BLOCK) & (r < mem_step0))
    def _():
        out_copy(r, slot).wait()
        oo_ref[...] = buf[slot].astype(BF16)


N_MEM_BLOCKS = 2 * D_X // PREP_BN


def _weight_prep(w_t, w_out, w_mem, xs, nw, mem, mnw):
    k = w_t.shape[1]
    m = xs.shape[0]
    mm = mem.shape[0]
    batch = mm // N_MEM
    mem_step0 = N_IN_BLOCKS + N_OUT_BLOCKS
    in_block = lambda r: (0, jnp.minimum(r, GATE_BLOCK))
    mem_j = lambda r: jnp.maximum(r - mem_step0, 0)
    per_d_x = D_X // PREP_BN
    resident = lambda *shape: pl.BlockSpec(shape, lambda r: (0,) * len(shape),
                                           pipeline_mode=pl.Buffered(1))
    return pl.pallas_call(
        _weight_prep_kernel,
        grid=(mem_step0 + N_MEM_BLOCKS,),
        in_specs=[pl.BlockSpec(memory_space=pl.ANY), pl.BlockSpec(memory_space=pl.ANY),
                  pl.BlockSpec(memory_space=pl.ANY),
                  resident(m, k), pl.BlockSpec((1, k), lambda r: (0, 0)),
                  resident(mm, k), pl.BlockSpec((1, k), lambda r: (0, 0))],
        out_specs=[pl.BlockSpec((k, PREP_BN), in_block),
                   pl.BlockSpec((PREP_BN, D_MODEL),
                                lambda r: (jnp.clip(r - N_IN_BLOCKS, 0, N_OUT_BLOCKS - 1), 0)),
                   pl.BlockSpec((m, PREP_BN), in_block),
                   pl.BlockSpec((None, mm, PREP_BN),
                                lambda r: (mem_j(r) // per_d_x, 0, lax.rem(mem_j(r), per_d_x))),
                   pl.BlockSpec((batch, N_MEM * H_X, HD_X), lambda r: (0, 0, 0)),
                   pl.BlockSpec((batch, N_MEM * H_X, HD_X), lambda r: (0, 0, 0))],
        out_shape=[jax.ShapeDtypeStruct((k, D_PROJ_PAD), BF16),
                   jax.ShapeDtypeStruct((D_MODEL, D_MODEL), BF16),
                   jax.ShapeDtypeStruct((m, D_PROJ_PAD), BF16),
                   jax.ShapeDtypeStruct((2, mm, D_X), F32),
                   jax.ShapeDtypeStruct((batch, N_MEM * H_X, HD_X), F32),
                   jax.ShapeDtypeStruct((batch, N_MEM * H_X, HD_X), F32)],
        scratch_shapes=[pltpu.VMEM((PREP_BUFS, PREP_BN, k), F32),
                        pltpu.SemaphoreType.DMA((PREP_BUFS,)),
                        pltpu.VMEM((m, k), BF16),
                        pltpu.VMEM((mm, k), BF16),
                        pltpu.VMEM((N_MEM_BLOCKS, k, PREP_BN), F32),
                        pltpu.SemaphoreType.DMA((N_MEM_BLOCKS,))],
        compiler_params=pltpu.CompilerParams(
            dimension_semantics=("arbitrary",), vmem_limit_bytes=VMEM_LIMIT),
        name="weight_prep",
    )(w_t, w_out, w_mem, xs, nw.reshape(1, k), mem, mnw.reshape(1, k))


def _out_proj_kernel(br_ref, w_ref, x_ref, fw_ref, y_ref):
    acc = _dot(br_ref[...], w_ref[...]) + x_ref[...]
    ms = jnp.mean(acc * acc, axis=-1, keepdims=True)
    y_ref[...] = acc * lax.rsqrt(ms + EPS) * fw_ref[...]


def _out_proj(br, w, x, fw, *, bm, name):
    m, k = br.shape
    n = w.shape[1]
    return pl.pallas_call(
        _out_proj_kernel,
        grid=(m // bm,),
        in_specs=[pl.BlockSpec((bm, k), lambda i: (i, 0)),
                  pl.BlockSpec((k, n), lambda i: (0, 0)),
                  pl.BlockSpec((bm, n), lambda i: (i, 0)),
                  pl.BlockSpec((1, n), lambda i: (0, 0))],
        out_specs=pl.BlockSpec((bm, n), lambda i: (i, 0)),
        out_shape=jax.ShapeDtypeStruct((m, n), F32),
        compiler_params=pltpu.CompilerParams(
            dimension_semantics=("arbitrary",), vmem_limit_bytes=VMEM_LIMIT),
        name=name,
    )(br, w, x, fw.reshape(1, n))


PT = 256
GLA_ROWS = 128
N_MIXER_STAGES = 2 + 4 * (PT // GLA_ROWS) + 1 + 2 * H_X


def _group_a_chunks(proj_ref, wa_ref, abt_ref, avw_ref):
    v = proj_ref[:, C_AV:C_AV + D_A].astype(F32)
    n_chunks = v.shape[0] // CHUNK_A
    ms = jnp.mean(v * v, axis=-1, keepdims=True)
    vb = (v * lax.rsqrt(ms + EPS) * avw_ref[...]).astype(BF16)
    tril = _iota((CHUNK_A, CHUNK_A), 1) <= _iota((CHUNK_A, CHUNK_A), 0)
    r = []
    for gi in range(H_A):
        w = jnp.where(tril, wa_ref[gi], 0.0).astype(BF16)
        ws = slice(WIN_START[gi], WIN_START[gi] + WIN)
        rhs = jnp.concatenate([vb[c * CHUNK_A:(c + 1) * CHUNK_A, ws] for c in range(n_chunks)], axis=1)
        res = _dot(w, rhs)
        r.append(jnp.concatenate([res[:, c * WIN:(c + 1) * WIN] for c in range(n_chunks)], axis=0))
    mixed = _merge_windows(r)
    col_g = _head_id(_iota((CHUNK_A, D_A), 1), HD_A, H_A)
    bias = jnp.zeros((CHUNK_A, D_A), F32)
    for gi in range(H_A):
        bias = jnp.where(col_g == gi, abt_ref[:, gi:gi + 1], bias)
    bias = jnp.concatenate([bias] * n_chunks, axis=0)
    u = proj_ref[:, C_AU:C_AU + D_A].astype(F32)
    g = proj_ref[:, C_AG:C_AG + D_A].astype(F32)
    return u * (mixed + bias) * _silu(g)


def _prompt_block(proj_ref, out_ref, mk_ref, mv_ref, wa_ref, abt_ref, avw_ref, bwa_ref, bba_ref,
                  onw_ref, sbd_ref, o_ref, la_ref):
    out_ref[:, 0:D_A] = _group_a_chunks(proj_ref, wa_ref, abt_ref, avw_ref).astype(BF16)
    yield

    pre = _dot(proj_ref[:, C_BR:C_BR + LANES], bwa_ref[...]) + bba_ref[...]
    la_ref[...] = _log_sigmoid(pre) * (1.0 / GATE_TAU)
    yield
    gl = GLA_ROWS
    tril_bf = jnp.where(_iota((gl, gl), 1) <= _iota((gl, gl), 0), 1.0, 0.0).astype(BF16)
    win_col = _iota((gl, WIN), 1)
    col_hq = _head_id(_iota((gl, D_BK), 1), DK_B, H_B)
    tril_heads = (_iota((H_B * gl, gl), 1) <= jnp.bitwise_and(_iota((H_B * gl, gl), 0), gl - 1))
    for c in range(PT // gl):
        sl = slice(c * gl, (c + 1) * gl)
        log_a = la_ref[sl, :]
        q = proj_ref[sl, C_BQ:C_BQ + D_BK].astype(F32)
        k = proj_ref[sl, C_BK:C_BK + D_BK].astype(F32)
        vb = proj_ref[sl, C_BV:C_BV + D_B]
        hi, lo = _split(log_a)
        cum2 = _dot(tril_bf, jnp.concatenate([hi, lo], axis=1))
        yield
        cum = cum2[:, :D_BK] + cum2[:, D_BK:]
        tot_row = cum[gl - 1:gl, :]
        mid_row = cum[CHUNK_B - 1:CHUNK_B, :]
        la_t = log_a.T
        q_s = q * (DK_B ** -0.5)
        q_dec = q_s * jnp.exp(cum)
        q_in = q_s * jnp.exp(cum - mid_row)
        k_in = k * jnp.exp(mid_row - cum)
        k_out = k * jnp.exp(tot_row - cum)
        q_heads = jnp.concatenate([jnp.where(col_hq == h, q_in, 0.0) for h in range(H_B)], axis=0)
        sc = _dot_nt(q_heads.astype(BF16), k_in.astype(BF16))
        yield
        sc = jnp.where(tril_heads, sc, 0.0).astype(BF16)
        o_intra = _merge_windows(
            [_dot(sc[h * gl:(h + 1) * gl], vb[:, WIN_START[h]:WIN_START[h] + WIN])
             for h in range(H_B)])
        s_old = sbd_ref[...]
        o_inter = _dot(q_dec.astype(BF16), s_old.astype(BF16))
        yield
        kot = k_out.T.astype(BF16)
        for h in range(H_B):
            rs = slice(h * DK_B, (h + 1) * DK_B)
            ws = slice(WIN_START[h], WIN_START[h] + WIN)
            dec = jnp.exp(jnp.sum(la_t[rs], axis=1, keepdims=True))
            lo_col = h * DV_B - WIN_START[h]
            in_head = (win_col >= lo_col) & (win_col < lo_col + DV_B)
            v_h = jnp.where(in_head, vb[:, ws], jnp.zeros((), BF16))
            kv = _dot(kot[rs], v_h)
            sbd_ref[rs, ws] = s_old[rs, ws] * dec + kv
        o_ref[sl, :] = o_intra + o_inter
        yield
    o = o_ref[...]
    o2 = o * o
    t = [o2[:, j * LANES:(j + 1) * LANES] for j in range(D_B // LANES)]
    lo = _iota((PT, LANES), 1) < HEAD_SPLIT
    t1a = jnp.where(lo, t[1], 0.0)
    t4a = jnp.where(lo, t[4], 0.0)
    sums = (t[0] + t1a, (t[1] - t1a) + t[2], t[3] + t4a, (t[4] - t4a) + t[5])
    inv = [lax.rsqrt(jnp.sum(x, axis=-1, keepdims=True) * (1.0 / DV_B) + EPS) for x in sums]
    inv = jnp.concatenate(
        [jnp.broadcast_to(inv[0], (PT, LANES)), jnp.where(lo, inv[0], inv[1]),
         jnp.broadcast_to(inv[1], (PT, LANES)), jnp.broadcast_to(inv[2], (PT, LANES)),
         jnp.where(lo, inv[2], inv[3]), jnp.broadcast_to(inv[3], (PT, LANES))], axis=1)
    o_n = o * inv * onw_ref[...]
    bg = proj_ref[:, C_BG:C_BG + D_B].astype(F32)
    out_ref[:, D_A:D_A + D_B] = (o_n * _silu(bg)).astype(BF16)
    yield

    for h in range(H_X):
        hs = slice(h * HD_X, (h + 1) * HD_X)
        qh = proj_ref[:, C_XQ + h * HD_X:C_XQ + (h + 1) * HD_X]
        s = _dot_nt(qh, mk_ref[:, hs].astype(BF16)) * (HD_X ** -0.5)
        yield
        e = jnp.exp(s - jnp.max(s, axis=-1, keepdims=True))
        den = jnp.sum(e, axis=-1, keepdims=True)
        ox = _dot(e.astype(BF16), mv_ref[:, hs].astype(BF16)) / den
        xg = proj_ref[:, C_XG + h * HD_X:C_XG + (h + 1) * HD_X].astype(F32)
        out_ref[:, D_A + D_B + h * HD_X:D_A + D_B + (h + 1) * HD_X] = (ox * _silu(xg)).astype(BF16)
        yield


def _prompt_layer_kernel(xn_ref, wi_ref, wo_ref, nw_ref, fw_ref, mk_ref, mv_ref, wa_ref, abt_ref,
                         avw_ref, bwa_ref, bba_ref, onw_ref, y_ref, st_ref,
                         pa_ref, pb_ref, xk_ref, br_ref, h_ref, sbd_ref, o_ref, la_ref, *, nt):
    s = pl.program_id(0)
    cur = jnp.maximum(s - 1, 0)
    t = lax.rem(cur, nt)

    @pl.when(t == 0)
    def _():
        sbd_ref[...] = jnp.zeros_like(sbd_ref)

    def in_proj_stages(pn_ref):
        x = xn_ref[...]
        ms = jnp.mean(x * x, axis=-1, keepdims=True)
        h_ref[...] = (x * lax.rsqrt(ms + EPS) * nw_ref[...]).astype(BF16)
        yield
        for c0 in range(0, D_PROJ_PAD, MXU_COLS):
            cols = pl.ds(c0, MXU_COLS)
            pn_ref[:, cols] = _dot(h_ref[...], wi_ref[:, cols]).astype(BF16)
            yield
        xk_ref[...] = xn_ref[...]

    def out_proj_stages():
        ssq = jnp.zeros((PT, 1), F32)
        for c0 in range(0, D_MODEL, OUT_BN):
            cols = pl.ds(c0, OUT_BN)
            acc = _dot(br_ref[...], wo_ref[:, cols]) + xk_ref[:, cols]
            y_ref[:, cols] = acc
            ssq = ssq + jnp.sum(acc * acc, axis=-1, keepdims=True)
            yield
        y_ref[...] = y_ref[...] * lax.rsqrt(ssq * (1.0 / D_MODEL) + EPS) * fw_ref[...]
        yield

    def run(order, streams):
        for name in order:
            next(streams[name])
        for name, gen in streams.items():
            assert next(gen, "done") == "done", name

    def mixer_stages(pc_ref):
        return _prompt_block(pc_ref, br_ref, mk_ref, mv_ref, wa_ref, abt_ref, avw_ref, bwa_ref,
                             bba_ref, onw_ref, sbd_ref, o_ref, la_ref)

    n_p = 1 + D_PROJ_PAD // MXU_COLS
    n_m = N_MIXER_STAGES
    n_o = 1 + D_MODEL // OUT_BN

    def interleave(counts):
        order = []
        for i in range(max(counts.values())):
            order += [name for name, n in counts.items() if i < n]
        return order

    def body(pn_ref, pc_ref):
        order = ["P"] + interleave({"M": n_m, "P": n_p - 1}) + ["O"] * n_o
        run(order, {"P": in_proj_stages(pn_ref), "M": mixer_stages(pc_ref), "O": out_proj_stages()})

    last = pl.num_programs(0) - 1

    @pl.when(s == 0)
    def _():
        run(["P"] * n_p, {"P": in_proj_stages(pa_ref)})

    @pl.when((lax.rem(s, 2) == 0) & (s > 0) & (s < last))
    def _():
        body(pa_ref, pb_ref)

    @pl.when(lax.rem(s, 2) == 1)
    def _():
        body(pb_ref, pa_ref)

    @pl.when(s == last)
    def _():
        run(["M"] * n_m + ["O"] * n_o, {"M": mixer_stages(pb_ref), "O": out_proj_stages()})

    @pl.when((t == nt - 1) & (s > 0))
    def _():
        for h in range(H_B):
            off = h * DV_B - WIN_START[h]
            blk = sbd_ref[h * DK_B:(h + 1) * DK_B, WIN_START[h]:WIN_START[h] + WIN]
            if off:
                blk = pltpu.roll(blk, WIN - off, 1)
            st_ref[0, h] = blk[:, :DV_B]


def _prompt_layer(xp, w_in_bf, w_out_bf, nw, fw, memkv, wa, abt, avw, bwa, bba, onw, *, batch, seq):
    nt = seq // PT
    nblk = batch * nt
    assert nblk % 2 == 0
    cur = lambda s: jnp.maximum(s - 1, 0)
    const = lambda *shape: pl.BlockSpec(shape, lambda s: (0,) * len(shape))
    resident = lambda *shape: pl.BlockSpec(shape, lambda s: (0,) * len(shape),
                                           pipeline_mode=pl.Buffered(1))
    return pl.pallas_call(
        functools.partial(_prompt_layer_kernel, nt=nt),
        grid=(nblk + 1,),
        in_specs=[pl.BlockSpec((PT, D_MODEL), lambda s: (jnp.minimum(s, nblk - 1), 0)),
                  resident(D_MODEL, D_PROJ_PAD), resident(D_MODEL, D_MODEL),
                  const(1, D_MODEL), const(1, D_MODEL),
                  pl.BlockSpec((None, N_MEM, D_X), lambda s: (0, cur(s) // nt, 0)),
                  pl.BlockSpec((None, N_MEM, D_X), lambda s: (1, cur(s) // nt, 0)),
                  const(H_A, CHUNK_A, CHUNK_A), const(CHUNK_A, H_A), const(1, D_A),
                  const(LANES, D_BK), const(1, D_BK), const(1, D_B)],
        out_specs=[pl.BlockSpec((PT, D_MODEL), lambda s: (cur(s), 0)),
                   pl.BlockSpec((1, H_B, DK_B, DV_B), lambda s: (cur(s) // nt, 0, 0, 0))],
        out_shape=[jax.ShapeDtypeStruct((batch * seq, D_MODEL), F32),
                   jax.ShapeDtypeStruct((batch, H_B, DK_B, DV_B), F32)],
        scratch_shapes=[pltpu.VMEM((PT, D_PROJ_PAD), BF16),
                        pltpu.VMEM((PT, D_PROJ_PAD), BF16),
                        pltpu.VMEM((PT, D_MODEL), F32),
                        pltpu.VMEM((PT, D_MODEL), BF16),
                        pltpu.VMEM((PT, D_MODEL), BF16),
                        pltpu.VMEM((D_BK, D_B), F32),
                        pltpu.VMEM((PT, D_B), F32),
                        pltpu.VMEM((PT, D_BK), F32)],
        compiler_params=pltpu.CompilerParams(
            dimension_semantics=("arbitrary",), vmem_limit_bytes=VMEM_LIMIT),
        name="prompt_layer",
    )(xp, w_in_bf, w_out_bf, nw.reshape(1, D_MODEL), fw.reshape(1, D_MODEL), memkv, memkv,
      wa, abt, avw, bwa, bba, onw)


NS = 8
TS = 8
SB = NS * TS
SBP = 128


def _sample_mixer_kernel(proj_ref, st_ref, ck_ref, cv_ref, wa_ref, abt_ref, avw_ref, bwa_ref,
                         bba_ref, onw_ref, out_ref, stn_ref, cvs_ref,
                         qin_ref, xq_ref, kot_ref, lat_ref, vhm_ref, ghm_ref, ohm_ref, ox_ref):
    ri = _iota((SB, SB), 0)
    ci = _iota((SB, SB), 1)
    same_seq = jnp.right_shift(ri, 3) == jnp.right_shift(ci, 3)
    causal = same_seq & (ci <= ri)

    a_br, vn = _group_a(proj_ref, wa_ref, abt_ref, avw_ref, causal)
    out_ref[:, 0:D_A] = a_br.astype(BF16)
    cvs_ref[...] = vn

    q = proj_ref[:, C_BQ:C_BQ + D_BK].astype(F32)
    k = proj_ref[:, C_BK:C_BK + D_BK].astype(F32)
    vb = proj_ref[:, C_BV:C_BV + D_B]
    bgb = proj_ref[:, C_BG:C_BG + D_B]
    pre = _dot(proj_ref[:, C_BR:C_BR + LANES], bwa_ref[...]) + bba_ref[...]
    log_a = _log_sigmoid(pre) * (1.0 / GATE_TAU)
    causal_bf = jnp.where(causal, 1.0, 0.0).astype(BF16)
    seq_bf = jnp.where(same_seq, 1.0, 0.0).astype(BF16)
    hi, lo = _split(log_a)
    cum = _dot(causal_bf, hi) + _dot(causal_bf, lo)
    tot = _dot(seq_bf, hi) + _dot(seq_bf, lo)
    q_in = q * (DK_B ** -0.5) * jnp.exp(cum)
    k_in = k * jnp.exp(-cum)
    k_out = k * jnp.exp(tot - cum)
    qin_ref[...] = q_in
    zpad = jnp.zeros((SBP - SB, D_BK), F32)
    kot_ref[...] = jnp.concatenate([k_out, zpad], axis=0).T.astype(BF16)
    lat_ref[...] = jnp.concatenate([log_a, zpad], axis=0).T
    xq_ref[...] = proj_ref[:, C_XQ:C_XQ + D_X].astype(F32)

    wins = _gla_intra_windows(q_in, k_in, vb, causal)
    sel_r = _iota((D_B, DV_B), 0)
    sel_c = _iota((D_B, DV_B), 1)
    vhm_ref[...] = jnp.zeros_like(vhm_ref)
    for h in range(H_B):
        off = h * DV_B - WIN_START[h]
        w = wins[h]
        if off:
            w = pltpu.roll(w, WIN - off, 1)
        ohm_ref[h] = w[:, :DV_B]
        sel = jnp.where(sel_r == sel_c + h * DV_B, 1.0, 0.0).astype(BF16)
        vhm_ref[h, 0:SB, :] = _dot(vb, sel).astype(BF16)
        ghm_ref[h] = _dot(bgb, sel)

    mask_x = (jnp.right_shift(_iota((H_X * TS, N_MEM * H_X), 0), 3)
              == jnp.bitwise_and(_iota((H_X * TS, N_MEM * H_X), 1), H_X - 1))
    mask_b = jnp.right_shift(_iota((H_B * TS, D_BK), 0), 3) == _head_id(_iota((H_B * TS, D_BK), 1), DK_B, H_B)
    lane_seq = jnp.right_shift(_iota((D_BK, SBP), 1), 3)

    def per_seq(s, carry):
        r0 = pl.multiple_of(s * TS, TS)
        q8 = xq_ref[pl.ds(r0, TS), :]
        q32 = jnp.concatenate([q8[:, h * HD_X:(h + 1) * HD_X] for h in range(H_X)], axis=0)
        sc = _dot_nt(q32.astype(BF16), ck_ref[s].astype(BF16)) * (HD_X ** -0.5)
        sc = jnp.where(mask_x, sc, -1e30)
        e = jnp.exp(sc - jnp.max(sc, axis=-1, keepdims=True))
        den = jnp.sum(e, axis=-1, keepdims=True)
        o = _dot(e.astype(BF16), cv_ref[s].astype(BF16)) / den
        ox_ref[pl.ds(r0, TS), :] = jnp.concatenate(
            [o[h * TS:(h + 1) * TS] for h in range(H_X)], axis=1)
        qi8 = qin_ref[pl.ds(r0, TS), :]
        qbd2 = jnp.where(mask_b, jnp.concatenate([qi8] * H_B, axis=0), 0.0).astype(BF16)
        s0 = jnp.concatenate([st_ref[s, h] for h in range(H_B)], axis=0)
        o_inter = _dot(qbd2, s0.astype(BF16))
        for h in range(H_B):
            ohm_ref[h, pl.ds(r0, TS), :] += o_inter[h * TS:(h + 1) * TS]
        dec = jnp.exp(jnp.sum(jnp.where(lane_seq == s, lat_ref[...], 0.0), axis=1, keepdims=True))
        kot = jnp.where(lane_seq == s, kot_ref[...], jnp.zeros((), BF16))
        kv = jnp.concatenate(
            [_dot(kot[h * DK_B:(h + 1) * DK_B], vhm_ref[h]) for h in range(H_B)], axis=0)
        s_new = s0 * dec + kv
        for h in range(H_B):
            stn_ref[s, h] = s_new[h * DK_B:(h + 1) * DK_B]
        return carry

    lax.fori_loop(0, NS, per_seq, 0, unroll=4)

    selt_r = _iota((DV_B, D_B), 0)
    selt_c = _iota((DV_B, D_B), 1)
    b_br = jnp.zeros((SB, D_B), F32)
    for h in range(H_B):
        o_h = ohm_ref[h]
        ms = jnp.mean(o_h * o_h, axis=-1, keepdims=True)
        ob = (o_h * lax.rsqrt(ms + EPS) * onw_ref[...] * _silu(ghm_ref[h])).astype(BF16)
        selt = jnp.where(selt_c == selt_r + h * DV_B, 1.0, 0.0).astype(BF16)
        b_br = b_br + _dot(ob, selt)
    out_ref[:, D_A:D_A + D_B] = b_br.astype(BF16)

    xg = proj_ref[:, C_XG:C_XG + D_X].astype(F32)
    out_ref[:, D_A + D_B:D_MODEL] = (ox_ref[...] * _silu(xg)).astype(BF16)


def _sample_mixer(proj, state, ck, cv, wa, abt, avw, bwa, bba, onw):
    nseq = state.shape[1]
    const = lambda *shape: pl.BlockSpec(shape, lambda i: (0,) * len(shape))
    return pl.pallas_call(
        _sample_mixer_kernel,
        grid=(nseq // NS,),
        in_specs=[pl.BlockSpec((SB, D_PROJ), lambda i: (i, 0)),
                  pl.BlockSpec((None, NS, H_B, DK_B, DV_B), lambda i: (0, i, 0, 0, 0)),
                  pl.BlockSpec((NS, N_MEM * H_X, HD_X), lambda i: (i, 0, 0)),
                  pl.BlockSpec((NS, N_MEM * H_X, HD_X), lambda i: (i, 0, 0)),
                  const(H_A, SB, SB), const(SB, H_A), const(1, D_A),
                  const(LANES, D_BK), const(1, D_BK), const(1, DV_B)],
        out_specs=[pl.BlockSpec((SB, D_MODEL), lambda i: (i, 0)),
                   pl.BlockSpec((None, NS, H_B, DK_B, DV_B), lambda i: (0, i, 0, 0, 0)),
                   pl.BlockSpec((SB, D_A), lambda i: (i, 0))],
        out_shape=[jax.ShapeDtypeStruct((nseq * TS, D_MODEL), BF16),
                   jax.ShapeDtypeStruct((1, nseq, H_B, DK_B, DV_B), F32),
                   jax.ShapeDtypeStruct((nseq * TS, D_A), F32)],
        scratch_shapes=[pltpu.VMEM((SB, D_BK), F32),
                        pltpu.VMEM((SB, D_X), F32),
                        pltpu.VMEM((D_BK, SBP), BF16),
                        pltpu.VMEM((D_BK, SBP), F32),
                        pltpu.VMEM((H_B, SBP, DV_B), BF16),
                        pltpu.VMEM((H_B, SB, DV_B), F32),
                        pltpu.VMEM((H_B, SB, DV_B), F32),
                        pltpu.VMEM((SB, D_X), F32)],
        compiler_params=pltpu.CompilerParams(
            dimension_semantics=("arbitrary",), vmem_limit_bytes=VMEM_LIMIT),
        name="sample_mixer",
    )(proj, state, ck, cv, wa, abt, avw, bwa, bba, onw)


def kernel(x_prompt, x_sample, mem_prompt, state_gla, cache_mem_k, cache_mem_v, norm_w, w_in,
           a_vnorm_w, a_ws, a_bs, b_wa, b_ba, b_onorm_w, mem_norm_w, w_mem_kv, w_out, final_norm_w):
    batch, seq, _ = x_prompt.shape
    nseq, tdec, _ = x_sample.shape
    depth = w_in.shape[0]
    assert depth == 1 and tdec == TS and seq % PT == 0 and nseq % NS == 0

    xp = x_prompt.reshape(batch * seq, D_MODEL)
    xs = x_sample.reshape(nseq * TS, D_MODEL)
    mem = mem_prompt.reshape(batch * N_MEM, D_MODEL)
    w_in_bf, w_out_bf, proj_s, memkv, mem_k, mem_v = _weight_prep(
        jnp.transpose(w_in[0]), w_out[0], w_mem_kv[0], xs, norm_w[0], mem, mem_norm_w[0])
    bwa = jnp.concatenate([b_wa[0], jnp.zeros((LANES - GATE_RANK, D_BK), F32)], axis=0).astype(BF16)
    bba = b_ba[0].reshape(1, D_BK)
    avw = a_vnorm_w[0].reshape(1, D_A)
    onw_p = jnp.tile(b_onorm_w[0], H_B).reshape(1, D_B)
    onw_s = b_onorm_w[0].reshape(1, DV_B)
    wa_p = a_ws[0]
    abt_p = a_bs[0].T
    wa_s = jnp.tile(a_ws[0][:, :TS, :TS], (1, NS, NS))
    abt_s = jnp.tile(a_bs[0][:, :TS], (1, NS)).T

    br_s, st_s, cvs = _sample_mixer(
        proj_s, state_gla,
        cache_mem_k.reshape(nseq, N_MEM * H_X, HD_X), cache_mem_v.reshape(nseq, N_MEM * H_X, HD_X),
        wa_s, abt_s, avw, bwa, bba, onw_s)
    y_s = _out_proj(br_s, w_out_bf, xs, final_norm_w, bm=256, name="out_proj_s")

    y_p, st_p = _prompt_layer(xp, w_in_bf, w_out_bf, norm_w[0], final_norm_w, memkv, wa_p, abt_p,
                              avw, bwa, bba, onw_p, batch=batch, seq=seq)

    return (y_p.reshape(batch, seq, D_MODEL),
            y_s.reshape(nseq, TS, D_MODEL),
            mem_k.reshape(1, batch, N_MEM, H_X, HD_X),
            mem_v.reshape(1, batch, N_MEM, H_X, HD_X),
            st_p.reshape(1, batch, H_B, DK_B, DV_B),
            st_s,
            cvs.reshape(1, nseq, TS, D_A))
```

```python
import functools

import jax
import jax.numpy as jnp
from jax import lax
from jax.experimental import pallas as pl
from jax.experimental.pallas import tpu as pltpu

F32 = jnp.float32
BF16 = jnp.bfloat16

D_MODEL = 2048
D_A = 768
H_A = 4
HD_A = 192
CHUNK_A = 128
D_B = 768
H_B = 4
DV_B = 192
DK_B = 96
D_BK = 384
GATE_RANK = 16
GATE_TAU = 16.0
CHUNK_B = 64
D_X = 512
H_X = 4
HD_X = 128
N_MEM = 256
EPS = 1e-6

LANES = 128
MXU_COLS = 256
C_AU, C_AV, C_AG = 0, 768, 1536
C_BQ, C_BK, C_BV, C_BG = 2304, 2688, 3072, 3840
C_XQ, C_XG, C_BR = 4608, 5120, 5632
D_PROJ = 5760
D_PROJ_PAD = 5888
WIN_START = (0, 128, 384, 512)
WIN = 256
HEAD_SPLIT = DV_B - LANES
IN_TILE = 512
OUT_BN = 512

VMEM_LIMIT = 60 * 1024 * 1024


def _dot(a, b):
    return jnp.dot(a, b, preferred_element_type=F32)


def _dot_nt(a, b):
    return lax.dot_general(a, b, (((1,), (1,)), ((), ())), preferred_element_type=F32)


def _split(x):
    hi = x.astype(BF16)
    lo = (x - hi.astype(F32)).astype(BF16)
    return hi, lo


def _silu(x):
    return x / (1.0 + jnp.exp(-x))


def _log_sigmoid(x):
    return jnp.minimum(x, 0.0) - jnp.log1p(jnp.exp(-jnp.abs(x)))


def _head_id(idx, width, n):
    h = jnp.zeros_like(idx)
    for i in range(1, n):
        h = h + (idx >= i * width).astype(jnp.int32)
    return h


def _iota(shape, dim):
    return lax.broadcasted_iota(jnp.int32, shape, dim)


def _merge_windows(r):
    rows = r[0].shape[0]
    lo = _iota((rows, LANES), 1) < HEAD_SPLIT
    tiles = [r[0][:, :LANES], jnp.where(lo, r[0][:, LANES:], r[1][:, :LANES]), r[1][:, LANES:],
             r[2][:, :LANES], jnp.where(lo, r[2][:, LANES:], r[3][:, :LANES]), r[3][:, LANES:]]
    return jnp.concatenate(tiles, axis=1)


def _group_a(proj_ref, wa_ref, abt_ref, avw_ref, mask):
    u = proj_ref[:, C_AU:C_AU + D_A].astype(F32)
    v = proj_ref[:, C_AV:C_AV + D_A].astype(F32)
    g = proj_ref[:, C_AG:C_AG + D_A].astype(F32)
    rows = u.shape[0]
    ms = jnp.mean(v * v, axis=-1, keepdims=True)
    vn = v * lax.rsqrt(ms + EPS) * avw_ref[...]
    vb = vn.astype(BF16)
    r = []
    for gi in range(H_A):
        w = jnp.where(mask, wa_ref[gi], 0.0).astype(BF16)
        r.append(_dot(w, vb[:, WIN_START[gi]:WIN_START[gi] + WIN]))
    mixed = _merge_windows(r)
    col_g = _head_id(_iota((rows, D_A), 1), HD_A, H_A)
    bias = jnp.zeros((rows, D_A), F32)
    for gi in range(H_A):
        bias = jnp.where(col_g == gi, abt_ref[:, gi:gi + 1], bias)
    return u * (mixed + bias) * _silu(g), vn


def _gla_intra_windows(q_in, k_in, vb, mask):
    rows = q_in.shape[0]
    col_h = _head_id(_iota((rows, D_BK), 1), DK_B, H_B)
    kb = k_in.astype(BF16)
    r = []
    for h in range(H_B):
        qh = jnp.where(col_h == h, q_in, 0.0).astype(BF16)
        s = jnp.where(mask, _dot_nt(qh, kb), 0.0).astype(BF16)
        r.append(_dot(s, vb[:, WIN_START[h]:WIN_START[h] + WIN]))
    return r


PREP_BN = MXU_COLS
N_MAIN_BLOCKS = C_BG // PREP_BN


GATE_BLOCK = D_PROJ_PAD // PREP_BN - 1
N_IN_BLOCKS = GATE_BLOCK + 1
N_OUT_BLOCKS = D_MODEL // PREP_BN
PREP_BUFS = 4


def _weight_prep_kernel(wt_hbm, wo_hbm, wm_hbm, xs_ref, nw_ref, mem_ref, mnw_ref,
                        oi_ref, oo_ref, ps_ref, mkv_ref, mk_ref, mv_ref,
                        buf, sem, hs_ref, hm_ref, wm_buf, wm_sem):
    r = pl.program_id(0)
    slot = lax.rem(r, PREP_BUFS)
    mem_step0 = N_IN_BLOCKS + N_OUT_BLOCKS

    def mem_copy(j):
        return pltpu.make_async_copy(wm_hbm.at[:, pl.ds(j * PREP_BN, PREP_BN)], wm_buf.at[j],
                                     wm_sem.at[j])

    for j in range(N_MEM_BLOCKS):
        @pl.when(r == N_IN_BLOCKS + j)
        def _(j=j):
            mem_copy(j).start()

    @pl.when(r == N_IN_BLOCKS)
    def _():
        x = mem_ref[...]
        ms = jnp.mean(x * x, axis=-1, keepdims=True)
        hm_ref[...] = (x * lax.rsqrt(ms + EPS) * mnw_ref[...]).astype(BF16)

    for j in range(N_MEM_BLOCKS):
        @pl.when(r == mem_step0 + j)
        def _(j=j):
            mem_copy(j).wait()
            kv = _dot(hm_ref[...], wm_buf[j].astype(BF16))
            mkv_ref[...] = kv
            dst = mk_ref if j * PREP_BN < D_X else mv_ref
            for hh in range(PREP_BN // HD_X):
                head = (j * PREP_BN % D_X) // HD_X + hh
                for b in range(kv.shape[0] // N_MEM):
                    dst[b, pl.ds(head, N_MEM, stride=H_X), :] = (
                        kv[b * N_MEM:(b + 1) * N_MEM, hh * HD_X:(hh + 1) * HD_X])

    def in_copy(rr, sl):
        start = pl.multiple_of(jnp.where(rr < N_MAIN_BLOCKS, rr * PREP_BN, rr * PREP_BN + GATE_RANK), 8)
        return pltpu.make_async_copy(wt_hbm.at[pl.ds(start, PREP_BN), :], buf.at[sl], sem.at[sl])

    def gate_copy(sl):
        return pltpu.make_async_copy(wt_hbm.at[pl.ds(C_BG, GATE_RANK), :],
                                     buf.at[sl, pl.ds(0, GATE_RANK), :], sem.at[sl])

    def out_copy(rr, sl):
        start = pl.multiple_of((rr - N_IN_BLOCKS) * PREP_BN, PREP_BN)
        return pltpu.make_async_copy(wo_hbm.at[pl.ds(start, PREP_BN), :], buf.at[sl], sem.at[sl])

    def start_fetch(rr, sl):
        @pl.when(rr < GATE_BLOCK)
        def _():
            in_copy(rr, sl).start()

        @pl.when(rr == GATE_BLOCK)
        def _():
            gate_copy(sl).start()

        @pl.when(rr > GATE_BLOCK)
        def _():
            out_copy(rr, sl).start()

    @pl.when(r == 0)
    def _():
        for ahead in range(PREP_BUFS - 1):
            start_fetch(r + ahead, ahead)

    nxt = r + PREP_BUFS - 1

    @pl.when(nxt < mem_step0)
    def _():
        start_fetch(nxt, lax.rem(nxt, PREP_BUFS))

    @pl.when(r == 0)
    def _():
        x = xs_ref[...]
        ms = jnp.mean(x * x, axis=-1, keepdims=True)
        hs_ref[...] = (x * lax.rsqrt(ms + EPS) * nw_ref[...]).astype(BF16)

    @pl.when(r < GATE_BLOCK)
    def _():
        in_copy(r, slot).wait()
        oi_ref[...] = buf[slot].T.astype(BF16)
        ps_ref[...] = _dot(hs_ref[...], oi_ref[...]).astype(BF16)

    @pl.when(r == GATE_BLOCK)
    def _():
        gate_copy(slot).wait()
        rows = jnp.concatenate([buf[slot, 0:GATE_RANK, :],
                                jnp.zeros((PREP_BN - GATE_RANK, D_MODEL), F32)], axis=0)
        oi_ref[...] = rows.T.astype(BF16)
        ps_ref[...] = _dot(hs_ref[...], oi_ref[...]).astype(BF16)

    @pl.when((r > GATE_BLOCK) & (r < mem_step0))
    def _():
        out_copy(r, slot).wait()
        oo_ref[...] = buf[slot].astype(BF16)


N_MEM_BLOCKS = 2 * D_X // PREP_BN


def _weight_prep(w_t, w_out, w_mem, xs, nw, mem, mnw):
    k = w_t.shape[1]
    m = xs.shape[0]
    mm = mem.shape[0]
    batch = mm // N_MEM
    mem_step0 = N_IN_BLOCKS + N_OUT_BLOCKS
    in_block = lambda r: (0, jnp.minimum(r, GATE_BLOCK))
    mem_j = lambda r: jnp.maximum(r - mem_step0, 0)
    per_d_x = D_X // PREP_BN
    resident = lambda *shape: pl.BlockSpec(shape, lambda r: (0,) * len(shape),
                                           pipeline_mode=pl.Buffered(1))
    return pl.pallas_call(
        _weight_prep_kernel,
        grid=(mem_step0 + N_MEM_BLOCKS,),
        in_specs=[pl.BlockSpec(memory_space=pl.ANY), pl.BlockSpec(memory_space=pl.ANY),
                  pl.BlockSpec(memory_space=pl.ANY),
                  resident(m, k), pl.BlockSpec((1, k), lambda r: (0, 0)),
                  resident(mm, k), pl.BlockSpec((1, k), lambda r: (0, 0))],
        out_specs=[pl.BlockSpec((k, PREP_BN), in_block),
                   pl.BlockSpec((PREP_BN, D_MODEL),
                                lambda r: (jnp.clip(r - N_IN_BLOCKS, 0, N_OUT_BLOCKS - 1), 0)),
                   pl.BlockSpec((m, PREP_BN), in_block),
                   pl.BlockSpec((None, mm, PREP_BN),
                                lambda r: (mem_j(r) // per_d_x, 0, lax.rem(mem_j(r), per_d_x))),
                   pl.BlockSpec((batch, N_MEM * H_X, HD_X), lambda r: (0, 0, 0)),
                   pl.BlockSpec((batch, N_MEM * H_X, HD_X), lambda r: (0, 0, 0))],
        out_shape=[jax.ShapeDtypeStruct((k, D_PROJ_PAD), BF16),
                   jax.ShapeDtypeStruct((D_MODEL, D_MODEL), BF16),
                   jax.ShapeDtypeStruct((m, D_PROJ_PAD), BF16),
                   jax.ShapeDtypeStruct((2, mm, D_X), F32),
                   jax.ShapeDtypeStruct((batch, N_MEM * H_X, HD_X), F32),
                   jax.ShapeDtypeStruct((batch, N_MEM * H_X, HD_X), F32)],
        scratch_shapes=[pltpu.VMEM((PREP_BUFS, PREP_BN, k), F32),
                        pltpu.SemaphoreType.DMA((PREP_BUFS,)),
                        pltpu.VMEM((m, k), BF16),
                        pltpu.VMEM((mm, k), BF16),
                        pltpu.VMEM((N_MEM_BLOCKS, k, PREP_BN), F32),
                        pltpu.SemaphoreType.DMA((N_MEM_BLOCKS,))],
        compiler_params=pltpu.CompilerParams(
            dimension_semantics=("arbitrary",), vmem_limit_bytes=VMEM_LIMIT),
        name="weight_prep",
    )(w_t, w_out, w_mem, xs, nw.reshape(1, k), mem, mnw.reshape(1, k))


def _out_proj_kernel(br_ref, w_ref, x_ref, fw_ref, y_ref):
    acc = _dot(br_ref[...], w_ref[...]) + x_ref[...]
    ms = jnp.mean(acc * acc, axis=-1, keepdims=True)
    y_ref[...] = acc * lax.rsqrt(ms + EPS) * fw_ref[...]


def _out_proj(br, w, x, fw, *, bm, name):
    m, k = br.shape
    n = w.shape[1]
    return pl.pallas_call(
        _out_proj_kernel,
        grid=(m // bm,),
        in_specs=[pl.BlockSpec((bm, k), lambda i: (i, 0)),
                  pl.BlockSpec((k, n), lambda i: (0, 0)),
                  pl.BlockSpec((bm, n), lambda i: (i, 0)),
                  pl.BlockSpec((1, n), lambda i: (0, 0))],
        out_specs=pl.BlockSpec((bm, n), lambda i: (i, 0)),
        out_shape=jax.ShapeDtypeStruct((m, n), F32),
        compiler_params=pltpu.CompilerParams(
            dimension_semantics=("arbitrary",), vmem_limit_bytes=VMEM_LIMIT),
        name=name,
    )(br, w, x, fw.reshape(1, n))


PT = 256
GLA_ROWS = 128
N_MIXER_STAGES = 2 + 4 * (PT // GLA_ROWS) + 1 + 2 * H_X


def _group_a_chunks(proj_ref, wa_ref, abt_ref, avw_ref):
    v = proj_ref[:, C_AV:C_AV + D_A].astype(F32)
    n_chunks = v.shape[0] // CHUNK_A
    ms = jnp.mean(v * v, axis=-1, keepdims=True)
    vb = (v * lax.rsqrt(ms + EPS) * avw_ref[...]).astype(BF16)
    tril = _iota((CHUNK_A, CHUNK_A), 1) <= _iota((CHUNK_A, CHUNK_A), 0)
    r = []
    for gi in range(H_A):
        w = jnp.where(tril, wa_ref[gi], 0.0).astype(BF16)
        ws = slice(WIN_START[gi], WIN_START[gi] + WIN)
        rhs = jnp.concatenate([vb[c * CHUNK_A:(c + 1) * CHUNK_A, ws] for c in range(n_chunks)], axis=1)
        res = _dot(w, rhs)
        r.append(jnp.concatenate([res[:, c * WIN:(c + 1) * WIN] for c in range(n_chunks)], axis=0))
    mixed = _merge_windows(r)
    col_g = _head_id(_iota((CHUNK_A, D_A), 1), HD_A, H_A)
    bias = jnp.zeros((CHUNK_A, D_A), F32)
    for gi in range(H_A):
        bias = jnp.where(col_g == gi, abt_ref[:, gi:gi + 1], bias)
    bias = jnp.concatenate([bias] * n_chunks, axis=0)
    u = proj_ref[:, C_AU:C_AU + D_A].astype(F32)
    g = proj_ref[:, C_AG:C_AG + D_A].astype(F32)
    return u * (mixed + bias) * _silu(g)


def _prompt_block(proj_ref, out_ref, mk_ref, mv_ref, wa_ref, abt_ref, avw_ref, bwa_ref, bba_ref,
                  onw_ref, sbd_ref, o_ref, la_ref):
    out_ref[:, 0:D_A] = _group_a_chunks(proj_ref, wa_ref, abt_ref, avw_ref).astype(BF16)
    yield

    pre = _dot(proj_ref[:, C_BR:C_BR + LANES], bwa_ref[...]) + bba_ref[...]
    la_ref[...] = _log_sigmoid(pre) * (1.0 / GATE_TAU)
    yield
    gl = GLA_ROWS
    tril_bf = jnp.where(_iota((gl, gl), 1) <= _iota((gl, gl), 0), 1.0, 0.0).astype(BF16)
    win_col = _iota((gl, WIN), 1)
    col_hq = _head_id(_iota((gl, D_BK), 1), DK_B, H_B)
    tril_heads = (_iota((H_B * gl, gl), 1) <= jnp.bitwise_and(_iota((H_B * gl, gl), 0), gl - 1))
    for c in range(PT // gl):
        sl = slice(c * gl, (c + 1) * gl)
        log_a = la_ref[sl, :]
        q = proj_ref[sl, C_BQ:C_BQ + D_BK].astype(F32)
        k = proj_ref[sl, C_BK:C_BK + D_BK].astype(F32)
        vb = proj_ref[sl, C_BV:C_BV + D_B]
        hi, lo = _split(log_a)
        cum2 = _dot(tril_bf, jnp.concatenate([hi, lo], axis=1))
        yield
        cum = cum2[:, :D_BK] + cum2[:, D_BK:]
        tot_row = cum[gl - 1:gl, :]
        mid_row = cum[CHUNK_B - 1:CHUNK_B, :]
        la_t = log_a.T
        q_s = q * (DK_B ** -0.5)
        q_dec = q_s * jnp.exp(cum)
        q_in = q_s * jnp.exp(cum - mid_row)
        k_in = k * jnp.exp(mid_row - cum)
        k_out = k * jnp.exp(tot_row - cum)
        q_heads = jnp.concatenate([jnp.where(col_hq == h, q_in, 0.0) for h in range(H_B)], axis=0)
        sc = _dot_nt(q_heads.astype(BF16), k_in.astype(BF16))
        yield
        sc = jnp.where(tril_heads, sc, 0.0).astype(BF16)
        o_intra = _merge_windows(
            [_dot(sc[h * gl:(h + 1) * gl], vb[:, WIN_START[h]:WIN_START[h] + WIN])
             for h in range(H_B)])
        s_old = sbd_ref[...]
        o_inter = _dot(q_dec.astype(BF16), s_old.astype(BF16))
        yield
        kot = k_out.T.astype(BF16)
        for h in range(H_B):
            rs = slice(h * DK_B, (h + 1) * DK_B)
            ws = slice(WIN_START[h], WIN_START[h] + WIN)
            dec = jnp.exp(jnp.sum(la_t[rs], axis=1, keepdims=True))
            lo_col = h * DV_B - WIN_START[h]
            in_head = (win_col >= lo_col) & (win_col < lo_col + DV_B)
            v_h = jnp.where(in_head, vb[:, ws], jnp.zeros((), BF16))
            kv = _dot(kot[rs], v_h)
            sbd_ref[rs, ws] = s_old[rs, ws] * dec + kv
        o_ref[sl, :] = o_intra + o_inter
        yield
    o = o_ref[...]
    o2 = o * o
    t = [o2[:, j * LANES:(j + 1) * LANES] for j in range(D_B // LANES)]
    lo = _iota((PT, LANES), 1) < HEAD_SPLIT
    t1a = jnp.where(lo, t[1], 0.0)
    t4a = jnp.where(lo, t[4], 0.0)
    sums = (t[0] + t1a, (t[1] - t1a) + t[2], t[3] + t4a, (t[4] - t4a) + t[5])
    inv = [lax.rsqrt(jnp.sum(x, axis=-1, keepdims=True) * (1.0 / DV_B) + EPS) for x in sums]
    inv = jnp.concatenate(
        [jnp.broadcast_to(inv[0], (PT, LANES)), jnp.where(lo, inv[0], inv[1]),
         jnp.broadcast_to(inv[1], (PT, LANES)), jnp.broadcast_to(inv[2], (PT, LANES)),
         jnp.where(lo, inv[2], inv[3]), jnp.broadcast_to(inv[3], (PT, LANES))], axis=1)
    o_n = o * inv * onw_ref[...]
    bg = proj_ref[:, C_BG:C_BG + D_B].astype(F32)
    out_ref[:, D_A:D_A + D_B] = (o_n * _silu(bg)).astype(BF16)
    yield

    for h in range(H_X):
        hs = slice(h * HD_X, (h + 1) * HD_X)
        qh = proj_ref[:, C_XQ + h * HD_X:C_XQ + (h + 1) * HD_X]
        s = _dot_nt(qh, mk_ref[:, hs].astype(BF16)) * (HD_X ** -0.5)
        yield
        e = jnp.exp(s - jnp.max(s, axis=-1, keepdims=True))
        den = jnp.sum(e, axis=-1, keepdims=True)
        ox = _dot(e.astype(BF16), mv_ref[:, hs].astype(BF16)) / den
        xg = proj_ref[:, C_XG + h * HD_X:C_XG + (h + 1) * HD_X].astype(F32)
        out_ref[:, D_A + D_B + h * HD_X:D_A + D_B + (h + 1) * HD_X] = (ox * _silu(xg)).astype(BF16)
        yield


def _prompt_layer_kernel(xn_ref, wi_ref, wo_ref, nw_ref, fw_ref, mk_ref, mv_ref, wa_ref, abt_ref,
                         avw_ref, bwa_ref, bba_ref, onw_ref, y_ref, st_ref,
                         pa_ref, pb_ref, xk_ref, br_ref, h_ref, sbd_ref, o_ref, la_ref, *, nt):
    s = pl.program_id(0)
    cur = jnp.maximum(s - 1, 0)
    t = lax.rem(cur, nt)

    @pl.when(t == 0)
    def _():
        sbd_ref[...] = jnp.zeros_like(sbd_ref)

    def in_proj_stages(pn_ref):
        x = xn_ref[...]
        ms = jnp.mean(x * x, axis=-1, keepdims=True)
        h_ref[...] = (x * lax.rsqrt(ms + EPS) * nw_ref[...]).astype(BF16)
        yield
        for c0 in range(0, D_PROJ_PAD, IN_TILE):
            cols = pl.ds(c0, min(IN_TILE, D_PROJ_PAD - c0))
            pn_ref[:, cols] = _dot(h_ref[...], wi_ref[:, cols]).astype(BF16)
            yield
        xk_ref[...] = xn_ref[...]

    def out_proj_stages():
        ssq = jnp.zeros((PT, 1), F32)
        for c0 in range(0, D_MODEL, OUT_BN):
            cols = pl.ds(c0, OUT_BN)
            acc = _dot(br_ref[...], wo_ref[:, cols]) + xk_ref[:, cols]
            y_ref[:, cols] = acc
            ssq = ssq + jnp.sum(acc * acc, axis=-1, keepdims=True)
            yield
        y_ref[...] = y_ref[...] * lax.rsqrt(ssq * (1.0 / D_MODEL) + EPS) * fw_ref[...]
        yield

    def run(order, streams):
        for name in order:
            next(streams[name])
        for name, gen in streams.items():
            assert next(gen, "done") == "done", name

    def mixer_stages(pc_ref):
        return _prompt_block(pc_ref, br_ref, mk_ref, mv_ref, wa_ref, abt_ref, avw_ref, bwa_ref,
                             bba_ref, onw_ref, sbd_ref, o_ref, la_ref)

    n_p = 1 + -(-D_PROJ_PAD // IN_TILE)
    n_m = N_MIXER_STAGES
    n_o = 1 + D_MODEL // OUT_BN

    def interleave(counts):
        order = []
        for i in range(max(counts.values())):
            order += [name for name, n in counts.items() if i < n]
        return order

    def body(pn_ref, pc_ref):
        order = ["P"] + interleave({"M": n_m, "P": n_p - 1}) + ["O"] * n_o
        run(order, {"P": in_proj_stages(pn_ref), "M": mixer_stages(pc_ref), "O": out_proj_stages()})

    last = pl.num_programs(0) - 1

    @pl.when(s == 0)
    def _():
        run(["P"] * n_p, {"P": in_proj_stages(pa_ref)})

    @pl.when((lax.rem(s, 2) == 0) & (s > 0) & (s < last))
    def _():
        body(pa_ref, pb_ref)

    @pl.when(lax.rem(s, 2) == 1)
    def _():
        body(pb_ref, pa_ref)

    @pl.when(s == last)
    def _():
        run(["M"] * n_m + ["O"] * n_o, {"M": mixer_stages(pb_ref), "O": out_proj_stages()})

    @pl.when((t == nt - 1) & (s > 0))
    def _():
        for h in range(H_B):
            off = h * DV_B - WIN_START[h]
            blk = sbd_ref[h * DK_B:(h + 1) * DK_B, WIN_START[h]:WIN_START[h] + WIN]
            if off:
                blk = pltpu.roll(blk, WIN - off, 1)
            st_ref[0, h] = blk[:, :DV_B]


def _prompt_layer(xp, w_in_bf, w_out_bf, nw, fw, memkv, wa, abt, avw, bwa, bba, onw, *, batch, seq):
    nt = seq // PT
    nblk = batch * nt
    assert nblk % 2 == 0
    cur = lambda s: jnp.maximum(s - 1, 0)
    const = lambda *shape: pl.BlockSpec(shape, lambda s: (0,) * len(shape))
    resident = lambda *shape: pl.BlockSpec(shape, lambda s: (0,) * len(shape),
                                           pipeline_mode=pl.Buffered(1))
    return pl.pallas_call(
        functools.partial(_prompt_layer_kernel, nt=nt),
        grid=(nblk + 1,),
        in_specs=[pl.BlockSpec((PT, D_MODEL), lambda s: (jnp.minimum(s, nblk - 1), 0)),
                  resident(D_MODEL, D_PROJ_PAD), resident(D_MODEL, D_MODEL),
                  const(1, D_MODEL), const(1, D_MODEL),
                  pl.BlockSpec((None, N_MEM, D_X), lambda s: (0, cur(s) // nt, 0)),
                  pl.BlockSpec((None, N_MEM, D_X), lambda s: (1, cur(s) // nt, 0)),
                  const(H_A, CHUNK_A, CHUNK_A), const(CHUNK_A, H_A), const(1, D_A),
                  const(LANES, D_BK), const(1, D_BK), const(1, D_B)],
        out_specs=[pl.BlockSpec((PT, D_MODEL), lambda s: (cur(s), 0)),
                   pl.BlockSpec((1, H_B, DK_B, DV_B), lambda s: (cur(s) // nt, 0, 0, 0))],
        out_shape=[jax.ShapeDtypeStruct((batch * seq, D_MODEL), F32),
                   jax.ShapeDtypeStruct((batch, H_B, DK_B, DV_B), F32)],
        scratch_shapes=[pltpu.VMEM((PT, D_PROJ_PAD), BF16),
                        pltpu.VMEM((PT, D_PROJ_PAD), BF16),
                        pltpu.VMEM((PT, D_MODEL), F32),
                        pltpu.VMEM((PT, D_MODEL), BF16),
                        pltpu.VMEM((PT, D_MODEL), BF16),
                        pltpu.VMEM((D_BK, D_B), F32),
                        pltpu.VMEM((PT, D_B), F32),
                        pltpu.VMEM((PT, D_BK), F32)],
        compiler_params=pltpu.CompilerParams(
            dimension_semantics=("arbitrary",), vmem_limit_bytes=VMEM_LIMIT),
        name="prompt_layer",
    )(xp, w_in_bf, w_out_bf, nw.reshape(1, D_MODEL), fw.reshape(1, D_MODEL), memkv, memkv,
      wa, abt, avw, bwa, bba, onw)


NS = 8
TS = 8
SB = NS * TS
SBP = 128


def _sample_mixer_kernel(proj_ref, st_ref, ck_ref, cv_ref, wa_ref, abt_ref, avw_ref, bwa_ref,
                         bba_ref, onw_ref, out_ref, stn_ref, cvs_ref,
                         qin_ref, xq_ref, kot_ref, lat_ref, vhm_ref, ghm_ref, ohm_ref, ox_ref):
    ri = _iota((SB, SB), 0)
    ci = _iota((SB, SB), 1)
    same_seq = jnp.right_shift(ri, 3) == jnp.right_shift(ci, 3)
    causal = same_seq & (ci <= ri)

    a_br, vn = _group_a(proj_ref, wa_ref, abt_ref, avw_ref, causal)
    out_ref[:, 0:D_A] = a_br.astype(BF16)
    cvs_ref[...] = vn

    q = proj_ref[:, C_BQ:C_BQ + D_BK].astype(F32)
    k = proj_ref[:, C_BK:C_BK + D_BK].astype(F32)
    vb = proj_ref[:, C_BV:C_BV + D_B]
    bgb = proj_ref[:, C_BG:C_BG + D_B]
    pre = _dot(proj_ref[:, C_BR:C_BR + LANES], bwa_ref[...]) + bba_ref[...]
    log_a = _log_sigmoid(pre) * (1.0 / GATE_TAU)
    causal_bf = jnp.where(causal, 1.0, 0.0).astype(BF16)
    seq_bf = jnp.where(same_seq, 1.0, 0.0).astype(BF16)
    hi, lo = _split(log_a)
    cum = _dot(causal_bf, hi) + _dot(causal_bf, lo)
    tot = _dot(seq_bf, hi) + _dot(seq_bf, lo)
    q_in = q * (DK_B ** -0.5) * jnp.exp(cum)
    k_in = k * jnp.exp(-cum)
    k_out = k * jnp.exp(tot - cum)
    qin_ref[...] = q_in
    zpad = jnp.zeros((SBP - SB, D_BK), F32)
    kot_ref[...] = jnp.concatenate([k_out, zpad], axis=0).T.astype(BF16)
    lat_ref[...] = jnp.concatenate([log_a, zpad], axis=0).T
    xq_ref[...] = proj_ref[:, C_XQ:C_XQ + D_X].astype(F32)

    wins = _gla_intra_windows(q_in, k_in, vb, causal)
    sel_r = _iota((D_B, DV_B), 0)
    sel_c = _iota((D_B, DV_B), 1)
    vhm_ref[...] = jnp.zeros_like(vhm_ref)
    for h in range(H_B):
        off = h * DV_B - WIN_START[h]
        w = wins[h]
        if off:
            w = pltpu.roll(w, WIN - off, 1)
        ohm_ref[h] = w[:, :DV_B]
        sel = jnp.where(sel_r == sel_c + h * DV_B, 1.0, 0.0).astype(BF16)
        vhm_ref[h, 0:SB, :] = _dot(vb, sel).astype(BF16)
        ghm_ref[h] = _dot(bgb, sel)

    mask_x = (jnp.right_shift(_iota((H_X * TS, N_MEM * H_X), 0), 3)
              == jnp.bitwise_and(_iota((H_X * TS, N_MEM * H_X), 1), H_X - 1))
    mask_b = jnp.right_shift(_iota((H_B * TS, D_BK), 0), 3) == _head_id(_iota((H_B * TS, D_BK), 1), DK_B, H_B)
    lane_seq = jnp.right_shift(_iota((D_BK, SBP), 1), 3)

    def per_seq(s, carry):
        r0 = pl.multiple_of(s * TS, TS)
        q8 = xq_ref[pl.ds(r0, TS), :]
        q32 = jnp.concatenate([q8[:, h * HD_X:(h + 1) * HD_X] for h in range(H_X)], axis=0)
        sc = _dot_nt(q32.astype(BF16), ck_ref[s].astype(BF16)) * (HD_X ** -0.5)
        sc = jnp.where(mask_x, sc, -1e30)
        e = jnp.exp(sc - jnp.max(sc, axis=-1, keepdims=True))
        den = jnp.sum(e, axis=-1, keepdims=True)
        o = _dot(e.astype(BF16), cv_ref[s].astype(BF16)) / den
        ox_ref[pl.ds(r0, TS), :] = jnp.concatenate(
            [o[h * TS:(h + 1) * TS] for h in range(H_X)], axis=1)
        qi8 = qin_ref[pl.ds(r0, TS), :]
        qbd2 = jnp.where(mask_b, jnp.concatenate([qi8] * H_B, axis=0), 0.0).astype(BF16)
        s0 = jnp.concatenate([st_ref[s, h] for h in range(H_B)], axis=0)
        o_inter = _dot(qbd2, s0.astype(BF16))
        for h in range(H_B):
            ohm_ref[h, pl.ds(r0, TS), :] += o_inter[h * TS:(h + 1) * TS]
        dec = jnp.exp(jnp.sum(jnp.where(lane_seq == s, lat_ref[...], 0.0), axis=1, keepdims=True))
        kot = jnp.where(lane_seq == s, kot_ref[...], jnp.zeros((), BF16))
        kv = jnp.concatenate(
            [_dot(kot[h * DK_B:(h + 1) * DK_B], vhm_ref[h]) for h in range(H_B)], axis=0)
        s_new = s0 * dec + kv
        for h in range(H_B):
            stn_ref[s, h] = s_new[h * DK_B:(h + 1) * DK_B]
        return carry

    lax.fori_loop(0, NS, per_seq, 0, unroll=4)

    selt_r = _iota((DV_B, D_B), 0)
    selt_c = _iota((DV_B, D_B), 1)
    b_br = jnp.zeros((SB, D_B), F32)
    for h in range(H_B):
        o_h = ohm_ref[h]
        ms = jnp.mean(o_h * o_h, axis=-1, keepdims=True)
        ob = (o_h * lax.rsqrt(ms + EPS) * onw_ref[...] * _silu(ghm_ref[h])).astype(BF16)
        selt = jnp.where(selt_c == selt_r + h * DV_B, 1.0, 0.0).astype(BF16)
        b_br = b_br + _dot(ob, selt)
    out_ref[:, D_A:D_A + D_B] = b_br.astype(BF16)

    xg = proj_ref[:, C_XG:C_XG + D_X].astype(F32)
    out_ref[:, D_A + D_B:D_MODEL] = (ox_ref[...] * _silu(xg)).astype(BF16)


def _sample_mixer(proj, state, ck, cv, wa, abt, avw, bwa, bba, onw):
    nseq = state.shape[1]
    const = lambda *shape: pl.BlockSpec(shape, lambda i: (0,) * len(shape))
    return pl.pallas_call(
        _sample_mixer_kernel,
        grid=(nseq // NS,),
        in_specs=[pl.BlockSpec((SB, D_PROJ), lambda i: (i, 0)),
                  pl.BlockSpec((None, NS, H_B, DK_B, DV_B), lambda i: (0, i, 0, 0, 0)),
                  pl.BlockSpec((NS, N_MEM * H_X, HD_X), lambda i: (i, 0, 0)),
                  pl.BlockSpec((NS, N_MEM * H_X, HD_X), lambda i: (i, 0, 0)),
                  const(H_A, SB, SB), const(SB, H_A), const(1, D_A),
                  const(LANES, D_BK), const(1, D_BK), const(1, DV_B)],
        out_specs=[pl.BlockSpec((SB, D_MODEL), lambda i: (i, 0)),
                   pl.BlockSpec((None, NS, H_B, DK_B, DV_B), lambda i: (0, i, 0, 0, 0)),
                   pl.BlockSpec((SB, D_A), lambda i: (i, 0))],
        out_shape=[jax.ShapeDtypeStruct((nseq * TS, D_MODEL), BF16),
                   jax.ShapeDtypeStruct((1, nseq, H_B, DK_B, DV_B), F32),
                   jax.ShapeDtypeStruct((nseq * TS, D_A), F32)],
        scratch_shapes=[pltpu.VMEM((SB, D_BK), F32),
                        pltpu.VMEM((SB, D_X), F32),
                        pltpu.VMEM((D_BK, SBP), BF16),
                        pltpu.VMEM((D_BK, SBP), F32),
                        pltpu.VMEM((H_B, SBP, DV_B), BF16),
                        pltpu.VMEM((H_B, SB, DV_B), F32),
                        pltpu.VMEM((H_B, SB, DV_B), F32),
                        pltpu.VMEM((SB, D_X), F32)],
        compiler_params=pltpu.CompilerParams(
            dimension_semantics=("arbitrary",), vmem_limit_bytes=VMEM_LIMIT),
        name="sample_mixer",
    )(proj, state, ck, cv, wa, abt, avw, bwa, bba, onw)


def kernel(x_prompt, x_sample, mem_prompt, state_gla, cache_mem_k, cache_mem_v, norm_w, w_in,
           a_vnorm_w, a_ws, a_bs, b_wa, b_ba, b_onorm_w, mem_norm_w, w_mem_kv, w_out, final_norm_w):
    batch, seq, _ = x_prompt.shape
    nseq, tdec, _ = x_sample.shape
    depth = w_in.shape[0]
    assert depth == 1 and tdec == TS and seq % PT == 0 and nseq % NS == 0

    xp = x_prompt.reshape(batch * seq, D_MODEL)
    xs = x_sample.reshape(nseq * TS, D_MODEL)
    mem = mem_prompt.reshape(batch * N_MEM, D_MODEL)
    w_in_bf, w_out_bf, proj_s, memkv, mem_k, mem_v = _weight_prep(
        jnp.transpose(w_in[0]), w_out[0], w_mem_kv[0], xs, norm_w[0], mem, mem_norm_w[0])
    bwa = jnp.concatenate([b_wa[0], jnp.zeros((LANES - GATE_RANK, D_BK), F32)], axis=0).astype(BF16)
    bba = b_ba[0].reshape(1, D_BK)
    avw = a_vnorm_w[0].reshape(1, D_A)
    onw_p = jnp.tile(b_onorm_w[0], H_B).reshape(1, D_B)
    onw_s = b_onorm_w[0].reshape(1, DV_B)
    wa_p = a_ws[0]
    abt_p = a_bs[0].T
    wa_s = jnp.tile(a_ws[0][:, :TS, :TS], (1, NS, NS))
    abt_s = jnp.tile(a_bs[0][:, :TS], (1, NS)).T

    br_s, st_s, cvs = _sample_mixer(
        proj_s, state_gla,
        cache_mem_k.reshape(nseq, N_MEM * H_X, HD_X), cache_mem_v.reshape(nseq, N_MEM * H_X, HD_X),
        wa_s, abt_s, avw, bwa, bba, onw_s)
    y_s = _out_proj(br_s, w_out_bf, xs, final_norm_w, bm=256, name="out_proj_s")

    y_p, st_p = _prompt_layer(xp, w_in_bf, w_out_bf, norm_w[0], final_norm_w, memkv, wa_p, abt_p,
                              avw, bwa, bba, onw_p, batch=batch, seq=seq)

    return (y_p.reshape(batch, seq, D_MODEL),
            y_s.reshape(nseq, TS, D_MODEL),
            mem_k.reshape(1, batch, N_MEM, H_X, HD_X),
            mem_v.reshape(1, batch, N_MEM, H_X, HD_X),
            st_p.reshape(1, batch, H_B, DK_B, DV_B),
            st_s,
            cvs.reshape(1, nseq, TS, D_A))
```

```python
import functools

import jax
import jax.numpy as jnp
from jax import lax
from jax.experimental import pallas as pl
from jax.experimental.pallas import tpu as pltpu

F32 = jnp.float32
BF16 = jnp.bfloat16

D_MODEL = 2048
D_A = 768
H_A = 4
HD_A = 192
CHUNK_A = 128
D_B = 768
H_B = 4
DV_B = 192
DK_B = 96
D_BK = 384
GATE_RANK = 16
GATE_TAU = 16.0
CHUNK_B = 64
D_X = 512
H_X = 4
HD_X = 128
N_MEM = 256
EPS = 1e-6

LANES = 128
MXU_COLS = 256
C_AU, C_AV, C_AG = 0, 768, 1536
C_BQ, C_BK, C_BV, C_BG = 2304, 2688, 3072, 3840
C_XQ, C_XG, C_BR = 4608, 5120, 5632
D_PROJ = 5760
D_PROJ_PAD = 5888
WIN_START = (0, 128, 384, 512)
WIN = 256
HEAD_SPLIT = DV_B - LANES
OUT_BN = 512

VMEM_LIMIT = 60 * 1024 * 1024


def _dot(a, b):
    return jnp.dot(a, b, preferred_element_type=F32)


def _dot_nt(a, b):
    return lax.dot_general(a, b, (((1,), (1,)), ((), ())), preferred_element_type=F32)


def _split(x):
    hi = x.astype(BF16)
    lo = (x - hi.astype(F32)).astype(BF16)
    return hi, lo


def _silu(x):
    return x / (1.0 + jnp.exp(-x))


def _log_sigmoid(x):
    return jnp.minimum(x, 0.0) - jnp.log1p(jnp.exp(-jnp.abs(x)))


def _head_id(idx, width, n):
    h = jnp.zeros_like(idx)
    for i in range(1, n):
        h = h + (idx >= i * width).astype(jnp.int32)
    return h


def _iota(shape, dim):
    return lax.broadcasted_iota(jnp.int32, shape, dim)


def _merge_windows(r):
    rows = r[0].shape[0]
    lo = _iota((rows, LANES), 1) < HEAD_SPLIT
    tiles = [r[0][:, :LANES], jnp.where(lo, r[0][:, LANES:], r[1][:, :LANES]), r[1][:, LANES:],
             r[2][:, :LANES], jnp.where(lo, r[2][:, LANES:], r[3][:, :LANES]), r[3][:, LANES:]]
    return jnp.concatenate(tiles, axis=1)


def _group_a(proj_ref, wa_ref, abt_ref, avw_ref, mask):
    u = proj_ref[:, C_AU:C_AU + D_A].astype(F32)
    v = proj_ref[:, C_AV:C_AV + D_A].astype(F32)
    g = proj_ref[:, C_AG:C_AG + D_A].astype(F32)
    rows = u.shape[0]
    ms = jnp.mean(v * v, axis=-1, keepdims=True)
    vn = v * lax.rsqrt(ms + EPS) * avw_ref[...]
    vb = vn.astype(BF16)
    r = []
    for gi in range(H_A):
        w = jnp.where(mask, wa_ref[gi], 0.0).astype(BF16)
        r.append(_dot(w, vb[:, WIN_START[gi]:WIN_START[gi] + WIN]))
    mixed = _merge_windows(r)
    col_g = _head_id(_iota((rows, D_A), 1), HD_A, H_A)
    bias = jnp.zeros((rows, D_A), F32)
    for gi in range(H_A):
        bias = jnp.where(col_g == gi, abt_ref[:, gi:gi + 1], bias)
    return u * (mixed + bias) * _silu(g), vn


def _gla_intra_windows(q_in, k_in, vb, mask):
    rows = q_in.shape[0]
    col_h = _head_id(_iota((rows, D_BK), 1), DK_B, H_B)
    kb = k_in.astype(BF16)
    r = []
    for h in range(H_B):
        qh = jnp.where(col_h == h, q_in, 0.0).astype(BF16)
        s = jnp.where(mask, _dot_nt(qh, kb), 0.0).astype(BF16)
        r.append(_dot(s, vb[:, WIN_START[h]:WIN_START[h] + WIN]))
    return r


PREP_BN = MXU_COLS
N_MAIN_BLOCKS = C_BG // PREP_BN


GATE_BLOCK = D_PROJ_PAD // PREP_BN - 1
N_IN_BLOCKS = GATE_BLOCK + 1
N_OUT_BLOCKS = D_MODEL // PREP_BN
PREP_BUFS = 4


def _weight_prep_kernel(wt_hbm, wo_hbm, wm_hbm, xs_ref, nw_ref, mem_ref, mnw_ref,
                        oi_ref, oo_ref, ps_ref, mkv_ref, mk_ref, mv_ref,
                        buf, sem, hs_ref, hm_ref, wm_buf, wm_sem):
    r = pl.program_id(0)
    slot = lax.rem(r, PREP_BUFS)
    mem_step0 = N_IN_BLOCKS + N_OUT_BLOCKS

    def mem_copy(j):
        return pltpu.make_async_copy(wm_hbm.at[:, pl.ds(j * PREP_BN, PREP_BN)], wm_buf.at[j],
                                     wm_sem.at[j])

    for j in range(N_MEM_BLOCKS):
        @pl.when(r == N_IN_BLOCKS + j)
        def _(j=j):
            mem_copy(j).start()

    @pl.when(r == N_IN_BLOCKS)
    def _():
        x = mem_ref[...]
        ms = jnp.mean(x * x, axis=-1, keepdims=True)
        hm_ref[...] = (x * lax.rsqrt(ms + EPS) * mnw_ref[...]).astype(BF16)

    for j in range(N_MEM_BLOCKS):
        @pl.when(r == mem_step0 + j)
        def _(j=j):
            mem_copy(j).wait()
            kv = _dot(hm_ref[...], wm_buf[j].astype(BF16))
            mkv_ref[...] = kv
            dst = mk_ref if j * PREP_BN < D_X else mv_ref
            for hh in range(PREP_BN // HD_X):
                head = (j * PREP_BN % D_X) // HD_X + hh
                for b in range(kv.shape[0] // N_MEM):
                    dst[b, pl.ds(head, N_MEM, stride=H_X), :] = (
                        kv[b * N_MEM:(b + 1) * N_MEM, hh * HD_X:(hh + 1) * HD_X])

    def in_copy(rr, sl):
        start = pl.multiple_of(jnp.where(rr < N_MAIN_BLOCKS, rr * PREP_BN, rr * PREP_BN + GATE_RANK), 8)
        return pltpu.make_async_copy(wt_hbm.at[pl.ds(start, PREP_BN), :], buf.at[sl], sem.at[sl])

    def gate_copy(sl):
        return pltpu.make_async_copy(wt_hbm.at[pl.ds(C_BG, GATE_RANK), :],
                                     buf.at[sl, pl.ds(0, GATE_RANK), :], sem.at[sl])

    def out_copy(rr, sl):
        start = pl.multiple_of((rr - N_IN_BLOCKS) * PREP_BN, PREP_BN)
        return pltpu.make_async_copy(wo_hbm.at[pl.ds(start, PREP_BN), :], buf.at[sl], sem.at[sl])

    def start_fetch(rr, sl):
        @pl.when(rr < GATE_BLOCK)
        def _():
            in_copy(rr, sl).start()

        @pl.when(rr == GATE_BLOCK)
        def _():
            gate_copy(sl).start()

        @pl.when(rr > GATE_BLOCK)
        def _():
            out_copy(rr, sl).start()

    @pl.when(r == 0)
    def _():
        for ahead in range(PREP_BUFS - 1):
            start_fetch(r + ahead, ahead)

    nxt = r + PREP_BUFS - 1

    @pl.when(nxt < mem_step0)
    def _():
        start_fetch(nxt, lax.rem(nxt, PREP_BUFS))

    @pl.when(r == 0)
    def _():
        x = xs_ref[...]
        ms = jnp.mean(x * x, axis=-1, keepdims=True)
        hs_ref[...] = (x * lax.rsqrt(ms + EPS) * nw_ref[...]).astype(BF16)

    @pl.when(r < GATE_BLOCK)
    def _():
        in_copy(r, slot).wait()
        oi_ref[...] = buf[slot].T.astype(BF16)
        ps_ref[...] = _dot(hs_ref[...], oi_ref[...]).astype(BF16)

    @pl.when(r == GATE_BLOCK)
    def _():
        gate_copy(slot).wait()
        rows = jnp.concatenate([buf[slot, 0:GATE_RANK, :],
                                jnp.zeros((PREP_BN - GATE_RANK, D_MODEL), F32)], axis=0)
        oi_ref[...] = rows.T.astype(BF16)
        ps_ref[...] = _dot(hs_ref[...], oi_ref[...]).astype(BF16)

    @pl.when((r > GATE_BLOCK) & (r < mem_step0))
    def _():
        out_copy(r, slot).wait()
        oo_ref[...] = buf[slot].astype(BF16)


N_MEM_BLOCKS = 2 * D_X // PREP_BN


def _weight_prep(w_t, w_out, w_mem, xs, nw, mem, mnw):
    k = w_t.shape[1]
    m = xs.shape[0]
    mm = mem.shape[0]
    batch = mm // N_MEM
    mem_step0 = N_IN_BLOCKS + N_OUT_BLOCKS
    in_block = lambda r: (0, jnp.minimum(r, GATE_BLOCK))
    mem_j = lambda r: jnp.maximum(r - mem_step0, 0)
    per_d_x = D_X // PREP_BN
    resident = lambda *shape: pl.BlockSpec(shape, lambda r: (0,) * len(shape),
                                           pipeline_mode=pl.Buffered(1))
    return pl.pallas_call(
        _weight_prep_kernel,
        grid=(mem_step0 + N_MEM_BLOCKS,),
        in_specs=[pl.BlockSpec(memory_space=pl.ANY), pl.BlockSpec(memory_space=pl.ANY),
                  pl.BlockSpec(memory_space=pl.ANY),
                  resident(m, k), pl.BlockSpec((1, k), lambda r: (0, 0)),
                  resident(mm, k), pl.BlockSpec((1, k), lambda r: (0, 0))],
        out_specs=[pl.BlockSpec((k, PREP_BN), in_block),
                   pl.BlockSpec((PREP_BN, D_MODEL),
                                lambda r: (jnp.clip(r - N_IN_BLOCKS, 0, N_OUT_BLOCKS - 1), 0)),
                   pl.BlockSpec((m, PREP_BN), in_block),
                   pl.BlockSpec((None, mm, PREP_BN),
                                lambda r: (mem_j(r) // per_d_x, 0, lax.rem(mem_j(r), per_d_x))),
                   pl.BlockSpec((batch, N_MEM * H_X, HD_X), lambda r: (0, 0, 0)),
                   pl.BlockSpec((batch, N_MEM * H_X, HD_X), lambda r: (0, 0, 0))],
        out_shape=[jax.ShapeDtypeStruct((k, D_PROJ_PAD), BF16),
                   jax.ShapeDtypeStruct((D_MODEL, D_MODEL), BF16),
                   jax.ShapeDtypeStruct((m, D_PROJ_PAD), BF16),
                   jax.ShapeDtypeStruct((2, mm, D_X), F32),
                   jax.ShapeDtypeStruct((batch, N_MEM * H_X, HD_X), F32),
                   jax.ShapeDtypeStruct((batch, N_MEM * H_X, HD_X), F32)],
        scratch_shapes=[pltpu.VMEM((PREP_BUFS, PREP_BN, k), F32),
                        pltpu.SemaphoreType.DMA((PREP_BUFS,)),
                        pltpu.VMEM((m, k), BF16),
                        pltpu.VMEM((mm, k), BF16),
                        pltpu.VMEM((N_MEM_BLOCKS, k, PREP_BN), F32),
                        pltpu.SemaphoreType.DMA((N_MEM_BLOCKS,))],
        compiler_params=pltpu.CompilerParams(
            dimension_semantics=("arbitrary",), vmem_limit_bytes=VMEM_LIMIT),
        name="weight_prep",
    )(w_t, w_out, w_mem, xs, nw.reshape(1, k), mem, mnw.reshape(1, k))


def _out_proj_kernel(br_ref, w_ref, x_ref, fw_ref, y_ref):
    acc = _dot(br_ref[...], w_ref[...]) + x_ref[...]
    ms = jnp.mean(acc * acc, axis=-1, keepdims=True)
    y_ref[...] = acc * lax.rsqrt(ms + EPS) * fw_ref[...]


def _out_proj(br, w, x, fw, *, bm, name):
    m, k = br.shape
    n = w.shape[1]
    return pl.pallas_call(
        _out_proj_kernel,
        grid=(m // bm,),
        in_specs=[pl.BlockSpec((bm, k), lambda i: (i, 0)),
                  pl.BlockSpec((k, n), lambda i: (0, 0)),
                  pl.BlockSpec((bm, n), lambda i: (i, 0)),
                  pl.BlockSpec((1, n), lambda i: (0, 0))],
        out_specs=pl.BlockSpec((bm, n), lambda i: (i, 0)),
        out_shape=jax.ShapeDtypeStruct((m, n), F32),
        compiler_params=pltpu.CompilerParams(
            dimension_semantics=("arbitrary",), vmem_limit_bytes=VMEM_LIMIT),
        name=name,
    )(br, w, x, fw.reshape(1, n))


PT = 256
GLA_ROWS = 128
N_MIXER_STAGES = 2 + 4 * (PT // GLA_ROWS) + 1 + 2 * H_X


def _group_a_chunks(proj_ref, wa_ref, abt_ref, avw_ref):
    v = proj_ref[:, C_AV:C_AV + D_A].astype(F32)
    n_chunks = v.shape[0] // CHUNK_A
    ms = jnp.mean(v * v, axis=-1, keepdims=True)
    vb = (v * lax.rsqrt(ms + EPS) * avw_ref[...]).astype(BF16)
    tril = _iota((CHUNK_A, CHUNK_A), 1) <= _iota((CHUNK_A, CHUNK_A), 0)
    r = []
    for gi in range(H_A):
        w = jnp.where(tril, wa_ref[gi], 0.0).astype(BF16)
        ws = slice(WIN_START[gi], WIN_START[gi] + WIN)
        rhs = jnp.concatenate([vb[c * CHUNK_A:(c + 1) * CHUNK_A, ws] for c in range(n_chunks)], axis=1)
        res = _dot(w, rhs)
        r.append(jnp.concatenate([res[:, c * WIN:(c + 1) * WIN] for c in range(n_chunks)], axis=0))
    mixed = _merge_windows(r)
    col_g = _head_id(_iota((CHUNK_A, D_A), 1), HD_A, H_A)
    bias = jnp.zeros((CHUNK_A, D_A), F32)
    for gi in range(H_A):
        bias = jnp.where(col_g == gi, abt_ref[:, gi:gi + 1], bias)
    bias = jnp.concatenate([bias] * n_chunks, axis=0)
    u = proj_ref[:, C_AU:C_AU + D_A].astype(F32)
    g = proj_ref[:, C_AG:C_AG + D_A].astype(F32)
    return u * (mixed + bias) * _silu(g)


def _prompt_block(proj_ref, out_ref, mk_ref, mv_ref, wa_ref, abt_ref, avw_ref, bwa_ref, bba_ref,
                  onw_ref, sbd_ref, o_ref, la_ref):
    out_ref[:, 0:D_A] = _group_a_chunks(proj_ref, wa_ref, abt_ref, avw_ref).astype(BF16)
    yield

    pre = _dot(proj_ref[:, C_BR:C_BR + LANES], bwa_ref[...]) + bba_ref[...]
    la_ref[...] = _log_sigmoid(pre) * (1.0 / GATE_TAU)
    yield
    gl = GLA_ROWS
    tril_bf = jnp.where(_iota((gl, gl), 1) <= _iota((gl, gl), 0), 1.0, 0.0).astype(BF16)
    win_col = _iota((gl, WIN), 1)
    col_hq = _head_id(_iota((gl, D_BK), 1), DK_B, H_B)
    tril_heads = (_iota((H_B * gl, gl), 1) <= jnp.bitwise_and(_iota((H_B * gl, gl), 0), gl - 1))
    for c in range(PT // gl):
        sl = slice(c * gl, (c + 1) * gl)
        log_a = la_ref[sl, :]
        q = proj_ref[sl, C_BQ:C_BQ + D_BK].astype(F32)
        k = proj_ref[sl, C_BK:C_BK + D_BK].astype(F32)
        vb = proj_ref[sl, C_BV:C_BV + D_B]
        hi, lo = _split(log_a)
        cum2 = _dot(tril_bf, jnp.concatenate([hi, lo], axis=1))
        yield
        cum = cum2[:, :D_BK] + cum2[:, D_BK:]
        tot_row = cum[gl - 1:gl, :]
        mid_row = cum[CHUNK_B - 1:CHUNK_B, :]
        la_t = log_a.T
        q_s = q * (DK_B ** -0.5)
        q_dec = q_s * jnp.exp(cum)
        q_in = q_s * jnp.exp(cum - mid_row)
        k_in = k * jnp.exp(mid_row - cum)
        k_out = k * jnp.exp(tot_row - cum)
        q_heads = jnp.concatenate([jnp.where(col_hq == h, q_in, 0.0) for h in range(H_B)], axis=0)
        sc = _dot_nt(q_heads.astype(BF16), k_in.astype(BF16))
        yield
        sc = jnp.where(tril_heads, sc, 0.0).astype(BF16)
        o_intra = _merge_windows(
            [_dot(sc[h * gl:(h + 1) * gl], vb[:, WIN_START[h]:WIN_START[h] + WIN])
             for h in range(H_B)])
        s_old = sbd_ref[...]
        o_inter = _dot(q_dec.astype(BF16), s_old.astype(BF16))
        yield
        kot = k_out.T.astype(BF16)
        for h in range(H_B):
            rs = slice(h * DK_B, (h + 1) * DK_B)
            ws = slice(WIN_START[h], WIN_START[h] + WIN)
            dec = jnp.exp(jnp.sum(la_t[rs], axis=1, keepdims=True))
            lo_col = h * DV_B - WIN_START[h]
            in_head = (win_col >= lo_col) & (win_col < lo_col + DV_B)
            v_h = jnp.where(in_head, vb[:, ws], jnp.zeros((), BF16))
            kv = _dot(kot[rs], v_h)
            sbd_ref[rs, ws] = s_old[rs, ws] * dec + kv
        o_ref[sl, :] = o_intra + o_inter
        yield
    o = o_ref[...]
    o2 = o * o
    t = [o2[:, j * LANES:(j + 1) * LANES] for j in range(D_B // LANES)]
    lo = _iota((PT, LANES), 1) < HEAD_SPLIT
    t1a = jnp.where(lo, t[1], 0.0)
    t4a = jnp.where(lo, t[4], 0.0)
    sums = (t[0] + t1a, (t[1] - t1a) + t[2], t[3] + t4a, (t[4] - t4a) + t[5])
    inv = [lax.rsqrt(jnp.sum(x, axis=-1, keepdims=True) * (1.0 / DV_B) + EPS) for x in sums]
    inv = jnp.concatenate(
        [jnp.broadcast_to(inv[0], (PT, LANES)), jnp.where(lo, inv[0], inv[1]),
         jnp.broadcast_to(inv[1], (PT, LANES)), jnp.broadcast_to(inv[2], (PT, LANES)),
         jnp.where(lo, inv[2], inv[3]), jnp.broadcast_to(inv[3], (PT, LANES))], axis=1)
    o_n = o * inv * onw_ref[...]
    bg = proj_ref[:, C_BG:C_BG + D_B].astype(F32)
    out_ref[:, D_A:D_A + D_B] = (o_n * _silu(bg)).astype(BF16)
    yield

    for h in range(H_X):
        hs = slice(h * HD_X, (h + 1) * HD_X)
        qh = proj_ref[:, C_XQ + h * HD_X:C_XQ + (h + 1) * HD_X]
        s = _dot_nt(qh, mk_ref[:, hs].astype(BF16)) * (HD_X ** -0.5)
        yield
        e = jnp.exp(s - jnp.max(s, axis=-1, keepdims=True))
        den = jnp.sum(e, axis=-1, keepdims=True)
        ox = _dot(e.astype(BF16), mv_ref[:, hs].astype(BF16)) / den
        xg = proj_ref[:, C_XG + h * HD_X:C_XG + (h + 1) * HD_X].astype(F32)
        out_ref[:, D_A + D_B + h * HD_X:D_A + D_B + (h + 1) * HD_X] = (ox * _silu(xg)).astype(BF16)
        yield


def _prompt_layer_kernel(xn_ref, wi_ref, wo_ref, nw_ref, fw_ref, mk_ref, mv_ref, wa_ref, abt_ref,
                         avw_ref, bwa_ref, bba_ref, onw_ref, y_ref, st_ref,
                         pa_ref, pb_ref, xk_ref, br_ref, h_ref, sbd_ref, o_ref, la_ref, *, nt):
    s = pl.program_id(0)
    cur = jnp.maximum(s - 1, 0)
    t = lax.rem(cur, nt)

    @pl.when(t == 0)
    def _():
        sbd_ref[...] = jnp.zeros_like(sbd_ref)

    def in_proj_stages(pn_ref):
        x = xn_ref[...]
        ms = jnp.mean(x * x, axis=-1, keepdims=True)
        h_ref[...] = (x * lax.rsqrt(ms + EPS) * nw_ref[...]).astype(BF16)
        yield
        for c0 in range(0, D_PROJ_PAD, MXU_COLS):
            cols = pl.ds(c0, MXU_COLS)
            pn_ref[:, cols] = _dot(h_ref[...], wi_ref[:, cols]).astype(BF16)
            yield
        xk_ref[...] = xn_ref[...]

    def out_proj_stages():
        ssq = jnp.zeros((PT, 1), F32)
        for c0 in range(0, D_MODEL, OUT_BN):
            cols = pl.ds(c0, OUT_BN)
            acc = _dot(br_ref[...], wo_ref[:, cols]) + xk_ref[:, cols]
            y_ref[:, cols] = acc
            ssq = ssq + jnp.sum(acc * acc, axis=-1, keepdims=True)
            yield
        y_ref[...] = y_ref[...] * lax.rsqrt(ssq * (1.0 / D_MODEL) + EPS) * fw_ref[...]
        yield

    def run(order, streams):
        for name in order:
            next(streams[name])
        for name, gen in streams.items():
            assert next(gen, "done") == "done", name

    def mixer_stages(pc_ref):
        return _prompt_block(pc_ref, br_ref, mk_ref, mv_ref, wa_ref, abt_ref, avw_ref, bwa_ref,
                             bba_ref, onw_ref, sbd_ref, o_ref, la_ref)

    n_p = 1 + D_PROJ_PAD // MXU_COLS
    n_m = N_MIXER_STAGES
    n_o = 1 + D_MODEL // OUT_BN

    def interleave(counts):
        order = []
        for i in range(max(counts.values())):
            order += [name for name, n in counts.items() if i < n]
        return order

    def body(pn_ref, pc_ref):
        order = ["P"] + interleave({"M": n_m, "P": n_p - 1}) + ["O"] * n_o
        run(order, {"P": in_proj_stages(pn_ref), "M": mixer_stages(pc_ref), "O": out_proj_stages()})

    last = pl.num_programs(0) - 1

    @pl.when(s == 0)
    def _():
        run(["P"] * n_p, {"P": in_proj_stages(pa_ref)})

    @pl.when((lax.rem(s, 2) == 0) & (s > 0) & (s < last))
    def _():
        body(pa_ref, pb_ref)

    @pl.when(lax.rem(s, 2) == 1)
    def _():
        body(pb_ref, pa_ref)

    @pl.when(s == last)
    def _():
        run(["M"] * n_m + ["O"] * n_o, {"M": mixer_stages(pb_ref), "O": out_proj_stages()})

    @pl.when((t == nt - 1) & (s > 0))
    def _():
        for h in range(H_B):
            off = h * DV_B - WIN_START[h]
            blk = sbd_ref[h * DK_B:(h + 1) * DK_B, WIN_START[h]:WIN_START[h] + WIN]
            if off:
                blk = pltpu.roll(blk, WIN - off, 1)
            st_ref[0, h] = blk[:, :DV_B]


def _prompt_layer(xp, w_in_bf, w_out_bf, nw, fw, memkv, wa, abt, avw, bwa, bba, onw, *, batch, seq):
    nt = seq // PT
    nblk = batch * nt
    assert nblk % 2 == 0
    cur = lambda s: jnp.maximum(s - 1, 0)
    const = lambda *shape: pl.BlockSpec(shape, lambda s: (0,) * len(shape))
    resident = lambda *shape: pl.BlockSpec(shape, lambda s: (0,) * len(shape),
                                           pipeline_mode=pl.Buffered(1))
    return pl.pallas_call(
        functools.partial(_prompt_layer_kernel, nt=nt),
        grid=(nblk + 1,),
        in_specs=[pl.BlockSpec((PT, D_MODEL), lambda s: (jnp.minimum(s, nblk - 1), 0)),
                  resident(D_MODEL, D_PROJ_PAD), resident(D_MODEL, D_MODEL),
                  const(1, D_MODEL), const(1, D_MODEL),
                  pl.BlockSpec((None, N_MEM, D_X), lambda s: (0, cur(s) // nt, 0)),
                  pl.BlockSpec((None, N_MEM, D_X), lambda s: (1, cur(s) // nt, 0)),
                  const(H_A, CHUNK_A, CHUNK_A), const(CHUNK_A, H_A), const(1, D_A),
                  const(LANES, D_BK), const(1, D_BK), const(1, D_B)],
        out_specs=[pl.BlockSpec((PT, D_MODEL), lambda s: (cur(s), 0)),
                   pl.BlockSpec((1, H_B, DK_B, DV_B), lambda s: (cur(s) // nt, 0, 0, 0))],
        out_shape=[jax.ShapeDtypeStruct((batch * seq, D_MODEL), F32),
                   jax.ShapeDtypeStruct((batch, H_B, DK_B, DV_B), F32)],
        scratch_shapes=[pltpu.VMEM((PT, D_PROJ_PAD), BF16),
                        pltpu.VMEM((PT, D_PROJ_PAD), BF16),
                        pltpu.VMEM((PT, D_MODEL), F32),
                        pltpu.VMEM((PT, D_MODEL), BF16),
                        pltpu.VMEM((PT, D_MODEL), BF16),
                        pltpu.VMEM((D_BK, D_B), F32),
                        pltpu.VMEM((PT, D_B), F32),
                        pltpu.VMEM((PT, D_BK), F32)],
        compiler_params=pltpu.CompilerParams(
            dimension_semantics=("arbitrary",), vmem_limit_bytes=VMEM_LIMIT),
        name="prompt_layer",
    )(xp, w_in_bf, w_out_bf, nw.reshape(1, D_MODEL), fw.reshape(1, D_MODEL), memkv, memkv,
      wa, abt, avw, bwa, bba, onw)


NS = 8
TS = 8
SB = NS * TS
SBP = 128


def _sample_mixer_kernel(proj_ref, st_ref, ck_ref, cv_ref, wa_ref, abt_ref, avw_ref, bwa_ref,
                         bba_ref, onw_ref, out_ref, stn_ref, cvs_ref,
                         qin_ref, xq_ref, kot_ref, lat_ref, vhm_ref, ghm_ref, ohm_ref, ox_ref):
    ri = _iota((SB, SB), 0)
    ci = _iota((SB, SB), 1)
    same_seq = jnp.right_shift(ri, 3) == jnp.right_shift(ci, 3)
    causal = same_seq & (ci <= ri)

    a_br, vn = _group_a(proj_ref, wa_ref, abt_ref, avw_ref, causal)
    out_ref[:, 0:D_A] = a_br.astype(BF16)
    cvs_ref[...] = vn

    q = proj_ref[:, C_BQ:C_BQ + D_BK].astype(F32)
    k = proj_ref[:, C_BK:C_BK + D_BK].astype(F32)
    vb = proj_ref[:, C_BV:C_BV + D_B]
    bgb = proj_ref[:, C_BG:C_BG + D_B]
    pre = _dot(proj_ref[:, C_BR:C_BR + LANES], bwa_ref[...]) + bba_ref[...]
    log_a = _log_sigmoid(pre) * (1.0 / GATE_TAU)
    causal_bf = jnp.where(causal, 1.0, 0.0).astype(BF16)
    seq_bf = jnp.where(same_seq, 1.0, 0.0).astype(BF16)
    hi, lo = _split(log_a)
    cum = _dot(causal_bf, hi) + _dot(causal_bf, lo)
    tot = _dot(seq_bf, hi) + _dot(seq_bf, lo)
    q_in = q * (DK_B ** -0.5) * jnp.exp(cum)
    k_in = k * jnp.exp(-cum)
    k_out = k * jnp.exp(tot - cum)
    qin_ref[...] = q_in
    zpad = jnp.zeros((SBP - SB, D_BK), F32)
    kot_ref[...] = jnp.concatenate([k_out, zpad], axis=0).T.astype(BF16)
    lat_ref[...] = jnp.concatenate([log_a, zpad], axis=0).T
    xq_ref[...] = proj_ref[:, C_XQ:C_XQ + D_X].astype(F32)

    wins = _gla_intra_windows(q_in, k_in, vb, causal)
    sel_r = _iota((D_B, DV_B), 0)
    sel_c = _iota((D_B, DV_B), 1)
    vhm_ref[...] = jnp.zeros_like(vhm_ref)
    for h in range(H_B):
        off = h * DV_B - WIN_START[h]
        w = wins[h]
        if off:
            w = pltpu.roll(w, WIN - off, 1)
        ohm_ref[h] = w[:, :DV_B]
        sel = jnp.where(sel_r == sel_c + h * DV_B, 1.0, 0.0).astype(BF16)
        vhm_ref[h, 0:SB, :] = _dot(vb, sel).astype(BF16)
        ghm_ref[h] = _dot(bgb, sel)

    mask_x = (jnp.right_shift(_iota((H_X * TS, N_MEM * H_X), 0), 3)
              == jnp.bitwise_and(_iota((H_X * TS, N_MEM * H_X), 1), H_X - 1))
    mask_b = jnp.right_shift(_iota((H_B * TS, D_BK), 0), 3) == _head_id(_iota((H_B * TS, D_BK), 1), DK_B, H_B)
    lane_seq = jnp.right_shift(_iota((D_BK, SBP), 1), 3)

    def per_seq(s, carry):
        r0 = pl.multiple_of(s * TS, TS)
        q8 = xq_ref[pl.ds(r0, TS), :]
        q32 = jnp.concatenate([q8[:, h * HD_X:(h + 1) * HD_X] for h in range(H_X)], axis=0)
        sc = _dot_nt(q32.astype(BF16), ck_ref[s].astype(BF16)) * (HD_X ** -0.5)
        sc = jnp.where(mask_x, sc, -1e30)
        e = jnp.exp(sc - jnp.max(sc, axis=-1, keepdims=True))
        den = jnp.sum(e, axis=-1, keepdims=True)
        o = _dot(e.astype(BF16), cv_ref[s].astype(BF16)) / den
        ox_ref[pl.ds(r0, TS), :] = jnp.concatenate(
            [o[h * TS:(h + 1) * TS] for h in range(H_X)], axis=1)
        qi8 = qin_ref[pl.ds(r0, TS), :]
        qbd2 = jnp.where(mask_b, jnp.concatenate([qi8] * H_B, axis=0), 0.0).astype(BF16)
        s0 = jnp.concatenate([st_ref[s, h] for h in range(H_B)], axis=0)
        o_inter = _dot(qbd2, s0.astype(BF16))
        for h in range(H_B):
            ohm_ref[h, pl.ds(r0, TS), :] += o_inter[h * TS:(h + 1) * TS]
        dec = jnp.exp(jnp.sum(jnp.where(lane_seq == s, lat_ref[...], 0.0), axis=1, keepdims=True))
        kot = jnp.where(lane_seq == s, kot_ref[...], jnp.zeros((), BF16))
        kv = jnp.concatenate(
            [_dot(kot[h * DK_B:(h + 1) * DK_B], vhm_ref[h]) for h in range(H_B)], axis=0)
        s_new = s0 * dec + kv
        for h in range(H_B):
            stn_ref[s, h] = s_new[h * DK_B:(h + 1) * DK_B]
        return carry

    lax.fori_loop(0, NS, per_seq, 0, unroll=4)

    selt_r = _iota((DV_B, D_B), 0)
    selt_c = _iota((DV_B, D_B), 1)
    b_br = jnp.zeros((SB, D_B), F32)
    for h in range(H_B):
        o_h = ohm_ref[h]
        ms = jnp.mean(o_h * o_h, axis=-1, keepdims=True)
        ob = (o_h * lax.rsqrt(ms + EPS) * onw_ref[...] * _silu(ghm_ref[h])).astype(BF16)
        selt = jnp.where(selt_c == selt_r + h * DV_B, 1.0, 0.0).astype(BF16)
        b_br = b_br + _dot(ob, selt)
    out_ref[:, D_A:D_A + D_B] = b_br.astype(BF16)

    xg = proj_ref[:, C_XG:C_XG + D_X].astype(F32)
    out_ref[:, D_A + D_B:D_MODEL] = (ox_ref[...] * _silu(xg)).astype(BF16)


def _sample_mixer(proj, state, ck, cv, wa, abt, avw, bwa, bba, onw):
    nseq = state.shape[1]
    const = lambda *shape: pl.BlockSpec(shape, lambda i: (0,) * len(shape))
    return pl.pallas_call(
        _sample_mixer_kernel,
        grid=(nseq // NS,),
        in_specs=[pl.BlockSpec((SB, D_PROJ), lambda i: (i, 0)),
                  pl.BlockSpec((None, NS, H_B, DK_B, DV_B), lambda i: (0, i, 0, 0, 0)),
                  pl.BlockSpec((NS, N_MEM * H_X, HD_X), lambda i: (i, 0, 0)),
                  pl.BlockSpec((NS, N_MEM * H_X, HD_X), lambda i: (i, 0, 0)),
                  const(H_A, SB, SB), const(SB, H_A), const(1, D_A),
                  const(LANES, D_BK), const(1, D_BK), const(1, DV_B)],
        out_specs=[pl.BlockSpec((SB, D_MODEL), lambda i: (i, 0)),
                   pl.BlockSpec((None, NS, H_B, DK_B, DV_B), lambda i: (0, i, 0, 0, 0)),
                   pl.BlockSpec((SB, D_A), lambda i: (i, 0))],
        out_shape=[jax.ShapeDtypeStruct((nseq * TS, D_MODEL), BF16),
                   jax.ShapeDtypeStruct((1, nseq, H_B, DK_B, DV_B), F32),
                   jax.ShapeDtypeStruct((nseq * TS, D_A), F32)],
        scratch_shapes=[pltpu.VMEM((SB, D_BK), F32),
                        pltpu.VMEM((SB, D_X), F32),
                        pltpu.VMEM((D_BK, SBP), BF16),
                        pltpu.VMEM((D_BK, SBP), F32),
                        pltpu.VMEM((H_B, SBP, DV_B), BF16),
                        pltpu.VMEM((H_B, SB, DV_B), F32),
                        pltpu.VMEM((H_B, SB, DV_B), F32),
                        pltpu.VMEM((SB, D_X), F32)],
        compiler_params=pltpu.CompilerParams(
            dimension_semantics=("arbitrary",), vmem_limit_bytes=VMEM_LIMIT),
        name="sample_mixer",
    )(proj, state, ck, cv, wa, abt, avw, bwa, bba, onw)


def kernel(x_prompt, x_sample, mem_prompt, state_gla, cache_mem_k, cache_mem_v, norm_w, w_in,
           a_vnorm_w, a_ws, a_bs, b_wa, b_ba, b_onorm_w, mem_norm_w, w_mem_kv, w_out, final_norm_w):
    batch, seq, _ = x_prompt.shape
    nseq, tdec, _ = x_sample.shape
    depth = w_in.shape[0]
    assert depth == 1 and tdec == TS and seq % PT == 0 and nseq % NS == 0

    xp = x_prompt.reshape(batch * seq, D_MODEL)
    xs = x_sample.reshape(nseq * TS, D_MODEL)
    mem = mem_prompt.reshape(batch * N_MEM, D_MODEL)
    w_in_bf, w_out_bf, proj_s, memkv, mem_k, mem_v = _weight_prep(
        jnp.transpose(w_in[0]), w_out[0], w_mem_kv[0], xs, norm_w[0], mem, mem_norm_w[0])
    bwa = jnp.concatenate([b_wa[0], jnp.zeros((LANES - GATE_RANK, D_BK), F32)], axis=0).astype(BF16)
    bba = b_ba[0].reshape(1, D_BK)
    avw = a_vnorm_w[0].reshape(1, D_A)
    onw_p = jnp.tile(b_onorm_w[0], H_B).reshape(1, D_B)
    onw_s = b_onorm_w[0].reshape(1, DV_B)
    wa_p = a_ws[0]
    abt_p = a_bs[0].T
    wa_s = jnp.tile(a_ws[0][:, :TS, :TS], (1, NS, NS))
    abt_s = jnp.tile(a_bs[0][:, :TS], (1, NS)).T

    br_s, st_s, cvs = _sample_mixer(
        proj_s, state_gla,
        cache_mem_k.reshape(nseq, N_MEM * H_X, HD_X), cache_mem_v.reshape(nseq, N_MEM * H_X, HD_X),
        wa_s, abt_s, avw, bwa, bba, onw_s)
    y_p, st_p = _prompt_layer(xp, w_in_bf, w_out_bf, norm_w[0], final_norm_w, memkv, wa_p, abt_p,
                              avw, bwa, bba, onw_p, batch=batch, seq=seq)
    y_s = _out_proj(br_s, w_out_bf, xs, final_norm_w, bm=256, name="out_proj_s")

    return (y_p.reshape(batch, seq, D_MODEL),
            y_s.reshape(nseq, TS, D_MODEL),
            mem_k.reshape(1, batch, N_MEM, H_X, HD_X),
            mem_v.reshape(1, batch, N_MEM, H_X, HD_X),
            st_p.reshape(1, batch, H_B, DK_B, DV_B),
            st_s,
            cvs.reshape(1, nseq, TS, D_A))
```

```python
import functools

import jax
import jax.numpy as jnp
from jax import lax
from jax.experimental import pallas as pl
from jax.experimental.pallas import tpu as pltpu

F32 = jnp.float32
BF16 = jnp.bfloat16

D_MODEL = 2048
D_A = 768
H_A = 4
HD_A = 192
CHUNK_A = 128
D_B = 768
H_B = 4
DV_B = 192
DK_B = 96
D_BK = 384
GATE_RANK = 16
GATE_TAU = 16.0
CHUNK_B = 64
D_X = 512
H_X = 4
HD_X = 128
N_MEM = 256
EPS = 1e-6

LANES = 128
MXU_COLS = 256
C_AU, C_AV, C_AG = 0, 768, 1536
C_BQ, C_BK, C_BV, C_BG = 2304, 2688, 3072, 3840
C_XQ, C_XG, C_BR = 4608, 5120, 5632
D_PROJ = 5760
D_PROJ_PAD = 5888
WIN_START = (0, 128, 384, 512)
WIN = 256
HEAD_SPLIT = DV_B - LANES
OUT_BN = 512

VMEM_LIMIT = 60 * 1024 * 1024


def _dot(a, b):
    return jnp.dot(a, b, preferred_element_type=F32)


def _dot_nt(a, b):
    return lax.dot_general(a, b, (((1,), (1,)), ((), ())), preferred_element_type=F32)


def _split(x):
    hi = x.astype(BF16)
    lo = (x - hi.astype(F32)).astype(BF16)
    return hi, lo


def _silu(x):
    return x / (1.0 + jnp.exp(-x))


def _log_sigmoid(x):
    return jnp.minimum(x, 0.0) - jnp.log1p(jnp.exp(-jnp.abs(x)))


def _head_id(idx, width, n):
    h = jnp.zeros_like(idx)
    for i in range(1, n):
        h = h + (idx >= i * width).astype(jnp.int32)
    return h


def _iota(shape, dim):
    return lax.broadcasted_iota(jnp.int32, shape, dim)


def _merge_windows(r):
    rows = r[0].shape[0]
    lo = _iota((rows, LANES), 1) < HEAD_SPLIT
    tiles = [r[0][:, :LANES], jnp.where(lo, r[0][:, LANES:], r[1][:, :LANES]), r[1][:, LANES:],
             r[2][:, :LANES], jnp.where(lo, r[2][:, LANES:], r[3][:, :LANES]), r[3][:, LANES:]]
    return jnp.concatenate(tiles, axis=1)


def _group_a(proj_ref, wa_ref, abt_ref, avw_ref, mask):
    u = proj_ref[:, C_AU:C_AU + D_A].astype(F32)
    v = proj_ref[:, C_AV:C_AV + D_A].astype(F32)
    g = proj_ref[:, C_AG:C_AG + D_A].astype(F32)
    rows = u.shape[0]
    ms = jnp.mean(v * v, axis=-1, keepdims=True)
    vn = v * lax.rsqrt(ms + EPS) * avw_ref[...]
    vb = vn.astype(BF16)
    r = []
    for gi in range(H_A):
        w = jnp.where(mask, wa_ref[gi], 0.0).astype(BF16)
        r.append(_dot(w, vb[:, WIN_START[gi]:WIN_START[gi] + WIN]))
    mixed = _merge_windows(r)
    col_g = _head_id(_iota((rows, D_A), 1), HD_A, H_A)
    bias = jnp.zeros((rows, D_A), F32)
    for gi in range(H_A):
        bias = jnp.where(col_g == gi, abt_ref[:, gi:gi + 1], bias)
    return u * (mixed + bias) * _silu(g), vn


def _gla_intra_windows(q_in, k_in, vb, mask):
    rows = q_in.shape[0]
    col_h = _head_id(_iota((rows, D_BK), 1), DK_B, H_B)
    kb = k_in.astype(BF16)
    r = []
    for h in range(H_B):
        qh = jnp.where(col_h == h, q_in, 0.0).astype(BF16)
        s = jnp.where(mask, _dot_nt(qh, kb), 0.0).astype(BF16)
        r.append(_dot(s, vb[:, WIN_START[h]:WIN_START[h] + WIN]))
    return r


PREP_BN = MXU_COLS
N_MAIN_BLOCKS = C_BG // PREP_BN


GATE_BLOCK = D_PROJ_PAD // PREP_BN - 1
N_IN_BLOCKS = GATE_BLOCK + 1
N_OUT_BLOCKS = D_MODEL // PREP_BN
PREP_BUFS = 4


def _weight_prep_kernel(wt_hbm, wo_hbm, wm_hbm, xs_ref, nw_ref, mem_ref, mnw_ref,
                        oi_ref, oo_ref, ps_ref, mkv_ref, mk_ref, mv_ref,
                        buf, sem, hs_ref, hm_ref, wm_buf, wm_sem):
    r = pl.program_id(0)
    slot = lax.rem(r, PREP_BUFS)
    mem_step0 = N_IN_BLOCKS + N_OUT_BLOCKS

    def mem_copy(j):
        return pltpu.make_async_copy(wm_hbm.at[:, pl.ds(j * PREP_BN, PREP_BN)], wm_buf.at[j],
                                     wm_sem.at[j])

    for j in range(N_MEM_BLOCKS):
        @pl.when(r == N_IN_BLOCKS + j)
        def _(j=j):
            mem_copy(j).start()

    @pl.when(r == N_IN_BLOCKS)
    def _():
        x = mem_ref[...]
        ms = jnp.mean(x * x, axis=-1, keepdims=True)
        hm_ref[...] = (x * lax.rsqrt(ms + EPS) * mnw_ref[...]).astype(BF16)

    for j in range(N_MEM_BLOCKS):
        @pl.when(r == mem_step0 + j)
        def _(j=j):
            mem_copy(j).wait()
            kv = _dot(hm_ref[...], wm_buf[j].astype(BF16))
            mkv_ref[...] = kv
            dst = mk_ref if j * PREP_BN < D_X else mv_ref
            for hh in range(PREP_BN // HD_X):
                head = (j * PREP_BN % D_X) // HD_X + hh
                for b in range(kv.shape[0] // N_MEM):
                    dst[b, pl.ds(head, N_MEM, stride=H_X), :] = (
                        kv[b * N_MEM:(b + 1) * N_MEM, hh * HD_X:(hh + 1) * HD_X])

    def in_copy(rr, sl):
        start = pl.multiple_of(jnp.where(rr < N_MAIN_BLOCKS, rr * PREP_BN, rr * PREP_BN + GATE_RANK), 8)
        return pltpu.make_async_copy(wt_hbm.at[pl.ds(start, PREP_BN), :], buf.at[sl], sem.at[sl])

    def gate_copy(sl):
        return pltpu.make_async_copy(wt_hbm.at[pl.ds(C_BG, GATE_RANK), :],
                                     buf.at[sl, pl.ds(0, GATE_RANK), :], sem.at[sl])

    def out_copy(rr, sl):
        start = pl.multiple_of((rr - N_IN_BLOCKS) * PREP_BN, PREP_BN)
        return pltpu.make_async_copy(wo_hbm.at[pl.ds(start, PREP_BN), :], buf.at[sl], sem.at[sl])

    def start_fetch(rr, sl):
        @pl.when(rr < GATE_BLOCK)
        def _():
            in_copy(rr, sl).start()

        @pl.when(rr == GATE_BLOCK)
        def _():
            gate_copy(sl).start()

        @pl.when(rr > GATE_BLOCK)
        def _():
            out_copy(rr, sl).start()

    @pl.when(r == 0)
    def _():
        for ahead in range(PREP_BUFS - 1):
            start_fetch(r + ahead, ahead)

    nxt = r + PREP_BUFS - 1

    @pl.when(nxt < mem_step0)
    def _():
        start_fetch(nxt, lax.rem(nxt, PREP_BUFS))

    @pl.when(r == 0)
    def _():
        x = xs_ref[...]
        ms = jnp.mean(x * x, axis=-1, keepdims=True)
        hs_ref[...] = (x * lax.rsqrt(ms + EPS) * nw_ref[...]).astype(BF16)

    @pl.when(r < GATE_BLOCK)
    def _():
        in_copy(r, slot).wait()
        oi_ref[...] = buf[slot].T.astype(BF16)
        ps_ref[...] = _dot(hs_ref[...], oi_ref[...]).astype(BF16)

    @pl.when(r == GATE_BLOCK)
    def _():
        gate_copy(slot).wait()
        rows = jnp.concatenate([buf[slot, 0:GATE_RANK, :],
                                jnp.zeros((PREP_BN - GATE_RANK, D_MODEL), F32)], axis=0)
        oi_ref[...] = rows.T.astype(BF16)
        ps_ref[...] = _dot(hs_ref[...], oi_ref[...]).astype(BF16)

    @pl.when((r > GATE_BLOCK) & (r < mem_step0))
    def _():
        out_copy(r, slot).wait()
        oo_ref[...] = buf[slot].astype(BF16)


N_MEM_BLOCKS = 2 * D_X // PREP_BN


def _weight_prep(w_t, w_out, w_mem, xs, nw, mem, mnw):
    k = w_t.shape[1]
    m = xs.shape[0]
    mm = mem.shape[0]
    batch = mm // N_MEM
    mem_step0 = N_IN_BLOCKS + N_OUT_BLOCKS
    in_block = lambda r: (0, jnp.minimum(r, GATE_BLOCK))
    mem_j = lambda r: jnp.maximum(r - mem_step0, 0)
    per_d_x = D_X // PREP_BN
    resident = lambda *shape: pl.BlockSpec(shape, lambda r: (0,) * len(shape),
                                           pipeline_mode=pl.Buffered(1))
    return pl.pallas_call(
        _weight_prep_kernel,
        grid=(mem_step0 + N_MEM_BLOCKS,),
        in_specs=[pl.BlockSpec(memory_space=pl.ANY), pl.BlockSpec(memory_space=pl.ANY),
                  pl.BlockSpec(memory_space=pl.ANY),
                  resident(m, k), pl.BlockSpec((1, k), lambda r: (0, 0)),
                  resident(mm, k), pl.BlockSpec((1, k), lambda r: (0, 0))],
        out_specs=[pl.BlockSpec((k, PREP_BN), in_block),
                   pl.BlockSpec((PREP_BN, D_MODEL),
                                lambda r: (jnp.clip(r - N_IN_BLOCKS, 0, N_OUT_BLOCKS - 1), 0)),
                   pl.BlockSpec((m, PREP_BN), in_block),
                   pl.BlockSpec((None, mm, PREP_BN),
                                lambda r: (mem_j(r) // per_d_x, 0, lax.rem(mem_j(r), per_d_x))),
                   pl.BlockSpec((batch, N_MEM * H_X, HD_X), lambda r: (0, 0, 0)),
                   pl.BlockSpec((batch, N_MEM * H_X, HD_X), lambda r: (0, 0, 0))],
        out_shape=[jax.ShapeDtypeStruct((k, D_PROJ_PAD), BF16),
                   jax.ShapeDtypeStruct((D_MODEL, D_MODEL), BF16),
                   jax.ShapeDtypeStruct((m, D_PROJ_PAD), BF16),
                   jax.ShapeDtypeStruct((2, mm, D_X), F32),
                   jax.ShapeDtypeStruct((batch, N_MEM * H_X, HD_X), F32),
                   jax.ShapeDtypeStruct((batch, N_MEM * H_X, HD_X), F32)],
        scratch_shapes=[pltpu.VMEM((PREP_BUFS, PREP_BN, k), F32),
                        pltpu.SemaphoreType.DMA((PREP_BUFS,)),
                        pltpu.VMEM((m, k), BF16),
                        pltpu.VMEM((mm, k), BF16),
                        pltpu.VMEM((N_MEM_BLOCKS, k, PREP_BN), F32),
                        pltpu.SemaphoreType.DMA((N_MEM_BLOCKS,))],
        compiler_params=pltpu.CompilerParams(
            dimension_semantics=("arbitrary",), vmem_limit_bytes=VMEM_LIMIT),
        name="weight_prep",
    )(w_t, w_out, w_mem, xs, nw.reshape(1, k), mem, mnw.reshape(1, k))


def _out_proj_kernel(br_ref, w_ref, x_ref, fw_ref, y_ref):
    acc = _dot(br_ref[...], w_ref[...]) + x_ref[...]
    ms = jnp.mean(acc * acc, axis=-1, keepdims=True)
    y_ref[...] = acc * lax.rsqrt(ms + EPS) * fw_ref[...]


def _out_proj(br, w, x, fw, *, bm, name):
    m, k = br.shape
    n = w.shape[1]
    return pl.pallas_call(
        _out_proj_kernel,
        grid=(m // bm,),
        in_specs=[pl.BlockSpec((bm, k), lambda i: (i, 0)),
                  pl.BlockSpec((k, n), lambda i: (0, 0)),
                  pl.BlockSpec((bm, n), lambda i: (i, 0)),
                  pl.BlockSpec((1, n), lambda i: (0, 0))],
        out_specs=pl.BlockSpec((bm, n), lambda i: (i, 0)),
        out_shape=jax.ShapeDtypeStruct((m, n), F32),
        compiler_params=pltpu.CompilerParams(
            dimension_semantics=("arbitrary",), vmem_limit_bytes=VMEM_LIMIT),
        name=name,
    )(br, w, x, fw.reshape(1, n))


PT = 256
GLA_ROWS = 128
N_MIXER_STAGES = 2 + 4 * (PT // GLA_ROWS) + 1 + 2 * H_X


def _group_a_chunks(proj_ref, wa_ref, abt_ref, avw_ref):
    v = proj_ref[:, C_AV:C_AV + D_A].astype(F32)
    n_chunks = v.shape[0] // CHUNK_A
    ms = jnp.mean(v * v, axis=-1, keepdims=True)
    vb = (v * lax.rsqrt(ms + EPS) * avw_ref[...]).astype(BF16)
    tril = _iota((CHUNK_A, CHUNK_A), 1) <= _iota((CHUNK_A, CHUNK_A), 0)
    r = []
    for gi in range(H_A):
        w = jnp.where(tril, wa_ref[gi], 0.0).astype(BF16)
        ws = slice(WIN_START[gi], WIN_START[gi] + WIN)
        rhs = jnp.concatenate([vb[c * CHUNK_A:(c + 1) * CHUNK_A, ws] for c in range(n_chunks)], axis=1)
        res = _dot(w, rhs)
        r.append(jnp.concatenate([res[:, c * WIN:(c + 1) * WIN] for c in range(n_chunks)], axis=0))
    mixed = _merge_windows(r)
    col_g = _head_id(_iota((CHUNK_A, D_A), 1), HD_A, H_A)
    bias = jnp.zeros((CHUNK_A, D_A), F32)
    for gi in range(H_A):
        bias = jnp.where(col_g == gi, abt_ref[:, gi:gi + 1], bias)
    bias = jnp.concatenate([bias] * n_chunks, axis=0)
    u = proj_ref[:, C_AU:C_AU + D_A].astype(F32)
    g = proj_ref[:, C_AG:C_AG + D_A].astype(F32)
    return u * (mixed + bias) * _silu(g)


def _prompt_block(proj_ref, out_ref, mk_ref, mv_ref, wa_ref, abt_ref, avw_ref, bwa_ref, bba_ref,
                  onw_ref, sbd_ref, o_ref, la_ref):
    out_ref[:, 0:D_A] = _group_a_chunks(proj_ref, wa_ref, abt_ref, avw_ref).astype(BF16)
    yield

    pre = _dot(proj_ref[:, C_BR:C_BR + LANES], bwa_ref[...]) + bba_ref[...]
    la_ref[...] = _log_sigmoid(pre) * (1.0 / GATE_TAU)
    yield
    gl = GLA_ROWS
    tril_bf = jnp.where(_iota((gl, gl), 1) <= _iota((gl, gl), 0), 1.0, 0.0).astype(BF16)
    win_col = _iota((gl, WIN), 1)
    col_hq = _head_id(_iota((gl, D_BK), 1), DK_B, H_B)
    tril_heads = (_iota((H_B * gl, gl), 1) <= jnp.bitwise_and(_iota((H_B * gl, gl), 0), gl - 1))
    for c in range(PT // gl):
        sl = slice(c * gl, (c + 1) * gl)
        log_a = la_ref[sl, :]
        q = proj_ref[sl, C_BQ:C_BQ + D_BK].astype(F32)
        k = proj_ref[sl, C_BK:C_BK + D_BK].astype(F32)
        vb = proj_ref[sl, C_BV:C_BV + D_B]
        hi, lo = _split(log_a)
        cum2 = _dot(tril_bf, jnp.concatenate([hi, lo], axis=1))
        yield
        cum = cum2[:, :D_BK] + cum2[:, D_BK:]
        tot_row = cum[gl - 1:gl, :]
        mid_row = cum[CHUNK_B - 1:CHUNK_B, :]
        la_t = log_a.T
        q_s = q * (DK_B ** -0.5)
        q_dec = q_s * jnp.exp(cum)
        q_in = q_s * jnp.exp(cum - mid_row)
        k_in = k * jnp.exp(mid_row - cum)
        k_out = k * jnp.exp(tot_row - cum)
        q_heads = jnp.concatenate([jnp.where(col_hq == h, q_in, 0.0) for h in range(H_B)], axis=0)
        sc = _dot_nt(q_heads.astype(BF16), k_in.astype(BF16))
        yield
        sc = jnp.where(tril_heads, sc, 0.0).astype(BF16)
        o_intra = _merge_windows(
            [_dot(sc[h * gl:(h + 1) * gl], vb[:, WIN_START[h]:WIN_START[h] + WIN])
             for h in range(H_B)])
        s_old = sbd_ref[...]
        o_inter = _dot(q_dec.astype(BF16), s_old.astype(BF16))
        yield
        kot = k_out.T.astype(BF16)
        for h in range(H_B):
            rs = slice(h * DK_B, (h + 1) * DK_B)
            ws = slice(WIN_START[h], WIN_START[h] + WIN)
            dec = jnp.exp(jnp.sum(la_t[rs], axis=1, keepdims=True))
            lo_col = h * DV_B - WIN_START[h]
            in_head = (win_col >= lo_col) & (win_col < lo_col + DV_B)
            v_h = jnp.where(in_head, vb[:, ws], jnp.zeros((), BF16))
            kv = _dot(kot[rs], v_h)
            sbd_ref[rs, ws] = s_old[rs, ws] * dec + kv
        o_ref[sl, :] = o_intra + o_inter
        yield
    o = o_ref[...]
    o2 = o * o
    t = [o2[:, j * LANES:(j + 1) * LANES] for j in range(D_B // LANES)]
    lo = _iota((PT, LANES), 1) < HEAD_SPLIT
    t1a = jnp.where(lo, t[1], 0.0)
    t4a = jnp.where(lo, t[4], 0.0)
    sums = (t[0] + t1a, (t[1] - t1a) + t[2], t[3] + t4a, (t[4] - t4a) + t[5])
    inv = [lax.rsqrt(jnp.sum(x, axis=-1, keepdims=True) * (1.0 / DV_B) + EPS) for x in sums]
    inv = jnp.concatenate(
        [jnp.broadcast_to(inv[0], (PT, LANES)), jnp.where(lo, inv[0], inv[1]),
         jnp.broadcast_to(inv[1], (PT, LANES)), jnp.broadcast_to(inv[2], (PT, LANES)),
         jnp.where(lo, inv[2], inv[3]), jnp.broadcast_to(inv[3], (PT, LANES))], axis=1)
    o_n = o * inv * onw_ref[...]
    bg = proj_ref[:, C_BG:C_BG + D_B].astype(F32)
    out_ref[:, D_A:D_A + D_B] = (o_n * _silu(bg)).astype(BF16)
    yield

    for h in range(H_X):
        hs = slice(h * HD_X, (h + 1) * HD_X)
        qh = proj_ref[:, C_XQ + h * HD_X:C_XQ + (h + 1) * HD_X]
        s = _dot_nt(qh, mk_ref[:, hs].astype(BF16)) * (HD_X ** -0.5)
        yield
        e = jnp.exp(s - jnp.max(s, axis=-1, keepdims=True))
        den = jnp.sum(e, axis=-1, keepdims=True)
        ox = _dot(e.astype(BF16), mv_ref[:, hs].astype(BF16)) / den
        xg = proj_ref[:, C_XG + h * HD_X:C_XG + (h + 1) * HD_X].astype(F32)
        out_ref[:, D_A + D_B + h * HD_X:D_A + D_B + (h + 1) * HD_X] = (ox * _silu(xg)).astype(BF16)
        yield


def _prompt_layer_kernel(xn_ref, wi_ref, wo_ref, nw_ref, fw_ref, mk_ref, mv_ref, wa_ref, abt_ref,
                         avw_ref, bwa_ref, bba_ref, onw_ref, y_ref, st_ref,
                         pa_ref, pb_ref, xk_ref, br_ref, h_ref, sbd_ref, o_ref, la_ref, *, nt):
    s = pl.program_id(0)
    cur = jnp.maximum(s - 1, 0)
    t = lax.rem(cur, nt)

    @pl.when(t == 0)
    def _():
        sbd_ref[...] = jnp.zeros_like(sbd_ref)

    def in_proj_stages(pn_ref):
        x = xn_ref[...]
        ms = jnp.mean(x * x, axis=-1, keepdims=True)
        h_ref[...] = (x * lax.rsqrt(ms + EPS) * nw_ref[...]).astype(BF16)
        yield
        for c0 in range(0, D_PROJ_PAD, MXU_COLS):
            cols = pl.ds(c0, MXU_COLS)
            pn_ref[:, cols] = _dot(h_ref[...], wi_ref[:, cols]).astype(BF16)
            yield
        xk_ref[...] = xn_ref[...]

    def out_proj_stages():
        ssq = jnp.zeros((PT, 1), F32)
        for c0 in range(0, D_MODEL, OUT_BN):
            cols = pl.ds(c0, OUT_BN)
            acc = _dot(br_ref[...], wo_ref[:, cols]) + xk_ref[:, cols]
            y_ref[:, cols] = acc
            ssq = ssq + jnp.sum(acc * acc, axis=-1, keepdims=True)
            yield
        y_ref[...] = y_ref[...] * lax.rsqrt(ssq * (1.0 / D_MODEL) + EPS) * fw_ref[...]
        yield

    def run(order, streams):
        for name in order:
            next(streams[name])
        for name, gen in streams.items():
            assert next(gen, "done") == "done", name

    def mixer_stages(pc_ref):
        return _prompt_block(pc_ref, br_ref, mk_ref, mv_ref, wa_ref, abt_ref, avw_ref, bwa_ref,
                             bba_ref, onw_ref, sbd_ref, o_ref, la_ref)

    n_p = 1 + D_PROJ_PAD // MXU_COLS
    n_m = N_MIXER_STAGES
    n_o = 1 + D_MODEL // OUT_BN

    def interleave(counts):
        order = []
        for i in range(max(counts.values())):
            order += [name for name, n in counts.items() if i < n]
        return order

    def body(pn_ref, pc_ref):
        order = ["P"] + interleave({"M": n_m, "P": n_p - 1}) + ["O"] * n_o
        run(order, {"P": in_proj_stages(pn_ref), "M": mixer_stages(pc_ref), "O": out_proj_stages()})

    last = pl.num_programs(0) - 1

    @pl.when(s == 0)
    def _():
        run(["P"] * n_p, {"P": in_proj_stages(pa_ref)})

    @pl.when((lax.rem(s, 2) == 0) & (s > 0) & (s < last))
    def _():
        body(pa_ref, pb_ref)

    @pl.when(lax.rem(s, 2) == 1)
    def _():
        body(pb_ref, pa_ref)

    @pl.when(s == last)
    def _():
        run(["M"] * n_m + ["O"] * n_o, {"M": mixer_stages(pb_ref), "O": out_proj_stages()})

    @pl.when((t == nt - 1) & (s > 0))
    def _():
        for h in range(H_B):
            off = h * DV_B - WIN_START[h]
            blk = sbd_ref[h * DK_B:(h + 1) * DK_B, WIN_START[h]:WIN_START[h] + WIN]
            if off:
                blk = pltpu.roll(blk, WIN - off, 1)
            st_ref[0, h] = blk[:, :DV_B]


def _prompt_layer(xp, w_in_bf, w_out_bf, nw, fw, memkv, wa, abt, avw, bwa, bba, onw, *, batch, seq):
    nt = seq // PT
    nblk = batch * nt
    assert nblk % 2 == 0
    cur = lambda s: jnp.maximum(s - 1, 0)
    const = lambda *shape: pl.BlockSpec(shape, lambda s: (0,) * len(shape))
    resident = lambda *shape: pl.BlockSpec(shape, lambda s: (0,) * len(shape),
                                           pipeline_mode=pl.Buffered(1))
    return pl.pallas_call(
        functools.partial(_prompt_layer_kernel, nt=nt),
        grid=(nblk + 1,),
        in_specs=[pl.BlockSpec((PT, D_MODEL), lambda s: (jnp.minimum(s, nblk - 1), 0)),
                  resident(D_MODEL, D_PROJ_PAD), resident(D_MODEL, D_MODEL),
                  const(1, D_MODEL), const(1, D_MODEL),
                  pl.BlockSpec((None, N_MEM, D_X), lambda s: (0, cur(s) // nt, 0)),
                  pl.BlockSpec((None, N_MEM, D_X), lambda s: (1, cur(s) // nt, 0)),
                  const(H_A, CHUNK_A, CHUNK_A), const(CHUNK_A, H_A), const(1, D_A),
                  const(LANES, D_BK), const(1, D_BK), const(1, D_B)],
        out_specs=[pl.BlockSpec((PT, D_MODEL), lambda s: (cur(s), 0)),
                   pl.BlockSpec((1, H_B, DK_B, DV_B), lambda s: (cur(s) // nt, 0, 0, 0))],
        out_shape=[jax.ShapeDtypeStruct((batch * seq, D_MODEL), F32),
                   jax.ShapeDtypeStruct((batch, H_B, DK_B, DV_B), F32)],
        scratch_shapes=[pltpu.VMEM((PT, D_PROJ_PAD), BF16),
                        pltpu.VMEM((PT, D_PROJ_PAD), BF16),
                        pltpu.VMEM((PT, D_MODEL), F32),
                        pltpu.VMEM((PT, D_MODEL), BF16),
                        pltpu.VMEM((PT, D_MODEL), BF16),
                        pltpu.VMEM((D_BK, D_B), F32),
                        pltpu.VMEM((PT, D_B), F32),
                        pltpu.VMEM((PT, D_BK), F32)],
        compiler_params=pltpu.CompilerParams(
            dimension_semantics=("arbitrary",), vmem_limit_bytes=VMEM_LIMIT),
        name="prompt_layer",
    )(xp, w_in_bf, w_out_bf, nw.reshape(1, D_MODEL), fw.reshape(1, D_MODEL), memkv, memkv,
      wa, abt, avw, bwa, bba, onw)


NS = 8
TS = 8
SB = NS * TS
SBP = 128


def _sample_mixer_kernel(proj_ref, st_ref, ck_ref, cv_ref, wa_ref, abt_ref, avw_ref, bwa_ref,
                         bba_ref, onw_ref, out_ref, stn_ref, cvs_ref,
                         qin_ref, xq_ref, kot_ref, lat_ref, vhm_ref, ghm_ref, ohm_ref, ox_ref):
    ri = _iota((SB, SB), 0)
    ci = _iota((SB, SB), 1)
    same_seq = jnp.right_shift(ri, 3) == jnp.right_shift(ci, 3)
    causal = same_seq & (ci <= ri)

    a_br, vn = _group_a(proj_ref, wa_ref, abt_ref, avw_ref, causal)
    out_ref[:, 0:D_A] = a_br.astype(BF16)
    cvs_ref[...] = vn

    q = proj_ref[:, C_BQ:C_BQ + D_BK].astype(F32)
    k = proj_ref[:, C_BK:C_BK + D_BK].astype(F32)
    vb = proj_ref[:, C_BV:C_BV + D_B]
    bgb = proj_ref[:, C_BG:C_BG + D_B]
    pre = _dot(proj_ref[:, C_BR:C_BR + LANES], bwa_ref[...]) + bba_ref[...]
    log_a = _log_sigmoid(pre) * (1.0 / GATE_TAU)
    causal_bf = jnp.where(causal, 1.0, 0.0).astype(BF16)
    seq_bf = jnp.where(same_seq, 1.0, 0.0).astype(BF16)
    hi, lo = _split(log_a)
    cum = _dot(causal_bf, hi) + _dot(causal_bf, lo)
    tot = _dot(seq_bf, hi) + _dot(seq_bf, lo)
    q_in = q * (DK_B ** -0.5) * jnp.exp(cum)
    k_in = k * jnp.exp(-cum)
    k_out = k * jnp.exp(tot - cum)
    qin_ref[...] = q_in
    zpad = jnp.zeros((SBP - SB, D_BK), F32)
    kot_ref[...] = jnp.concatenate([k_out, zpad], axis=0).T.astype(BF16)
    lat_ref[...] = jnp.concatenate([log_a, zpad], axis=0).T
    xq_ref[...] = proj_ref[:, C_XQ:C_XQ + D_X].astype(F32)

    wins = _gla_intra_windows(q_in, k_in, vb, causal)
    sel_r = _iota((D_B, DV_B), 0)
    sel_c = _iota((D_B, DV_B), 1)
    vhm_ref[...] = jnp.zeros_like(vhm_ref)
    for h in range(H_B):
        off = h * DV_B - WIN_START[h]
        w = wins[h]
        if off:
            w = pltpu.roll(w, WIN - off, 1)
        ohm_ref[h] = w[:, :DV_B]
        sel = jnp.where(sel_r == sel_c + h * DV_B, 1.0, 0.0).astype(BF16)
        vhm_ref[h, 0:SB, :] = _dot(vb, sel).astype(BF16)
        ghm_ref[h] = _dot(bgb, sel)

    mask_x = (jnp.right_shift(_iota((H_X * TS, N_MEM * H_X), 0), 3)
              == jnp.bitwise_and(_iota((H_X * TS, N_MEM * H_X), 1), H_X - 1))
    mask_b = jnp.right_shift(_iota((H_B * TS, D_BK), 0), 3) == _head_id(_iota((H_B * TS, D_BK), 1), DK_B, H_B)
    lane_seq = jnp.right_shift(_iota((D_BK, SBP), 1), 3)

    def per_seq(s, carry):
        r0 = pl.multiple_of(s * TS, TS)
        q8 = xq_ref[pl.ds(r0, TS), :]
        q32 = jnp.concatenate([q8[:, h * HD_X:(h + 1) * HD_X] for h in range(H_X)], axis=0)
        sc = _dot_nt(q32.astype(BF16), ck_ref[s].astype(BF16)) * (HD_X ** -0.5)
        sc = jnp.where(mask_x, sc, -jnp.inf)
        e = jnp.exp(sc - jnp.max(sc, axis=-1, keepdims=True))
        den = jnp.sum(e, axis=-1, keepdims=True)
        o = _dot(e.astype(BF16), cv_ref[s].astype(BF16)) / den
        ox_ref[pl.ds(r0, TS), :] = jnp.concatenate(
            [o[h * TS:(h + 1) * TS] for h in range(H_X)], axis=1)
        qi8 = qin_ref[pl.ds(r0, TS), :]
        qbd2 = jnp.where(mask_b, jnp.concatenate([qi8] * H_B, axis=0), 0.0).astype(BF16)
        s0 = jnp.concatenate([st_ref[s, h] for h in range(H_B)], axis=0)
        o_inter = _dot(qbd2, s0.astype(BF16))
        for h in range(H_B):
            ohm_ref[h, pl.ds(r0, TS), :] += o_inter[h * TS:(h + 1) * TS]
        dec = jnp.exp(jnp.sum(jnp.where(lane_seq == s, lat_ref[...], 0.0), axis=1, keepdims=True))
        kot = jnp.where(lane_seq == s, kot_ref[...], jnp.zeros((), BF16))
        kv = jnp.concatenate(
            [_dot(kot[h * DK_B:(h + 1) * DK_B], vhm_ref[h]) for h in range(H_B)], axis=0)
        s_new = s0 * dec + kv
        for h in range(H_B):
            stn_ref[s, h] = s_new[h * DK_B:(h + 1) * DK_B]
        return carry

    lax.fori_loop(0, NS, per_seq, 0, unroll=4)

    selt_r = _iota((DV_B, D_B), 0)
    selt_c = _iota((DV_B, D_B), 1)
    b_br = jnp.zeros((SB, D_B), F32)
    for h in range(H_B):
        o_h = ohm_ref[h]
        ms = jnp.mean(o_h * o_h, axis=-1, keepdims=True)
        ob = (o_h * lax.rsqrt(ms + EPS) * onw_ref[...] * _silu(ghm_ref[h])).astype(BF16)
        selt = jnp.where(selt_c == selt_r + h * DV_B, 1.0, 0.0).astype(BF16)
        b_br = b_br + _dot(ob, selt)
    out_ref[:, D_A:D_A + D_B] = b_br.astype(BF16)

    xg = proj_ref[:, C_XG:C_XG + D_X].astype(F32)
    out_ref[:, D_A + D_B:D_MODEL] = (ox_ref[...] * _silu(xg)).astype(BF16)


def _sample_mixer(proj, state, ck, cv, wa, abt, avw, bwa, bba, onw):
    nseq = state.shape[1]
    const = lambda *shape: pl.BlockSpec(shape, lambda i: (0,) * len(shape))
    return pl.pallas_call(
        _sample_mixer_kernel,
        grid=(nseq // NS,),
        in_specs=[pl.BlockSpec((SB, D_PROJ), lambda i: (i, 0)),
                  pl.BlockSpec((None, NS, H_B, DK_B, DV_B), lambda i: (0, i, 0, 0, 0)),
                  pl.BlockSpec((NS, N_MEM * H_X, HD_X), lambda i: (i, 0, 0)),
                  pl.BlockSpec((NS, N_MEM * H_X, HD_X), lambda i: (i, 0, 0)),
                  const(H_A, SB, SB), const(SB, H_A), const(1, D_A),
                  const(LANES, D_BK), const(1, D_BK), const(1, DV_B)],
        out_specs=[pl.BlockSpec((SB, D_MODEL), lambda i: (i, 0)),
                   pl.BlockSpec((None, NS, H_B, DK_B, DV_B), lambda i: (0, i, 0, 0, 0)),
                   pl.BlockSpec((SB, D_A), lambda i: (i, 0))],
        out_shape=[jax.ShapeDtypeStruct((nseq * TS, D_MODEL), BF16),
                   jax.ShapeDtypeStruct((1, nseq, H_B, DK_B, DV_B), F32),
                   jax.ShapeDtypeStruct((nseq * TS, D_A), F32)],
        scratch_shapes=[pltpu.VMEM((SB, D_BK), F32),
                        pltpu.VMEM((SB, D_X), F32),
                        pltpu.VMEM((D_BK, SBP), BF16),
                        pltpu.VMEM((D_BK, SBP), F32),
                        pltpu.VMEM((H_B, SBP, DV_B), BF16),
                        pltpu.VMEM((H_B, SB, DV_B), F32),
                        pltpu.VMEM((H_B, SB, DV_B), F32),
                        pltpu.VMEM((SB, D_X), F32)],
        compiler_params=pltpu.CompilerParams(
            dimension_semantics=("arbitrary",), vmem_limit_bytes=VMEM_LIMIT),
        name="sample_mixer",
    )(proj, state, ck, cv, wa, abt, avw, bwa, bba, onw)


def kernel(x_prompt, x_sample, mem_prompt, state_gla, cache_mem_k, cache_mem_v, norm_w, w_in,
           a_vnorm_w, a_ws, a_bs, b_wa, b_ba, b_onorm_w, mem_norm_w, w_mem_kv, w_out, final_norm_w):
    batch, seq, _ = x_prompt.shape
    nseq, tdec, _ = x_sample.shape
    depth = w_in.shape[0]
    assert depth == 1 and tdec == TS and seq % PT == 0 and nseq % NS == 0

    xp = x_prompt.reshape(batch * seq, D_MODEL)
    xs = x_sample.reshape(nseq * TS, D_MODEL)
    mem = mem_prompt.reshape(batch * N_MEM, D_MODEL)
    w_in_bf, w_out_bf, proj_s, memkv, mem_k, mem_v = _weight_prep(
        jnp.transpose(w_in[0]), w_out[0], w_mem_kv[0], xs, norm_w[0], mem, mem_norm_w[0])
    bwa = jnp.concatenate([b_wa[0], jnp.zeros((LANES - GATE_RANK, D_BK), F32)], axis=0).astype(BF16)
    bba = b_ba[0].reshape(1, D_BK)
    avw = a_vnorm_w[0].reshape(1, D_A)
    onw_p = jnp.tile(b_onorm_w[0], H_B).reshape(1, D_B)
    onw_s = b_onorm_w[0].reshape(1, DV_B)
    wa_p = a_ws[0]
    abt_p = a_bs[0].T
    wa_s = jnp.tile(a_ws[0][:, :TS, :TS], (1, NS, NS))
    abt_s = jnp.tile(a_bs[0][:, :TS], (1, NS)).T

    br_s, st_s, cvs = _sample_mixer(
        proj_s, state_gla,
        cache_mem_k.reshape(nseq, N_MEM * H_X, HD_X), cache_mem_v.reshape(nseq, N_MEM * H_X, HD_X),
        wa_s, abt_s, avw, bwa, bba, onw_s)
    y_s = _out_proj(br_s, w_out_bf, xs, final_norm_w, bm=256, name="out_proj_s")

    y_p, st_p = _prompt_layer(xp, w_in_bf, w_out_bf, norm_w[0], final_norm_w, memkv, wa_p, abt_p,
                              avw, bwa, bba, onw_p, batch=batch, seq=seq)

    return (y_p.reshape(batch, seq, D_MODEL),
            y_s.reshape(nseq, TS, D_MODEL),
            mem_k.reshape(1, batch, N_MEM, H_X, HD_X),
            mem_v.reshape(1, batch, N_MEM, H_X, HD_X),
            st_p.reshape(1, batch, H_B, DK_B, DV_B),
            st_s,
            cvs.reshape(1, nseq, TS, D_A))
```

```python
import functools

import jax
import jax.numpy as jnp
from jax import lax
from jax.experimental import pallas as pl
from jax.experimental.pallas import tpu as pltpu

F32 = jnp.float32
BF16 = jnp.bfloat16

D_MODEL = 2048
D_A = 768
H_A = 4
HD_A = 192
CHUNK_A = 128
D_B = 768
H_B = 4
DV_B = 192
DK_B = 96
D_BK = 384
GATE_RANK = 16
GATE_TAU = 16.0
CHUNK_B = 64
D_X = 512
H_X = 4
HD_X = 128
N_MEM = 256
EPS = 1e-6

LANES = 128
MXU_COLS = 256
C_AU, C_AV, C_AG = 0, 768, 1536
C_BQ, C_BK, C_BV, C_BG = 2304, 2688, 3072, 3840
C_XQ, C_XG, C_BR = 4608, 5120, 5632
D_PROJ = 5760
D_PROJ_PAD = 5888
WIN_START = (0, 128, 384, 512)
WIN = 256
HEAD_SPLIT = DV_B - LANES
OUT_BN = 512

VMEM_LIMIT = 60 * 1024 * 1024


def _dot(a, b):
    return jnp.dot(a, b, preferred_element_type=F32)


def _dot_nt(a, b):
    return lax.dot_general(a, b, (((1,), (1,)), ((), ())), preferred_element_type=F32)


def _split(x):
    hi = x.astype(BF16)
    lo = (x - hi.astype(F32)).astype(BF16)
    return hi, lo


def _silu(x):
    return x / (1.0 + jnp.exp(-x))


def _log_sigmoid(x):
    return jnp.minimum(x, 0.0) - jnp.log1p(jnp.exp(-jnp.abs(x)))


def _head_id(idx, width, n):
    h = jnp.zeros_like(idx)
    for i in range(1, n):
        h = h + (idx >= i * width).astype(jnp.int32)
    return h


def _iota(shape, dim):
    return lax.broadcasted_iota(jnp.int32, shape, dim)


def _merge_windows(r):
    rows = r[0].shape[0]
    lo = _iota((rows, LANES), 1) < HEAD_SPLIT
    tiles = [r[0][:, :LANES], jnp.where(lo, r[0][:, LANES:], r[1][:, :LANES]), r[1][:, LANES:],
             r[2][:, :LANES], jnp.where(lo, r[2][:, LANES:], r[3][:, :LANES]), r[3][:, LANES:]]
    return jnp.concatenate(tiles, axis=1)


def _group_a(proj_ref, wa_ref, abt_ref, avw_ref, mask):
    u = proj_ref[:, C_AU:C_AU + D_A].astype(F32)
    v = proj_ref[:, C_AV:C_AV + D_A].astype(F32)
    g = proj_ref[:, C_AG:C_AG + D_A].astype(F32)
    rows = u.shape[0]
    ms = jnp.mean(v * v, axis=-1, keepdims=True)
    vn = v * lax.rsqrt(ms + EPS) * avw_ref[...]
    vb = vn.astype(BF16)
    r = []
    for gi in range(H_A):
        w = jnp.where(mask, wa_ref[gi], 0.0).astype(BF16)
        r.append(_dot(w, vb[:, WIN_START[gi]:WIN_START[gi] + WIN]))
    mixed = _merge_windows(r)
    col_g = _head_id(_iota((rows, D_A), 1), HD_A, H_A)
    bias = jnp.zeros((rows, D_A), F32)
    for gi in range(H_A):
        bias = jnp.where(col_g == gi, abt_ref[:, gi:gi + 1], bias)
    return u * (mixed + bias) * _silu(g), vn


def _gla_intra_windows(q_in, k_in, vb, mask):
    rows = q_in.shape[0]
    col_h = _head_id(_iota((rows, D_BK), 1), DK_B, H_B)
    kb = k_in.astype(BF16)
    r = []
    for h in range(H_B):
        qh = jnp.where(col_h == h, q_in, 0.0).astype(BF16)
        s = jnp.where(mask, _dot_nt(qh, kb), 0.0).astype(BF16)
        r.append(_dot(s, vb[:, WIN_START[h]:WIN_START[h] + WIN]))
    return r


PREP_BN = MXU_COLS
N_MAIN_BLOCKS = C_BG // PREP_BN


GATE_BLOCK = D_PROJ_PAD // PREP_BN - 1
N_IN_BLOCKS = GATE_BLOCK + 1
N_OUT_BLOCKS = D_MODEL // PREP_BN
PREP_BUFS = 4


def _weight_prep_kernel(wt_hbm, wo_hbm, wm_hbm, xs_ref, nw_ref, mem_ref, mnw_ref,
                        oi_ref, oo_ref, ps_ref, mkv_ref, mk_ref, mv_ref,
                        buf, sem, hs_ref, hm_ref, wm_buf, wm_sem, wb_ref):
    r = pl.program_id(0)
    slot = lax.rem(r, PREP_BUFS)
    mem_step0 = N_IN_BLOCKS + N_OUT_BLOCKS

    def mem_copy(j):
        return pltpu.make_async_copy(wm_hbm.at[:, pl.ds(j * PREP_BN, PREP_BN)], wm_buf.at[j],
                                     wm_sem.at[j])

    for j in range(N_MEM_BLOCKS):
        @pl.when(r == N_IN_BLOCKS + j)
        def _(j=j):
            mem_copy(j).start()

    @pl.when(r == N_IN_BLOCKS)
    def _():
        x = mem_ref[...]
        ms = jnp.mean(x * x, axis=-1, keepdims=True)
        hm_ref[...] = (x * lax.rsqrt(ms + EPS) * mnw_ref[...]).astype(BF16)

    for j in range(N_MEM_BLOCKS):
        @pl.when(r == mem_step0 + j)
        def _(j=j):
            mem_copy(j).wait()
            kv = _dot(hm_ref[...], wm_buf[j].astype(BF16))
            mkv_ref[...] = kv
            dst = mk_ref if j * PREP_BN < D_X else mv_ref
            for hh in range(PREP_BN // HD_X):
                head = (j * PREP_BN % D_X) // HD_X + hh
                for b in range(kv.shape[0] // N_MEM):
                    dst[b, pl.ds(head, N_MEM, stride=H_X), :] = (
                        kv[b * N_MEM:(b + 1) * N_MEM, hh * HD_X:(hh + 1) * HD_X])

    def in_copy(rr, sl):
        start = pl.multiple_of(jnp.where(rr < N_MAIN_BLOCKS, rr * PREP_BN, rr * PREP_BN + GATE_RANK), 8)
        return pltpu.make_async_copy(wt_hbm.at[pl.ds(start, PREP_BN), :], buf.at[sl], sem.at[sl])

    def gate_copy(sl):
        return pltpu.make_async_copy(wt_hbm.at[pl.ds(C_BG, GATE_RANK), :],
                                     buf.at[sl, pl.ds(0, GATE_RANK), :], sem.at[sl])

    def out_copy(rr, sl):
        start = pl.multiple_of((rr - N_IN_BLOCKS) * PREP_BN, PREP_BN)
        return pltpu.make_async_copy(wo_hbm.at[pl.ds(start, PREP_BN), :], buf.at[sl], sem.at[sl])

    def start_fetch(rr, sl):
        @pl.when(rr < GATE_BLOCK)
        def _():
            in_copy(rr, sl).start()

        @pl.when(rr == GATE_BLOCK)
        def _():
            gate_copy(sl).start()

        @pl.when(rr > GATE_BLOCK)
        def _():
            out_copy(rr, sl).start()

    @pl.when(r == 0)
    def _():
        for ahead in range(PREP_BUFS - 1):
            start_fetch(r + ahead, ahead)

    nxt = r + PREP_BUFS - 1

    @pl.when(nxt < mem_step0)
    def _():
        start_fetch(nxt, lax.rem(nxt, PREP_BUFS))

    @pl.when(r == 0)
    def _():
        x = xs_ref[...]
        ms = jnp.mean(x * x, axis=-1, keepdims=True)
        hs_ref[...] = (x * lax.rsqrt(ms + EPS) * nw_ref[...]).astype(BF16)

    cur = lax.rem(r, 2)

    def project_previous():
        ps_ref[...] = _dot(hs_ref[...], wb_ref[1 - cur]).astype(BF16)

    def emit(block):
        wb = block.T.astype(BF16)
        oi_ref[...] = wb
        wb_ref[cur] = wb

    @pl.when(r == 0)
    def _():
        in_copy(r, slot).wait()
        emit(buf[slot])

    @pl.when((r > 0) & (r < GATE_BLOCK))
    def _():
        project_previous()
        in_copy(r, slot).wait()
        emit(buf[slot])

    @pl.when(r == GATE_BLOCK)
    def _():
        project_previous()
        gate_copy(slot).wait()
        emit(jnp.concatenate([buf[slot, 0:GATE_RANK, :],
                              jnp.zeros((PREP_BN - GATE_RANK, D_MODEL), F32)], axis=0))

    @pl.when(r == N_IN_BLOCKS)
    def _():
        project_previous()

    @pl.when((r > GATE_BLOCK) & (r < mem_step0))
    def _():
        out_copy(r, slot).wait()
        oo_ref[...] = buf[slot].astype(BF16)


N_MEM_BLOCKS = 2 * D_X // PREP_BN


def _weight_prep(w_t, w_out, w_mem, xs, nw, mem, mnw):
    k = w_t.shape[1]
    m = xs.shape[0]
    mm = mem.shape[0]
    batch = mm // N_MEM
    mem_step0 = N_IN_BLOCKS + N_OUT_BLOCKS
    in_block = lambda r: (0, jnp.minimum(r, GATE_BLOCK))
    mem_j = lambda r: jnp.maximum(r - mem_step0, 0)
    per_d_x = D_X // PREP_BN
    resident = lambda *shape: pl.BlockSpec(shape, lambda r: (0,) * len(shape),
                                           pipeline_mode=pl.Buffered(1))
    return pl.pallas_call(
        _weight_prep_kernel,
        grid=(mem_step0 + N_MEM_BLOCKS,),
        in_specs=[pl.BlockSpec(memory_space=pl.ANY), pl.BlockSpec(memory_space=pl.ANY),
                  pl.BlockSpec(memory_space=pl.ANY),
                  resident(m, k), pl.BlockSpec((1, k), lambda r: (0, 0)),
                  resident(mm, k), pl.BlockSpec((1, k), lambda r: (0, 0))],
        out_specs=[pl.BlockSpec((k, PREP_BN), in_block),
                   pl.BlockSpec((PREP_BN, D_MODEL),
                                lambda r: (jnp.clip(r - N_IN_BLOCKS, 0, N_OUT_BLOCKS - 1), 0)),
                   pl.BlockSpec((m, PREP_BN), lambda r: (0, jnp.clip(r - 1, 0, GATE_BLOCK))),
                   pl.BlockSpec((None, mm, PREP_BN),
                                lambda r: (mem_j(r) // per_d_x, 0, lax.rem(mem_j(r), per_d_x))),
                   pl.BlockSpec((batch, N_MEM * H_X, HD_X), lambda r: (0, 0, 0)),
                   pl.BlockSpec((batch, N_MEM * H_X, HD_X), lambda r: (0, 0, 0))],
        out_shape=[jax.ShapeDtypeStruct((k, D_PROJ_PAD), BF16),
                   jax.ShapeDtypeStruct((D_MODEL, D_MODEL), BF16),
                   jax.ShapeDtypeStruct((m, D_PROJ_PAD), BF16),
                   jax.ShapeDtypeStruct((2, mm, D_X), F32),
                   jax.ShapeDtypeStruct((batch, N_MEM * H_X, HD_X), F32),
                   jax.ShapeDtypeStruct((batch, N_MEM * H_X, HD_X), F32)],
        scratch_shapes=[pltpu.VMEM((PREP_BUFS, PREP_BN, k), F32),
                        pltpu.SemaphoreType.DMA((PREP_BUFS,)),
                        pltpu.VMEM((m, k), BF16),
                        pltpu.VMEM((mm, k), BF16),
                        pltpu.VMEM((N_MEM_BLOCKS, k, PREP_BN), F32),
                        pltpu.SemaphoreType.DMA((N_MEM_BLOCKS,)),
                        pltpu.VMEM((2, k, PREP_BN), BF16)],
        compiler_params=pltpu.CompilerParams(
            dimension_semantics=("arbitrary",), vmem_limit_bytes=VMEM_LIMIT),
        name="weight_prep",
    )(w_t, w_out, w_mem, xs, nw.reshape(1, k), mem, mnw.reshape(1, k))


def _out_proj_kernel(br_ref, w_ref, x_ref, fw_ref, y_ref):
    acc = _dot(br_ref[...], w_ref[...]) + x_ref[...]
    ms = jnp.mean(acc * acc, axis=-1, keepdims=True)
    y_ref[...] = acc * lax.rsqrt(ms + EPS) * fw_ref[...]


def _out_proj(br, w, x, fw, *, bm, name):
    m, k = br.shape
    n = w.shape[1]
    return pl.pallas_call(
        _out_proj_kernel,
        grid=(m // bm,),
        in_specs=[pl.BlockSpec((bm, k), lambda i: (i, 0)),
                  pl.BlockSpec((k, n), lambda i: (0, 0)),
                  pl.BlockSpec((bm, n), lambda i: (i, 0)),
                  pl.BlockSpec((1, n), lambda i: (0, 0))],
        out_specs=pl.BlockSpec((bm, n), lambda i: (i, 0)),
        out_shape=jax.ShapeDtypeStruct((m, n), F32),
        compiler_params=pltpu.CompilerParams(
            dimension_semantics=("arbitrary",), vmem_limit_bytes=VMEM_LIMIT),
        name=name,
    )(br, w, x, fw.reshape(1, n))


PT = 256
GLA_ROWS = 128
N_MIXER_STAGES = 2 + 4 * (PT // GLA_ROWS) + 1 + 2 * H_X


def _group_a_chunks(proj_ref, wa_ref, abt_ref, avw_ref):
    v = proj_ref[:, C_AV:C_AV + D_A].astype(F32)
    n_chunks = v.shape[0] // CHUNK_A
    ms = jnp.mean(v * v, axis=-1, keepdims=True)
    vb = (v * lax.rsqrt(ms + EPS) * avw_ref[...]).astype(BF16)
    tril = _iota((CHUNK_A, CHUNK_A), 1) <= _iota((CHUNK_A, CHUNK_A), 0)
    r = []
    for gi in range(H_A):
        w = jnp.where(tril, wa_ref[gi], 0.0).astype(BF16)
        ws = slice(WIN_START[gi], WIN_START[gi] + WIN)
        rhs = jnp.concatenate([vb[c * CHUNK_A:(c + 1) * CHUNK_A, ws] for c in range(n_chunks)], axis=1)
        res = _dot(w, rhs)
        r.append(jnp.concatenate([res[:, c * WIN:(c + 1) * WIN] for c in range(n_chunks)], axis=0))
    mixed = _merge_windows(r)
    col_g = _head_id(_iota((CHUNK_A, D_A), 1), HD_A, H_A)
    bias = jnp.zeros((CHUNK_A, D_A), F32)
    for gi in range(H_A):
        bias = jnp.where(col_g == gi, abt_ref[:, gi:gi + 1], bias)
    bias = jnp.concatenate([bias] * n_chunks, axis=0)
    u = proj_ref[:, C_AU:C_AU + D_A].astype(F32)
    g = proj_ref[:, C_AG:C_AG + D_A].astype(F32)
    return u * (mixed + bias) * _silu(g)


def _prompt_block(proj_ref, out_ref, mk_ref, mv_ref, wa_ref, abt_ref, avw_ref, bwa_ref, bba_ref,
                  onw_ref, sbd_ref, o_ref, la_ref):
    out_ref[:, 0:D_A] = _group_a_chunks(proj_ref, wa_ref, abt_ref, avw_ref).astype(BF16)
    yield

    pre = _dot(proj_ref[:, C_BR:C_BR + LANES], bwa_ref[...]) + bba_ref[...]
    la_ref[...] = _log_sigmoid(pre) * (1.0 / GATE_TAU)
    yield
    gl = GLA_ROWS
    tril_bf = jnp.where(_iota((gl, gl), 1) <= _iota((gl, gl), 0), 1.0, 0.0).astype(BF16)
    win_col = _iota((gl, WIN), 1)
    col_hq = _head_id(_iota((gl, D_BK), 1), DK_B, H_B)
    tril_heads = (_iota((H_B * gl, gl), 1) <= jnp.bitwise_and(_iota((H_B * gl, gl), 0), gl - 1))
    for c in range(PT // gl):
        sl = slice(c * gl, (c + 1) * gl)
        log_a = la_ref[sl, :]
        q = proj_ref[sl, C_BQ:C_BQ + D_BK].astype(F32)
        k = proj_ref[sl, C_BK:C_BK + D_BK].astype(F32)
        vb = proj_ref[sl, C_BV:C_BV + D_B]
        hi, lo = _split(log_a)
        cum2 = _dot(tril_bf, jnp.concatenate([hi, lo], axis=1))
        yield
        cum = cum2[:, :D_BK] + cum2[:, D_BK:]
        tot_row = cum[gl - 1:gl, :]
        mid_row = cum[CHUNK_B - 1:CHUNK_B, :]
        la_t = log_a.T
        q_s = q * (DK_B ** -0.5)
        q_dec = q_s * jnp.exp(cum)
        q_in = q_s * jnp.exp(cum - mid_row)
        k_in = k * jnp.exp(mid_row - cum)
        k_out = k * jnp.exp(tot_row - cum)
        q_heads = jnp.concatenate([jnp.where(col_hq == h, q_in, 0.0) for h in range(H_B)], axis=0)
        sc = _dot_nt(q_heads.astype(BF16), k_in.astype(BF16))
        yield
        sc = jnp.where(tril_heads, sc, 0.0).astype(BF16)
        o_intra = _merge_windows(
            [_dot(sc[h * gl:(h + 1) * gl], vb[:, WIN_START[h]:WIN_START[h] + WIN])
             for h in range(H_B)])
        s_old = sbd_ref[...]
        o_inter = _dot(q_dec.astype(BF16), s_old.astype(BF16))
        yield
        kot = k_out.T.astype(BF16)
        for h in range(H_B):
            rs = slice(h * DK_B, (h + 1) * DK_B)
            ws = slice(WIN_START[h], WIN_START[h] + WIN)
            dec = jnp.exp(jnp.sum(la_t[rs], axis=1, keepdims=True))
            lo_col = h * DV_B - WIN_START[h]
            in_head = (win_col >= lo_col) & (win_col < lo_col + DV_B)
            v_h = jnp.where(in_head, vb[:, ws], jnp.zeros((), BF16))
            kv = _dot(kot[rs], v_h)
            sbd_ref[rs, ws] = s_old[rs, ws] * dec + kv
        o_ref[sl, :] = o_intra + o_inter
        yield
    o = o_ref[...]
    o2 = o * o
    t = [o2[:, j * LANES:(j + 1) * LANES] for j in range(D_B // LANES)]
    lo = _iota((PT, LANES), 1) < HEAD_SPLIT
    t1a = jnp.where(lo, t[1], 0.0)
    t4a = jnp.where(lo, t[4], 0.0)
    sums = (t[0] + t1a, (t[1] - t1a) + t[2], t[3] + t4a, (t[4] - t4a) + t[5])
    inv = [lax.rsqrt(jnp.sum(x, axis=-1, keepdims=True) * (1.0 / DV_B) + EPS) for x in sums]
    inv = jnp.concatenate(
        [jnp.broadcast_to(inv[0], (PT, LANES)), jnp.where(lo, inv[0], inv[1]),
         jnp.broadcast_to(inv[1], (PT, LANES)), jnp.broadcast_to(inv[2], (PT, LANES)),
         jnp.where(lo, inv[2], inv[3]), jnp.broadcast_to(inv[3], (PT, LANES))], axis=1)
    o_n = o * inv * onw_ref[...]
    bg = proj_ref[:, C_BG:C_BG + D_B].astype(F32)
    out_ref[:, D_A:D_A + D_B] = (o_n * _silu(bg)).astype(BF16)
    yield

    for h in range(H_X):
        hs = slice(h * HD_X, (h + 1) * HD_X)
        qh = proj_ref[:, C_XQ + h * HD_X:C_XQ + (h + 1) * HD_X]
        s = _dot_nt(qh, mk_ref[:, hs].astype(BF16)) * (HD_X ** -0.5)
        yield
        e = jnp.exp(s - jnp.max(s, axis=-1, keepdims=True))
        den = jnp.sum(e, axis=-1, keepdims=True)
        ox = _dot(e.astype(BF16), mv_ref[:, hs].astype(BF16)) / den
        xg = proj_ref[:, C_XG + h * HD_X:C_XG + (h + 1) * HD_X].astype(F32)
        out_ref[:, D_A + D_B + h * HD_X:D_A + D_B + (h + 1) * HD_X] = (ox * _silu(xg)).astype(BF16)
        yield


def _prompt_layer_kernel(xn_ref, wi_ref, wo_ref, nw_ref, fw_ref, mk_ref, mv_ref, wa_ref, abt_ref,
                         avw_ref, bwa_ref, bba_ref, onw_ref, y_ref, st_ref,
                         pa_ref, pb_ref, xk_ref, br_ref, h_ref, sbd_ref, o_ref, la_ref, *, nt):
    s = pl.program_id(0)
    cur = jnp.maximum(s - 1, 0)
    t = lax.rem(cur, nt)

    @pl.when(t == 0)
    def _():
        sbd_ref[...] = jnp.zeros_like(sbd_ref)

    def in_proj_stages(pn_ref):
        x = xn_ref[...]
        ms = jnp.mean(x * x, axis=-1, keepdims=True)
        h_ref[...] = (x * lax.rsqrt(ms + EPS) * nw_ref[...]).astype(BF16)
        yield
        for c0 in range(0, D_PROJ_PAD, MXU_COLS):
            cols = pl.ds(c0, MXU_COLS)
            pn_ref[:, cols] = _dot(h_ref[...], wi_ref[:, cols]).astype(BF16)
            yield
        xk_ref[...] = xn_ref[...]

    def out_proj_stages():
        ssq = jnp.zeros((PT, 1), F32)
        for c0 in range(0, D_MODEL, OUT_BN):
            cols = pl.ds(c0, OUT_BN)
            acc = _dot(br_ref[...], wo_ref[:, cols]) + xk_ref[:, cols]
            y_ref[:, cols] = acc
            ssq = ssq + jnp.sum(acc * acc, axis=-1, keepdims=True)
            yield
        y_ref[...] = y_ref[...] * lax.rsqrt(ssq * (1.0 / D_MODEL) + EPS) * fw_ref[...]
        yield

    def run(order, streams):
        for name in order:
            next(streams[name])
        for name, gen in streams.items():
            assert next(gen, "done") == "done", name

    def mixer_stages(pc_ref):
        return _prompt_block(pc_ref, br_ref, mk_ref, mv_ref, wa_ref, abt_ref, avw_ref, bwa_ref,
                             bba_ref, onw_ref, sbd_ref, o_ref, la_ref)

    n_p = 1 + D_PROJ_PAD // MXU_COLS
    n_m = N_MIXER_STAGES
    n_o = 1 + D_MODEL // OUT_BN

    def interleave(counts):
        order = []
        for i in range(max(counts.values())):
            order += [name for name, n in counts.items() if i < n]
        return order

    def body(pn_ref, pc_ref):
        order = ["P"] + interleave({"M": n_m, "P": n_p - 1}) + ["O"] * n_o
        run(order, {"P": in_proj_stages(pn_ref), "M": mixer_stages(pc_ref), "O": out_proj_stages()})

    last = pl.num_programs(0) - 1

    @pl.when(s == 0)
    def _():
        run(["P"] * n_p, {"P": in_proj_stages(pa_ref)})

    @pl.when((lax.rem(s, 2) == 0) & (s > 0) & (s < last))
    def _():
        body(pa_ref, pb_ref)

    @pl.when(lax.rem(s, 2) == 1)
    def _():
        body(pb_ref, pa_ref)

    @pl.when(s == last)
    def _():
        run(["M"] * n_m + ["O"] * n_o, {"M": mixer_stages(pb_ref), "O": out_proj_stages()})

    @pl.when((t == nt - 1) & (s > 0))
    def _():
        for h in range(H_B):
            off = h * DV_B - WIN_START[h]
            blk = sbd_ref[h * DK_B:(h + 1) * DK_B, WIN_START[h]:WIN_START[h] + WIN]
            if off:
                blk = pltpu.roll(blk, WIN - off, 1)
            st_ref[0, h] = blk[:, :DV_B]


def _prompt_layer(xp, w_in_bf, w_out_bf, nw, fw, memkv, wa, abt, avw, bwa, bba, onw, *, batch, seq):
    nt = seq // PT
    nblk = batch * nt
    assert nblk % 2 == 0
    cur = lambda s: jnp.maximum(s - 1, 0)
    const = lambda *shape: pl.BlockSpec(shape, lambda s: (0,) * len(shape))
    resident = lambda *shape: pl.BlockSpec(shape, lambda s: (0,) * len(shape),
                                           pipeline_mode=pl.Buffered(1))
    return pl.pallas_call(
        functools.partial(_prompt_layer_kernel, nt=nt),
        grid=(nblk + 1,),
        in_specs=[pl.BlockSpec((PT, D_MODEL), lambda s: (jnp.minimum(s, nblk - 1), 0)),
                  resident(D_MODEL, D_PROJ_PAD), resident(D_MODEL, D_MODEL),
                  const(1, D_MODEL), const(1, D_MODEL),
                  pl.BlockSpec((None, N_MEM, D_X), lambda s: (0, cur(s) // nt, 0)),
                  pl.BlockSpec((None, N_MEM, D_X), lambda s: (1, cur(s) // nt, 0)),
                  const(H_A, CHUNK_A, CHUNK_A), const(CHUNK_A, H_A), const(1, D_A),
                  const(LANES, D_BK), const(1, D_BK), const(1, D_B)],
        out_specs=[pl.BlockSpec((PT, D_MODEL), lambda s: (cur(s), 0)),
                   pl.BlockSpec((1, H_B, DK_B, DV_B), lambda s: (cur(s) // nt, 0, 0, 0))],
        out_shape=[jax.ShapeDtypeStruct((batch * seq, D_MODEL), F32),
                   jax.ShapeDtypeStruct((batch, H_B, DK_B, DV_B), F32)],
        scratch_shapes=[pltpu.VMEM((PT, D_PROJ_PAD), BF16),
                        pltpu.VMEM((PT, D_PROJ_PAD), BF16),
                        pltpu.VMEM((PT, D_MODEL), F32),
                        pltpu.VMEM((PT, D_MODEL), BF16),
                        pltpu.VMEM((PT, D_MODEL), BF16),
                        pltpu.VMEM((D_BK, D_B), F32),
                        pltpu.VMEM((PT, D_B), F32),
                        pltpu.VMEM((PT, D_BK), F32)],
        compiler_params=pltpu.CompilerParams(
            dimension_semantics=("arbitrary",), vmem_limit_bytes=VMEM_LIMIT),
        name="prompt_layer",
    )(xp, w_in_bf, w_out_bf, nw.reshape(1, D_MODEL), fw.reshape(1, D_MODEL), memkv, memkv,
      wa, abt, avw, bwa, bba, onw)


NS = 8
TS = 8
SB = NS * TS
SBP = 128


def _sample_mixer_kernel(proj_ref, st_ref, ck_ref, cv_ref, wa_ref, abt_ref, avw_ref, bwa_ref,
                         bba_ref, onw_ref, out_ref, stn_ref, cvs_ref,
                         qin_ref, xq_ref, kot_ref, lat_ref, vhm_ref, ghm_ref, ohm_ref, ox_ref):
    ri = _iota((SB, SB), 0)
    ci = _iota((SB, SB), 1)
    same_seq = jnp.right_shift(ri, 3) == jnp.right_shift(ci, 3)
    causal = same_seq & (ci <= ri)

    a_br, vn = _group_a(proj_ref, wa_ref, abt_ref, avw_ref, causal)
    out_ref[:, 0:D_A] = a_br.astype(BF16)
    cvs_ref[...] = vn

    q = proj_ref[:, C_BQ:C_BQ + D_BK].astype(F32)
    k = proj_ref[:, C_BK:C_BK + D_BK].astype(F32)
    vb = proj_ref[:, C_BV:C_BV + D_B]
    bgb = proj_ref[:, C_BG:C_BG + D_B]
    pre = _dot(proj_ref[:, C_BR:C_BR + LANES], bwa_ref[...]) + bba_ref[...]
    log_a = _log_sigmoid(pre) * (1.0 / GATE_TAU)
    causal_bf = jnp.where(causal, 1.0, 0.0).astype(BF16)
    seq_bf = jnp.where(same_seq, 1.0, 0.0).astype(BF16)
    hi, lo = _split(log_a)
    cum = _dot(causal_bf, hi) + _dot(causal_bf, lo)
    tot = _dot(seq_bf, hi) + _dot(seq_bf, lo)
    q_in = q * (DK_B ** -0.5) * jnp.exp(cum)
    k_in = k * jnp.exp(-cum)
    k_out = k * jnp.exp(tot - cum)
    qin_ref[...] = q_in
    zpad = jnp.zeros((SBP - SB, D_BK), F32)
    kot_ref[...] = jnp.concatenate([k_out, zpad], axis=0).T.astype(BF16)
    lat_ref[...] = jnp.concatenate([log_a, zpad], axis=0).T
    xq_ref[...] = proj_ref[:, C_XQ:C_XQ + D_X].astype(F32)

    wins = _gla_intra_windows(q_in, k_in, vb, causal)
    sel_r = _iota((D_B, DV_B), 0)
    sel_c = _iota((D_B, DV_B), 1)
    vhm_ref[...] = jnp.zeros_like(vhm_ref)
    for h in range(H_B):
        off = h * DV_B - WIN_START[h]
        w = wins[h]
        if off:
            w = pltpu.roll(w, WIN - off, 1)
        ohm_ref[h] = w[:, :DV_B]
        sel = jnp.where(sel_r == sel_c + h * DV_B, 1.0, 0.0).astype(BF16)
        vhm_ref[h, 0:SB, :] = _dot(vb, sel).astype(BF16)
        ghm_ref[h] = _dot(bgb, sel)

    mask_x = (jnp.right_shift(_iota((H_X * TS, N_MEM * H_X), 0), 3)
              == jnp.bitwise_and(_iota((H_X * TS, N_MEM * H_X), 1), H_X - 1))
    mask_b = jnp.right_shift(_iota((H_B * TS, D_BK), 0), 3) == _head_id(_iota((H_B * TS, D_BK), 1), DK_B, H_B)
    lane_seq = jnp.right_shift(_iota((D_BK, SBP), 1), 3)

    def per_seq(s, carry):
        r0 = pl.multiple_of(s * TS, TS)
        q8 = xq_ref[pl.ds(r0, TS), :]
        q32 = jnp.concatenate([q8[:, h * HD_X:(h + 1) * HD_X] for h in range(H_X)], axis=0)
        sc = _dot_nt(q32.astype(BF16), ck_ref[s].astype(BF16)) * (HD_X ** -0.5)
        sc = jnp.where(mask_x, sc, -jnp.inf)
        e = jnp.exp(sc - jnp.max(sc, axis=-1, keepdims=True))
        den = jnp.sum(e, axis=-1, keepdims=True)
        o = _dot(e.astype(BF16), cv_ref[s].astype(BF16)) / den
        ox_ref[pl.ds(r0, TS), :] = jnp.concatenate(
            [o[h * TS:(h + 1) * TS] for h in range(H_X)], axis=1)
        qi8 = qin_ref[pl.ds(r0, TS), :]
        qbd2 = jnp.where(mask_b, jnp.concatenate([qi8] * H_B, axis=0), 0.0).astype(BF16)
        s0 = jnp.concatenate([st_ref[s, h] for h in range(H_B)], axis=0)
        o_inter = _dot(qbd2, s0.astype(BF16))
        for h in range(H_B):
            ohm_ref[h, pl.ds(r0, TS), :] += o_inter[h * TS:(h + 1) * TS]
        dec = jnp.exp(jnp.sum(jnp.where(lane_seq == s, lat_ref[...], 0.0), axis=1, keepdims=True))
        kot = jnp.where(lane_seq == s, kot_ref[...], jnp.zeros((), BF16))
        kv = jnp.concatenate(
            [_dot(kot[h * DK_B:(h + 1) * DK_B], vhm_ref[h]) for h in range(H_B)], axis=0)
        s_new = s0 * dec + kv
        for h in range(H_B):
            stn_ref[s, h] = s_new[h * DK_B:(h + 1) * DK_B]
        return carry

    lax.fori_loop(0, NS, per_seq, 0, unroll=4)

    selt_r = _iota((DV_B, D_B), 0)
    selt_c = _iota((DV_B, D_B), 1)
    b_br = jnp.zeros((SB, D_B), F32)
    for h in range(H_B):
        o_h = ohm_ref[h]
        ms = jnp.mean(o_h * o_h, axis=-1, keepdims=True)
        ob = (o_h * lax.rsqrt(ms + EPS) * onw_ref[...] * _silu(ghm_ref[h])).astype(BF16)
        selt = jnp.where(selt_c == selt_r + h * DV_B, 1.0, 0.0).astype(BF16)
        b_br = b_br + _dot(ob, selt)
    out_ref[:, D_A:D_A + D_B] = b_br.astype(BF16)

    xg = proj_ref[:, C_XG:C_XG + D_X].astype(F32)
    out_ref[:, D_A + D_B:D_MODEL] = (ox_ref[...] * _silu(xg)).astype(BF16)


def _sample_mixer(proj, state, ck, cv, wa, abt, avw, bwa, bba, onw):
    nseq = state.shape[1]
    const = lambda *shape: pl.BlockSpec(shape, lambda i: (0,) * len(shape))
    return pl.pallas_call(
        _sample_mixer_kernel,
        grid=(nseq // NS,),
        in_specs=[pl.BlockSpec((SB, D_PROJ), lambda i: (i, 0)),
                  pl.BlockSpec((None, NS, H_B, DK_B, DV_B), lambda i: (0, i, 0, 0, 0)),
                  pl.BlockSpec((NS, N_MEM * H_X, HD_X), lambda i: (i, 0, 0)),
                  pl.BlockSpec((NS, N_MEM * H_X, HD_X), lambda i: (i, 0, 0)),
                  const(H_A, SB, SB), const(SB, H_A), const(1, D_A),
                  const(LANES, D_BK), const(1, D_BK), const(1, DV_B)],
        out_specs=[pl.BlockSpec((SB, D_MODEL), lambda i: (i, 0)),
                   pl.BlockSpec((None, NS, H_B, DK_B, DV_B), lambda i: (0, i, 0, 0, 0)),
                   pl.BlockSpec((SB, D_A), lambda i: (i, 0))],
        out_shape=[jax.ShapeDtypeStruct((nseq * TS, D_MODEL), BF16),
                   jax.ShapeDtypeStruct((1, nseq, H_B, DK_B, DV_B), F32),
                   jax.ShapeDtypeStruct((nseq * TS, D_A), F32)],
        scratch_shapes=[pltpu.VMEM((SB, D_BK), F32),
                        pltpu.VMEM((SB, D_X), F32),
                        pltpu.VMEM((D_BK, SBP), BF16),
                        pltpu.VMEM((D_BK, SBP), F32),
                        pltpu.VMEM((H_B, SBP, DV_B), BF16),
                        pltpu.VMEM((H_B, SB, DV_B), F32),
                        pltpu.VMEM((H_B, SB, DV_B), F32),
                        pltpu.VMEM((SB, D_X), F32)],
        compiler_params=pltpu.CompilerParams(
            dimension_semantics=("arbitrary",), vmem_limit_bytes=VMEM_LIMIT),
        name="sample_mixer",
    )(proj, state, ck, cv, wa, abt, avw, bwa, bba, onw)


def kernel(x_prompt, x_sample, mem_prompt, state_gla, cache_mem_k, cache_mem_v, norm_w, w_in,
           a_vnorm_w, a_ws, a_bs, b_wa, b_ba, b_onorm_w, mem_norm_w, w_mem_kv, w_out, final_norm_w):
    batch, seq, _ = x_prompt.shape
    nseq, tdec, _ = x_sample.shape
    depth = w_in.shape[0]
    assert depth == 1 and tdec == TS and seq % PT == 0 and nseq % NS == 0

    xp = x_prompt.reshape(batch * seq, D_MODEL)
    xs = x_sample.reshape(nseq * TS, D_MODEL)
    mem = mem_prompt.reshape(batch * N_MEM, D_MODEL)
    w_in_bf, w_out_bf, proj_s, memkv, mem_k, mem_v = _weight_prep(
        jnp.transpose(w_in[0]), w_out[0], w_mem_kv[0], xs, norm_w[0], mem, mem_norm_w[0])
    bwa = jnp.concatenate([b_wa[0], jnp.zeros((LANES - GATE_RANK, D_BK), F32)], axis=0).astype(BF16)
    bba = b_ba[0].reshape(1, D_BK)
    avw = a_vnorm_w[0].reshape(1, D_A)
    onw_p = jnp.tile(b_onorm_w[0], H_B).reshape(1, D_B)
    onw_s = b_onorm_w[0].reshape(1, DV_B)
    wa_p = a_ws[0]
    abt_p = a_bs[0].T
    wa_s = jnp.tile(a_ws[0][:, :TS, :TS], (1, NS, NS))
    abt_s = jnp.tile(a_bs[0][:, :TS], (1, NS)).T

    br_s, st_s, cvs = _sample_mixer(
        proj_s, state_gla,
        cache_mem_k.reshape(nseq, N_MEM * H_X, HD_X), cache_mem_v.reshape(nseq, N_MEM * H_X, HD_X),
        wa_s, abt_s, avw, bwa, bba, onw_s)
    y_s = _out_proj(br_s, w_out_bf, xs, final_norm_w, bm=256, name="out_proj_s")

    y_p, st_p = _prompt_layer(xp, w_in_bf, w_out_bf, norm_w[0], final_norm_w, memkv, wa_p, abt_p,
                              avw, bwa, bba, onw_p, batch=batch, seq=seq)

    return (y_p.reshape(batch, seq, D_MODEL),
            y_s.reshape(nseq, TS, D_MODEL),
            mem_k.reshape(1, batch, N_MEM, H_X, HD_X),
            mem_v.reshape(1, batch, N_MEM, H_X, HD_X),
            st_p.reshape(1, batch, H_B, DK_B, DV_B),
            st_s,
            cvs.reshape(1, nseq, TS, D_A))
```

```python
import functools

import jax
import jax.numpy as jnp
from jax import lax
from jax.experimental import pallas as pl
from jax.experimental.pallas import tpu as pltpu

F32 = jnp.float32
BF16 = jnp.bfloat16

D_MODEL = 2048
D_A = 768
H_A = 4
HD_A = 192
CHUNK_A = 128
D_B = 768
H_B = 4
DV_B = 192
DK_B = 96
D_BK = 384
GATE_RANK = 16
GATE_TAU = 16.0
CHUNK_B = 64
D_X = 512
H_X = 4
HD_X = 128
N_MEM = 256
EPS = 1e-6

LANES = 128
MXU_COLS = 256
C_AU, C_AV, C_AG = 0, 768, 1536
C_BQ, C_BK, C_BV, C_BG = 2304, 2688, 3072, 3840
C_XQ, C_XG, C_BR = 4608, 5120, 5632
D_PROJ = 5760
D_PROJ_PAD = 5888
WIN_START = (0, 128, 384, 512)
WIN = 256
HEAD_SPLIT = DV_B - LANES
OUT_BN = 512

VMEM_LIMIT = 60 * 1024 * 1024


def _dot(a, b):
    return jnp.dot(a, b, preferred_element_type=F32)


def _dot_nt(a, b):
    return lax.dot_general(a, b, (((1,), (1,)), ((), ())), preferred_element_type=F32)


def _split(x):
    hi = x.astype(BF16)
    lo = (x - hi.astype(F32)).astype(BF16)
    return hi, lo


def _silu(x):
    return x / (1.0 + jnp.exp(-x))


def _log_sigmoid(x):
    return jnp.minimum(x, 0.0) - jnp.log1p(jnp.exp(-jnp.abs(x)))


def _head_id(idx, width, n):
    h = jnp.zeros_like(idx)
    for i in range(1, n):
        h = h + (idx >= i * width).astype(jnp.int32)
    return h


def _iota(shape, dim):
    return lax.broadcasted_iota(jnp.int32, shape, dim)


def _merge_windows(r):
    rows = r[0].shape[0]
    lo = _iota((rows, LANES), 1) < HEAD_SPLIT
    tiles = [r[0][:, :LANES], jnp.where(lo, r[0][:, LANES:], r[1][:, :LANES]), r[1][:, LANES:],
             r[2][:, :LANES], jnp.where(lo, r[2][:, LANES:], r[3][:, :LANES]), r[3][:, LANES:]]
    return jnp.concatenate(tiles, axis=1)


def _group_a(proj_ref, wa_ref, abt_ref, avw_ref, mask):
    u = proj_ref[:, C_AU:C_AU + D_A].astype(F32)
    v = proj_ref[:, C_AV:C_AV + D_A].astype(F32)
    g = proj_ref[:, C_AG:C_AG + D_A].astype(F32)
    rows = u.shape[0]
    ms = jnp.mean(v * v, axis=-1, keepdims=True)
    vn = v * lax.rsqrt(ms + EPS) * avw_ref[...]
    vb = vn.astype(BF16)
    r = []
    for gi in range(H_A):
        w = jnp.where(mask, wa_ref[gi], 0.0).astype(BF16)
        r.append(_dot(w, vb[:, WIN_START[gi]:WIN_START[gi] + WIN]))
    mixed = _merge_windows(r)
    col_g = _head_id(_iota((rows, D_A), 1), HD_A, H_A)
    bias = jnp.zeros((rows, D_A), F32)
    for gi in range(H_A):
        bias = jnp.where(col_g == gi, abt_ref[:, gi:gi + 1], bias)
    return u * (mixed + bias) * _silu(g), vn


def _gla_intra_windows(q_in, k_in, vb, mask):
    rows = q_in.shape[0]
    col_h = _head_id(_iota((rows, D_BK), 1), DK_B, H_B)
    kb = k_in.astype(BF16)
    r = []
    for h in range(H_B):
        qh = jnp.where(col_h == h, q_in, 0.0).astype(BF16)
        s = jnp.where(mask, _dot_nt(qh, kb), 0.0).astype(BF16)
        r.append(_dot(s, vb[:, WIN_START[h]:WIN_START[h] + WIN]))
    return r


PREP_BN = MXU_COLS
N_MAIN_BLOCKS = C_BG // PREP_BN


GATE_BLOCK = D_PROJ_PAD // PREP_BN - 1
N_IN_BLOCKS = GATE_BLOCK + 1
N_OUT_BLOCKS = D_MODEL // PREP_BN
PREP_BUFS = 4


def _weight_prep_kernel(wt_hbm, wo_hbm, wm_hbm, xs_ref, nw_ref, mem_ref, mnw_ref,
                        oi_ref, oo_ref, ps_ref, mkv_ref, mk_ref, mv_ref,
                        buf, sem, hs_ref, hm_ref, wm_buf, wm_sem):
    r = pl.program_id(0)
    slot = lax.rem(r, PREP_BUFS)
    mem_step0 = N_IN_BLOCKS + N_OUT_BLOCKS

    def mem_copy(j):
        return pltpu.make_async_copy(wm_hbm.at[:, pl.ds(j * PREP_BN, PREP_BN)], wm_buf.at[j],
                                     wm_sem.at[j])

    for j in range(N_MEM_BLOCKS):
        @pl.when(r == N_IN_BLOCKS + j)
        def _(j=j):
            mem_copy(j).start()

    @pl.when(r == N_IN_BLOCKS)
    def _():
        x = mem_ref[...]
        ms = jnp.mean(x * x, axis=-1, keepdims=True)
        hm_ref[...] = (x * lax.rsqrt(ms + EPS) * mnw_ref[...]).astype(BF16)

    for j in range(N_MEM_BLOCKS):
        @pl.when(r == mem_step0 + j)
        def _(j=j):
            mem_copy(j).wait()
            kv = _dot(hm_ref[...], wm_buf[j].astype(BF16))
            mkv_ref[...] = kv
            dst = mk_ref if j * PREP_BN < D_X else mv_ref
            for hh in range(PREP_BN // HD_X):
                head = (j * PREP_BN % D_X) // HD_X + hh
                for b in range(kv.shape[0] // N_MEM):
                    dst[b, pl.ds(head, N_MEM, stride=H_X), :] = (
                        kv[b * N_MEM:(b + 1) * N_MEM, hh * HD_X:(hh + 1) * HD_X])

    def in_copy(rr, sl):
        start = pl.multiple_of(jnp.where(rr < N_MAIN_BLOCKS, rr * PREP_BN, rr * PREP_BN + GATE_RANK), 8)
        return pltpu.make_async_copy(wt_hbm.at[pl.ds(start, PREP_BN), :], buf.at[sl], sem.at[sl])

    def gate_copy(sl):
        return pltpu.make_async_copy(wt_hbm.at[pl.ds(C_BG, GATE_RANK), :],
                                     buf.at[sl, pl.ds(0, GATE_RANK), :], sem.at[sl])

    def out_copy(rr, sl):
        start = pl.multiple_of((rr - N_IN_BLOCKS) * PREP_BN, PREP_BN)
        return pltpu.make_async_copy(wo_hbm.at[pl.ds(start, PREP_BN), :], buf.at[sl], sem.at[sl])

    def start_fetch(rr, sl):
        @pl.when(rr < GATE_BLOCK)
        def _():
            in_copy(rr, sl).start()

        @pl.when(rr == GATE_BLOCK)
        def _():
            gate_copy(sl).start()

        @pl.when(rr > GATE_BLOCK)
        def _():
            out_copy(rr, sl).start()

    @pl.when(r == 0)
    def _():
        for ahead in range(PREP_BUFS - 1):
            start_fetch(r + ahead, ahead)

    nxt = r + PREP_BUFS - 1

    @pl.when(nxt < mem_step0)
    def _():
        start_fetch(nxt, lax.rem(nxt, PREP_BUFS))

    @pl.when(r == 0)
    def _():
        x = xs_ref[...]
        ms = jnp.mean(x * x, axis=-1, keepdims=True)
        hs_ref[...] = (x * lax.rsqrt(ms + EPS) * nw_ref[...]).astype(BF16)

    @pl.when(r < GATE_BLOCK)
    def _():
        in_copy(r, slot).wait()
        oi_ref[...] = buf[slot].T.astype(BF16)
        ps_ref[...] = _dot(hs_ref[...], oi_ref[...]).astype(BF16)

    @pl.when(r == GATE_BLOCK)
    def _():
        gate_copy(slot).wait()
        rows = jnp.concatenate([buf[slot, 0:GATE_RANK, :],
                                jnp.zeros((PREP_BN - GATE_RANK, D_MODEL), F32)], axis=0)
        oi_ref[...] = rows.T.astype(BF16)
        ps_ref[...] = _dot(hs_ref[...], oi_ref[...]).astype(BF16)

    @pl.when((r > GATE_BLOCK) & (r < mem_step0))
    def _():
        out_copy(r, slot).wait()
        oo_ref[...] = buf[slot].astype(BF16)


N_MEM_BLOCKS = 2 * D_X // PREP_BN


def _weight_prep(w_t, w_out, w_mem, xs, nw, mem, mnw):
    k = w_t.shape[1]
    m = xs.shape[0]
    mm = mem.shape[0]
    batch = mm // N_MEM
    mem_step0 = N_IN_BLOCKS + N_OUT_BLOCKS
    in_block = lambda r: (0, jnp.minimum(r, GATE_BLOCK))
    mem_j = lambda r: jnp.maximum(r - mem_step0, 0)
    per_d_x = D_X // PREP_BN
    resident = lambda *shape: pl.BlockSpec(shape, lambda r: (0,) * len(shape),
                                           pipeline_mode=pl.Buffered(1))
    return pl.pallas_call(
        _weight_prep_kernel,
        grid=(mem_step0 + N_MEM_BLOCKS,),
        in_specs=[pl.BlockSpec(memory_space=pl.ANY), pl.BlockSpec(memory_space=pl.ANY),
                  pl.BlockSpec(memory_space=pl.ANY),
                  resident(m, k), pl.BlockSpec((1, k), lambda r: (0, 0)),
                  resident(mm, k), pl.BlockSpec((1, k), lambda r: (0, 0))],
        out_specs=[pl.BlockSpec((k, PREP_BN), in_block),
                   pl.BlockSpec((PREP_BN, D_MODEL),
                                lambda r: (jnp.clip(r - N_IN_BLOCKS, 0, N_OUT_BLOCKS - 1), 0)),
                   pl.BlockSpec((m, PREP_BN), in_block),
                   pl.BlockSpec((None, mm, PREP_BN),
                                lambda r: (mem_j(r) // per_d_x, 0, lax.rem(mem_j(r), per_d_x))),
                   pl.BlockSpec((batch, N_MEM * H_X, HD_X), lambda r: (0, 0, 0)),
                   pl.BlockSpec((batch, N_MEM * H_X, HD_X), lambda r: (0, 0, 0))],
        out_shape=[jax.ShapeDtypeStruct((k, D_PROJ_PAD), BF16),
                   jax.ShapeDtypeStruct((D_MODEL, D_MODEL), BF16),
                   jax.ShapeDtypeStruct((m, D_PROJ_PAD), BF16),
                   jax.ShapeDtypeStruct((2, mm, D_X), F32),
                   jax.ShapeDtypeStruct((batch, N_MEM * H_X, HD_X), F32),
                   jax.ShapeDtypeStruct((batch, N_MEM * H_X, HD_X), F32)],
        scratch_shapes=[pltpu.VMEM((PREP_BUFS, PREP_BN, k), F32),
                        pltpu.SemaphoreType.DMA((PREP_BUFS,)),
                        pltpu.VMEM((m, k), BF16),
                        pltpu.VMEM((mm, k), BF16),
                        pltpu.VMEM((N_MEM_BLOCKS, k, PREP_BN), F32),
                        pltpu.SemaphoreType.DMA((N_MEM_BLOCKS,))],
        compiler_params=pltpu.CompilerParams(
            dimension_semantics=("arbitrary",), vmem_limit_bytes=VMEM_LIMIT),
        name="weight_prep",
    )(w_t, w_out, w_mem, xs, nw.reshape(1, k), mem, mnw.reshape(1, k))


def _out_proj_kernel(br_ref, w_ref, x_ref, fw_ref, y_ref, acc_ref, ssq_ref):
    j = pl.program_id(1)
    nj = pl.num_programs(1)
    acc = _dot(br_ref[...], w_ref[...]) + x_ref[...]
    acc_ref[j] = acc
    part = jnp.sum(acc * acc, axis=-1, keepdims=True)

    @pl.when(j == 0)
    def _():
        ssq_ref[...] = part

    @pl.when(j > 0)
    def _():
        ssq_ref[...] += part

    @pl.when(j == nj - 1)
    def _():
        scale = lax.rsqrt(ssq_ref[...] * (1.0 / D_MODEL) + EPS)
        for c in range(D_MODEL // OUT_BN):
            cols = slice(c * OUT_BN, (c + 1) * OUT_BN)
            y_ref[:, cols] = acc_ref[c] * scale * fw_ref[:, cols]


def _out_proj(br, w, x, fw, *, bm, name):
    m, k = br.shape
    n = w.shape[1]
    assert n == D_MODEL
    return pl.pallas_call(
        _out_proj_kernel,
        grid=(m // bm, n // OUT_BN),
        in_specs=[pl.BlockSpec((bm, k), lambda i, j: (i, 0)),
                  pl.BlockSpec((k, OUT_BN), lambda i, j: (0, j)),
                  pl.BlockSpec((bm, OUT_BN), lambda i, j: (i, j)),
                  pl.BlockSpec((1, n), lambda i, j: (0, 0))],
        out_specs=pl.BlockSpec((bm, n), lambda i, j: (i, 0)),
        out_shape=jax.ShapeDtypeStruct((m, n), F32),
        scratch_shapes=[pltpu.VMEM((n // OUT_BN, bm, OUT_BN), F32), pltpu.VMEM((bm, 1), F32)],
        compiler_params=pltpu.CompilerParams(
            dimension_semantics=("arbitrary", "arbitrary"), vmem_limit_bytes=VMEM_LIMIT),
        name=name,
    )(br, w, x, fw.reshape(1, n))


PT = 256
GLA_ROWS = 128
N_MIXER_STAGES = 2 + 4 * (PT // GLA_ROWS) + 1 + 2 * H_X


def _group_a_chunks(proj_ref, wa_ref, abt_ref, avw_ref):
    v = proj_ref[:, C_AV:C_AV + D_A].astype(F32)
    n_chunks = v.shape[0] // CHUNK_A
    ms = jnp.mean(v * v, axis=-1, keepdims=True)
    vb = (v * lax.rsqrt(ms + EPS) * avw_ref[...]).astype(BF16)
    tril = _iota((CHUNK_A, CHUNK_A), 1) <= _iota((CHUNK_A, CHUNK_A), 0)
    r = []
    for gi in range(H_A):
        w = jnp.where(tril, wa_ref[gi], 0.0).astype(BF16)
        ws = slice(WIN_START[gi], WIN_START[gi] + WIN)
        rhs = jnp.concatenate([vb[c * CHUNK_A:(c + 1) * CHUNK_A, ws] for c in range(n_chunks)], axis=1)
        res = _dot(w, rhs)
        r.append(jnp.concatenate([res[:, c * WIN:(c + 1) * WIN] for c in range(n_chunks)], axis=0))
    mixed = _merge_windows(r)
    col_g = _head_id(_iota((CHUNK_A, D_A), 1), HD_A, H_A)
    bias = jnp.zeros((CHUNK_A, D_A), F32)
    for gi in range(H_A):
        bias = jnp.where(col_g == gi, abt_ref[:, gi:gi + 1], bias)
    bias = jnp.concatenate([bias] * n_chunks, axis=0)
    u = proj_ref[:, C_AU:C_AU + D_A].astype(F32)
    g = proj_ref[:, C_AG:C_AG + D_A].astype(F32)
    return u * (mixed + bias) * _silu(g)


def _prompt_block(proj_ref, out_ref, mk_ref, mv_ref, wa_ref, abt_ref, avw_ref, bwa_ref, bba_ref,
                  onw_ref, sbd_ref, o_ref, la_ref):
    out_ref[:, 0:D_A] = _group_a_chunks(proj_ref, wa_ref, abt_ref, avw_ref).astype(BF16)
    yield

    pre = _dot(proj_ref[:, C_BR:C_BR + LANES], bwa_ref[...]) + bba_ref[...]
    la_ref[...] = _log_sigmoid(pre) * (1.0 / GATE_TAU)
    yield
    gl = GLA_ROWS
    tril_bf = jnp.where(_iota((gl, gl), 1) <= _iota((gl, gl), 0), 1.0, 0.0).astype(BF16)
    win_col = _iota((gl, WIN), 1)
    col_hq = _head_id(_iota((gl, D_BK), 1), DK_B, H_B)
    tril_heads = (_iota((H_B * gl, gl), 1) <= jnp.bitwise_and(_iota((H_B * gl, gl), 0), gl - 1))
    for c in range(PT // gl):
        sl = slice(c * gl, (c + 1) * gl)
        log_a = la_ref[sl, :]
        q = proj_ref[sl, C_BQ:C_BQ + D_BK].astype(F32)
        k = proj_ref[sl, C_BK:C_BK + D_BK].astype(F32)
        vb = proj_ref[sl, C_BV:C_BV + D_B]
        hi, lo = _split(log_a)
        cum2 = _dot(tril_bf, jnp.concatenate([hi, lo], axis=1))
        yield
        cum = cum2[:, :D_BK] + cum2[:, D_BK:]
        tot_row = cum[gl - 1:gl, :]
        mid_row = cum[CHUNK_B - 1:CHUNK_B, :]
        la_t = log_a.T
        q_s = q * (DK_B ** -0.5)
        q_dec = q_s * jnp.exp(cum)
        q_in = q_s * jnp.exp(cum - mid_row)
        k_in = k * jnp.exp(mid_row - cum)
        k_out = k * jnp.exp(tot_row - cum)
        q_heads = jnp.concatenate([jnp.where(col_hq == h, q_in, 0.0) for h in range(H_B)], axis=0)
        sc = _dot_nt(q_heads.astype(BF16), k_in.astype(BF16))
        yield
        sc = jnp.where(tril_heads, sc, 0.0).astype(BF16)
        o_intra = _merge_windows(
            [_dot(sc[h * gl:(h + 1) * gl], vb[:, WIN_START[h]:WIN_START[h] + WIN])
             for h in range(H_B)])
        s_old = sbd_ref[...]
        o_inter = _dot(q_dec.astype(BF16), s_old.astype(BF16))
        yield
        kot = k_out.T.astype(BF16)
        for h in range(H_B):
            rs = slice(h * DK_B, (h + 1) * DK_B)
            ws = slice(WIN_START[h], WIN_START[h] + WIN)
            dec = jnp.exp(jnp.sum(la_t[rs], axis=1, keepdims=True))
            lo_col = h * DV_B - WIN_START[h]
            in_head = (win_col >= lo_col) & (win_col < lo_col + DV_B)
            v_h = jnp.where(in_head, vb[:, ws], jnp.zeros((), BF16))
            kv = _dot(kot[rs], v_h)
            sbd_ref[rs, ws] = s_old[rs, ws] * dec + kv
        o_ref[sl, :] = o_intra + o_inter
        yield
    o = o_ref[...]
    o2 = o * o
    t = [o2[:, j * LANES:(j + 1) * LANES] for j in range(D_B // LANES)]
    lo = _iota((PT, LANES), 1) < HEAD_SPLIT
    t1a = jnp.where(lo, t[1], 0.0)
    t4a = jnp.where(lo, t[4], 0.0)
    sums = (t[0] + t1a, (t[1] - t1a) + t[2], t[3] + t4a, (t[4] - t4a) + t[5])
    inv = [lax.rsqrt(jnp.sum(x, axis=-1, keepdims=True) * (1.0 / DV_B) + EPS) for x in sums]
    inv = jnp.concatenate(
        [jnp.broadcast_to(inv[0], (PT, LANES)), jnp.where(lo, inv[0], inv[1]),
         jnp.broadcast_to(inv[1], (PT, LANES)), jnp.broadcast_to(inv[2], (PT, LANES)),
         jnp.where(lo, inv[2], inv[3]), jnp.broadcast_to(inv[3], (PT, LANES))], axis=1)
    o_n = o * inv * onw_ref[...]
    bg = proj_ref[:, C_BG:C_BG + D_B].astype(F32)
    out_ref[:, D_A:D_A + D_B] = (o_n * _silu(bg)).astype(BF16)
    yield

    for h in range(H_X):
        hs = slice(h * HD_X, (h + 1) * HD_X)
        qh = proj_ref[:, C_XQ + h * HD_X:C_XQ + (h + 1) * HD_X]
        s = _dot_nt(qh, mk_ref[:, hs].astype(BF16)) * (HD_X ** -0.5)
        yield
        e = jnp.exp(s - jnp.max(s, axis=-1, keepdims=True))
        den = jnp.sum(e, axis=-1, keepdims=True)
        ox = _dot(e.astype(BF16), mv_ref[:, hs].astype(BF16)) / den
        xg = proj_ref[:, C_XG + h * HD_X:C_XG + (h + 1) * HD_X].astype(F32)
        out_ref[:, D_A + D_B + h * HD_X:D_A + D_B + (h + 1) * HD_X] = (ox * _silu(xg)).astype(BF16)
        yield


def _prompt_layer_kernel(xn_ref, wi_ref, wo_ref, nw_ref, fw_ref, mk_ref, mv_ref, wa_ref, abt_ref,
                         avw_ref, bwa_ref, bba_ref, onw_ref, y_ref, st_ref,
                         pa_ref, pb_ref, xk_ref, br_ref, h_ref, sbd_ref, o_ref, la_ref, *, nt):
    s = pl.program_id(0)
    cur = jnp.maximum(s - 1, 0)
    t = lax.rem(cur, nt)

    @pl.when(t == 0)
    def _():
        sbd_ref[...] = jnp.zeros_like(sbd_ref)

    def in_proj_stages(pn_ref):
        x = xn_ref[...]
        ms = jnp.mean(x * x, axis=-1, keepdims=True)
        h_ref[...] = (x * lax.rsqrt(ms + EPS) * nw_ref[...]).astype(BF16)
        yield
        for c0 in range(0, D_PROJ_PAD, MXU_COLS):
            cols = pl.ds(c0, MXU_COLS)
            pn_ref[:, cols] = _dot(h_ref[...], wi_ref[:, cols]).astype(BF16)
            yield
        xk_ref[...] = xn_ref[...]

    def out_proj_stages():
        ssq = jnp.zeros((PT, 1), F32)
        for c0 in range(0, D_MODEL, OUT_BN):
            cols = pl.ds(c0, OUT_BN)
            acc = _dot(br_ref[...], wo_ref[:, cols]) + xk_ref[:, cols]
            y_ref[:, cols] = acc
            ssq = ssq + jnp.sum(acc * acc, axis=-1, keepdims=True)
            yield
        y_ref[...] = y_ref[...] * lax.rsqrt(ssq * (1.0 / D_MODEL) + EPS) * fw_ref[...]
        yield

    def run(order, streams):
        for name in order:
            next(streams[name])
        for name, gen in streams.items():
            assert next(gen, "done") == "done", name

    def mixer_stages(pc_ref):
        return _prompt_block(pc_ref, br_ref, mk_ref, mv_ref, wa_ref, abt_ref, avw_ref, bwa_ref,
                             bba_ref, onw_ref, sbd_ref, o_ref, la_ref)

    n_p = 1 + D_PROJ_PAD // MXU_COLS
    n_m = N_MIXER_STAGES
    n_o = 1 + D_MODEL // OUT_BN

    def interleave(counts):
        order = []
        for i in range(max(counts.values())):
            order += [name for name, n in counts.items() if i < n]
        return order

    def body(pn_ref, pc_ref):
        order = ["P"] + interleave({"M": n_m, "P": n_p - 1}) + ["O"] * n_o
        run(order, {"P": in_proj_stages(pn_ref), "M": mixer_stages(pc_ref), "O": out_proj_stages()})

    last = pl.num_programs(0) - 1

    @pl.when(s == 0)
    def _():
        run(["P"] * n_p, {"P": in_proj_stages(pa_ref)})

    @pl.when((lax.rem(s, 2) == 0) & (s > 0) & (s < last))
    def _():
        body(pa_ref, pb_ref)

    @pl.when(lax.rem(s, 2) == 1)
    def _():
        body(pb_ref, pa_ref)

    @pl.when(s == last)
    def _():
        run(["M"] * n_m + ["O"] * n_o, {"M": mixer_stages(pb_ref), "O": out_proj_stages()})

    @pl.when((t == nt - 1) & (s > 0))
    def _():
        for h in range(H_B):
            off = h * DV_B - WIN_START[h]
            blk = sbd_ref[h * DK_B:(h + 1) * DK_B, WIN_START[h]:WIN_START[h] + WIN]
            if off:
                blk = pltpu.roll(blk, WIN - off, 1)
            st_ref[0, h] = blk[:, :DV_B]


def _prompt_layer(xp, w_in_bf, w_out_bf, nw, fw, memkv, wa, abt, avw, bwa, bba, onw, *, batch, seq):
    nt = seq // PT
    nblk = batch * nt
    assert nblk % 2 == 0
    cur = lambda s: jnp.maximum(s - 1, 0)
    const = lambda *shape: pl.BlockSpec(shape, lambda s: (0,) * len(shape))
    resident = lambda *shape: pl.BlockSpec(shape, lambda s: (0,) * len(shape),
                                           pipeline_mode=pl.Buffered(1))
    return pl.pallas_call(
        functools.partial(_prompt_layer_kernel, nt=nt),
        grid=(nblk + 1,),
        in_specs=[pl.BlockSpec((PT, D_MODEL), lambda s: (jnp.minimum(s, nblk - 1), 0)),
                  resident(D_MODEL, D_PROJ_PAD), resident(D_MODEL, D_MODEL),
                  const(1, D_MODEL), const(1, D_MODEL),
                  pl.BlockSpec((None, N_MEM, D_X), lambda s: (0, cur(s) // nt, 0)),
                  pl.BlockSpec((None, N_MEM, D_X), lambda s: (1, cur(s) // nt, 0)),
                  const(H_A, CHUNK_A, CHUNK_A), const(CHUNK_A, H_A), const(1, D_A),
                  const(LANES, D_BK), const(1, D_BK), const(1, D_B)],
        out_specs=[pl.BlockSpec((PT, D_MODEL), lambda s: (cur(s), 0)),
                   pl.BlockSpec((1, H_B, DK_B, DV_B), lambda s: (cur(s) // nt, 0, 0, 0))],
        out_shape=[jax.ShapeDtypeStruct((batch * seq, D_MODEL), F32),
                   jax.ShapeDtypeStruct((batch, H_B, DK_B, DV_B), F32)],
        scratch_shapes=[pltpu.VMEM((PT, D_PROJ_PAD), BF16),
                        pltpu.VMEM((PT, D_PROJ_PAD), BF16),
                        pltpu.VMEM((PT, D_MODEL), F32),
                        pltpu.VMEM((PT, D_MODEL), BF16),
                        pltpu.VMEM((PT, D_MODEL), BF16),
                        pltpu.VMEM((D_BK, D_B), F32),
                        pltpu.VMEM((PT, D_B), F32),
                        pltpu.VMEM((PT, D_BK), F32)],
        compiler_params=pltpu.CompilerParams(
            dimension_semantics=("arbitrary",), vmem_limit_bytes=VMEM_LIMIT),
        name="prompt_layer",
    )(xp, w_in_bf, w_out_bf, nw.reshape(1, D_MODEL), fw.reshape(1, D_MODEL), memkv, memkv,
      wa, abt, avw, bwa, bba, onw)


NS = 8
TS = 8
SB = NS * TS
SBP = 128


def _sample_mixer_kernel(proj_ref, st_ref, ck_ref, cv_ref, wa_ref, abt_ref, avw_ref, bwa_ref,
                         bba_ref, onw_ref, out_ref, stn_ref, cvs_ref,
                         qin_ref, xq_ref, kot_ref, lat_ref, vhm_ref, ghm_ref, ohm_ref, ox_ref):
    ri = _iota((SB, SB), 0)
    ci = _iota((SB, SB), 1)
    same_seq = jnp.right_shift(ri, 3) == jnp.right_shift(ci, 3)
    causal = same_seq & (ci <= ri)

    a_br, vn = _group_a(proj_ref, wa_ref, abt_ref, avw_ref, causal)
    out_ref[:, 0:D_A] = a_br.astype(BF16)
    cvs_ref[...] = vn

    q = proj_ref[:, C_BQ:C_BQ + D_BK].astype(F32)
    k = proj_ref[:, C_BK:C_BK + D_BK].astype(F32)
    vb = proj_ref[:, C_BV:C_BV + D_B]
    bgb = proj_ref[:, C_BG:C_BG + D_B]
    pre = _dot(proj_ref[:, C_BR:C_BR + LANES], bwa_ref[...]) + bba_ref[...]
    log_a = _log_sigmoid(pre) * (1.0 / GATE_TAU)
    causal_bf = jnp.where(causal, 1.0, 0.0).astype(BF16)
    seq_bf = jnp.where(same_seq, 1.0, 0.0).astype(BF16)
    hi, lo = _split(log_a)
    cum = _dot(causal_bf, hi) + _dot(causal_bf, lo)
    tot = _dot(seq_bf, hi) + _dot(seq_bf, lo)
    q_in = q * (DK_B ** -0.5) * jnp.exp(cum)
    k_in = k * jnp.exp(-cum)
    k_out = k * jnp.exp(tot - cum)
    qin_ref[...] = q_in
    zpad = jnp.zeros((SBP - SB, D_BK), F32)
    kot_ref[...] = jnp.concatenate([k_out, zpad], axis=0).T.astype(BF16)
    lat_ref[...] = jnp.concatenate([log_a, zpad], axis=0).T
    xq_ref[...] = proj_ref[:, C_XQ:C_XQ + D_X].astype(F32)

    wins = _gla_intra_windows(q_in, k_in, vb, causal)
    sel_r = _iota((D_B, DV_B), 0)
    sel_c = _iota((D_B, DV_B), 1)
    vhm_ref[...] = jnp.zeros_like(vhm_ref)
    for h in range(H_B):
        off = h * DV_B - WIN_START[h]
        w = wins[h]
        if off:
            w = pltpu.roll(w, WIN - off, 1)
        ohm_ref[h] = w[:, :DV_B]
        sel = jnp.where(sel_r == sel_c + h * DV_B, 1.0, 0.0).astype(BF16)
        vhm_ref[h, 0:SB, :] = _dot(vb, sel).astype(BF16)
        ghm_ref[h] = _dot(bgb, sel)

    mask_x = (jnp.right_shift(_iota((H_X * TS, N_MEM * H_X), 0), 3)
              == jnp.bitwise_and(_iota((H_X * TS, N_MEM * H_X), 1), H_X - 1))
    mask_b = jnp.right_shift(_iota((H_B * TS, D_BK), 0), 3) == _head_id(_iota((H_B * TS, D_BK), 1), DK_B, H_B)
    lane_seq = jnp.right_shift(_iota((D_BK, SBP), 1), 3)

    def per_seq(s, carry):
        r0 = pl.multiple_of(s * TS, TS)
        q8 = xq_ref[pl.ds(r0, TS), :]
        q32 = jnp.concatenate([q8[:, h * HD_X:(h + 1) * HD_X] for h in range(H_X)], axis=0)
        sc = _dot_nt(q32.astype(BF16), ck_ref[s].astype(BF16)) * (HD_X ** -0.5)
        sc = jnp.where(mask_x, sc, -jnp.inf)
        e = jnp.exp(sc - jnp.max(sc, axis=-1, keepdims=True))
        den = jnp.sum(e, axis=-1, keepdims=True)
        o = _dot(e.astype(BF16), cv_ref[s].astype(BF16)) / den
        ox_ref[pl.ds(r0, TS), :] = jnp.concatenate(
            [o[h * TS:(h + 1) * TS] for h in range(H_X)], axis=1)
        qi8 = qin_ref[pl.ds(r0, TS), :]
        qbd2 = jnp.where(mask_b, jnp.concatenate([qi8] * H_B, axis=0), 0.0).astype(BF16)
        s0 = jnp.concatenate([st_ref[s, h] for h in range(H_B)], axis=0)
        o_inter = _dot(qbd2, s0.astype(BF16))
        for h in range(H_B):
            ohm_ref[h, pl.ds(r0, TS), :] += o_inter[h * TS:(h + 1) * TS]
        dec = jnp.exp(jnp.sum(jnp.where(lane_seq == s, lat_ref[...], 0.0), axis=1, keepdims=True))
        kot = jnp.where(lane_seq == s, kot_ref[...], jnp.zeros((), BF16))
        kv = jnp.concatenate(
            [_dot(kot[h * DK_B:(h + 1) * DK_B], vhm_ref[h]) for h in range(H_B)], axis=0)
        s_new = s0 * dec + kv
        for h in range(H_B):
            stn_ref[s, h] = s_new[h * DK_B:(h + 1) * DK_B]
        return carry

    lax.fori_loop(0, NS, per_seq, 0, unroll=4)

    selt_r = _iota((DV_B, D_B), 0)
    selt_c = _iota((DV_B, D_B), 1)
    b_br = jnp.zeros((SB, D_B), F32)
    for h in range(H_B):
        o_h = ohm_ref[h]
        ms = jnp.mean(o_h * o_h, axis=-1, keepdims=True)
        ob = (o_h * lax.rsqrt(ms + EPS) * onw_ref[...] * _silu(ghm_ref[h])).astype(BF16)
        selt = jnp.where(selt_c == selt_r + h * DV_B, 1.0, 0.0).astype(BF16)
        b_br = b_br + _dot(ob, selt)
    out_ref[:, D_A:D_A + D_B] = b_br.astype(BF16)

    xg = proj_ref[:, C_XG:C_XG + D_X].astype(F32)
    out_ref[:, D_A + D_B:D_MODEL] = (ox_ref[...] * _silu(xg)).astype(BF16)


def _sample_mixer(proj, state, ck, cv, wa, abt, avw, bwa, bba, onw):
    nseq = state.shape[1]
    const = lambda *shape: pl.BlockSpec(shape, lambda i: (0,) * len(shape))
    return pl.pallas_call(
        _sample_mixer_kernel,
        grid=(nseq // NS,),
        in_specs=[pl.BlockSpec((SB, D_PROJ), lambda i: (i, 0)),
                  pl.BlockSpec((None, NS, H_B, DK_B, DV_B), lambda i: (0, i, 0, 0, 0)),
                  pl.BlockSpec((NS, N_MEM * H_X, HD_X), lambda i: (i, 0, 0)),
                  pl.BlockSpec((NS, N_MEM * H_X, HD_X), lambda i: (i, 0, 0)),
                  const(H_A, SB, SB), const(SB, H_A), const(1, D_A),
                  const(LANES, D_BK), const(1, D_BK), const(1, DV_B)],
        out_specs=[pl.BlockSpec((SB, D_MODEL), lambda i: (i, 0)),
                   pl.BlockSpec((None, NS, H_B, DK_B, DV_B), lambda i: (0, i, 0, 0, 0)),
                   pl.BlockSpec((SB, D_A), lambda i: (i, 0))],
        out_shape=[jax.ShapeDtypeStruct((nseq * TS, D_MODEL), BF16),
                   jax.ShapeDtypeStruct((1, nseq, H_B, DK_B, DV_B), F32),
                   jax.ShapeDtypeStruct((nseq * TS, D_A), F32)],
        scratch_shapes=[pltpu.VMEM((SB, D_BK), F32),
                        pltpu.VMEM((SB, D_X), F32),
                        pltpu.VMEM((D_BK, SBP), BF16),
                        pltpu.VMEM((D_BK, SBP), F32),
                        pltpu.VMEM((H_B, SBP, DV_B), BF16),
                        pltpu.VMEM((H_B, SB, DV_B), F32),
                        pltpu.VMEM((H_B, SB, DV_B), F32),
                        pltpu.VMEM((SB, D_X), F32)],
        compiler_params=pltpu.CompilerParams(
            dimension_semantics=("arbitrary",), vmem_limit_bytes=VMEM_LIMIT),
        name="sample_mixer",
    )(proj, state, ck, cv, wa, abt, avw, bwa, bba, onw)


def kernel(x_prompt, x_sample, mem_prompt, state_gla, cache_mem_k, cache_mem_v, norm_w, w_in,
           a_vnorm_w, a_ws, a_bs, b_wa, b_ba, b_onorm_w, mem_norm_w, w_mem_kv, w_out, final_norm_w):
    batch, seq, _ = x_prompt.shape
    nseq, tdec, _ = x_sample.shape
    depth = w_in.shape[0]
    assert depth == 1 and tdec == TS and seq % PT == 0 and nseq % NS == 0

    xp = x_prompt.reshape(batch * seq, D_MODEL)
    xs = x_sample.reshape(nseq * TS, D_MODEL)
    mem = mem_prompt.reshape(batch * N_MEM, D_MODEL)
    w_in_bf, w_out_bf, proj_s, memkv, mem_k, mem_v = _weight_prep(
        jnp.transpose(w_in[0]), w_out[0], w_mem_kv[0], xs, norm_w[0], mem, mem_norm_w[0])
    bwa = jnp.concatenate([b_wa[0], jnp.zeros((LANES - GATE_RANK, D_BK), F32)], axis=0).astype(BF16)
    bba = b_ba[0].reshape(1, D_BK)
    avw = a_vnorm_w[0].reshape(1, D_A)
    onw_p = jnp.tile(b_onorm_w[0], H_B).reshape(1, D_B)
    onw_s = b_onorm_w[0].reshape(1, DV_B)
    wa_p = a_ws[0]
    abt_p = a_bs[0].T
    wa_s = jnp.tile(a_ws[0][:, :TS, :TS], (1, NS, NS))
    abt_s = jnp.tile(a_bs[0][:, :TS], (1, NS)).T

    br_s, st_s, cvs = _sample_mixer(
        proj_s, state_gla,
        cache_mem_k.reshape(nseq, N_MEM * H_X, HD_X), cache_mem_v.reshape(nseq, N_MEM * H_X, HD_X),
        wa_s, abt_s, avw, bwa, bba, onw_s)
    y_s = _out_proj(br_s, w_out_bf, xs, final_norm_w, bm=256, name="out_proj_s")

    y_p, st_p = _prompt_layer(xp, w_in_bf, w_out_bf, norm_w[0], final_norm_w, memkv, wa_p, abt_p,
                              avw, bwa, bba, onw_p, batch=batch, seq=seq)

    return (y_p.reshape(batch, seq, D_MODEL),
            y_s.reshape(nseq, TS, D_MODEL),
            mem_k.reshape(1, batch, N_MEM, H_X, HD_X),
            mem_v.reshape(1, batch, N_MEM, H_X, HD_X),
            st_p.reshape(1, batch, H_B, DK_B, DV_B),
            st_s,
            cvs.reshape(1, nseq, TS, D_A))
```

```python
import functools

import jax
import jax.numpy as jnp
from jax import lax
from jax.experimental import pallas as pl
from jax.experimental.pallas import tpu as pltpu

F32 = jnp.float32
BF16 = jnp.bfloat16

D_MODEL = 2048
D_A = 768
H_A = 4
HD_A = 192
CHUNK_A = 128
D_B = 768
H_B = 4
DV_B = 192
DK_B = 96
D_BK = 384
GATE_RANK = 16
GATE_TAU = 16.0
CHUNK_B = 64
D_X = 512
H_X = 4
HD_X = 128
N_MEM = 256
EPS = 1e-6

LANES = 128
MXU_COLS = 256
C_AU, C_AV, C_AG = 0, 768, 1536
C_BQ, C_BK, C_BV, C_BG = 2304, 2688, 3072, 3840
C_XQ, C_XG, C_BR = 4608, 5120, 5632
D_PROJ = 5760
D_PROJ_PAD = 5888
WIN_START = (0, 128, 384, 512)
WIN = 256
HEAD_SPLIT = DV_B - LANES
OUT_BN = 512

VMEM_LIMIT = 60 * 1024 * 1024


def _dot(a, b):
    return jnp.dot(a, b, preferred_element_type=F32)


def _dot_nt(a, b):
    return lax.dot_general(a, b, (((1,), (1,)), ((), ())), preferred_element_type=F32)


def _split(x):
    hi = x.astype(BF16)
    lo = (x - hi.astype(F32)).astype(BF16)
    return hi, lo


def _silu(x):
    return x / (1.0 + jnp.exp(-x))


def _log_sigmoid(x):
    return jnp.minimum(x, 0.0) - jnp.log1p(jnp.exp(-jnp.abs(x)))


def _head_id(idx, width, n):
    h = jnp.zeros_like(idx)
    for i in range(1, n):
        h = h + (idx >= i * width).astype(jnp.int32)
    return h


def _iota(shape, dim):
    return lax.broadcasted_iota(jnp.int32, shape, dim)


def _merge_windows(r):
    rows = r[0].shape[0]
    lo = _iota((rows, LANES), 1) < HEAD_SPLIT
    tiles = [r[0][:, :LANES], jnp.where(lo, r[0][:, LANES:], r[1][:, :LANES]), r[1][:, LANES:],
             r[2][:, :LANES], jnp.where(lo, r[2][:, LANES:], r[3][:, :LANES]), r[3][:, LANES:]]
    return jnp.concatenate(tiles, axis=1)


def _group_a(proj_ref, wa_ref, abt_ref, avw_ref, mask):
    u = proj_ref[:, C_AU:C_AU + D_A].astype(F32)
    v = proj_ref[:, C_AV:C_AV + D_A].astype(F32)
    g = proj_ref[:, C_AG:C_AG + D_A].astype(F32)
    rows = u.shape[0]
    ms = jnp.mean(v * v, axis=-1, keepdims=True)
    vn = v * lax.rsqrt(ms + EPS) * avw_ref[...]
    vb = vn.astype(BF16)
    r = []
    for gi in range(H_A):
        w = jnp.where(mask, wa_ref[gi], 0.0).astype(BF16)
        r.append(_dot(w, vb[:, WIN_START[gi]:WIN_START[gi] + WIN]))
    mixed = _merge_windows(r)
    col_g = _head_id(_iota((rows, D_A), 1), HD_A, H_A)
    bias = jnp.zeros((rows, D_A), F32)
    for gi in range(H_A):
        bias = jnp.where(col_g == gi, abt_ref[:, gi:gi + 1], bias)
    return u * (mixed + bias) * _silu(g), vn


def _gla_intra_windows(q_in, k_in, vb, mask):
    rows = q_in.shape[0]
    col_h = _head_id(_iota((rows, D_BK), 1), DK_B, H_B)
    kb = k_in.astype(BF16)
    r = []
    for h in range(H_B):
        qh = jnp.where(col_h == h, q_in, 0.0).astype(BF16)
        s = jnp.where(mask, _dot_nt(qh, kb), 0.0).astype(BF16)
        r.append(_dot(s, vb[:, WIN_START[h]:WIN_START[h] + WIN]))
    return r


PREP_BN = MXU_COLS
N_MAIN_BLOCKS = C_BG // PREP_BN


GATE_BLOCK = D_PROJ_PAD // PREP_BN - 1
N_IN_BLOCKS = GATE_BLOCK + 1
N_OUT_BLOCKS = D_MODEL // PREP_BN
PREP_BUFS = 4


def _weight_prep_kernel(wt_hbm, wo_hbm, wm_hbm, xs_ref, nw_ref, mem_ref, mnw_ref,
                        oi_ref, oo_ref, ps_ref, mkv_ref, mk_ref, mv_ref,
                        buf, sem, hs_ref, hm_ref, wm_buf, wm_sem):
    r = pl.program_id(0)
    slot = lax.rem(r, PREP_BUFS)
    mem_step0 = N_IN_BLOCKS + N_OUT_BLOCKS

    def mem_copy(j):
        return pltpu.make_async_copy(wm_hbm.at[:, pl.ds(j * PREP_BN, PREP_BN)], wm_buf.at[j],
                                     wm_sem.at[j])

    for j in range(N_MEM_BLOCKS):
        @pl.when(r == N_IN_BLOCKS + j)
        def _(j=j):
            mem_copy(j).start()

    @pl.when(r == N_IN_BLOCKS)
    def _():
        x = mem_ref[...]
        ms = jnp.mean(x * x, axis=-1, keepdims=True)
        hm_ref[...] = (x * lax.rsqrt(ms + EPS) * mnw_ref[...]).astype(BF16)

    for j in range(N_MEM_BLOCKS):
        @pl.when(r == mem_step0 + j)
        def _(j=j):
            mem_copy(j).wait()
            kv = _dot(hm_ref[...], wm_buf[j].astype(BF16))
            mkv_ref[...] = kv
            dst = mk_ref if j * PREP_BN < D_X else mv_ref
            for hh in range(PREP_BN // HD_X):
                head = (j * PREP_BN % D_X) // HD_X + hh
                for b in range(kv.shape[0] // N_MEM):
                    dst[b, pl.ds(head, N_MEM, stride=H_X), :] = (
                        kv[b * N_MEM:(b + 1) * N_MEM, hh * HD_X:(hh + 1) * HD_X])

    def in_copy(rr, sl):
        start = pl.multiple_of(jnp.where(rr < N_MAIN_BLOCKS, rr * PREP_BN, rr * PREP_BN + GATE_RANK), 8)
        return pltpu.make_async_copy(wt_hbm.at[pl.ds(start, PREP_BN), :], buf.at[sl], sem.at[sl])

    def gate_copy(sl):
        return pltpu.make_async_copy(wt_hbm.at[pl.ds(C_BG, GATE_RANK), :],
                                     buf.at[sl, pl.ds(0, GATE_RANK), :], sem.at[sl])

    def out_copy(rr, sl):
        start = pl.multiple_of((rr - N_IN_BLOCKS) * PREP_BN, PREP_BN)
        return pltpu.make_async_copy(wo_hbm.at[pl.ds(start, PREP_BN), :], buf.at[sl], sem.at[sl])

    def start_fetch(rr, sl):
        @pl.when(rr < GATE_BLOCK)
        def _():
            in_copy(rr, sl).start()

        @pl.when(rr == GATE_BLOCK)
        def _():
            gate_copy(sl).start()

        @pl.when(rr > GATE_BLOCK)
        def _():
            out_copy(rr, sl).start()

    @pl.when(r == 0)
    def _():
        for ahead in range(PREP_BUFS - 1):
            start_fetch(r + ahead, ahead)

    nxt = r + PREP_BUFS - 1

    @pl.when(nxt < mem_step0)
    def _():
        start_fetch(nxt, lax.rem(nxt, PREP_BUFS))

    @pl.when(r == 0)
    def _():
        x = xs_ref[...]
        ms = jnp.mean(x * x, axis=-1, keepdims=True)
        hs_ref[...] = (x * lax.rsqrt(ms + EPS) * nw_ref[...]).astype(BF16)

    @pl.when(r < GATE_BLOCK)
    def _():
        in_copy(r, slot).wait()
        oi_ref[...] = buf[slot].T.astype(BF16)
        ps_ref[...] = _dot(hs_ref[...], oi_ref[...]).astype(BF16)

    @pl.when(r == GATE_BLOCK)
    def _():
        gate_copy(slot).wait()
        rows = jnp.concatenate([buf[slot, 0:GATE_RANK, :],
                                jnp.zeros((PREP_BN - GATE_RANK, D_MODEL), F32)], axis=0)
        oi_ref[...] = rows.T.astype(BF16)
        ps_ref[...] = _dot(hs_ref[...], oi_ref[...]).astype(BF16)

    @pl.when((r > GATE_BLOCK) & (r < mem_step0))
    def _():
        out_copy(r, slot).wait()
        oo_ref[...] = buf[slot].astype(BF16)


N_MEM_BLOCKS = 2 * D_X // PREP_BN


def _weight_prep(w_t, w_out, w_mem, xs, nw, mem, mnw):
    k = w_t.shape[1]
    m = xs.shape[0]
    mm = mem.shape[0]
    batch = mm // N_MEM
    mem_step0 = N_IN_BLOCKS + N_OUT_BLOCKS
    in_block = lambda r: (0, jnp.minimum(r, GATE_BLOCK))
    mem_j = lambda r: jnp.maximum(r - mem_step0, 0)
    per_d_x = D_X // PREP_BN
    resident = lambda *shape: pl.BlockSpec(shape, lambda r: (0,) * len(shape),
                                           pipeline_mode=pl.Buffered(1))
    return pl.pallas_call(
        _weight_prep_kernel,
        grid=(mem_step0 + N_MEM_BLOCKS,),
        in_specs=[pl.BlockSpec(memory_space=pl.ANY), pl.BlockSpec(memory_space=pl.ANY),
                  pl.BlockSpec(memory_space=pl.ANY),
                  resident(m, k), pl.BlockSpec((1, k), lambda r: (0, 0)),
                  resident(mm, k), pl.BlockSpec((1, k), lambda r: (0, 0))],
        out_specs=[pl.BlockSpec((k, PREP_BN), in_block),
                   pl.BlockSpec((PREP_BN, D_MODEL),
                                lambda r: (jnp.clip(r - N_IN_BLOCKS, 0, N_OUT_BLOCKS - 1), 0)),
                   pl.BlockSpec((m, PREP_BN), in_block),
                   pl.BlockSpec((None, mm, PREP_BN),
                                lambda r: (mem_j(r) // per_d_x, 0, lax.rem(mem_j(r), per_d_x))),
                   pl.BlockSpec((batch, N_MEM * H_X, HD_X), lambda r: (0, 0, 0)),
                   pl.BlockSpec((batch, N_MEM * H_X, HD_X), lambda r: (0, 0, 0))],
        out_shape=[jax.ShapeDtypeStruct((k, D_PROJ_PAD), BF16),
                   jax.ShapeDtypeStruct((D_MODEL, D_MODEL), BF16),
                   jax.ShapeDtypeStruct((m, D_PROJ_PAD), BF16),
                   jax.ShapeDtypeStruct((2, mm, D_X), F32),
                   jax.ShapeDtypeStruct((batch, N_MEM * H_X, HD_X), F32),
                   jax.ShapeDtypeStruct((batch, N_MEM * H_X, HD_X), F32)],
        scratch_shapes=[pltpu.VMEM((PREP_BUFS, PREP_BN, k), F32),
                        pltpu.SemaphoreType.DMA((PREP_BUFS,)),
                        pltpu.VMEM((m, k), BF16),
                        pltpu.VMEM((mm, k), BF16),
                        pltpu.VMEM((N_MEM_BLOCKS, k, PREP_BN), F32),
                        pltpu.SemaphoreType.DMA((N_MEM_BLOCKS,))],
        compiler_params=pltpu.CompilerParams(
            dimension_semantics=("arbitrary",), vmem_limit_bytes=VMEM_LIMIT),
        name="weight_prep",
    )(w_t, w_out, w_mem, xs, nw.reshape(1, k), mem, mnw.reshape(1, k))


def _out_proj_kernel(br_ref, w_ref, x_ref, fw_ref, y_ref):
    acc = _dot(br_ref[...], w_ref[...]) + x_ref[...]
    ms = jnp.mean(acc * acc, axis=-1, keepdims=True)
    y_ref[...] = acc * lax.rsqrt(ms + EPS) * fw_ref[...]


def _out_proj(br, w, x, fw, *, bm, name):
    m, k = br.shape
    n = w.shape[1]
    return pl.pallas_call(
        _out_proj_kernel,
        grid=(m // bm,),
        in_specs=[pl.BlockSpec((bm, k), lambda i: (i, 0)),
                  pl.BlockSpec((k, n), lambda i: (0, 0)),
                  pl.BlockSpec((bm, n), lambda i: (i, 0)),
                  pl.BlockSpec((1, n), lambda i: (0, 0))],
        out_specs=pl.BlockSpec((bm, n), lambda i: (i, 0)),
        out_shape=jax.ShapeDtypeStruct((m, n), F32),
        compiler_params=pltpu.CompilerParams(
            dimension_semantics=("arbitrary",), vmem_limit_bytes=VMEM_LIMIT),
        name=name,
    )(br, w, x, fw.reshape(1, n))


PT = 256
GLA_ROWS = 128
N_MIXER_STAGES = 2 + 4 * (PT // GLA_ROWS) + 1 + 2 * H_X


def _group_a_chunks(proj_ref, wa_ref, abt_ref, avw_ref):
    v = proj_ref[:, C_AV:C_AV + D_A].astype(F32)
    n_chunks = v.shape[0] // CHUNK_A
    ms = jnp.mean(v * v, axis=-1, keepdims=True)
    vb = (v * lax.rsqrt(ms + EPS) * avw_ref[...]).astype(BF16)
    tril = _iota((CHUNK_A, CHUNK_A), 1) <= _iota((CHUNK_A, CHUNK_A), 0)
    r = []
    for gi in range(H_A):
        w = jnp.where(tril, wa_ref[gi], 0.0).astype(BF16)
        ws = slice(WIN_START[gi], WIN_START[gi] + WIN)
        rhs = jnp.concatenate([vb[c * CHUNK_A:(c + 1) * CHUNK_A, ws] for c in range(n_chunks)], axis=1)
        res = _dot(w, rhs)
        r.append(jnp.concatenate([res[:, c * WIN:(c + 1) * WIN] for c in range(n_chunks)], axis=0))
    mixed = _merge_windows(r)
    col_g = _head_id(_iota((CHUNK_A, D_A), 1), HD_A, H_A)
    bias = jnp.zeros((CHUNK_A, D_A), F32)
    for gi in range(H_A):
        bias = jnp.where(col_g == gi, abt_ref[:, gi:gi + 1], bias)
    bias = jnp.concatenate([bias] * n_chunks, axis=0)
    u = proj_ref[:, C_AU:C_AU + D_A].astype(F32)
    g = proj_ref[:, C_AG:C_AG + D_A].astype(F32)
    return u * (mixed + bias) * _silu(g)


def _prompt_block(proj_ref, out_ref, mk_ref, mv_ref, wa_ref, abt_ref, avw_ref, bwa_ref, bba_ref,
                  onw_ref, sbd_ref, o_ref, la_ref):
    out_ref[:, 0:D_A] = _group_a_chunks(proj_ref, wa_ref, abt_ref, avw_ref).astype(BF16)
    yield

    pre = _dot(proj_ref[:, C_BR:C_BR + LANES], bwa_ref[...]) + bba_ref[...]
    la_ref[...] = _log_sigmoid(pre) * (1.0 / GATE_TAU)
    yield
    gl = GLA_ROWS
    tril_bf = jnp.where(_iota((gl, gl), 1) <= _iota((gl, gl), 0), 1.0, 0.0).astype(BF16)
    win_col = _iota((gl, WIN), 1)
    col_hq = _head_id(_iota((gl, D_BK), 1), DK_B, H_B)
    tril_heads = (_iota((H_B * gl, gl), 1) <= jnp.bitwise_and(_iota((H_B * gl, gl), 0), gl - 1))
    for c in range(PT // gl):
        sl = slice(c * gl, (c + 1) * gl)
        log_a = la_ref[sl, :]
        q = proj_ref[sl, C_BQ:C_BQ + D_BK].astype(F32)
        k = proj_ref[sl, C_BK:C_BK + D_BK].astype(F32)
        vb = proj_ref[sl, C_BV:C_BV + D_B]
        hi, lo = _split(log_a)
        cum2 = _dot(tril_bf, jnp.concatenate([hi, lo], axis=1))
        yield
        cum = cum2[:, :D_BK] + cum2[:, D_BK:]
        tot_row = cum[gl - 1:gl, :]
        mid_row = cum[CHUNK_B - 1:CHUNK_B, :]
        la_t = log_a.T
        q_s = q * (DK_B ** -0.5)
        q_dec = q_s * jnp.exp(cum)
        q_in = q_s * jnp.exp(cum - mid_row)
        k_in = k * jnp.exp(mid_row - cum)
        k_out = k * jnp.exp(tot_row - cum)
        q_heads = jnp.concatenate([jnp.where(col_hq == h, q_in, 0.0) for h in range(H_B)], axis=0)
        sc = _dot_nt(q_heads.astype(BF16), k_in.astype(BF16))
        yield
        sc = jnp.where(tril_heads, sc, 0.0).astype(BF16)
        o_intra = _merge_windows(
            [_dot(sc[h * gl:(h + 1) * gl], vb[:, WIN_START[h]:WIN_START[h] + WIN])
             for h in range(H_B)])
        s_old = sbd_ref[...]
        o_inter = _dot(q_dec.astype(BF16), s_old.astype(BF16))
        yield
        kot = k_out.T.astype(BF16)
        for h in range(H_B):
            rs = slice(h * DK_B, (h + 1) * DK_B)
            ws = slice(WIN_START[h], WIN_START[h] + WIN)
            dec = jnp.exp(jnp.sum(la_t[rs], axis=1, keepdims=True))
            lo_col = h * DV_B - WIN_START[h]
            in_head = (win_col >= lo_col) & (win_col < lo_col + DV_B)
            v_h = jnp.where(in_head, vb[:, ws], jnp.zeros((), BF16))
            kv = _dot(kot[rs], v_h)
            sbd_ref[rs, ws] = s_old[rs, ws] * dec + kv
        o_ref[sl, :] = o_intra + o_inter
        yield
    o = o_ref[...]
    o2 = o * o
    t = [o2[:, j * LANES:(j + 1) * LANES] for j in range(D_B // LANES)]
    lo = _iota((PT, LANES), 1) < HEAD_SPLIT
    t1a = jnp.where(lo, t[1], 0.0)
    t4a = jnp.where(lo, t[4], 0.0)
    sums = (t[0] + t1a, (t[1] - t1a) + t[2], t[3] + t4a, (t[4] - t4a) + t[5])
    inv = [lax.rsqrt(jnp.sum(x, axis=-1, keepdims=True) * (1.0 / DV_B) + EPS) for x in sums]
    inv = jnp.concatenate(
        [jnp.broadcast_to(inv[0], (PT, LANES)), jnp.where(lo, inv[0], inv[1]),
         jnp.broadcast_to(inv[1], (PT, LANES)), jnp.broadcast_to(inv[2], (PT, LANES)),
         jnp.where(lo, inv[2], inv[3]), jnp.broadcast_to(inv[3], (PT, LANES))], axis=1)
    o_n = o * inv * onw_ref[...]
    bg = proj_ref[:, C_BG:C_BG + D_B].astype(F32)
    out_ref[:, D_A:D_A + D_B] = (o_n * _silu(bg)).astype(BF16)
    yield

    for h in range(H_X):
        hs = slice(h * HD_X, (h + 1) * HD_X)
        qh = proj_ref[:, C_XQ + h * HD_X:C_XQ + (h + 1) * HD_X]
        s = _dot_nt(qh, mk_ref[:, hs].astype(BF16)) * (HD_X ** -0.5)
        yield
        e = jnp.exp(s - jnp.max(s, axis=-1, keepdims=True))
        den = jnp.sum(e, axis=-1, keepdims=True)
        ox = _dot(e.astype(BF16), mv_ref[:, hs].astype(BF16)) / den
        xg = proj_ref[:, C_XG + h * HD_X:C_XG + (h + 1) * HD_X].astype(F32)
        out_ref[:, D_A + D_B + h * HD_X:D_A + D_B + (h + 1) * HD_X] = (ox * _silu(xg)).astype(BF16)
        yield


def _prompt_layer_kernel(xn_ref, wi_ref, wo_ref, nw_ref, fw_ref, mk_ref, mv_ref, wa_ref, abt_ref,
                         avw_ref, bwa_ref, bba_ref, onw_ref, y_ref, st_ref,
                         pa_ref, pb_ref, xk_ref, br_ref, h_ref, sbd_ref, o_ref, la_ref, *, nt):
    s = pl.program_id(0)
    cur = jnp.maximum(s - 1, 0)
    t = lax.rem(cur, nt)

    @pl.when(t == 0)
    def _():
        sbd_ref[...] = jnp.zeros_like(sbd_ref)

    def in_proj_stages(pn_ref):
        x = xn_ref[...]
        ms = jnp.mean(x * x, axis=-1, keepdims=True)
        h_ref[...] = (x * lax.rsqrt(ms + EPS) * nw_ref[...]).astype(BF16)
        yield
        for c0 in range(0, D_PROJ_PAD, MXU_COLS):
            cols = pl.ds(c0, MXU_COLS)
            pn_ref[:, cols] = _dot(h_ref[...], wi_ref[:, cols]).astype(BF16)
            yield
        xk_ref[...] = xn_ref[...]

    def out_proj_stages():
        ssq = jnp.zeros((PT, 1), F32)
        for c0 in range(0, D_MODEL, OUT_BN):
            cols = pl.ds(c0, OUT_BN)
            acc = _dot(br_ref[...], wo_ref[:, cols]) + xk_ref[:, cols]
            y_ref[:, cols] = acc
            ssq = ssq + jnp.sum(acc * acc, axis=-1, keepdims=True)
            yield
        y_ref[...] = y_ref[...] * lax.rsqrt(ssq * (1.0 / D_MODEL) + EPS) * fw_ref[...]
        yield

    def run(order, streams):
        for name in order:
            next(streams[name])
        for name, gen in streams.items():
            assert next(gen, "done") == "done", name

    def mixer_stages(pc_ref):
        return _prompt_block(pc_ref, br_ref, mk_ref, mv_ref, wa_ref, abt_ref, avw_ref, bwa_ref,
                             bba_ref, onw_ref, sbd_ref, o_ref, la_ref)

    n_p = 1 + D_PROJ_PAD // MXU_COLS
    n_m = N_MIXER_STAGES
    n_o = 1 + D_MODEL // OUT_BN

    def interleave(counts):
        order = []
        for i in range(max(counts.values())):
            order += [name for name, n in counts.items() if i < n]
        return order

    def body(pn_ref, pc_ref):
        order = ["P"] + interleave({"M": n_m, "P": n_p - 1}) + ["O"] * n_o
        run(order, {"P": in_proj_stages(pn_ref), "M": mixer_stages(pc_ref), "O": out_proj_stages()})

    last = pl.num_programs(0) - 1

    @pl.when(s == 0)
    def _():
        run(["P"] * n_p, {"P": in_proj_stages(pa_ref)})

    @pl.when((lax.rem(s, 2) == 0) & (s > 0) & (s < last))
    def _():
        body(pa_ref, pb_ref)

    @pl.when(lax.rem(s, 2) == 1)
    def _():
        body(pb_ref, pa_ref)

    @pl.when(s == last)
    def _():
        run(["M"] * n_m + ["O"] * n_o, {"M": mixer_stages(pb_ref), "O": out_proj_stages()})

    @pl.when((t == nt - 1) & (s > 0))
    def _():
        for h in range(H_B):
            off = h * DV_B - WIN_START[h]
            blk = sbd_ref[h * DK_B:(h + 1) * DK_B, WIN_START[h]:WIN_START[h] + WIN]
            if off:
                blk = pltpu.roll(blk, WIN - off, 1)
            st_ref[0, h] = blk[:, :DV_B]


def _prompt_layer(xp, w_in_bf, w_out_bf, nw, fw, memkv, wa, abt, avw, bwa, bba, onw, *, batch, seq):
    nt = seq // PT
    nblk = batch * nt
    assert nblk % 2 == 0
    cur = lambda s: jnp.maximum(s - 1, 0)
    const = lambda *shape: pl.BlockSpec(shape, lambda s: (0,) * len(shape))
    resident = lambda *shape: pl.BlockSpec(shape, lambda s: (0,) * len(shape),
                                           pipeline_mode=pl.Buffered(1))
    return pl.pallas_call(
        functools.partial(_prompt_layer_kernel, nt=nt),
        grid=(nblk + 1,),
        in_specs=[pl.BlockSpec((PT, D_MODEL), lambda s: (jnp.minimum(s, nblk - 1), 0)),
                  resident(D_MODEL, D_PROJ_PAD), resident(D_MODEL, D_MODEL),
                  const(1, D_MODEL), const(1, D_MODEL),
                  pl.BlockSpec((None, N_MEM, D_X), lambda s: (0, cur(s) // nt, 0)),
                  pl.BlockSpec((None, N_MEM, D_X), lambda s: (1, cur(s) // nt, 0)),
                  const(H_A, CHUNK_A, CHUNK_A), const(CHUNK_A, H_A), const(1, D_A),
                  const(LANES, D_BK), const(1, D_BK), const(1, D_B)],
        out_specs=[pl.BlockSpec((PT, D_MODEL), lambda s: (cur(s), 0)),
                   pl.BlockSpec((1, H_B, DK_B, DV_B), lambda s: (cur(s) // nt, 0, 0, 0))],
        out_shape=[jax.ShapeDtypeStruct((batch * seq, D_MODEL), F32),
                   jax.ShapeDtypeStruct((batch, H_B, DK_B, DV_B), F32)],
        scratch_shapes=[pltpu.VMEM((PT, D_PROJ_PAD), BF16),
                        pltpu.VMEM((PT, D_PROJ_PAD), BF16),
                        pltpu.VMEM((PT, D_MODEL), F32),
                        pltpu.VMEM((PT, D_MODEL), BF16),
                        pltpu.VMEM((PT, D_MODEL), BF16),
                        pltpu.VMEM((D_BK, D_B), F32),
                        pltpu.VMEM((PT, D_B), F32),
                        pltpu.VMEM((PT, D_BK), F32)],
        compiler_params=pltpu.CompilerParams(
            dimension_semantics=("arbitrary",), vmem_limit_bytes=VMEM_LIMIT),
        name="prompt_layer",
    )(xp, w_in_bf, w_out_bf, nw.reshape(1, D_MODEL), fw.reshape(1, D_MODEL), memkv, memkv,
      wa, abt, avw, bwa, bba, onw)


NS = 8
TS = 8
SB = NS * TS
SBP = 128


def _sample_mixer_kernel(proj_ref, st_ref, ck_ref, cv_ref, wa_ref, abt_ref, avw_ref, bwa_ref,
                         bba_ref, onw_ref, out_ref, stn_ref, cvs_ref,
                         qin_ref, xq_ref, kot_ref, lat_ref, vhm_ref, ghm_ref, ohm_ref, ox_ref):
    ri = _iota((SB, SB), 0)
    ci = _iota((SB, SB), 1)
    same_seq = jnp.right_shift(ri, 3) == jnp.right_shift(ci, 3)
    causal = same_seq & (ci <= ri)

    a_br, vn = _group_a(proj_ref, wa_ref, abt_ref, avw_ref, causal)
    out_ref[:, 0:D_A] = a_br.astype(BF16)
    cvs_ref[...] = vn

    q = proj_ref[:, C_BQ:C_BQ + D_BK].astype(F32)
    k = proj_ref[:, C_BK:C_BK + D_BK].astype(F32)
    vb = proj_ref[:, C_BV:C_BV + D_B]
    bgb = proj_ref[:, C_BG:C_BG + D_B]
    pre = _dot(proj_ref[:, C_BR:C_BR + LANES], bwa_ref[...]) + bba_ref[...]
    log_a = _log_sigmoid(pre) * (1.0 / GATE_TAU)
    causal_bf = jnp.where(causal, 1.0, 0.0).astype(BF16)
    seq_bf = jnp.where(same_seq, 1.0, 0.0).astype(BF16)
    hi, lo = _split(log_a)
    cum = _dot(causal_bf, hi) + _dot(causal_bf, lo)
    tot = _dot(seq_bf, hi) + _dot(seq_bf, lo)
    q_in = q * (DK_B ** -0.5) * jnp.exp(cum)
    k_in = k * jnp.exp(-cum)
    k_out = k * jnp.exp(tot - cum)
    qin_ref[...] = q_in
    zpad = jnp.zeros((SBP - SB, D_BK), F32)
    kot_ref[...] = jnp.concatenate([k_out, zpad], axis=0).T.astype(BF16)
    lat_ref[...] = jnp.concatenate([log_a, zpad], axis=0).T
    xq_ref[...] = proj_ref[:, C_XQ:C_XQ + D_X].astype(F32)

    wins = _gla_intra_windows(q_in, k_in, vb, causal)
    sel_r = _iota((D_B, DV_B), 0)
    sel_c = _iota((D_B, DV_B), 1)
    vhm_ref[...] = jnp.zeros_like(vhm_ref)
    for h in range(H_B):
        off = h * DV_B - WIN_START[h]
        w = wins[h]
        if off:
            w = pltpu.roll(w, WIN - off, 1)
        ohm_ref[h] = w[:, :DV_B]
        sel = jnp.where(sel_r == sel_c + h * DV_B, 1.0, 0.0).astype(BF16)
        vhm_ref[h, 0:SB, :] = _dot(vb, sel).astype(BF16)
        ghm_ref[h] = _dot(bgb, sel)

    mask_x = (jnp.right_shift(_iota((H_X * TS, N_MEM * H_X), 0), 3)
              == jnp.bitwise_and(_iota((H_X * TS, N_MEM * H_X), 1), H_X - 1))
    mask_b = jnp.right_shift(_iota((H_B * TS, D_BK), 0), 3) == _head_id(_iota((H_B * TS, D_BK), 1), DK_B, H_B)
    lane_seq = jnp.right_shift(_iota((D_BK, SBP), 1), 3)

    def per_seq(s, carry):
        r0 = pl.multiple_of(s * TS, TS)
        q8 = xq_ref[pl.ds(r0, TS), :]
        q32 = jnp.concatenate([q8[:, h * HD_X:(h + 1) * HD_X] for h in range(H_X)], axis=0)
        sc = _dot_nt(q32.astype(BF16), ck_ref[s].astype(BF16)) * (HD_X ** -0.5)
        sc = jnp.where(mask_x, sc, -jnp.inf)
        e = jnp.exp(sc - jnp.max(sc, axis=-1, keepdims=True))
        den = jnp.sum(e, axis=-1, keepdims=True)
        o = _dot(e.astype(BF16), cv_ref[s].astype(BF16)) / den
        ox_ref[pl.ds(r0, TS), :] = jnp.concatenate(
            [o[h * TS:(h + 1) * TS] for h in range(H_X)], axis=1)
        qi8 = qin_ref[pl.ds(r0, TS), :]
        qbd2 = jnp.where(mask_b, jnp.concatenate([qi8] * H_B, axis=0), 0.0).astype(BF16)
        s0 = jnp.concatenate([st_ref[s, h] for h in range(H_B)], axis=0)
        o_inter = _dot(qbd2, s0.astype(BF16))
        for h in range(H_B):
            ohm_ref[h, pl.ds(r0, TS), :] += o_inter[h * TS:(h + 1) * TS]
        dec = jnp.exp(jnp.sum(jnp.where(lane_seq == s, lat_ref[...], 0.0), axis=1, keepdims=True))
        kot = jnp.where(lane_seq == s, kot_ref[...], jnp.zeros((), BF16))
        kv = jnp.concatenate(
            [_dot(kot[h * DK_B:(h + 1) * DK_B], vhm_ref[h]) for h in range(H_B)], axis=0)
        s_new = s0 * dec + kv
        for h in range(H_B):
            stn_ref[s, h] = s_new[h * DK_B:(h + 1) * DK_B]
        return carry

    lax.fori_loop(0, NS, per_seq, 0, unroll=8)

    selt_r = _iota((DV_B, D_B), 0)
    selt_c = _iota((DV_B, D_B), 1)
    b_br = jnp.zeros((SB, D_B), F32)
    for h in range(H_B):
        o_h = ohm_ref[h]
        ms = jnp.mean(o_h * o_h, axis=-1, keepdims=True)
        ob = (o_h * lax.rsqrt(ms + EPS) * onw_ref[...] * _silu(ghm_ref[h])).astype(BF16)
        selt = jnp.where(selt_c == selt_r + h * DV_B, 1.0, 0.0).astype(BF16)
        b_br = b_br + _dot(ob, selt)
    out_ref[:, D_A:D_A + D_B] = b_br.astype(BF16)

    xg = proj_ref[:, C_XG:C_XG + D_X].astype(F32)
    out_ref[:, D_A + D_B:D_MODEL] = (ox_ref[...] * _silu(xg)).astype(BF16)


def _sample_mixer(proj, state, ck, cv, wa, abt, avw, bwa, bba, onw):
    nseq = state.shape[1]
    const = lambda *shape: pl.BlockSpec(shape, lambda i: (0,) * len(shape))
    return pl.pallas_call(
        _sample_mixer_kernel,
        grid=(nseq // NS,),
        in_specs=[pl.BlockSpec((SB, D_PROJ), lambda i: (i, 0)),
                  pl.BlockSpec((None, NS, H_B, DK_B, DV_B), lambda i: (0, i, 0, 0, 0)),
                  pl.BlockSpec((NS, N_MEM * H_X, HD_X), lambda i: (i, 0, 0)),
                  pl.BlockSpec((NS, N_MEM * H_X, HD_X), lambda i: (i, 0, 0)),
                  const(H_A, SB, SB), const(SB, H_A), const(1, D_A),
                  const(LANES, D_BK), const(1, D_BK), const(1, DV_B)],
        out_specs=[pl.BlockSpec((SB, D_MODEL), lambda i: (i, 0)),
                   pl.BlockSpec((None, NS, H_B, DK_B, DV_B), lambda i: (0, i, 0, 0, 0)),
                   pl.BlockSpec((SB, D_A), lambda i: (i, 0))],
        out_shape=[jax.ShapeDtypeStruct((nseq * TS, D_MODEL), BF16),
                   jax.ShapeDtypeStruct((1, nseq, H_B, DK_B, DV_B), F32),
                   jax.ShapeDtypeStruct((nseq * TS, D_A), F32)],
        scratch_shapes=[pltpu.VMEM((SB, D_BK), F32),
                        pltpu.VMEM((SB, D_X), F32),
                        pltpu.VMEM((D_BK, SBP), BF16),
                        pltpu.VMEM((D_BK, SBP), F32),
                        pltpu.VMEM((H_B, SBP, DV_B), BF16),
                        pltpu.VMEM((H_B, SB, DV_B), F32),
                        pltpu.VMEM((H_B, SB, DV_B), F32),
                        pltpu.VMEM((SB, D_X), F32)],
        compiler_params=pltpu.CompilerParams(
            dimension_semantics=("arbitrary",), vmem_limit_bytes=VMEM_LIMIT),
        name="sample_mixer",
    )(proj, state, ck, cv, wa, abt, avw, bwa, bba, onw)


def kernel(x_prompt, x_sample, mem_prompt, state_gla, cache_mem_k, cache_mem_v, norm_w, w_in,
           a_vnorm_w, a_ws, a_bs, b_wa, b_ba, b_onorm_w, mem_norm_w, w_mem_kv, w_out, final_norm_w):
    batch, seq, _ = x_prompt.shape
    nseq, tdec, _ = x_sample.shape
    depth = w_in.shape[0]
    assert depth == 1 and tdec == TS and seq % PT == 0 and nseq % NS == 0

    xp = x_prompt.reshape(batch * seq, D_MODEL)
    xs = x_sample.reshape(nseq * TS, D_MODEL)
    mem = mem_prompt.reshape(batch * N_MEM, D_MODEL)
    w_in_bf, w_out_bf, proj_s, memkv, mem_k, mem_v = _weight_prep(
        jnp.transpose(w_in[0]), w_out[0], w_mem_kv[0], xs, norm_w[0], mem, mem_norm_w[0])
    bwa = jnp.concatenate([b_wa[0], jnp.zeros((LANES - GATE_RANK, D_BK), F32)], axis=0).astype(BF16)
    bba = b_ba[0].reshape(1, D_BK)
    avw = a_vnorm_w[0].reshape(1, D_A)
    onw_p = jnp.tile(b_onorm_w[0], H_B).reshape(1, D_B)
    onw_s = b_onorm_w[0].reshape(1, DV_B)
    wa_p = a_ws[0]
    abt_p = a_bs[0].T
    wa_s = jnp.tile(a_ws[0][:, :TS, :TS], (1, NS, NS))
    abt_s = jnp.tile(a_bs[0][:, :TS], (1, NS)).T

    br_s, st_s, cvs = _sample_mixer(
        proj_s, state_gla,
        cache_mem_k.reshape(nseq, N_MEM * H_X, HD_X), cache_mem_v.reshape(nseq, N_MEM * H_X, HD_X),
        wa_s, abt_s, avw, bwa, bba, onw_s)
    y_s = _out_proj(br_s, w_out_bf, xs, final_norm_w, bm=256, name="out_proj_s")

    y_p, st_p = _prompt_layer(xp, w_in_bf, w_out_bf, norm_w[0], final_norm_w, memkv, wa_p, abt_p,
                              avw, bwa, bba, onw_p, batch=batch, seq=seq)

    return (y_p.reshape(batch, seq, D_MODEL),
            y_s.reshape(nseq, TS, D_MODEL),
            mem_k.reshape(1, batch, N_MEM, H_X, HD_X),
            mem_v.reshape(1, batch, N_MEM, H_X, HD_X),
            st_p.reshape(1, batch, H_B, DK_B, DV_B),
            st_s,
            cvs.reshape(1, nseq, TS, D_A))
```

```python
import functools

import jax
import jax.numpy as jnp
from jax import lax
from jax.experimental import pallas as pl
from jax.experimental.pallas import tpu as pltpu

F32 = jnp.float32
BF16 = jnp.bfloat16

D_MODEL = 2048
D_A = 768
H_A = 4
HD_A = 192
CHUNK_A = 128
D_B = 768
H_B = 4
DV_B = 192
DK_B = 96
D_BK = 384
GATE_RANK = 16
GATE_TAU = 16.0
CHUNK_B = 64
D_X = 512
H_X = 4
HD_X = 128
N_MEM = 256
EPS = 1e-6

LANES = 128
MXU_COLS = 256
C_AU, C_AV, C_AG = 0, 768, 1536
C_BQ, C_BK, C_BV, C_BG = 2304, 2688, 3072, 3840
C_XQ, C_XG, C_BR = 4608, 5120, 5632
D_PROJ = 5760
D_PROJ_PAD = 5888
WIN_START = (0, 128, 384, 512)
WIN = 256
HEAD_SPLIT = DV_B - LANES
OUT_BN = 512

VMEM_LIMIT = 60 * 1024 * 1024


def _dot(a, b):
    return jnp.dot(a, b, preferred_element_type=F32)


def _dot_nt(a, b):
    return lax.dot_general(a, b, (((1,), (1,)), ((), ())), preferred_element_type=F32)


def _split(x):
    hi = x.astype(BF16)
    lo = (x - hi.astype(F32)).astype(BF16)
    return hi, lo


def _silu(x):
    return x / (1.0 + jnp.exp(-x))


def _log_sigmoid(x):
    return jnp.minimum(x, 0.0) - jnp.log1p(jnp.exp(-jnp.abs(x)))


def _head_id(idx, width, n):
    h = jnp.zeros_like(idx)
    for i in range(1, n):
        h = h + (idx >= i * width).astype(jnp.int32)
    return h


def _iota(shape, dim):
    return lax.broadcasted_iota(jnp.int32, shape, dim)


def _merge_windows(r):
    rows = r[0].shape[0]
    lo = _iota((rows, LANES), 1) < HEAD_SPLIT
    tiles = [r[0][:, :LANES], jnp.where(lo, r[0][:, LANES:], r[1][:, :LANES]), r[1][:, LANES:],
             r[2][:, :LANES], jnp.where(lo, r[2][:, LANES:], r[3][:, :LANES]), r[3][:, LANES:]]
    return jnp.concatenate(tiles, axis=1)


def _group_a(proj_ref, wa_ref, abt_ref, avw_ref, mask):
    u = proj_ref[:, C_AU:C_AU + D_A].astype(F32)
    v = proj_ref[:, C_AV:C_AV + D_A].astype(F32)
    g = proj_ref[:, C_AG:C_AG + D_A].astype(F32)
    rows = u.shape[0]
    ms = jnp.mean(v * v, axis=-1, keepdims=True)
    vn = v * lax.rsqrt(ms + EPS) * avw_ref[...]
    vb = vn.astype(BF16)
    r = []
    for gi in range(H_A):
        w = jnp.where(mask, wa_ref[gi], 0.0).astype(BF16)
        r.append(_dot(w, vb[:, WIN_START[gi]:WIN_START[gi] + WIN]))
    mixed = _merge_windows(r)
    col_g = _head_id(_iota((rows, D_A), 1), HD_A, H_A)
    bias = jnp.zeros((rows, D_A), F32)
    for gi in range(H_A):
        bias = jnp.where(col_g == gi, abt_ref[:, gi:gi + 1], bias)
    return u * (mixed + bias) * _silu(g), vn


def _gla_intra_windows(q_in, k_in, vb, mask):
    rows = q_in.shape[0]
    col_h = _head_id(_iota((rows, D_BK), 1), DK_B, H_B)
    kb = k_in.astype(BF16)
    r = []
    for h in range(H_B):
        qh = jnp.where(col_h == h, q_in, 0.0).astype(BF16)
        s = jnp.where(mask, _dot_nt(qh, kb), 0.0).astype(BF16)
        r.append(_dot(s, vb[:, WIN_START[h]:WIN_START[h] + WIN]))
    return r


PREP_BN = MXU_COLS
N_MAIN_BLOCKS = C_BG // PREP_BN


GATE_BLOCK = D_PROJ_PAD // PREP_BN - 1
N_IN_BLOCKS = GATE_BLOCK + 1
N_OUT_BLOCKS = D_MODEL // PREP_BN
PREP_BUFS = 4


def _weight_prep_kernel(wt_hbm, wo_hbm, wm_hbm, xs_ref, nw_ref, mem_ref, mnw_ref,
                        oi_ref, oo_ref, ps_ref, mkv_ref, mk_ref, mv_ref,
                        buf, sem, hs_ref, hm_ref, wm_buf, wm_sem):
    r = pl.program_id(0)
    slot = lax.rem(r, PREP_BUFS)
    mem_step0 = N_IN_BLOCKS + N_OUT_BLOCKS

    def mem_copy(j):
        return pltpu.make_async_copy(wm_hbm.at[:, pl.ds(j * PREP_BN, PREP_BN)], wm_buf.at[j],
                                     wm_sem.at[j])

    for j in range(N_MEM_BLOCKS):
        @pl.when(r == N_IN_BLOCKS + j)
        def _(j=j):
            mem_copy(j).start()

    @pl.when(r == N_IN_BLOCKS)
    def _():
        x = mem_ref[...]
        ms = jnp.mean(x * x, axis=-1, keepdims=True)
        hm_ref[...] = (x * lax.rsqrt(ms + EPS) * mnw_ref[...]).astype(BF16)

    for j in range(N_MEM_BLOCKS):
        @pl.when(r == mem_step0 + j)
        def _(j=j):
            mem_copy(j).wait()
            kv = _dot(hm_ref[...], wm_buf[j].astype(BF16))
            mkv_ref[...] = kv
            dst = mk_ref if j * PREP_BN < D_X else mv_ref
            for hh in range(PREP_BN // HD_X):
                head = (j * PREP_BN % D_X) // HD_X + hh
                for b in range(kv.shape[0] // N_MEM):
                    dst[b, pl.ds(head, N_MEM, stride=H_X), :] = (
                        kv[b * N_MEM:(b + 1) * N_MEM, hh * HD_X:(hh + 1) * HD_X])

    def in_copy(rr, sl):
        start = pl.multiple_of(jnp.where(rr < N_MAIN_BLOCKS, rr * PREP_BN, rr * PREP_BN + GATE_RANK), 8)
        return pltpu.make_async_copy(wt_hbm.at[pl.ds(start, PREP_BN), :], buf.at[sl], sem.at[sl])

    def gate_copy(sl):
        return pltpu.make_async_copy(wt_hbm.at[pl.ds(C_BG, GATE_RANK), :],
                                     buf.at[sl, pl.ds(0, GATE_RANK), :], sem.at[sl])

    def out_copy(rr, sl):
        start = pl.multiple_of((rr - N_IN_BLOCKS) * PREP_BN, PREP_BN)
        return pltpu.make_async_copy(wo_hbm.at[pl.ds(start, PREP_BN), :], buf.at[sl], sem.at[sl])

    def start_fetch(rr, sl):
        @pl.when(rr < GATE_BLOCK)
        def _():
            in_copy(rr, sl).start()

        @pl.when(rr == GATE_BLOCK)
        def _():
            gate_copy(sl).start()

        @pl.when(rr > GATE_BLOCK)
        def _():
            out_copy(rr, sl).start()

    @pl.when(r == 0)
    def _():
        for ahead in range(PREP_BUFS - 1):
            start_fetch(r + ahead, ahead)

    nxt = r + PREP_BUFS - 1

    @pl.when(nxt < mem_step0)
    def _():
        start_fetch(nxt, lax.rem(nxt, PREP_BUFS))

    @pl.when(r == 0)
    def _():
        x = xs_ref[...]
        ms = jnp.mean(x * x, axis=-1, keepdims=True)
        hs_ref[...] = (x * lax.rsqrt(ms + EPS) * nw_ref[...]).astype(BF16)

    @pl.when(r < GATE_BLOCK)
    def _():
        in_copy(r, slot).wait()
        oi_ref[...] = buf[slot].T.astype(BF16)
        ps_ref[...] = _dot(hs_ref[...], oi_ref[...]).astype(BF16)

    @pl.when(r == GATE_BLOCK)
    def _():
        gate_copy(slot).wait()
        rows = jnp.concatenate([buf[slot, 0:GATE_RANK, :],
                                jnp.zeros((PREP_BN - GATE_RANK, D_MODEL), F32)], axis=0)
        oi_ref[...] = rows.T.astype(BF16)
        ps_ref[...] = _dot(hs_ref[...], oi_ref[...]).astype(BF16)

    @pl.when((r > GATE_BLOCK) & (r < mem_step0))
    def _():
        out_copy(r, slot).wait()
        oo_ref[...] = buf[slot].astype(BF16)


N_MEM_BLOCKS = 2 * D_X // PREP_BN


def _weight_prep(w_t, w_out, w_mem, xs, nw, mem, mnw):
    k = w_t.shape[1]
    m = xs.shape[0]
    mm = mem.shape[0]
    batch = mm // N_MEM
    mem_step0 = N_IN_BLOCKS + N_OUT_BLOCKS
    in_block = lambda r: (0, jnp.minimum(r, GATE_BLOCK))
    mem_j = lambda r: jnp.maximum(r - mem_step0, 0)
    per_d_x = D_X // PREP_BN
    resident = lambda *shape: pl.BlockSpec(shape, lambda r: (0,) * len(shape),
                                           pipeline_mode=pl.Buffered(1))
    return pl.pallas_call(
        _weight_prep_kernel,
        grid=(mem_step0 + N_MEM_BLOCKS,),
        in_specs=[pl.BlockSpec(memory_space=pl.ANY), pl.BlockSpec(memory_space=pl.ANY),
                  pl.BlockSpec(memory_space=pl.ANY),
                  resident(m, k), pl.BlockSpec((1, k), lambda r: (0, 0)),
                  resident(mm, k), pl.BlockSpec((1, k), lambda r: (0, 0))],
        out_specs=[pl.BlockSpec((k, PREP_BN), in_block),
                   pl.BlockSpec((PREP_BN, D_MODEL),
                                lambda r: (jnp.clip(r - N_IN_BLOCKS, 0, N_OUT_BLOCKS - 1), 0)),
                   pl.BlockSpec((m, PREP_BN), in_block),
                   pl.BlockSpec((None, mm, PREP_BN),
                                lambda r: (mem_j(r) // per_d_x, 0, lax.rem(mem_j(r), per_d_x))),
                   pl.BlockSpec((batch, N_MEM * H_X, HD_X), lambda r: (0, 0, 0)),
                   pl.BlockSpec((batch, N_MEM * H_X, HD_X), lambda r: (0, 0, 0))],
        out_shape=[jax.ShapeDtypeStruct((k, D_PROJ_PAD), BF16),
                   jax.ShapeDtypeStruct((D_MODEL, D_MODEL), BF16),
                   jax.ShapeDtypeStruct((m, D_PROJ_PAD), BF16),
                   jax.ShapeDtypeStruct((2, mm, D_X), F32),
                   jax.ShapeDtypeStruct((batch, N_MEM * H_X, HD_X), F32),
                   jax.ShapeDtypeStruct((batch, N_MEM * H_X, HD_X), F32)],
        scratch_shapes=[pltpu.VMEM((PREP_BUFS, PREP_BN, k), F32),
                        pltpu.SemaphoreType.DMA((PREP_BUFS,)),
                        pltpu.VMEM((m, k), BF16),
                        pltpu.VMEM((mm, k), BF16),
                        pltpu.VMEM((N_MEM_BLOCKS, k, PREP_BN), F32),
                        pltpu.SemaphoreType.DMA((N_MEM_BLOCKS,))],
        compiler_params=pltpu.CompilerParams(
            dimension_semantics=("arbitrary",), vmem_limit_bytes=VMEM_LIMIT),
        name="weight_prep",
    )(w_t, w_out, w_mem, xs, nw.reshape(1, k), mem, mnw.reshape(1, k))


def _out_proj_kernel(br_ref, w_ref, x_ref, fw_ref, y_ref):
    acc = _dot(br_ref[...], w_ref[...]) + x_ref[...]
    ms = jnp.mean(acc * acc, axis=-1, keepdims=True)
    y_ref[...] = acc * lax.rsqrt(ms + EPS) * fw_ref[...]


def _out_proj(br, w, x, fw, *, bm, name):
    m, k = br.shape
    n = w.shape[1]
    return pl.pallas_call(
        _out_proj_kernel,
        grid=(m // bm,),
        in_specs=[pl.BlockSpec((bm, k), lambda i: (i, 0)),
                  pl.BlockSpec((k, n), lambda i: (0, 0)),
                  pl.BlockSpec((bm, n), lambda i: (i, 0)),
                  pl.BlockSpec((1, n), lambda i: (0, 0))],
        out_specs=pl.BlockSpec((bm, n), lambda i: (i, 0)),
        out_shape=jax.ShapeDtypeStruct((m, n), F32),
        compiler_params=pltpu.CompilerParams(
            dimension_semantics=("arbitrary",), vmem_limit_bytes=VMEM_LIMIT),
        name=name,
    )(br, w, x, fw.reshape(1, n))


PT = 256
GLA_ROWS = 128
N_MIXER_STAGES = 2 + 4 * (PT // GLA_ROWS) + 1 + 2 * H_X


def _group_a_chunks(proj_ref, wa_ref, abt_ref, avw_ref):
    v = proj_ref[:, C_AV:C_AV + D_A].astype(F32)
    n_chunks = v.shape[0] // CHUNK_A
    ms = jnp.mean(v * v, axis=-1, keepdims=True)
    vb = (v * lax.rsqrt(ms + EPS) * avw_ref[...]).astype(BF16)
    tril = _iota((CHUNK_A, CHUNK_A), 1) <= _iota((CHUNK_A, CHUNK_A), 0)
    r = []
    for gi in range(H_A):
        w = jnp.where(tril, wa_ref[gi], 0.0).astype(BF16)
        ws = slice(WIN_START[gi], WIN_START[gi] + WIN)
        rhs = jnp.concatenate([vb[c * CHUNK_A:(c + 1) * CHUNK_A, ws] for c in range(n_chunks)], axis=1)
        res = _dot(w, rhs)
        r.append(jnp.concatenate([res[:, c * WIN:(c + 1) * WIN] for c in range(n_chunks)], axis=0))
    mixed = _merge_windows(r)
    col_g = _head_id(_iota((CHUNK_A, D_A), 1), HD_A, H_A)
    bias = jnp.zeros((CHUNK_A, D_A), F32)
    for gi in range(H_A):
        bias = jnp.where(col_g == gi, abt_ref[:, gi:gi + 1], bias)
    bias = jnp.concatenate([bias] * n_chunks, axis=0)
    u = proj_ref[:, C_AU:C_AU + D_A].astype(F32)
    g = proj_ref[:, C_AG:C_AG + D_A].astype(F32)
    return u * (mixed + bias) * _silu(g)


def _prompt_block(proj_ref, out_ref, mk_ref, mv_ref, wa_ref, abt_ref, avw_ref, bwa_ref, bba_ref,
                  onw_ref, sbd_ref, o_ref, la_ref):
    out_ref[:, 0:D_A] = _group_a_chunks(proj_ref, wa_ref, abt_ref, avw_ref).astype(BF16)
    yield

    pre = _dot(proj_ref[:, C_BR:C_BR + LANES], bwa_ref[...]) + bba_ref[...]
    la_ref[...] = _log_sigmoid(pre) * (1.0 / GATE_TAU)
    yield
    gl = GLA_ROWS
    tril_bf = jnp.where(_iota((gl, gl), 1) <= _iota((gl, gl), 0), 1.0, 0.0).astype(BF16)
    win_col = _iota((gl, WIN), 1)
    col_hq = _head_id(_iota((gl, D_BK), 1), DK_B, H_B)
    tril_heads = (_iota((H_B * gl, gl), 1) <= jnp.bitwise_and(_iota((H_B * gl, gl), 0), gl - 1))
    for c in range(PT // gl):
        sl = slice(c * gl, (c + 1) * gl)
        log_a = la_ref[sl, :]
        q = proj_ref[sl, C_BQ:C_BQ + D_BK].astype(F32)
        k = proj_ref[sl, C_BK:C_BK + D_BK].astype(F32)
        vb = proj_ref[sl, C_BV:C_BV + D_B]
        hi, lo = _split(log_a)
        cum2 = _dot(tril_bf, jnp.concatenate([hi, lo], axis=1))
        yield
        cum = cum2[:, :D_BK] + cum2[:, D_BK:]
        tot_row = cum[gl - 1:gl, :]
        mid_row = cum[CHUNK_B - 1:CHUNK_B, :]
        la_t = log_a.T
        q_s = q * (DK_B ** -0.5)
        q_dec = q_s * jnp.exp(cum)
        q_in = q_s * jnp.exp(cum - mid_row)
        k_in = k * jnp.exp(mid_row - cum)
        k_out = k * jnp.exp(tot_row - cum)
        q_heads = jnp.concatenate([jnp.where(col_hq == h, q_in, 0.0) for h in range(H_B)], axis=0)
        sc = _dot_nt(q_heads.astype(BF16), k_in.astype(BF16))
        yield
        sc = jnp.where(tril_heads, sc, 0.0).astype(BF16)
        o_intra = _merge_windows(
            [_dot(sc[h * gl:(h + 1) * gl], vb[:, WIN_START[h]:WIN_START[h] + WIN])
             for h in range(H_B)])
        s_old = sbd_ref[...]
        o_inter = _dot(q_dec.astype(BF16), s_old.astype(BF16))
        yield
        kot = k_out.T.astype(BF16)
        for h in range(H_B):
            rs = slice(h * DK_B, (h + 1) * DK_B)
            ws = slice(WIN_START[h], WIN_START[h] + WIN)
            dec = jnp.exp(jnp.sum(la_t[rs], axis=1, keepdims=True))
            lo_col = h * DV_B - WIN_START[h]
            in_head = (win_col >= lo_col) & (win_col < lo_col + DV_B)
            v_h = jnp.where(in_head, vb[:, ws], jnp.zeros((), BF16))
            kv = _dot(kot[rs], v_h)
            sbd_ref[rs, ws] = s_old[rs, ws] * dec + kv
        o_ref[sl, :] = o_intra + o_inter
        yield
    o = o_ref[...]
    o2 = o * o
    t = [o2[:, j * LANES:(j + 1) * LANES] for j in range(D_B // LANES)]
    lo = _iota((PT, LANES), 1) < HEAD_SPLIT
    t1a = jnp.where(lo, t[1], 0.0)
    t4a = jnp.where(lo, t[4], 0.0)
    sums = (t[0] + t1a, (t[1] - t1a) + t[2], t[3] + t4a, (t[4] - t4a) + t[5])
    inv = [lax.rsqrt(jnp.sum(x, axis=-1, keepdims=True) * (1.0 / DV_B) + EPS) for x in sums]
    inv = jnp.concatenate(
        [jnp.broadcast_to(inv[0], (PT, LANES)), jnp.where(lo, inv[0], inv[1]),
         jnp.broadcast_to(inv[1], (PT, LANES)), jnp.broadcast_to(inv[2], (PT, LANES)),
         jnp.where(lo, inv[2], inv[3]), jnp.broadcast_to(inv[3], (PT, LANES))], axis=1)
    o_n = o * inv * onw_ref[...]
    bg = proj_ref[:, C_BG:C_BG + D_B].astype(F32)
    out_ref[:, D_A:D_A + D_B] = (o_n * _silu(bg)).astype(BF16)
    yield

    for h in range(H_X):
        hs = slice(h * HD_X, (h + 1) * HD_X)
        qh = proj_ref[:, C_XQ + h * HD_X:C_XQ + (h + 1) * HD_X]
        s = _dot_nt(qh, mk_ref[:, hs].astype(BF16)) * (HD_X ** -0.5)
        yield
        e = jnp.exp(s - jnp.max(s, axis=-1, keepdims=True))
        den = jnp.sum(e, axis=-1, keepdims=True)
        ox = _dot(e.astype(BF16), mv_ref[:, hs].astype(BF16)) / den
        xg = proj_ref[:, C_XG + h * HD_X:C_XG + (h + 1) * HD_X].astype(F32)
        out_ref[:, D_A + D_B + h * HD_X:D_A + D_B + (h + 1) * HD_X] = (ox * _silu(xg)).astype(BF16)
        yield


def _prompt_layer_kernel(xn_ref, wi_ref, wo_ref, nw_ref, fw_ref, mk_ref, mv_ref, wa_ref, abt_ref,
                         avw_ref, bwa_ref, bba_ref, onw_ref, y_ref, st_ref,
                         pp_ref, xk_ref, br_ref, h_ref, sbd_ref, o_ref, la_ref, *, nt):
    s = pl.program_id(0)
    cur = jnp.maximum(s - 1, 0)
    t = lax.rem(cur, nt)

    @pl.when(t == 0)
    def _():
        sbd_ref[...] = jnp.zeros_like(sbd_ref)

    def in_proj_stages(pn_ref):
        x = xn_ref[...]
        ms = jnp.mean(x * x, axis=-1, keepdims=True)
        h_ref[...] = (x * lax.rsqrt(ms + EPS) * nw_ref[...]).astype(BF16)
        yield
        for c0 in range(0, D_PROJ_PAD, MXU_COLS):
            cols = pl.ds(c0, MXU_COLS)
            pn_ref[:, cols] = _dot(h_ref[...], wi_ref[:, cols]).astype(BF16)
            yield
        xk_ref[...] = xn_ref[...]

    def out_proj_stages():
        ssq = jnp.zeros((PT, 1), F32)
        for c0 in range(0, D_MODEL, OUT_BN):
            cols = pl.ds(c0, OUT_BN)
            acc = _dot(br_ref[...], wo_ref[:, cols]) + xk_ref[:, cols]
            y_ref[:, cols] = acc
            ssq = ssq + jnp.sum(acc * acc, axis=-1, keepdims=True)
            yield
        y_ref[...] = y_ref[...] * lax.rsqrt(ssq * (1.0 / D_MODEL) + EPS) * fw_ref[...]
        yield

    def run(order, streams):
        for name in order:
            next(streams[name])
        for name, gen in streams.items():
            assert next(gen, "done") == "done", name

    def mixer_stages(pc_ref):
        return _prompt_block(pc_ref, br_ref, mk_ref, mv_ref, wa_ref, abt_ref, avw_ref, bwa_ref,
                             bba_ref, onw_ref, sbd_ref, o_ref, la_ref)

    n_p = 1 + D_PROJ_PAD // MXU_COLS
    n_m = N_MIXER_STAGES
    n_o = 1 + D_MODEL // OUT_BN

    def interleave(counts):
        order = []
        for i in range(max(counts.values())):
            order += [name for name, n in counts.items() if i < n]
        return order

    def body(pn_ref, pc_ref):
        order = ["P"] + interleave({"M": n_m, "P": n_p - 1}) + ["O"] * n_o
        run(order, {"P": in_proj_stages(pn_ref), "M": mixer_stages(pc_ref), "O": out_proj_stages()})

    last = pl.num_programs(0) - 1
    par = lax.rem(s, 2)
    pn_ref = pp_ref.at[par]
    pc_ref = pp_ref.at[1 - par]

    @pl.when(s == 0)
    def _():
        run(["P"] * n_p, {"P": in_proj_stages(pn_ref)})

    @pl.when((s > 0) & (s < last))
    def _():
        body(pn_ref, pc_ref)

    @pl.when(s == last)
    def _():
        run(["M"] * n_m + ["O"] * n_o, {"M": mixer_stages(pc_ref), "O": out_proj_stages()})

    @pl.when((t == nt - 1) & (s > 0))
    def _():
        for h in range(H_B):
            off = h * DV_B - WIN_START[h]
            blk = sbd_ref[h * DK_B:(h + 1) * DK_B, WIN_START[h]:WIN_START[h] + WIN]
            if off:
                blk = pltpu.roll(blk, WIN - off, 1)
            st_ref[0, h] = blk[:, :DV_B]


def _prompt_layer(xp, w_in_bf, w_out_bf, nw, fw, memkv, wa, abt, avw, bwa, bba, onw, *, batch, seq):
    nt = seq // PT
    nblk = batch * nt
    assert nblk % 2 == 0
    cur = lambda s: jnp.maximum(s - 1, 0)
    const = lambda *shape: pl.BlockSpec(shape, lambda s: (0,) * len(shape))
    resident = lambda *shape: pl.BlockSpec(shape, lambda s: (0,) * len(shape),
                                           pipeline_mode=pl.Buffered(1))
    return pl.pallas_call(
        functools.partial(_prompt_layer_kernel, nt=nt),
        grid=(nblk + 1,),
        in_specs=[pl.BlockSpec((PT, D_MODEL), lambda s: (jnp.minimum(s, nblk - 1), 0)),
                  resident(D_MODEL, D_PROJ_PAD), resident(D_MODEL, D_MODEL),
                  const(1, D_MODEL), const(1, D_MODEL),
                  pl.BlockSpec((None, N_MEM, D_X), lambda s: (0, cur(s) // nt, 0)),
                  pl.BlockSpec((None, N_MEM, D_X), lambda s: (1, cur(s) // nt, 0)),
                  const(H_A, CHUNK_A, CHUNK_A), const(CHUNK_A, H_A), const(1, D_A),
                  const(LANES, D_BK), const(1, D_BK), const(1, D_B)],
        out_specs=[pl.BlockSpec((PT, D_MODEL), lambda s: (cur(s), 0)),
                   pl.BlockSpec((1, H_B, DK_B, DV_B), lambda s: (cur(s) // nt, 0, 0, 0))],
        out_shape=[jax.ShapeDtypeStruct((batch * seq, D_MODEL), F32),
                   jax.ShapeDtypeStruct((batch, H_B, DK_B, DV_B), F32)],
        scratch_shapes=[pltpu.VMEM((2, PT, D_PROJ_PAD), BF16),
                        pltpu.VMEM((PT, D_MODEL), F32),
                        pltpu.VMEM((PT, D_MODEL), BF16),
                        pltpu.VMEM((PT, D_MODEL), BF16),
                        pltpu.VMEM((D_BK, D_B), F32),
                        pltpu.VMEM((PT, D_B), F32),
                        pltpu.VMEM((PT, D_BK), F32)],
        compiler_params=pltpu.CompilerParams(
            dimension_semantics=("arbitrary",), vmem_limit_bytes=VMEM_LIMIT + 2 * 1024 * 1024),
        name="prompt_layer",
    )(xp, w_in_bf, w_out_bf, nw.reshape(1, D_MODEL), fw.reshape(1, D_MODEL), memkv, memkv,
      wa, abt, avw, bwa, bba, onw)


NS = 8
TS = 8
SB = NS * TS
SBP = 128


def _sample_mixer_kernel(proj_ref, st_ref, ck_ref, cv_ref, wa_ref, abt_ref, avw_ref, bwa_ref,
                         bba_ref, onw_ref, out_ref, stn_ref, cvs_ref,
                         qin_ref, xq_ref, kot_ref, lat_ref, vhm_ref, ghm_ref, ohm_ref, ox_ref):
    ri = _iota((SB, SB), 0)
    ci = _iota((SB, SB), 1)
    same_seq = jnp.right_shift(ri, 3) == jnp.right_shift(ci, 3)
    causal = same_seq & (ci <= ri)

    a_br, vn = _group_a(proj_ref, wa_ref, abt_ref, avw_ref, causal)
    out_ref[:, 0:D_A] = a_br.astype(BF16)
    cvs_ref[...] = vn

    q = proj_ref[:, C_BQ:C_BQ + D_BK].astype(F32)
    k = proj_ref[:, C_BK:C_BK + D_BK].astype(F32)
    vb = proj_ref[:, C_BV:C_BV + D_B]
    bgb = proj_ref[:, C_BG:C_BG + D_B]
    pre = _dot(proj_ref[:, C_BR:C_BR + LANES], bwa_ref[...]) + bba_ref[...]
    log_a = _log_sigmoid(pre) * (1.0 / GATE_TAU)
    causal_bf = jnp.where(causal, 1.0, 0.0).astype(BF16)
    seq_bf = jnp.where(same_seq, 1.0, 0.0).astype(BF16)
    hi, lo = _split(log_a)
    cum = _dot(causal_bf, hi) + _dot(causal_bf, lo)
    tot = _dot(seq_bf, hi) + _dot(seq_bf, lo)
    q_in = q * (DK_B ** -0.5) * jnp.exp(cum)
    k_in = k * jnp.exp(-cum)
    k_out = k * jnp.exp(tot - cum)
    qin_ref[...] = q_in
    zpad = jnp.zeros((SBP - SB, D_BK), F32)
    kot_ref[...] = jnp.concatenate([k_out, zpad], axis=0).T.astype(BF16)
    lat_ref[...] = jnp.concatenate([log_a, zpad], axis=0).T
    xq_ref[...] = proj_ref[:, C_XQ:C_XQ + D_X].astype(F32)

    wins = _gla_intra_windows(q_in, k_in, vb, causal)
    sel_r = _iota((D_B, DV_B), 0)
    sel_c = _iota((D_B, DV_B), 1)
    vhm_ref[...] = jnp.zeros_like(vhm_ref)
    for h in range(H_B):
        off = h * DV_B - WIN_START[h]
        w = wins[h]
        if off:
            w = pltpu.roll(w, WIN - off, 1)
        ohm_ref[h] = w[:, :DV_B]
        sel = jnp.where(sel_r == sel_c + h * DV_B, 1.0, 0.0).astype(BF16)
        vhm_ref[h, 0:SB, :] = _dot(vb, sel).astype(BF16)
        ghm_ref[h] = _dot(bgb, sel)

    mask_x = (jnp.right_shift(_iota((H_X * TS, N_MEM * H_X), 0), 3)
              == jnp.bitwise_and(_iota((H_X * TS, N_MEM * H_X), 1), H_X - 1))
    mask_b = jnp.right_shift(_iota((H_B * TS, D_BK), 0), 3) == _head_id(_iota((H_B * TS, D_BK), 1), DK_B, H_B)
    lane_seq = jnp.right_shift(_iota((D_BK, SBP), 1), 3)

    def per_seq(s, carry):
        r0 = pl.multiple_of(s * TS, TS)
        q8 = xq_ref[pl.ds(r0, TS), :]
        q32 = jnp.concatenate([q8[:, h * HD_X:(h + 1) * HD_X] for h in range(H_X)], axis=0)
        sc = _dot_nt(q32.astype(BF16), ck_ref[s].astype(BF16)) * (HD_X ** -0.5)
        sc = jnp.where(mask_x, sc, -jnp.inf)
        e = jnp.exp(sc - jnp.max(sc, axis=-1, keepdims=True))
        den = jnp.sum(e, axis=-1, keepdims=True)
        o = _dot(e.astype(BF16), cv_ref[s].astype(BF16)) / den
        ox_ref[pl.ds(r0, TS), :] = jnp.concatenate(
            [o[h * TS:(h + 1) * TS] for h in range(H_X)], axis=1)
        qi8 = qin_ref[pl.ds(r0, TS), :]
        qbd2 = jnp.where(mask_b, jnp.concatenate([qi8] * H_B, axis=0), 0.0).astype(BF16)
        s0 = jnp.concatenate([st_ref[s, h] for h in range(H_B)], axis=0)
        o_inter = _dot(qbd2, s0.astype(BF16))
        for h in range(H_B):
            ohm_ref[h, pl.ds(r0, TS), :] += o_inter[h * TS:(h + 1) * TS]
        dec = jnp.exp(jnp.sum(jnp.where(lane_seq == s, lat_ref[...], 0.0), axis=1, keepdims=True))
        kot = jnp.where(lane_seq == s, kot_ref[...], jnp.zeros((), BF16))
        kv = jnp.concatenate(
            [_dot(kot[h * DK_B:(h + 1) * DK_B], vhm_ref[h]) for h in range(H_B)], axis=0)
        s_new = s0 * dec + kv
        for h in range(H_B):
            stn_ref[s, h] = s_new[h * DK_B:(h + 1) * DK_B]
        return carry

    lax.fori_loop(0, NS, per_seq, 0, unroll=4)

    selt_r = _iota((DV_B, D_B), 0)
    selt_c = _iota((DV_B, D_B), 1)
    b_br = jnp.zeros((SB, D_B), F32)
    for h in range(H_B):
        o_h = ohm_ref[h]
        ms = jnp.mean(o_h * o_h, axis=-1, keepdims=True)
        ob = (o_h * lax.rsqrt(ms + EPS) * onw_ref[...] * _silu(ghm_ref[h])).astype(BF16)
        selt = jnp.where(selt_c == selt_r + h * DV_B, 1.0, 0.0).astype(BF16)
        b_br = b_br + _dot(ob, selt)
    out_ref[:, D_A:D_A + D_B] = b_br.astype(BF16)

    xg = proj_ref[:, C_XG:C_XG + D_X].astype(F32)
    out_ref[:, D_A + D_B:D_MODEL] = (ox_ref[...] * _silu(xg)).astype(BF16)


def _sample_mixer(proj, state, ck, cv, wa, abt, avw, bwa, bba, onw):
    nseq = state.shape[1]
    const = lambda *shape: pl.BlockSpec(shape, lambda i: (0,) * len(shape))
    return pl.pallas_call(
        _sample_mixer_kernel,
        grid=(nseq // NS,),
        in_specs=[pl.BlockSpec((SB, D_PROJ), lambda i: (i, 0)),
                  pl.BlockSpec((None, NS, H_B, DK_B, DV_B), lambda i: (0, i, 0, 0, 0)),
                  pl.BlockSpec((NS, N_MEM * H_X, HD_X), lambda i: (i, 0, 0)),
                  pl.BlockSpec((NS, N_MEM * H_X, HD_X), lambda i: (i, 0, 0)),
                  const(H_A, SB, SB), const(SB, H_A), const(1, D_A),
                  const(LANES, D_BK), const(1, D_BK), const(1, DV_B)],
        out_specs=[pl.BlockSpec((SB, D_MODEL), lambda i: (i, 0)),
                   pl.BlockSpec((None, NS, H_B, DK_B, DV_B), lambda i: (0, i, 0, 0, 0)),
                   pl.BlockSpec((SB, D_A), lambda i: (i, 0))],
        out_shape=[jax.ShapeDtypeStruct((nseq * TS, D_MODEL), BF16),
                   jax.ShapeDtypeStruct((1, nseq, H_B, DK_B, DV_B), F32),
                   jax.ShapeDtypeStruct((nseq * TS, D_A), F32)],
        scratch_shapes=[pltpu.VMEM((SB, D_BK), F32),
                        pltpu.VMEM((SB, D_X), F32),
                        pltpu.VMEM((D_BK, SBP), BF16),
                        pltpu.VMEM((D_BK, SBP), F32),
                        pltpu.VMEM((H_B, SBP, DV_B), BF16),
                        pltpu.VMEM((H_B, SB, DV_B), F32),
                        pltpu.VMEM((H_B, SB, DV_B), F32),
                        pltpu.VMEM((SB, D_X), F32)],
        compiler_params=pltpu.CompilerParams(
            dimension_semantics=("arbitrary",), vmem_limit_bytes=VMEM_LIMIT),
        name="sample_mixer",
    )(proj, state, ck, cv, wa, abt, avw, bwa, bba, onw)


def kernel(x_prompt, x_sample, mem_prompt, state_gla, cache_mem_k, cache_mem_v, norm_w, w_in,
           a_vnorm_w, a_ws, a_bs, b_wa, b_ba, b_onorm_w, mem_norm_w, w_mem_kv, w_out, final_norm_w):
    batch, seq, _ = x_prompt.shape
    nseq, tdec, _ = x_sample.shape
    depth = w_in.shape[0]
    assert depth == 1 and tdec == TS and seq % PT == 0 and nseq % NS == 0

    xp = x_prompt.reshape(batch * seq, D_MODEL)
    xs = x_sample.reshape(nseq * TS, D_MODEL)
    mem = mem_prompt.reshape(batch * N_MEM, D_MODEL)
    w_in_bf, w_out_bf, proj_s, memkv, mem_k, mem_v = _weight_prep(
        jnp.transpose(w_in[0]), w_out[0], w_mem_kv[0], xs, norm_w[0], mem, mem_norm_w[0])
    bwa = jnp.concatenate([b_wa[0], jnp.zeros((LANES - GATE_RANK, D_BK), F32)], axis=0).astype(BF16)
    bba = b_ba[0].reshape(1, D_BK)
    avw = a_vnorm_w[0].reshape(1, D_A)
    onw_p = jnp.tile(b_onorm_w[0], H_B).reshape(1, D_B)
    onw_s = b_onorm_w[0].reshape(1, DV_B)
    wa_p = a_ws[0]
    abt_p = a_bs[0].T
    wa_s = jnp.tile(a_ws[0][:, :TS, :TS], (1, NS, NS))
    abt_s = jnp.tile(a_bs[0][:, :TS], (1, NS)).T

    br_s, st_s, cvs = _sample_mixer(
        proj_s, state_gla,
        cache_mem_k.reshape(nseq, N_MEM * H_X, HD_X), cache_mem_v.reshape(nseq, N_MEM * H_X, HD_X),
        wa_s, abt_s, avw, bwa, bba, onw_s)
    y_s = _out_proj(br_s, w_out_bf, xs, final_norm_w, bm=256, name="out_proj_s")

    y_p, st_p = _prompt_layer(xp, w_in_bf, w_out_bf, norm_w[0], final_norm_w, memkv, wa_p, abt_p,
                              avw, bwa, bba, onw_p, batch=batch, seq=seq)

    return (y_p.reshape(batch, seq, D_MODEL),
            y_s.reshape(nseq, TS, D_MODEL),
            mem_k.reshape(1, batch, N_MEM, H_X, HD_X),
            mem_v.reshape(1, batch, N_MEM, H_X, HD_X),
            st_p.reshape(1, batch, H_B, DK_B, DV_B),
            st_s,
            cvs.reshape(1, nseq, TS, D_A))
```
